```python
import jax, jax.numpy as jnp
from jax import lax
import numpy as np

D_MODEL = 1024
BATCH = 8
SEQ = 2048
DEPTH = 2

GRID_W = 64
CTX_LEN = 256
N_HEADS = 8
HEAD_DIM = 64
ATTN_W = N_HEADS * HEAD_DIM
CONV_W = 512
CONV_K = 3
NA_ROWS = 8
NA_COLS = 16
D_FF = 2816
N_EXPERTS = 8
TOP_K = 2
N_DENSE = (DEPTH + 1) // 2
N_MOE = DEPTH // 2
PROJ_W = 3 * ATTN_W + 3 * CONV_W
EPS = 1e-6
NEG_INF = -1e30

kernel_name = "hybrid_natten_shortconv_moe_dit_block"


def rmsnorm(x, g):
    xf = x.astype(jnp.float32)
    y = xf * lax.rsqrt(jnp.mean(xf * xf, axis=-1, keepdims=True) + EPS)
    return y.astype(x.dtype) * g


def modulate(x, g, shift, scale):
    return rmsnorm(x, g) * (1 + scale) + shift


def heads(t):
    return t.reshape(*t.shape[:-1], N_HEADS, HEAD_DIM)


def split_proj(p):
    a, cw = ATTN_W, CONV_W
    q = p[..., :a]
    k = p[..., a:2 * a]
    v = p[..., 2 * a:3 * a]
    bg = p[..., 3 * a:3 * a + cw]
    cg = p[..., 3 * a + cw:3 * a + 2 * cw]
    u = p[..., 3 * a + 2 * cw:]
    return q, k, v, bg, cg, u


def neighbourhood_attention(q, k, v, k_ctx, v_ctx, rpb):
    b, s, h, dh = q.shape
    rows = s // GRID_W
    kh = min(NA_ROWS, rows)
    scale = dh ** -0.5
    q = q.reshape(b, rows, GRID_W, h, dh)
    k = k.reshape(b, rows, GRID_W, h, dh)
    v = v.reshape(b, rows, GRID_W, h, dh)
    r = jnp.arange(rows)
    row_start = jnp.clip(r - kh // 2, 0, rows - kh)
    key_rows = row_start[:, None] + jnp.arange(kh)[None, :]
    k_rows = k[:, key_rows]
    v_rows = v[:, key_rows]
    col = jnp.arange(GRID_W)
    col_start = jnp.clip(col - NA_COLS // 2, 0, GRID_W - NA_COLS)
    in_win = (col[None, :] >= col_start[:, None]) & (col[None, :] < col_start[:, None] + NA_COLS)
    dr = key_rows - r[:, None]
    dc = jnp.clip(col[None, :] - col[:, None], -(NA_COLS - 1), NA_COLS - 1)
    bias = rpb[:, (dr + NA_ROWS - 1)[:, None, :, None], (dc + NA_COLS - 1)[None, :, None, :]]
    s_loc = jnp.einsum('brchd,briwhd->bhrciw', q, k_rows).astype(jnp.float32) * scale + bias[None].astype(jnp.float32)
    s_loc = jnp.where(in_win[:, None, :], s_loc, NEG_INF)
    s_ctx = jnp.einsum('brchd,blhd->bhrcl', q, k_ctx).astype(jnp.float32) * scale
    n_loc = kh * GRID_W
    p = jax.nn.softmax(jnp.concatenate([s_loc.reshape(b, h, rows, GRID_W, n_loc), s_ctx], axis=-1), axis=-1).astype(v.dtype)
    p_loc = p[..., :n_loc].reshape(b, h, rows, GRID_W, kh, GRID_W)
    p_ctx = p[..., n_loc:]
    o = jnp.einsum('bhrciw,briwhd->brchd', p_loc, v_rows) + jnp.einsum('bhrcl,blhd->brchd', p_ctx, v_ctx)
    return o.reshape(b, s, h * dh)


def context_attention(q, k, v):
    b, l, h, dh = q.shape
    s = jnp.einsum('blhd,bmhd->bhlm', q, k).astype(jnp.float32) * (dh ** -0.5)
    p = jax.nn.softmax(s, axis=-1).astype(v.dtype)
    return jnp.einsum('bhlm,bmhd->blhd', p, v).reshape(b, l, h * dh)


def short_conv(u, bg, cg, w_conv):
    s = u.shape[1]
    z = jnp.pad(cg * u, ((0, 0), (CONV_K // 2, CONV_K // 2), (0, 0)))
    y = z[:, 0:s] * w_conv[0]
    for tap in range(1, CONV_K):
        y = y + z[:, tap:tap + s] * w_conv[tap]
    return bg * y


def merge_branches(h, y_attn, y_conv, w_attn_out, w_conv_out, w_gate, b_gate, w_out):
    gates = jax.nn.sigmoid(h @ w_gate + b_gate)
    merged = gates[..., :D_MODEL] * (y_attn @ w_attn_out) + gates[..., D_MODEL:] * (y_conv @ w_conv_out)
    return merged @ w_out


def swiglu(h, wg, wu, wd):
    return (jax.nn.silu(h @ wg) * (h @ wu)) @ wd


def moe_swiglu(h, w_router, w_exp_gate, w_exp_up, w_exp_down):
    logits = (h @ w_router).astype(jnp.float32)
    top_val, top_idx = lax.top_k(logits, TOP_K)
    top_p = jax.nn.softmax(top_val, axis=-1)
    combine = jnp.einsum('...k,...ke->...e', top_p, jax.nn.one_hot(top_idx, N_EXPERTS, dtype=jnp.float32)).astype(h.dtype)
    y = combine[..., 0:1] * swiglu(h, w_exp_gate[0], w_exp_up[0], w_exp_down[0])
    for e in range(1, N_EXPERTS):
        y = y + combine[..., e:e + 1] * swiglu(h, w_exp_gate[e], w_exp_up[e], w_exp_down[e])
    return y


def setup_inputs(seed: int = 0) -> dict:
    key = jax.random.key(seed)
    ks = jax.random.split(key, 26)
    f32 = jnp.float32
    D = D_MODEL

    def nrm(k, shape, scale):
        return jax.random.normal(k, shape, f32) * scale

    return {
        "x": nrm(ks[0], (BATCH, SEQ, D), 1.0),
        "c": nrm(ks[1], (BATCH, D), 1.0),
        "ctx": nrm(ks[2], (BATCH, CTX_LEN, D), 1.0),
        "c_ctx": nrm(ks[3], (D,), 1.0),
        "w_ada": nrm(ks[4], (DEPTH, D, 6 * D), 0.5 * D ** -0.5),
        "b_ada": nrm(ks[5], (DEPTH, 6 * D), 0.02),
        "norm_mix": 1.0 + nrm(ks[6], (DEPTH, D), 0.02),
        "norm_ffn": 1.0 + nrm(ks[7], (DEPTH, D), 0.02),
        "w_in": nrm(ks[8], (DEPTH, D, PROJ_W), D ** -0.5),
        "w_conv": nrm(ks[9], (DEPTH, CONV_K, CONV_W), CONV_K ** -0.5),
        "rpb": nrm(ks[10], (DEPTH, N_HEADS, 2 * NA_ROWS - 1, 2 * NA_COLS - 1), 0.2),
        "w_attn_out": nrm(ks[11], (DEPTH, ATTN_W, D), ATTN_W ** -0.5),
        "w_conv_out": nrm(ks[12], (DEPTH, CONV_W, D), CONV_W ** -0.5),
        "w_gate": nrm(ks[13], (DEPTH, D, 2 * D), D ** -0.5),
        "b_gate": nrm(ks[14], (DEPTH, 2 * D), 0.02),
        "w_out": nrm(ks[15], (DEPTH, D, D), D ** -0.5),
        "w_ffn_gate": nrm(ks[16], (N_DENSE, D, D_FF), D ** -0.5),
        "w_ffn_up": nrm(ks[17], (N_DENSE, D, D_FF), D ** -0.5),
        "w_ffn_down": nrm(ks[18], (N_DENSE, D_FF, D), D_FF ** -0.5),
        "w_router": nrm(ks[19], (N_MOE, D, N_EXPERTS), D ** -0.5),
        "w_exp_gate": nrm(ks[20], (N_MOE, N_EXPERTS, D, D_FF), D ** -0.5),
        "w_exp_up": nrm(ks[21], (N_MOE, N_EXPERTS, D, D_FF), D ** -0.5),
        "w_exp_down": nrm(ks[22], (N_MOE, N_EXPERTS, D_FF, D), D_FF ** -0.5),
        "norm_final": 1.0 + nrm(ks[23], (D,), 0.02),
    }


def reference(x, c, ctx, c_ctx, w_ada, b_ada, norm_mix, norm_ffn, w_in, w_conv, rpb,
              w_attn_out, w_conv_out, w_gate, b_gate, w_out, w_ffn_gate, w_ffn_up, w_ffn_down,
              w_router, w_exp_gate, w_exp_up, w_exp_down, norm_final):
    s_lat = x.shape[1]
    cond_lat = jax.nn.silu(c)
    cond_ctx = jax.nn.silu(c_ctx)
    for l in range(DEPTH):
        last = l == DEPTH - 1
        mod_lat = (cond_lat @ w_ada[l] + b_ada[l])[:, None, :]
        mod_ctx = cond_ctx @ w_ada[l] + b_ada[l]
        sh1, sc1, g1, sh2, sc2, g2 = jnp.split(mod_lat, 6, axis=-1)
        sh1c, sc1c, g1c, sh2c, sc2c, g2c = jnp.split(mod_ctx, 6, axis=-1)

        h = modulate(x, norm_mix[l], sh1, sc1)
        hc = modulate(ctx, norm_mix[l], sh1c, sc1c)
        q, k, v, bg, cg, u = split_proj(h @ w_in[l])
        if last:
            kv_c = hc @ w_in[l][:, ATTN_W:3 * ATTN_W]
            k_c, v_c = kv_c[..., :ATTN_W], kv_c[..., ATTN_W:]
        else:
            q_c, k_c, v_c, bg_c, cg_c, u_c = split_proj(hc @ w_in[l])
        y_attn = neighbourhood_attention(heads(q), heads(k), heads(v), heads(k_c), heads(v_c), rpb[l])
        y_conv = short_conv(u, bg, cg, w_conv[l])
        x = x + g1 * merge_branches(h, y_attn, y_conv, w_attn_out[l], w_conv_out[l], w_gate[l], b_gate[l], w_out[l])
        if not last:
            yc_attn = context_attention(heads(q_c), heads(k_c), heads(v_c))
            yc_conv = short_conv(u_c, bg_c, cg_c, w_conv[l])
            ctx = ctx + g1c * merge_branches(hc, yc_attn, yc_conv, w_attn_out[l], w_conv_out[l], w_gate[l], b_gate[l], w_out[l])

        h = modulate(x, norm_ffn[l], sh2, sc2)
        if not last:
            h = jnp.concatenate([h, modulate(ctx, norm_ffn[l], sh2c, sc2c)], axis=1)
        if l % 2 == 0:
            y = swiglu(h, w_ffn_gate[l // 2], w_ffn_up[l // 2], w_ffn_down[l // 2])
        else:
            y = moe_swiglu(h, w_router[l // 2], w_exp_gate[l // 2], w_exp_up[l // 2], w_exp_down[l // 2])
        x = x + g2 * y[:, :s_lat]
        if not last:
            ctx = ctx + g2c * y[:, s_lat:]
    return rmsnorm(x, norm_final)
```

```python
import functools

import numpy as np
import jax
import jax.numpy as jnp
from jax import lax
from jax.experimental import pallas as pl
from jax.experimental.pallas import tpu as pltpu

_F32 = jnp.float32
_BF16 = jnp.bfloat16

_GRID_W = 64
_N_HEADS = 8
_HEAD_DIM = 64
_NA_ROWS = 8
_NA_COLS = 16
_CONV_K = 3
_TOP_K = 2
_EPS = 1e-6
_NEG_INF = -1e30

_LANES = 128
_BF16_SUBLANES = 16
_VMEM_LIMIT_BYTES = 56 * 1024 * 1024

_Q_ROWS = 4
_HEADS_PER_BLOCK = _LANES // _HEAD_DIM
_NT = (((1,), (1,)), ((), ()))


def _params(n_axes):
    return pltpu.CompilerParams(
        dimension_semantics=("arbitrary",) * n_axes,
        vmem_limit_bytes=_VMEM_LIMIT_BYTES)


def _resident(shape):
    nd = len(shape)
    return pl.BlockSpec(shape, lambda *_: (0,) * nd, pipeline_mode=pl.Buffered(1))


def _sigmoid(x):
    return 1.0 / (1.0 + jnp.exp(-x))


def _rmsnorm(x, g):
    ms = jnp.mean(x * x, axis=-1, keepdims=True)
    return (x * lax.rsqrt(ms + _EPS)) * g


def _modulate(x, g, shift, scale):
    return _rmsnorm(x, g) * (1.0 + scale) + shift


def _bdot(a, b):
    return jnp.dot(a, b, preferred_element_type=_F32)


def _ada_kernel(cond_ref, w_ref, b_ref, o_ref):
    cond = cond_ref[...]
    act = cond * _sigmoid(cond)
    o_ref[...] = jnp.dot(act, w_ref[...], precision=lax.Precision.HIGHEST,
                         preferred_element_type=_F32) + b_ref[...]


def _ada(cond, w_ada, b_ada, tn):
    depth, d, n_out = w_ada.shape
    rows = cond.shape[0]
    return pl.pallas_call(
        _ada_kernel,
        grid=(depth, n_out // tn),
        in_specs=[
            pl.BlockSpec((rows, d), lambda l, j: (0, 0)),
            pl.BlockSpec((None, d, tn), lambda l, j: (l, 0, j)),
            pl.BlockSpec((None, 1, tn), lambda l, j: (l, 0, j)),
        ],
        out_specs=pl.BlockSpec((None, rows, tn), lambda l, j: (l, 0, j)),
        out_shape=jax.ShapeDtypeStruct((depth, rows, n_out), _F32),
        compiler_params=_params(2),
        name="ada",
    )(cond, w_ada, b_ada.reshape(depth, 1, n_out))


def _mod_spec(d, chunk, row_fn):
    return pl.BlockSpec((None, 1, d), lambda i, *_: (row_fn(i), 0, chunk))


def _proj_kernel(x_ref, g_ref, sh_ref, sc_ref, w_ref, o_ref):
    h = _modulate(x_ref[...], g_ref[...], sh_ref[...], sc_ref[...])
    o_ref[...] = _bdot(h.astype(_BF16), w_ref[...]).astype(o_ref.dtype)


def _proj(x2d, mods, row_fn, g, w, tm):
    n, d = x2d.shape
    n_out = w.shape[1]
    return pl.pallas_call(
        _proj_kernel,
        grid=(n // tm,),
        in_specs=[
            pl.BlockSpec((tm, d), lambda i: (i, 0)),
            _resident((1, d)),
            _mod_spec(d, 0, row_fn),
            _mod_spec(d, 1, row_fn),
            _resident(w.shape),
        ],
        out_specs=pl.BlockSpec((tm, n_out), lambda i: (i, 0)),
        out_shape=jax.ShapeDtypeStruct((n, n_out), _BF16),
        compiler_params=_params(1),
        name="proj",
    )(x2d, g, mods, mods, w)


def _rpb_kernel(rpb_ref, o_ref):
    rows, n = o_ref.shape
    idx = lax.broadcasted_iota(jnp.int32, (rows, n), 1)
    cq = jnp.right_shift(idx, _GRID_W.bit_length() - 1)
    ck = jnp.bitwise_and(idx, _GRID_W - 1)
    dc = jnp.clip(ck - cq, -(_NA_COLS - 1), _NA_COLS - 1) + (_NA_COLS - 1)
    acc = jnp.zeros((rows, n), _F32)
    for j in range(2 * _NA_COLS - 1):
        acc = jnp.where(dc == j, rpb_ref[:, j:j + 1], acc)
    cs = jnp.clip(cq - _NA_COLS // 2, 0, _GRID_W - _NA_COLS)
    in_win = jnp.logical_and(ck >= cs, ck < cs + _NA_COLS)
    o_ref[...] = jnp.where(in_win, acc, _NEG_INF)


def _rpb_tiles(rpb_l):
    h, n_dr, n_dc = rpb_l.shape
    rows = -(-(h * n_dr) // 8) * 8
    flat = jnp.pad(rpb_l.reshape(h * n_dr, n_dc), ((0, rows - h * n_dr), (0, _LANES - n_dc)))
    out = pl.pallas_call(
        _rpb_kernel,
        out_shape=jax.ShapeDtypeStruct((rows, _GRID_W * _GRID_W), _F32),
        compiler_params=pltpu.CompilerParams(vmem_limit_bytes=_VMEM_LIMIT_BYTES),
        name="rpb",
    )(flat)
    return out[:h * n_dr].reshape(h, n_dr, _GRID_W, _GRID_W)


def _attn_plan(rows):
    kh = min(_NA_ROWS, rows)
    union = min(_Q_ROWS + kh - 1, rows)
    n_blocks = rows // _Q_ROWS
    starts, variant_of, patterns = [], [], []
    for t in range(n_blocks):
        start = int(np.clip(_Q_ROWS * t - kh // 2, 0, rows - union))
        pat = []
        for a in range(_Q_ROWS):
            r = _Q_ROWS * t + a
            rs = int(np.clip(r - kh // 2, 0, rows - kh))
            for j in range(union):
                kr = start + j
                pat.append(kr - r if rs <= kr < rs + kh else None)
        pat = tuple(pat)
        if pat not in patterns:
            patterns.append(pat)
        starts.append(start)
        variant_of.append(patterns.index(pat))
    return union, starts, variant_of, patterns


def _bias_table(tiles, union, patterns):
    h = tiles.shape[0]
    neg = jnp.full((h, _GRID_W, _GRID_W), _NEG_INF, _F32)
    variants = []
    for pat in patterns:
        blocks = [neg if dr is None else tiles[:, dr + _NA_ROWS - 1] for dr in pat]
        v = jnp.stack(blocks).reshape(_Q_ROWS, union, h, _GRID_W, _GRID_W)
        v = v.transpose(2, 0, 3, 1, 4).reshape(h, _Q_ROWS * _GRID_W, union * _GRID_W)
        variants.append(v)
    return jnp.stack(variants)


def _attend(qb, parts, lo_mask):
    outs = []
    for hh in range(_HEADS_PER_BLOCK):
        msk = lo_mask if hh == 0 else jnp.logical_not(lo_mask)
        qm = jnp.where(msk, qb, jnp.zeros_like(qb))
        scores = []
        for kk, _, bias_fn in parts:
            s = lax.dot_general(qm, kk, _NT, preferred_element_type=_F32)
            if bias_fn is not None:
                s = s + bias_fn(hh)
            scores.append(s)
        m = functools.reduce(jnp.maximum, [jnp.max(s, axis=-1, keepdims=True) for s in scores])
        acc = None
        den = None
        for s, (_, vv, _) in zip(scores, parts):
            e = jnp.exp(s - m)
            d = jnp.sum(e, axis=-1, keepdims=True)
            o = _bdot(e.astype(_BF16), vv)
            acc = o if acc is None else acc + o
            den = d if den is None else den + d
        outs.append(acc / den)
    return jnp.where(lo_mask, outs[0], outs[1])


def _attn_kernel(start_ref, var_ref, q_ref, k_ref, v_ref, kc_ref, vc_ref, tab_ref, *rest,
                 n_blocks, union_tokens, with_ctx):
    if with_ctx:
        qc_ref, o_ref, oc_ref = rest
    else:
        (o_ref,) = rest
    qb_tokens = _Q_ROWS * _GRID_W
    scale = _HEAD_DIM ** -0.5
    kc = kc_ref[...]
    vc = vc_ref[...]
    lo_q = lax.broadcasted_iota(jnp.int32, (qb_tokens, _LANES), 1) < _HEAD_DIM

    def body(t, carry):
        q0 = pl.multiple_of(t * qb_tokens, qb_tokens)
        k0 = pl.multiple_of(start_ref[t] * _GRID_W, _GRID_W)
        var = var_ref[t]
        qb = q_ref[pl.ds(q0, qb_tokens), :] * scale
        kb = k_ref[pl.ds(k0, union_tokens), :]
        vb = v_ref[pl.ds(k0, union_tokens), :]
        o = _attend(qb, [(kb, vb, lambda hh: tab_ref[var, hh]), (kc, vc, None)], lo_q)
        o_ref[pl.ds(q0, qb_tokens), :] = o.astype(o_ref.dtype)
        return carry

    lax.fori_loop(0, n_blocks, body, 0)

    if with_ctx:
        lo_c = lax.broadcasted_iota(jnp.int32, qc_ref.shape, 1) < _HEAD_DIM
        oc = _attend(qc_ref[...] * scale, [(kc, vc, None)], lo_c)
        oc_ref[...] = oc.astype(oc_ref.dtype)


def _attention(p, pc, kc_col, table, starts, variant_of, union, batch, seq, ctx_len, with_ctx):
    attn_w = _N_HEADS * _HEAD_DIM
    n_pairs = attn_w // _LANES
    n_var = table.shape[0]
    qb_tokens = _Q_ROWS * _GRID_W
    union_tokens = union * _GRID_W
    kern = functools.partial(_attn_kernel, n_blocks=len(starts), union_tokens=union_tokens,
                             with_ctx=with_ctx)
    in_specs = [
        pl.BlockSpec((seq, _LANES), lambda hp, b, *_: (b, hp)),
        pl.BlockSpec((seq, _LANES), lambda hp, b, *_: (b, n_pairs + hp)),
        pl.BlockSpec((seq, _LANES), lambda hp, b, *_: (b, 2 * n_pairs + hp)),
        pl.BlockSpec((ctx_len, _LANES), lambda hp, b, *_: (b, kc_col + hp)),
        pl.BlockSpec((ctx_len, _LANES), lambda hp, b, *_: (b, kc_col + n_pairs + hp)),
        pl.BlockSpec((n_var, _HEADS_PER_BLOCK, qb_tokens, union_tokens),
                     lambda hp, b, *_: (0, hp, 0, 0)),
    ]
    args = [p, p, p, pc, pc, table]
    out_specs = [pl.BlockSpec((seq, _LANES), lambda hp, b, *_: (b, hp))]
    out_shape = [jax.ShapeDtypeStruct((batch * seq, attn_w), _BF16)]
    if with_ctx:
        in_specs.append(pl.BlockSpec((ctx_len, _LANES), lambda hp, b, *_: (b, hp)))
        args.append(pc)
        out_specs.append(pl.BlockSpec((ctx_len, _LANES), lambda hp, b, *_: (b, hp)))
        out_shape.append(jax.ShapeDtypeStruct((batch * ctx_len, attn_w), _BF16))
    outs = pl.pallas_call(
        kern,
        grid_spec=pltpu.PrefetchScalarGridSpec(
            num_scalar_prefetch=2,
            grid=(n_pairs, batch),
            in_specs=in_specs,
            out_specs=out_specs,
        ),
        out_shape=out_shape,
        compiler_params=_params(2),
        name="attn",
    )(jnp.asarray(starts, jnp.int32), jnp.asarray(variant_of, jnp.int32), *args)
    return outs if with_ctx else (outs[0], None)


def _mixer_update(i, x_ref, ya_ref, bcu_ref, prev_ref, next_ref, nm_ref, sh1_ref, sc1_ref, g1_ref,
                  wconv_ref, wao_ref, wco_ref, wgate_ref, bgate_ref, wout_ref, seq_len):
    tm, d = x_ref.shape
    cw = wco_ref.shape[0]
    x = x_ref[...]
    hb = _modulate(x, nm_ref[...], sh1_ref[...], sc1_ref[...]).astype(_BF16)
    gates = _sigmoid(_bdot(hb, wgate_ref[...]) + bgate_ref[...])

    bcu = bcu_ref[...].astype(_F32)
    bg, z = bcu[:, :cw], bcu[:, cw:2 * cw] * bcu[:, 2 * cw:]
    prev = prev_ref[...].astype(_F32)
    nxt = next_ref[...].astype(_F32)
    has_prev = ((i * tm) % seq_len != 0).astype(_F32)
    has_next = (((i + 1) * tm) % seq_len != 0).astype(_F32)
    last = prev.shape[0] - 1
    z_prev = prev[last:last + 1, cw:2 * cw] * prev[last:last + 1, 2 * cw:] * has_prev
    z_next = nxt[0:1, cw:2 * cw] * nxt[0:1, 2 * cw:] * has_next
    row = lax.broadcasted_iota(jnp.int32, z.shape, 0)
    z_dn = jnp.where(row == 0, z_prev, pltpu.roll(z, 1, 0))
    z_up = jnp.where(row == tm - 1, z_next, pltpu.roll(z, tm - 1, 0))
    wconv = wconv_ref[...]
    y_conv = bg * (z_dn * wconv[0:1] + z * wconv[1:2] + z_up * wconv[2:3])

    a_proj = _bdot(ya_ref[...], wao_ref[...])
    c_proj = _bdot(y_conv.astype(_BF16), wco_ref[...])
    merged = gates[:, :d] * a_proj + gates[:, d:] * c_proj
    return x + g1_ref[...] * _bdot(merged.astype(_BF16), wout_ref[...])


_N_MIX_REFS = 15


def _merge_ffn_kernel(*refs, seq_len):
    mix = refs[:_N_MIX_REFS]
    nf_ref, sh2_ref, sc2_ref, g2_ref, wg_ref, wu_ref, wd_ref, o_ref = refs[_N_MIX_REFS:]
    x1 = _mixer_update(pl.program_id(0), *mix, seq_len=seq_len)
    h2 = _modulate(x1, nf_ref[...], sh2_ref[...], sc2_ref[...]).astype(_BF16)
    gate = _bdot(h2, wg_ref[...])
    act = (gate * _sigmoid(gate)) * _bdot(h2, wu_ref[...])
    o_ref[...] = x1 + g2_ref[...] * _bdot(act.astype(_BF16), wd_ref[...])


def _merge_router_kernel(*refs, seq_len, n_experts):
    mix = refs[:_N_MIX_REFS]
    nf_ref, sh2_ref, sc2_ref, wr_ref, x1_ref, h2_ref, comb_ref = refs[_N_MIX_REFS:]
    x1 = _mixer_update(pl.program_id(0), *mix, seq_len=seq_len)
    x1_ref[...] = x1
    h2 = _modulate(x1, nf_ref[...], sh2_ref[...], sc2_ref[...])
    h2_ref[...] = h2.astype(h2_ref.dtype)
    logits = jnp.dot(h2, wr_ref[...], precision=lax.Precision.HIGHEST,
                     preferred_element_type=_F32)
    lane = lax.broadcasted_iota(jnp.int32, logits.shape, 1)
    logits = jnp.where(lane < n_experts, logits, -jnp.inf)
    v1 = jnp.max(logits, axis=-1, keepdims=True)
    i1 = jnp.min(jnp.where(logits == v1, lane, _LANES), axis=-1, keepdims=True)
    rest = jnp.where(lane == i1, -jnp.inf, logits)
    v2 = jnp.max(rest, axis=-1, keepdims=True)
    i2 = jnp.min(jnp.where(rest == v2, lane, _LANES), axis=-1, keepdims=True)
    e2 = jnp.exp(v2 - v1)
    den = 1.0 + e2
    comb_ref[...] = jnp.where(lane == i1, 1.0 / den, jnp.where(lane == i2, e2 / den, 0.0))


def _mixer_specs(n, d, tm, p_cols, row_fn, weights):
    half = p_cols // 2
    sub = _BF16_SUBLANES
    n_sub = n // sub
    specs = [
        pl.BlockSpec((tm, d), lambda i: (i, 0)),
        pl.BlockSpec((tm, weights["wao"].shape[0]), lambda i: (i, 0)),
        pl.BlockSpec((tm, half), lambda i: (i, 1)),
        pl.BlockSpec((sub, half), lambda i: (jnp.maximum(i * (tm // sub) - 1, 0), 1)),
        pl.BlockSpec((sub, half), lambda i: (jnp.minimum((i + 1) * (tm // sub), n_sub - 1), 1)),
        _resident((1, d)),
        _mod_spec(d, 0, row_fn), _mod_spec(d, 1, row_fn), _mod_spec(d, 2, row_fn),
    ]
    for name in ("wconv", "wao", "wco", "wgate", "bgate", "wout"):
        specs.append(_resident(weights[name].shape))
    return specs


def _merge_ffn(x2d, ya, p, mods, row_fn, seq_len, wts, tm):
    n, d = x2d.shape
    specs = _mixer_specs(n, d, tm, p.shape[1], row_fn, wts)
    specs += [_resident((1, d)), _mod_spec(d, 3, row_fn), _mod_spec(d, 4, row_fn),
              _mod_spec(d, 5, row_fn),
              _resident(wts["ffn_gate"].shape), _resident(wts["ffn_up"].shape),
              _resident(wts["ffn_down"].shape)]
    return pl.pallas_call(
        functools.partial(_merge_ffn_kernel, seq_len=seq_len),
        grid=(n // tm,),
        in_specs=specs,
        out_specs=pl.BlockSpec((tm, d), lambda i: (i, 0)),
        out_shape=jax.ShapeDtypeStruct((n, d), _F32),
        compiler_params=_params(1),
        name="merge_ffn",
    )(x2d, ya, p, p, p, wts["norm_mix"], mods, mods, mods,
      wts["wconv"], wts["wao"], wts["wco"], wts["wgate"], wts["bgate"], wts["wout"],
      wts["norm_ffn"], mods, mods, mods, wts["ffn_gate"], wts["ffn_up"], wts["ffn_down"])


def _merge_router(x2d, ya, p, mods, row_fn, seq_len, wts, n_experts, tm):
    n, d = x2d.shape
    specs = _mixer_specs(n, d, tm, p.shape[1], row_fn, wts)
    specs += [_resident((1, d)), _mod_spec(d, 3, row_fn), _mod_spec(d, 4, row_fn),
              _resident(wts["router"].shape)]
    tile = lambda w: pl.BlockSpec((tm, w), lambda i: (i, 0))
    return pl.pallas_call(
        functools.partial(_merge_router_kernel, seq_len=seq_len, n_experts=n_experts),
        grid=(n // tm,),
        in_specs=specs,
        out_specs=[tile(d), tile(d), tile(_LANES)],
        out_shape=[jax.ShapeDtypeStruct((n, d), _F32),
                   jax.ShapeDtypeStruct((n, d), _BF16),
                   jax.ShapeDtypeStruct((n, _LANES), _F32)],
        compiler_params=_params(1),
        name="merge_router",
    )(x2d, ya, p, p, p, wts["norm_mix"], mods, mods, mods,
      wts["wconv"], wts["wao"], wts["wco"], wts["wgate"], wts["bgate"], wts["wout"],
      wts["norm_ffn"], mods, mods, wts["router"])


def _moe_kernel(h_ref, comb_ref, acc_in_ref, wg_ref, wu_ref, wd_ref, acc_ref):
    e = pl.program_id(0)
    h = h_ref[...]
    gate = _bdot(h, wg_ref[...])
    act = (gate * _sigmoid(gate)) * _bdot(h, wu_ref[...])
    y = _bdot(act.astype(_BF16), wd_ref[...])
    lane = lax.broadcasted_iota(jnp.int32, comb_ref.shape, 1)
    w = jnp.sum(jnp.where(lane == e, comb_ref[...], 0.0), axis=-1, keepdims=True)
    acc_ref[...] = jnp.where(e == 0, 0.0, acc_in_ref[...]) + w * y


def _moe(h2, comb, wg, wu, wd, tm):
    n, d = h2.shape
    n_experts, _, f = wg.shape
    acc0 = jnp.zeros((n, d), _F32)
    return pl.pallas_call(
        _moe_kernel,
        grid=(n_experts, n // tm),
        in_specs=[
            pl.BlockSpec((tm, d), lambda e, i: (i, 0)),
            pl.BlockSpec((tm, _LANES), lambda e, i: (i, 0)),
            pl.BlockSpec((tm, d), lambda e, i: (i, 0)),
            pl.BlockSpec((None, d, f), lambda e, i: (e, 0, 0)),
            pl.BlockSpec((None, d, f), lambda e, i: (e, 0, 0)),
            pl.BlockSpec((None, f, d), lambda e, i: (e, 0, 0)),
        ],
        out_specs=pl.BlockSpec((tm, d), lambda e, i: (i, 0)),
        out_shape=jax.ShapeDtypeStruct((n, d), _F32),
        input_output_aliases={2: 0},
        compiler_params=_params(2),
        name="moe",
    )(h2, comb, acc0, wg, wu, wd)


def _final_kernel(x_ref, y_ref, g2_ref, nf_ref, o_ref):
    o_ref[...] = _rmsnorm(x_ref[...] + g2_ref[...] * y_ref[...], nf_ref[...])


def _final(x1, y, mods, row_fn, norm_final, tm):
    n, d = x1.shape
    tile = pl.BlockSpec((tm, d), lambda i: (i, 0))
    return pl.pallas_call(
        _final_kernel,
        grid=(n // tm,),
        in_specs=[tile, tile, _mod_spec(d, 5, row_fn), _resident((1, d))],
        out_specs=tile,
        out_shape=jax.ShapeDtypeStruct((n, d), _F32),
        compiler_params=_params(1),
        name="final",
    )(x1, y, mods, norm_final)


def kernel(x, c, ctx, c_ctx, w_ada, b_ada, norm_mix, norm_ffn, w_in, w_conv, rpb, w_attn_out,
           w_conv_out, w_gate, b_gate, w_out, w_ffn_gate, w_ffn_up, w_ffn_down, w_router,
           w_exp_gate, w_exp_up, w_exp_down, norm_final):
    batch, seq, d = x.shape
    ctx_len = ctx.shape[1]
    depth = w_ada.shape[0]
    attn_w = _N_HEADS * _HEAD_DIM
    n_experts = w_router.shape[-1]
    rows = seq // _GRID_W
    n_pairs = attn_w // _LANES
    assert depth == 2 and seq % (_Q_ROWS * _GRID_W) == 0 and attn_w % _LANES == 0

    ctx_row = batch
    cond_rows = -(-(batch + 1) // 8) * 8
    cond = jnp.zeros((cond_rows, d), _F32).at[:batch].set(c).at[ctx_row].set(c_ctx)
    mods_all = _ada(cond, w_ada, b_ada, tn=1536).reshape(depth, cond_rows, 1, 6 * d)

    union, starts, variant_of, patterns = _attn_plan(rows)

    x2d = x.reshape(batch * seq, d)
    c2d = ctx.reshape(batch * ctx_len, d)
    tm_lat = 256
    ctx_row_fn = lambda i: ctx_row

    def lat_rows(tm):
        return lambda i: (i * tm) // seq

    out = None
    for l in range(depth):
        last = l == depth - 1
        mods = mods_all[l]
        wts = {
            "norm_mix": norm_mix[l].reshape(1, d),
            "norm_ffn": norm_ffn[l].reshape(1, d),
            "wconv": w_conv[l],
            "wao": w_attn_out[l].astype(_BF16),
            "wco": w_conv_out[l].astype(_BF16),
            "wgate": w_gate[l].astype(_BF16),
            "bgate": b_gate[l].reshape(1, -1),
            "wout": w_out[l].astype(_BF16),
        }
        w_in_l = w_in[l].astype(_BF16)
        p = _proj(x2d, mods, lat_rows(512), wts["norm_mix"], w_in_l, tm=512)
        if last:
            pc = _proj(c2d, mods, ctx_row_fn, wts["norm_mix"], w_in_l[:, attn_w:3 * attn_w], tm=512)
            kc_col = 0
        else:
            pc = _proj(c2d, mods, ctx_row_fn, wts["norm_mix"], w_in_l, tm=512)
            kc_col = n_pairs
        table = _bias_table(_rpb_tiles(rpb[l]), union, patterns)
        ya, yca = _attention(p, pc, kc_col, table, starts, variant_of, union, batch, seq,
                             ctx_len, with_ctx=not last)
        if not last:
            wts["ffn_gate"] = w_ffn_gate[l // 2].astype(_BF16)
            wts["ffn_up"] = w_ffn_up[l // 2].astype(_BF16)
            wts["ffn_down"] = w_ffn_down[l // 2].astype(_BF16)
            x2d = _merge_ffn(x2d, ya, p, mods, lat_rows(tm_lat), seq, wts, tm=tm_lat)
            c2d = _merge_ffn(c2d, yca, pc, mods, ctx_row_fn, ctx_len, wts, tm=ctx_len)
        else:
            wts["router"] = jnp.pad(w_router[l // 2], ((0, 0), (0, _LANES - n_experts)))
            x1, h2, comb = _merge_router(x2d, ya, p, mods, lat_rows(tm_lat), seq, wts,
                                         n_experts, tm=tm_lat)
            y = _moe(h2, comb, w_exp_gate[l // 2].astype(_BF16), w_exp_up[l // 2].astype(_BF16),
                     w_exp_down[l // 2].astype(_BF16), tm=256)
            out = _final(x1, y, mods, lat_rows(512), norm_final.reshape(1, d), tm=512)
    return out.reshape(batch, seq, d)
```

```python
import functools

import numpy as np
import jax
import jax.numpy as jnp
from jax import lax
from jax.experimental import pallas as pl
from jax.experimental.pallas import tpu as pltpu

_F32 = jnp.float32
_BF16 = jnp.bfloat16

_GRID_W = 64
_N_HEADS = 8
_HEAD_DIM = 64
_NA_ROWS = 8
_NA_COLS = 16
_CONV_K = 3
_TOP_K = 2
_EPS = 1e-6
_NEG_INF = -1e30

_LANES = 128
_F32_SUBLANES = 8
_BF16_SUBLANES = 16
_VMEM_LIMIT_BYTES = 56 * 1024 * 1024

_Q_ROWS = 4
_HEADS_PER_BLOCK = _LANES // _HEAD_DIM
_NT = (((1,), (1,)), ((), ()))


def _params(n_axes):
    return pltpu.CompilerParams(
        dimension_semantics=("arbitrary",) * n_axes,
        vmem_limit_bytes=_VMEM_LIMIT_BYTES)


def _resident(shape):
    nd = len(shape)
    return pl.BlockSpec(shape, lambda *_: (0,) * nd, pipeline_mode=pl.Buffered(1))


def _sigmoid(x):
    return 1.0 / (1.0 + jnp.exp(-x))


def _rmsnorm(x, g):
    ms = jnp.mean(x * x, axis=-1, keepdims=True)
    return (x * lax.rsqrt(ms + _EPS)) * g


def _modulate(x, g, shift, scale):
    return _rmsnorm(x, g) * (1.0 + scale) + shift


def _bdot(a, b):
    return jnp.dot(a, b, preferred_element_type=_F32)


def _indicator(mask):
    return jnp.where(mask, 1.0, 0.0).astype(_BF16)


def _ada_kernel(cond_ref, w_ref, b_ref, o_ref):
    cond = cond_ref[...]
    act = cond * _sigmoid(cond)
    o_ref[...] = jnp.dot(act, w_ref[...], precision=lax.Precision.HIGHEST,
                         preferred_element_type=_F32) + b_ref[...]


def _ada(cond, w_ada, b_ada, tn):
    depth, d, n_out = w_ada.shape
    rows = cond.shape[0]
    return pl.pallas_call(
        _ada_kernel,
        grid=(depth, n_out // tn),
        in_specs=[
            pl.BlockSpec((rows, d), lambda l, j: (0, 0)),
            pl.BlockSpec((None, d, tn), lambda l, j: (l, 0, j)),
            pl.BlockSpec((None, 1, tn), lambda l, j: (l, 0, j)),
        ],
        out_specs=pl.BlockSpec((None, rows, tn), lambda l, j: (l, 0, j)),
        out_shape=jax.ShapeDtypeStruct((depth, rows, n_out), _F32),
        compiler_params=_params(2),
        name="ada",
    )(cond, w_ada, b_ada.reshape(depth, 1, n_out))


def _mod_spec(d, chunk, row_fn):
    return pl.BlockSpec((None, 1, d), lambda i, *_: (row_fn(i), 0, chunk))


def _proj_kernel(x_ref, g_ref, sh_ref, sc_ref, w_ref, o_ref):
    h = _modulate(x_ref[...], g_ref[...], sh_ref[...], sc_ref[...])
    o_ref[...] = _bdot(h.astype(_BF16), w_ref[...]).astype(o_ref.dtype)


def _proj(x2d, mods, row_fn, g, w, tm):
    n, d = x2d.shape
    n_out = w.shape[1]
    return pl.pallas_call(
        _proj_kernel,
        grid=(n // tm,),
        in_specs=[
            pl.BlockSpec((tm, d), lambda i: (i, 0)),
            _resident((1, d)),
            _mod_spec(d, 0, row_fn),
            _mod_spec(d, 1, row_fn),
            _resident(w.shape),
        ],
        out_specs=pl.BlockSpec((tm, n_out), lambda i: (i, 0)),
        out_shape=jax.ShapeDtypeStruct((n, n_out), _BF16),
        compiler_params=_params(1),
        name="proj",
    )(x2d, g, mods, mods, w)


def _rpb_kernel(rpb_ref, o_ref):
    rows, n = o_ref.shape
    idx = lax.broadcasted_iota(jnp.int32, (rows, n), 1)
    cq = jnp.right_shift(idx, _GRID_W.bit_length() - 1)
    ck = jnp.bitwise_and(idx, _GRID_W - 1)
    dc = jnp.clip(ck - cq, -(_NA_COLS - 1), _NA_COLS - 1) + (_NA_COLS - 1)
    acc = jnp.zeros((rows, n), _F32)
    for j in range(2 * _NA_COLS - 1):
        acc = jnp.where(dc == j, rpb_ref[:, j:j + 1], acc)
    cs = jnp.clip(cq - _NA_COLS // 2, 0, _GRID_W - _NA_COLS)
    in_win = jnp.logical_and(ck >= cs, ck < cs + _NA_COLS)
    o_ref[...] = jnp.where(in_win, acc, _NEG_INF)


def _rpb_tiles(rpb_l):
    h, n_dr, n_dc = rpb_l.shape
    rows = -(-(h * n_dr) // 8) * 8
    flat = jnp.pad(rpb_l.reshape(h * n_dr, n_dc), ((0, rows - h * n_dr), (0, _LANES - n_dc)))
    out = pl.pallas_call(
        _rpb_kernel,
        out_shape=jax.ShapeDtypeStruct((rows, _GRID_W * _GRID_W), _F32),
        compiler_params=pltpu.CompilerParams(vmem_limit_bytes=_VMEM_LIMIT_BYTES),
        name="rpb",
    )(flat)
    return out[:h * n_dr].reshape(h, n_dr, _GRID_W, _GRID_W)


def _attn_plan(rows):
    kh = min(_NA_ROWS, rows)
    union = min(_Q_ROWS + kh - 1, rows)
    n_blocks = rows // _Q_ROWS
    starts, variant_of, patterns = [], [], []
    for t in range(n_blocks):
        start = int(np.clip(_Q_ROWS * t - kh // 2, 0, rows - union))
        pat = []
        for a in range(_Q_ROWS):
            r = _Q_ROWS * t + a
            rs = int(np.clip(r - kh // 2, 0, rows - kh))
            for j in range(union):
                kr = start + j
                pat.append(kr - r if rs <= kr < rs + kh else None)
        pat = tuple(pat)
        if pat not in patterns:
            patterns.append(pat)
        starts.append(start)
        variant_of.append(patterns.index(pat))
    return union, starts, variant_of, patterns


def _bias_table(tiles, union, patterns):
    h = tiles.shape[0]
    neg = jnp.full((h, _GRID_W, _GRID_W), _NEG_INF, _F32)
    variants = []
    for pat in patterns:
        blocks = [neg if dr is None else tiles[:, dr + _NA_ROWS - 1] for dr in pat]
        v = jnp.stack(blocks).reshape(_Q_ROWS, union, h, _GRID_W, _GRID_W)
        v = v.transpose(2, 0, 3, 1, 4).reshape(h, _Q_ROWS * _GRID_W, union * _GRID_W)
        variants.append(v)
    return jnp.stack(variants)


def _attend(qb, parts, lo_mask):
    outs = []
    for hh in range(_HEADS_PER_BLOCK):
        msk = lo_mask if hh == 0 else jnp.logical_not(lo_mask)
        qm = jnp.where(msk, qb, jnp.zeros_like(qb))
        scores = []
        for kk, _, bias_fn in parts:
            s = lax.dot_general(qm, kk, _NT, preferred_element_type=_F32)
            if bias_fn is not None:
                s = s + bias_fn(hh)
            scores.append(s)
        m = functools.reduce(jnp.maximum, [jnp.max(s, axis=-1, keepdims=True) for s in scores])
        acc = None
        den = None
        for s, (_, vv, _) in zip(scores, parts):
            e = jnp.exp(s - m)
            d = jnp.sum(e, axis=-1, keepdims=True)
            o = _bdot(e.astype(_BF16), vv)
            acc = o if acc is None else acc + o
            den = d if den is None else den + d
        outs.append(acc / den)
    return jnp.where(lo_mask, outs[0], outs[1])


def _attn_kernel(start_ref, var_ref, q_ref, k_ref, v_ref, kc_ref, vc_ref, tab_ref, *rest,
                 n_blocks, union_tokens, with_ctx):
    if with_ctx:
        qc_ref, o_ref, oc_ref = rest
    else:
        (o_ref,) = rest
    qb_tokens = _Q_ROWS * _GRID_W
    scale = _HEAD_DIM ** -0.5
    kc = kc_ref[...]
    vc = vc_ref[...]
    lo_q = lax.broadcasted_iota(jnp.int32, (qb_tokens, _LANES), 1) < _HEAD_DIM

    def body(t, carry):
        q0 = pl.multiple_of(t * qb_tokens, qb_tokens)
        k0 = pl.multiple_of(start_ref[t] * _GRID_W, _GRID_W)
        var = var_ref[t]
        qb = q_ref[pl.ds(q0, qb_tokens), :] * scale
        kb = k_ref[pl.ds(k0, union_tokens), :]
        vb = v_ref[pl.ds(k0, union_tokens), :]
        o = _attend(qb, [(kb, vb, lambda hh: tab_ref[var, hh]), (kc, vc, None)], lo_q)
        o_ref[pl.ds(q0, qb_tokens), :] = o.astype(o_ref.dtype)
        return carry

    lax.fori_loop(0, n_blocks, body, 0)

    if with_ctx:
        lo_c = lax.broadcasted_iota(jnp.int32, qc_ref.shape, 1) < _HEAD_DIM
        oc = _attend(qc_ref[...] * scale, [(kc, vc, None)], lo_c)
        oc_ref[...] = oc.astype(oc_ref.dtype)


def _attention(p, pc, kc_col, table, starts, variant_of, union, batch, seq, ctx_len, with_ctx):
    attn_w = _N_HEADS * _HEAD_DIM
    n_pairs = attn_w // _LANES
    n_var = table.shape[0]
    qb_tokens = _Q_ROWS * _GRID_W
    union_tokens = union * _GRID_W
    kern = functools.partial(_attn_kernel, n_blocks=len(starts), union_tokens=union_tokens,
                             with_ctx=with_ctx)
    in_specs = [
        pl.BlockSpec((seq, _LANES), lambda hp, b, *_: (b, hp)),
        pl.BlockSpec((seq, _LANES), lambda hp, b, *_: (b, n_pairs + hp)),
        pl.BlockSpec((seq, _LANES), lambda hp, b, *_: (b, 2 * n_pairs + hp)),
        pl.BlockSpec((ctx_len, _LANES), lambda hp, b, *_: (b, kc_col + hp)),
        pl.BlockSpec((ctx_len, _LANES), lambda hp, b, *_: (b, kc_col + n_pairs + hp)),
        pl.BlockSpec((n_var, _HEADS_PER_BLOCK, qb_tokens, union_tokens),
                     lambda hp, b, *_: (0, hp, 0, 0)),
    ]
    args = [p, p, p, pc, pc, table]
    out_specs = [pl.BlockSpec((seq, _LANES), lambda hp, b, *_: (b, hp))]
    out_shape = [jax.ShapeDtypeStruct((batch * seq, attn_w), _BF16)]
    if with_ctx:
        in_specs.append(pl.BlockSpec((ctx_len, _LANES), lambda hp, b, *_: (b, hp)))
        args.append(pc)
        out_specs.append(pl.BlockSpec((ctx_len, _LANES), lambda hp, b, *_: (b, hp)))
        out_shape.append(jax.ShapeDtypeStruct((batch * ctx_len, attn_w), _BF16))
    outs = pl.pallas_call(
        kern,
        grid_spec=pltpu.PrefetchScalarGridSpec(
            num_scalar_prefetch=2,
            grid=(n_pairs, batch),
            in_specs=in_specs,
            out_specs=out_specs,
        ),
        out_shape=out_shape,
        compiler_params=_params(2),
        name="attn",
    )(jnp.asarray(starts, jnp.int32), jnp.asarray(variant_of, jnp.int32), *args)
    return outs if with_ctx else (outs[0], None)


def _mixer_update(i, x_ref, ya_ref, bcu_ref, prev_ref, next_ref, nm_ref, sh1_ref, sc1_ref, g1_ref,
                  wconv_ref, wao_ref, wco_ref, wgate_ref, bgate_ref, wout_ref, seq_len):
    tm, d = x_ref.shape
    cw = wco_ref.shape[0]
    x = x_ref[...]
    hb = _modulate(x, nm_ref[...], sh1_ref[...], sc1_ref[...]).astype(_BF16)
    gates = _sigmoid(_bdot(hb, wgate_ref[...]) + bgate_ref[...])

    bcu = bcu_ref[...].astype(_F32)
    bg, z = bcu[:, :cw], bcu[:, cw:2 * cw] * bcu[:, 2 * cw:]
    prev = prev_ref[...].astype(_F32)
    nxt = next_ref[...].astype(_F32)
    has_prev = ((i * tm) % seq_len != 0).astype(_F32)
    has_next = (((i + 1) * tm) % seq_len != 0).astype(_F32)
    last = prev.shape[0] - 1
    z_prev = prev[last:last + 1, cw:2 * cw] * prev[last:last + 1, 2 * cw:] * has_prev
    z_next = nxt[0:1, cw:2 * cw] * nxt[0:1, 2 * cw:] * has_next
    row = lax.broadcasted_iota(jnp.int32, z.shape, 0)
    z_dn = jnp.where(row == 0, z_prev, pltpu.roll(z, 1, 0))
    z_up = jnp.where(row == tm - 1, z_next, pltpu.roll(z, tm - 1, 0))
    wconv = wconv_ref[...]
    y_conv = bg * (z_dn * wconv[0:1] + z * wconv[1:2] + z_up * wconv[2:3])

    a_proj = _bdot(ya_ref[...], wao_ref[...])
    c_proj = _bdot(y_conv.astype(_BF16), wco_ref[...])
    merged = gates[:, :d] * a_proj + gates[:, d:] * c_proj
    return x + g1_ref[...] * _bdot(merged.astype(_BF16), wout_ref[...])


_N_MIX_REFS = 15


def _merge_ffn_kernel(*refs, seq_len):
    mix = refs[:_N_MIX_REFS]
    nf_ref, sh2_ref, sc2_ref, g2_ref, wg_ref, wu_ref, wd_ref, o_ref = refs[_N_MIX_REFS:]
    x1 = _mixer_update(pl.program_id(0), *mix, seq_len=seq_len)
    h2 = _modulate(x1, nf_ref[...], sh2_ref[...], sc2_ref[...]).astype(_BF16)
    gate = _bdot(h2, wg_ref[...])
    act = (gate * _sigmoid(gate)) * _bdot(h2, wu_ref[...])
    o_ref[...] = x1 + g2_ref[...] * _bdot(act.astype(_BF16), wd_ref[...])


def _top2(vals, idx, n_idx, axis):
    v1 = jnp.max(vals, axis=axis, keepdims=True)
    i1 = jnp.min(jnp.where(vals == v1, idx, n_idx), axis=axis, keepdims=True)
    rest = jnp.where(idx == i1, -jnp.inf, vals)
    v2 = jnp.max(rest, axis=axis, keepdims=True)
    i2 = jnp.min(jnp.where(rest == v2, idx, n_idx), axis=axis, keepdims=True)
    return i1, i2, v1, v2


def _pick(idx, sel, vals, axis):
    return jnp.sum(jnp.where(idx == sel, vals, 0.0), axis=axis, keepdims=True)


def _merge_router_kernel(*refs, seq_len, n_experts):
    mix = refs[:_N_MIX_REFS]
    nf_ref, sh2_ref, sc2_ref, wr_ref, x1_ref, h2_ref, ms_ref, ml_ref, cnt_ref = refs[_N_MIX_REFS:]
    x1 = _mixer_update(pl.program_id(0), *mix, seq_len=seq_len)
    x1_ref[...] = x1
    h2 = _modulate(x1, nf_ref[...], sh2_ref[...], sc2_ref[...])
    h2_ref[...] = h2.astype(h2_ref.dtype)
    tm = h2.shape[0]
    logits = jnp.dot(h2, wr_ref[...], precision=lax.Precision.HIGHEST,
                     preferred_element_type=_F32)
    row = lax.broadcasted_iota(jnp.int32, (tm, tm), 0)
    col = lax.broadcasted_iota(jnp.int32, (tm, tm), 1)

    lane = lax.broadcasted_iota(jnp.int32, logits.shape, 1)
    i1, i2, v1, v2 = _top2(jnp.where(lane < n_experts, logits, -jnp.inf), lane, _LANES, -1)
    e2 = jnp.exp(v2 - v1)
    den = 1.0 + e2
    sel = _indicator(jnp.logical_or(lane == i1, lane == i2))
    ranks = _bdot(_indicator(col < row), sel)
    fields = (i1.astype(_F32), i2.astype(_F32), _pick(lane, i1, ranks, -1),
              _pick(lane, i2, ranks, -1), 1.0 / den, e2 / den)
    meta = jnp.zeros(logits.shape, _F32)
    for k, f in enumerate(fields):
        meta = jnp.where(lane == k, f, meta)
    ms_ref[...] = meta

    lt = logits.T[:n_experts]
    sub = lax.broadcasted_iota(jnp.int32, lt.shape, 0)
    j1, j2, _, _ = _top2(lt, sub, n_experts, 0)
    sel_t = _indicator(jnp.logical_or(sub == j1, sub == j2))
    ranks_t = _bdot(sel_t, _indicator(row < col))
    fields_t = (j1.astype(_F32), j2.astype(_F32), _pick(sub, j1, ranks_t, 0),
                _pick(sub, j2, ranks_t, 0))
    meta_t = jnp.zeros(lt.shape, _F32)
    for k, f in enumerate(fields_t):
        meta_t = jnp.where(sub == k, f, meta_t)
    ml_ref[...] = meta_t
    cnt = jnp.sum(sel_t.astype(_F32), axis=1, keepdims=True)
    cnt_ref[...] = jnp.broadcast_to(cnt, cnt_ref.shape)


def _mixer_specs(n, d, tm, p_cols, row_fn, weights):
    half = p_cols // 2
    sub = _BF16_SUBLANES
    n_sub = n // sub
    specs = [
        pl.BlockSpec((tm, d), lambda i: (i, 0)),
        pl.BlockSpec((tm, weights["wao"].shape[0]), lambda i: (i, 0)),
        pl.BlockSpec((tm, half), lambda i: (i, 1)),
        pl.BlockSpec((sub, half), lambda i: (jnp.maximum(i * (tm // sub) - 1, 0), 1)),
        pl.BlockSpec((sub, half), lambda i: (jnp.minimum((i + 1) * (tm // sub), n_sub - 1), 1)),
        _resident((1, d)),
        _mod_spec(d, 0, row_fn), _mod_spec(d, 1, row_fn), _mod_spec(d, 2, row_fn),
    ]
    for name in ("wconv", "wao", "wco", "wgate", "bgate", "wout"):
        specs.append(_resident(weights[name].shape))
    return specs


def _merge_ffn(x2d, ya, p, mods, row_fn, seq_len, wts, tm):
    n, d = x2d.shape
    specs = _mixer_specs(n, d, tm, p.shape[1], row_fn, wts)
    specs += [_resident((1, d)), _mod_spec(d, 3, row_fn), _mod_spec(d, 4, row_fn),
              _mod_spec(d, 5, row_fn),
              _resident(wts["ffn_gate"].shape), _resident(wts["ffn_up"].shape),
              _resident(wts["ffn_down"].shape)]
    return pl.pallas_call(
        functools.partial(_merge_ffn_kernel, seq_len=seq_len),
        grid=(n // tm,),
        in_specs=specs,
        out_specs=pl.BlockSpec((tm, d), lambda i: (i, 0)),
        out_shape=jax.ShapeDtypeStruct((n, d), _F32),
        compiler_params=_params(1),
        name="merge_ffn",
    )(x2d, ya, p, p, p, wts["norm_mix"], mods, mods, mods,
      wts["wconv"], wts["wao"], wts["wco"], wts["wgate"], wts["bgate"], wts["wout"],
      wts["norm_ffn"], mods, mods, mods, wts["ffn_gate"], wts["ffn_up"], wts["ffn_down"])


def _merge_router(x2d, ya, p, mods, row_fn, seq_len, wts, n_experts, tm):
    n, d = x2d.shape
    specs = _mixer_specs(n, d, tm, p.shape[1], row_fn, wts)
    specs += [_resident((1, d)), _mod_spec(d, 3, row_fn), _mod_spec(d, 4, row_fn),
              _resident(wts["router"].shape)]
    tile = lambda w: pl.BlockSpec((tm, w), lambda i: (i, 0))
    n_tiles = n // tm
    return pl.pallas_call(
        functools.partial(_merge_router_kernel, seq_len=seq_len, n_experts=n_experts),
        grid=(n_tiles,),
        in_specs=specs,
        out_specs=[tile(d), tile(d), tile(_LANES),
                   pl.BlockSpec((n_experts, tm), lambda i: (i, 0)),
                   pl.BlockSpec((n_experts, _LANES), lambda i: (i, 0))],
        out_shape=[jax.ShapeDtypeStruct((n, d), _F32),
                   jax.ShapeDtypeStruct((n, d), _BF16),
                   jax.ShapeDtypeStruct((n, _LANES), _F32),
                   jax.ShapeDtypeStruct((n_tiles * n_experts, tm), _F32),
                   jax.ShapeDtypeStruct((n_tiles * n_experts, _LANES), _F32)],
        compiler_params=_params(1),
        name="merge_router",
    )(x2d, ya, p, p, p, wts["norm_mix"], mods, mods, mods,
      wts["wconv"], wts["wao"], wts["wco"], wts["wgate"], wts["bgate"], wts["wout"],
      wts["norm_ffn"], mods, mods, wts["router"])


_SEG_ALIGN = _BF16_SUBLANES
_FFN_ROWS = 256


def _round_up(v, m):
    return (v + m - 1) // m * m


def _local_slots(tm, n_experts):
    return _round_up(_TOP_K * tm + n_experts * (_SEG_ALIGN - 1), _SEG_ALIGN)


def _routing_plan(cnt, n_slots):
    pad = _round_up(cnt, _SEG_ALIGN)
    region = _round_up(jnp.sum(pad, axis=0), _FFN_ROWS)
    ends = jnp.cumsum(region)
    seg_start = (ends - region)[None, :] + jnp.cumsum(pad, axis=0) - pad
    loc_off = jnp.cumsum(pad, axis=1) - pad
    n_used = (ends[-1] // _FFN_ROWS).astype(jnp.int32).reshape(1)
    tile_row = jnp.arange(n_slots // _FFN_ROWS, dtype=jnp.int32) * _FFN_ROWS
    tile_expert = jnp.minimum(jnp.searchsorted(ends, tile_row, side="right"),
                              cnt.shape[1] - 1).astype(jnp.int32)
    flat = lambda a: a.reshape(-1).astype(jnp.int32)
    return flat(seg_start), flat(pad), flat(loc_off), tile_expert, n_used


def _segment_copies(i, n_experts, seg_ref, pad_ref, loc_ref, hbm_ref, buf_ref, sem, to_hbm):
    copies = []
    for e in range(n_experts):
        n = pl.multiple_of(pad_ref[i * n_experts + e], _SEG_ALIGN)
        loc = pl.multiple_of(loc_ref[i * n_experts + e], _SEG_ALIGN)
        seg = pl.multiple_of(seg_ref[i * n_experts + e], _SEG_ALIGN)
        local, remote = buf_ref.at[pl.ds(loc, n)], hbm_ref.at[pl.ds(seg, n)]
        src, dst = (local, remote) if to_hbm else (remote, local)
        copies.append((n, pltpu.make_async_copy(src, dst, sem)))
    return copies


def _run_copies(copies):
    for n, cp in copies:
        pl.when(n > 0)(cp.start)
    for n, cp in copies:
        pl.when(n > 0)(cp.wait)


def _expert_offset(expert, loc_ref, i, n_experts):
    off = jnp.zeros(expert.shape, jnp.int32)
    for e in range(n_experts):
        off = jnp.where(expert == e, loc_ref[i * n_experts + e], off)
    return off


def _dispatch_kernel(seg_ref, pad_ref, loc_ref, h_ref, ml_ref, xs_in_ref, xs_ref, buf, sem,
                     *, n_experts):
    del xs_in_ref
    i = pl.program_id(0)
    tm = h_ref.shape[0]
    meta = ml_ref[...]
    slot1 = _expert_offset(meta[0:1].astype(jnp.int32), loc_ref, i, n_experts) + meta[2:3].astype(jnp.int32)
    slot2 = _expert_offset(meta[1:2].astype(jnp.int32), loc_ref, i, n_experts) + meta[3:4].astype(jnp.int32)
    slot = lax.broadcasted_iota(jnp.int32, (buf.shape[0], tm), 0)
    perm = _indicator(jnp.logical_or(slot == slot1, slot == slot2))
    buf[...] = _bdot(perm, h_ref[...]).astype(buf.dtype)
    _run_copies(_segment_copies(i, n_experts, seg_ref, pad_ref, loc_ref, xs_ref, buf, sem, True))


def _dispatch(h2, meta_l, plan, n_slots, n_experts, tm):
    n, d = h2.shape
    seg_start, pad, loc_off = plan
    return pl.pallas_call(
        functools.partial(_dispatch_kernel, n_experts=n_experts),
        grid_spec=pltpu.PrefetchScalarGridSpec(
            num_scalar_prefetch=3,
            grid=(n // tm,),
            in_specs=[pl.BlockSpec((tm, d), lambda i, *_: (i, 0)),
                      pl.BlockSpec((n_experts, tm), lambda i, *_: (i, 0)),
                      pl.BlockSpec(memory_space=pl.ANY)],
            out_specs=pl.BlockSpec(memory_space=pl.ANY),
            scratch_shapes=[pltpu.VMEM((_local_slots(tm, n_experts), d), _BF16),
                            pltpu.SemaphoreType.DMA(())],
        ),
        out_shape=jax.ShapeDtypeStruct((n_slots, d), _BF16),
        input_output_aliases={5: 0},
        compiler_params=_params(1),
        name="dispatch",
    )(seg_start, pad, loc_off, h2, meta_l, jnp.zeros((n_slots, d), _BF16))


def _expert_ffn_kernel(te_ref, nu_ref, x_ref, wg_ref, wu_ref, wd_ref, o_ref):
    del te_ref
    j = pl.program_id(0)

    @pl.when(j < nu_ref[0])
    def _():
        x = x_ref[...]
        gate = _bdot(x, wg_ref[...])
        act = (gate * _sigmoid(gate)) * _bdot(x, wu_ref[...])
        o_ref[...] = _bdot(act.astype(_BF16), wd_ref[...]).astype(o_ref.dtype)

    @pl.when(j >= nu_ref[0])
    def _():
        o_ref[...] = jnp.zeros_like(o_ref)


def _expert_ffn(xs, tile_expert, n_used, wg, wu, wd):
    n_slots, d = xs.shape
    f = wg.shape[2]
    return pl.pallas_call(
        _expert_ffn_kernel,
        grid_spec=pltpu.PrefetchScalarGridSpec(
            num_scalar_prefetch=2,
            grid=(n_slots // _FFN_ROWS,),
            in_specs=[
                pl.BlockSpec((_FFN_ROWS, d), lambda j, te, nu: (jnp.minimum(j, nu[0] - 1), 0)),
                pl.BlockSpec((None, d, f), lambda j, te, nu: (te[j], 0, 0)),
                pl.BlockSpec((None, d, f), lambda j, te, nu: (te[j], 0, 0)),
                pl.BlockSpec((None, f, d), lambda j, te, nu: (te[j], 0, 0)),
            ],
            out_specs=pl.BlockSpec((_FFN_ROWS, d), lambda j, te, nu: (j, 0)),
        ),
        out_shape=jax.ShapeDtypeStruct((n_slots, d), _BF16),
        compiler_params=_params(1),
        name="expert_ffn",
    )(tile_expert, n_used, xs, wg, wu, wd)


def _combine_kernel(seg_ref, pad_ref, loc_ref, ys_ref, ms_ref, x1_ref, g2_ref, nf_ref, o_ref,
                    buf, sem, *, n_experts):
    i = pl.program_id(0)
    tm = x1_ref.shape[0]

    @pl.when(i == 0)
    def _():
        buf[...] = jnp.zeros_like(buf)

    _run_copies(_segment_copies(i, n_experts, seg_ref, pad_ref, loc_ref, ys_ref, buf, sem, False))
    meta = ms_ref[...]
    col = lambda k: meta[:, k:k + 1]
    slot1 = _expert_offset(col(0).astype(jnp.int32), loc_ref, i, n_experts) + col(2).astype(jnp.int32)
    slot2 = _expert_offset(col(1).astype(jnp.int32), loc_ref, i, n_experts) + col(3).astype(jnp.int32)
    slot = lax.broadcasted_iota(jnp.int32, (tm, buf.shape[0]), 1)
    weights = jnp.where(slot == slot1, col(4), jnp.where(slot == slot2, col(5), 0.0))
    y = _bdot(weights.astype(_BF16), buf[...])
    o_ref[...] = _rmsnorm(x1_ref[...] + g2_ref[...] * y, nf_ref[...])


def _combine(ys, meta_s, x1, plan, mods, row_fn, norm_final, n_experts, tm):
    n, d = x1.shape
    seg_start, pad, loc_off = plan
    return pl.pallas_call(
        functools.partial(_combine_kernel, n_experts=n_experts),
        grid_spec=pltpu.PrefetchScalarGridSpec(
            num_scalar_prefetch=3,
            grid=(n // tm,),
            in_specs=[pl.BlockSpec(memory_space=pl.ANY),
                      pl.BlockSpec((tm, _LANES), lambda i, *_: (i, 0)),
                      pl.BlockSpec((tm, d), lambda i, *_: (i, 0)),
                      _mod_spec(d, 5, row_fn),
                      _resident((1, d))],
            out_specs=pl.BlockSpec((tm, d), lambda i, *_: (i, 0)),
            scratch_shapes=[pltpu.VMEM((_local_slots(tm, n_experts), d), _BF16),
                            pltpu.SemaphoreType.DMA(())],
        ),
        out_shape=jax.ShapeDtypeStruct((n, d), _F32),
        compiler_params=_params(1),
        name="combine",
    )(seg_start, pad, loc_off, ys, meta_s, x1, mods, norm_final)


def kernel(x, c, ctx, c_ctx, w_ada, b_ada, norm_mix, norm_ffn, w_in, w_conv, rpb, w_attn_out,
           w_conv_out, w_gate, b_gate, w_out, w_ffn_gate, w_ffn_up, w_ffn_down, w_router,
           w_exp_gate, w_exp_up, w_exp_down, norm_final):
    batch, seq, d = x.shape
    ctx_len = ctx.shape[1]
    depth = w_ada.shape[0]
    attn_w = _N_HEADS * _HEAD_DIM
    n_experts = w_router.shape[-1]
    rows = seq // _GRID_W
    n_pairs = attn_w // _LANES
    assert depth == 2 and seq % (_Q_ROWS * _GRID_W) == 0 and attn_w % _LANES == 0
    assert n_experts == _F32_SUBLANES

    ctx_row = batch
    cond_rows = -(-(batch + 1) // 8) * 8
    cond = jnp.zeros((cond_rows, d), _F32).at[:batch].set(c).at[ctx_row].set(c_ctx)
    mods_all = _ada(cond, w_ada, b_ada, tn=1536).reshape(depth, cond_rows, 1, 6 * d)

    union, starts, variant_of, patterns = _attn_plan(rows)

    x2d = x.reshape(batch * seq, d)
    c2d = ctx.reshape(batch * ctx_len, d)
    tm_lat = 256
    ctx_row_fn = lambda i: ctx_row

    def lat_rows(tm):
        return lambda i: (i * tm) // seq

    out = None
    for l in range(depth):
        last = l == depth - 1
        mods = mods_all[l]
        wts = {
            "norm_mix": norm_mix[l].reshape(1, d),
            "norm_ffn": norm_ffn[l].reshape(1, d),
            "wconv": w_conv[l],
            "wao": w_attn_out[l].astype(_BF16),
            "wco": w_conv_out[l].astype(_BF16),
            "wgate": w_gate[l].astype(_BF16),
            "bgate": b_gate[l].reshape(1, -1),
            "wout": w_out[l].astype(_BF16),
        }
        w_in_l = w_in[l].astype(_BF16)
        p = _proj(x2d, mods, lat_rows(512), wts["norm_mix"], w_in_l, tm=512)
        if last:
            pc = _proj(c2d, mods, ctx_row_fn, wts["norm_mix"], w_in_l[:, attn_w:3 * attn_w], tm=512)
            kc_col = 0
        else:
            pc = _proj(c2d, mods, ctx_row_fn, wts["norm_mix"], w_in_l, tm=512)
            kc_col = n_pairs
        table = _bias_table(_rpb_tiles(rpb[l]), union, patterns)
        ya, yca = _attention(p, pc, kc_col, table, starts, variant_of, union, batch, seq,
                             ctx_len, with_ctx=not last)
        if not last:
            wts["ffn_gate"] = w_ffn_gate[l // 2].astype(_BF16)
            wts["ffn_up"] = w_ffn_up[l // 2].astype(_BF16)
            wts["ffn_down"] = w_ffn_down[l // 2].astype(_BF16)
            x2d = _merge_ffn(x2d, ya, p, mods, lat_rows(tm_lat), seq, wts, tm=tm_lat)
            c2d = _merge_ffn(c2d, yca, pc, mods, ctx_row_fn, ctx_len, wts, tm=ctx_len)
        else:
            wts["router"] = jnp.pad(w_router[l // 2], ((0, 0), (0, _LANES - n_experts)))
            tm_moe = 512
            n_tok = batch * seq
            n_tiles = n_tok // tm_moe
            x1, h2, meta_s, meta_l, cnt = _merge_router(x2d, ya, p, mods, lat_rows(tm_moe), seq,
                                                        wts, n_experts, tm=tm_moe)
            n_slots = _round_up(_TOP_K * n_tok + n_tiles * n_experts * (_SEG_ALIGN - 1)
                                + n_experts * (_FFN_ROWS - 1), _FFN_ROWS)
            cnt = cnt[:, 0].reshape(n_tiles, n_experts).astype(jnp.int32)
            seg_start, pad, loc_off, tile_expert, n_used = _routing_plan(cnt, n_slots)
            plan = (seg_start, pad, loc_off)
            xs = _dispatch(h2, meta_l, plan, n_slots, n_experts, tm=tm_moe)
            ys = _expert_ffn(xs, tile_expert, n_used, w_exp_gate[l // 2].astype(_BF16),
                             w_exp_up[l // 2].astype(_BF16), w_exp_down[l // 2].astype(_BF16))
            out = _combine(ys, meta_s, x1, plan, mods, lat_rows(tm_moe), norm_final.reshape(1, d),
                           n_experts, tm=tm_moe)
    return out.reshape(batch, seq, d)
```

```python
import functools

import numpy as np
import jax
import jax.numpy as jnp
from jax import lax
from jax.experimental import pallas as pl
from jax.experimental.pallas import tpu as pltpu

_F32 = jnp.float32
_BF16 = jnp.bfloat16

_GRID_W = 64
_N_HEADS = 8
_HEAD_DIM = 64
_NA_ROWS = 8
_NA_COLS = 16
_CONV_K = 3
_TOP_K = 2
_EPS = 1e-6
_NEG_INF = -1e30

_LANES = 128
_F32_SUBLANES = 8
_BF16_SUBLANES = 16
_VMEM_LIMIT_BYTES = 56 * 1024 * 1024

_Q_ROWS = 4
_ATTN_UNROLL = 2
_HEADS_PER_BLOCK = _LANES // _HEAD_DIM
_NT = (((1,), (1,)), ((), ()))
_TN = (((0,), (0,)), ((), ()))


def _params(n_axes):
    return pltpu.CompilerParams(
        dimension_semantics=("arbitrary",) * n_axes,
        vmem_limit_bytes=_VMEM_LIMIT_BYTES)


def _resident(shape):
    nd = len(shape)
    return pl.BlockSpec(shape, lambda *_: (0,) * nd, pipeline_mode=pl.Buffered(1))


def _sigmoid(x):
    return 1.0 / (1.0 + jnp.exp(-x))


def _rmsnorm(x, g):
    ms = jnp.mean(x * x, axis=-1, keepdims=True)
    return (x * lax.rsqrt(ms + _EPS)) * g


def _modulate(x, g, shift, scale):
    return _rmsnorm(x, g) * (1.0 + scale) + shift


def _bdot(a, b):
    return jnp.dot(a, b, preferred_element_type=_F32)


def _indicator(mask):
    return jnp.where(mask, 1.0, 0.0).astype(_BF16)


def _ada_kernel(cond_ref, w_ref, b_ref, o_ref):
    cond = cond_ref[...]
    act = cond * _sigmoid(cond)
    o_ref[...] = jnp.dot(act, w_ref[...], precision=lax.Precision.HIGHEST,
                         preferred_element_type=_F32) + b_ref[...]


def _ada(cond, w_ada, b_ada, tn):
    depth, d, n_out = w_ada.shape
    rows = cond.shape[0]
    return pl.pallas_call(
        _ada_kernel,
        grid=(depth, n_out // tn),
        in_specs=[
            pl.BlockSpec((rows, d), lambda l, j: (0, 0)),
            pl.BlockSpec((None, d, tn), lambda l, j: (l, 0, j)),
            pl.BlockSpec((None, 1, tn), lambda l, j: (l, 0, j)),
        ],
        out_specs=pl.BlockSpec((None, rows, tn), lambda l, j: (l, 0, j)),
        out_shape=jax.ShapeDtypeStruct((depth, rows, n_out), _F32),
        compiler_params=_params(2),
        name="ada",
    )(cond, w_ada, b_ada.reshape(depth, 1, n_out))


def _mod_spec(d, chunk, row_fn):
    return pl.BlockSpec((None, 1, d), lambda i, *_: (row_fn(i), 0, chunk))


def _proj_kernel(x_ref, g_ref, sh_ref, sc_ref, w_ref, o_ref):
    h = _modulate(x_ref[...], g_ref[...], sh_ref[...], sc_ref[...])
    o_ref[...] = _bdot(h.astype(_BF16), w_ref[...]).astype(o_ref.dtype)


def _proj(x2d, mods, row_fn, g, w, tm):
    n, d = x2d.shape
    n_out = w.shape[1]
    return pl.pallas_call(
        _proj_kernel,
        grid=(n // tm,),
        in_specs=[
            pl.BlockSpec((tm, d), lambda i: (i, 0)),
            _resident((1, d)),
            _mod_spec(d, 0, row_fn),
            _mod_spec(d, 1, row_fn),
            _resident(w.shape),
        ],
        out_specs=pl.BlockSpec((tm, n_out), lambda i: (i, 0)),
        out_shape=jax.ShapeDtypeStruct((n, n_out), _BF16),
        compiler_params=_params(1),
        name="proj",
    )(x2d, g, mods, mods, w)


def _rpb_kernel(rpb_ref, o_ref):
    rows, n = o_ref.shape
    idx = lax.broadcasted_iota(jnp.int32, (rows, n), 1)
    cq = jnp.right_shift(idx, _GRID_W.bit_length() - 1)
    ck = jnp.bitwise_and(idx, _GRID_W - 1)
    dc = jnp.clip(ck - cq, -(_NA_COLS - 1), _NA_COLS - 1) + (_NA_COLS - 1)
    acc = jnp.zeros((rows, n), _F32)
    for j in range(2 * _NA_COLS - 1):
        acc = jnp.where(dc == j, rpb_ref[:, j:j + 1], acc)
    cs = jnp.clip(cq - _NA_COLS // 2, 0, _GRID_W - _NA_COLS)
    in_win = jnp.logical_and(ck >= cs, ck < cs + _NA_COLS)
    o_ref[...] = jnp.where(in_win, acc, _NEG_INF)


def _rpb_tiles(rpb_l):
    h, n_dr, n_dc = rpb_l.shape
    rows = -(-(h * n_dr) // 8) * 8
    flat = jnp.pad(rpb_l.reshape(h * n_dr, n_dc), ((0, rows - h * n_dr), (0, _LANES - n_dc)))
    out = pl.pallas_call(
        _rpb_kernel,
        out_shape=jax.ShapeDtypeStruct((rows, _GRID_W * _GRID_W), _F32),
        compiler_params=pltpu.CompilerParams(vmem_limit_bytes=_VMEM_LIMIT_BYTES),
        name="rpb",
    )(flat)
    return out[:h * n_dr].reshape(h, n_dr, _GRID_W, _GRID_W)


def _attn_plan(rows):
    kh = min(_NA_ROWS, rows)
    union = min(_Q_ROWS + kh - 1, rows)
    n_blocks = rows // _Q_ROWS
    starts, variant_of, patterns = [], [], []
    for t in range(n_blocks):
        start = int(np.clip(_Q_ROWS * t - kh // 2, 0, rows - union))
        pat = []
        for a in range(_Q_ROWS):
            r = _Q_ROWS * t + a
            rs = int(np.clip(r - kh // 2, 0, rows - kh))
            for j in range(union):
                kr = start + j
                pat.append(kr - r if rs <= kr < rs + kh else None)
        pat = tuple(pat)
        if pat not in patterns:
            patterns.append(pat)
        starts.append(start)
        variant_of.append(patterns.index(pat))
    return union, starts, variant_of, patterns


def _bias_table(tiles, union, patterns):
    h, n_dr = tiles.shape[:2]
    tiles = jnp.concatenate([tiles, jnp.full((h, 1, _GRID_W, _GRID_W), _NEG_INF, _F32)], axis=1)
    idx = np.array([[n_dr if dr is None else dr + _NA_ROWS - 1 for dr in pat] for pat in patterns],
                   np.int32).reshape(len(patterns), _Q_ROWS, union)
    blocks = tiles[:, idx]
    return blocks.transpose(1, 0, 2, 4, 3, 5).reshape(
        len(patterns), h, _Q_ROWS * _GRID_W, union * _GRID_W)


def _first_head_lanes(shape):
    return lax.broadcasted_iota(jnp.int32, shape, len(shape) - 1) < _HEAD_DIM


def _values_with_ones(v):
    lo = _first_head_lanes(v.shape)
    ones = jnp.ones_like(v)
    return jnp.where(lo, v, ones), jnp.where(lo, ones, v)


def _attend(qb, parts):
    lo = _first_head_lanes(qb.shape)
    res = []
    for hh in range(_HEADS_PER_BLOCK):
        qm = jnp.where(lo if hh == 0 else jnp.logical_not(lo), qb, jnp.zeros_like(qb))
        scores = []
        for kk, _, bias_fn in parts:
            s = lax.dot_general(qm, kk, _NT, preferred_element_type=_F32)
            if bias_fn is not None:
                s = s + bias_fn(hh)
            scores.append(s)
        m = functools.reduce(jnp.maximum, [jnp.max(s, axis=-1, keepdims=True) for s in scores])
        acc = None
        for s, (_, v1, _) in zip(scores, parts):
            o = _bdot(jnp.exp(s - m).astype(_BF16), v1[hh])
            acc = o if acc is None else acc + o
        res.append(acc)
    num = jnp.where(lo, res[0], res[1])
    den = pltpu.roll(jnp.where(lo, res[1], res[0]), _HEAD_DIM, 1)
    return num / den


def _attn_kernel(start_ref, var_ref, q_ref, k_ref, v_ref, kc_ref, vc_ref, tab_ref, *rest,
                 n_blocks, union_tokens, with_ctx):
    if with_ctx:
        qc_ref, o_ref, oc_ref = rest
    else:
        (o_ref,) = rest
    qb_tokens = _Q_ROWS * _GRID_W
    scale = _HEAD_DIM ** -0.5
    kc = kc_ref[...]
    vc1 = _values_with_ones(vc_ref[...])

    def body(t, carry):
        q0 = pl.multiple_of(t * qb_tokens, qb_tokens)
        k0 = pl.multiple_of(start_ref[t] * _GRID_W, _GRID_W)
        var = var_ref[t]
        qb = q_ref[pl.ds(q0, qb_tokens), :] * scale
        kb = k_ref[pl.ds(k0, union_tokens), :]
        vb1 = _values_with_ones(v_ref[pl.ds(k0, union_tokens), :])
        o = _attend(qb, [(kb, vb1, lambda hh: tab_ref[var, hh]), (kc, vc1, None)])
        o_ref[pl.ds(q0, qb_tokens), :] = o.astype(o_ref.dtype)
        return carry

    lax.fori_loop(0, n_blocks, body, 0, unroll=_ATTN_UNROLL)

    if with_ctx:
        oc_ref[...] = _attend(qc_ref[...] * scale, [(kc, vc1, None)]).astype(oc_ref.dtype)


def _attention(p, pc, kc_col, table, starts, variant_of, union, batch, seq, ctx_len, with_ctx):
    attn_w = _N_HEADS * _HEAD_DIM
    n_pairs = attn_w // _LANES
    n_var = table.shape[0]
    qb_tokens = _Q_ROWS * _GRID_W
    union_tokens = union * _GRID_W
    kern = functools.partial(_attn_kernel, n_blocks=len(starts), union_tokens=union_tokens,
                             with_ctx=with_ctx)
    in_specs = [
        pl.BlockSpec((seq, _LANES), lambda hp, b, *_: (b, hp)),
        pl.BlockSpec((seq, _LANES), lambda hp, b, *_: (b, n_pairs + hp)),
        pl.BlockSpec((seq, _LANES), lambda hp, b, *_: (b, 2 * n_pairs + hp)),
        pl.BlockSpec((ctx_len, _LANES), lambda hp, b, *_: (b, kc_col + hp)),
        pl.BlockSpec((ctx_len, _LANES), lambda hp, b, *_: (b, kc_col + n_pairs + hp)),
        pl.BlockSpec((n_var, _HEADS_PER_BLOCK, qb_tokens, union_tokens),
                     lambda hp, b, *_: (0, hp, 0, 0)),
    ]
    args = [p, p, p, pc, pc, table]
    out_specs = [pl.BlockSpec((seq, _LANES), lambda hp, b, *_: (b, hp))]
    out_shape = [jax.ShapeDtypeStruct((batch * seq, attn_w), _BF16)]
    if with_ctx:
        in_specs.append(pl.BlockSpec((ctx_len, _LANES), lambda hp, b, *_: (b, hp)))
        args.append(pc)
        out_specs.append(pl.BlockSpec((ctx_len, _LANES), lambda hp, b, *_: (b, hp)))
        out_shape.append(jax.ShapeDtypeStruct((batch * ctx_len, attn_w), _BF16))
    outs = pl.pallas_call(
        kern,
        grid_spec=pltpu.PrefetchScalarGridSpec(
            num_scalar_prefetch=2,
            grid=(n_pairs, batch),
            in_specs=in_specs,
            out_specs=out_specs,
        ),
        out_shape=out_shape,
        compiler_params=_params(2),
        name="attn",
    )(jnp.asarray(starts, jnp.int32), jnp.asarray(variant_of, jnp.int32), *args)
    return outs if with_ctx else (outs[0], None)


def _mixer_update(i, x_ref, ya_ref, bcu_ref, prev_ref, next_ref, nm_ref, sh1_ref, sc1_ref, g1_ref,
                  wconv_ref, wao_ref, wco_ref, wgate_ref, bgate_ref, wout_ref, seq_len):
    tm, d = x_ref.shape
    cw = wco_ref.shape[0]
    x = x_ref[...]
    hb = _modulate(x, nm_ref[...], sh1_ref[...], sc1_ref[...]).astype(_BF16)
    gates = _sigmoid(_bdot(hb, wgate_ref[...]) + bgate_ref[...])

    bcu = bcu_ref[...].astype(_F32)
    bg, z = bcu[:, :cw], bcu[:, cw:2 * cw] * bcu[:, 2 * cw:]
    prev = prev_ref[...].astype(_F32)
    nxt = next_ref[...].astype(_F32)
    has_prev = ((i * tm) % seq_len != 0).astype(_F32)
    has_next = (((i + 1) * tm) % seq_len != 0).astype(_F32)
    last = prev.shape[0] - 1
    z_prev = prev[last:last + 1, cw:2 * cw] * prev[last:last + 1, 2 * cw:] * has_prev
    z_next = nxt[0:1, cw:2 * cw] * nxt[0:1, 2 * cw:] * has_next
    row = lax.broadcasted_iota(jnp.int32, z.shape, 0)
    z_dn = jnp.where(row == 0, z_prev, pltpu.roll(z, 1, 0))
    z_up = jnp.where(row == tm - 1, z_next, pltpu.roll(z, tm - 1, 0))
    wconv = wconv_ref[...]
    y_conv = bg * (z_dn * wconv[0:1] + z * wconv[1:2] + z_up * wconv[2:3])

    a_proj = _bdot(ya_ref[...], wao_ref[...])
    c_proj = _bdot(y_conv.astype(_BF16), wco_ref[...])
    merged = gates[:, :d] * a_proj + gates[:, d:] * c_proj
    return x + g1_ref[...] * _bdot(merged.astype(_BF16), wout_ref[...])


_N_MIX_REFS = 15


def _merge_ffn_kernel(*refs, seq_len):
    mix = refs[:_N_MIX_REFS]
    nf_ref, sh2_ref, sc2_ref, g2_ref, wg_ref, wu_ref, wd_ref, o_ref = refs[_N_MIX_REFS:]
    x1 = _mixer_update(pl.program_id(0), *mix, seq_len=seq_len)
    h2 = _modulate(x1, nf_ref[...], sh2_ref[...], sc2_ref[...]).astype(_BF16)
    gate = _bdot(h2, wg_ref[...])
    act = (gate * _sigmoid(gate)) * _bdot(h2, wu_ref[...])
    o_ref[...] = x1 + g2_ref[...] * _bdot(act.astype(_BF16), wd_ref[...])


def _top2(vals, idx, n_idx, axis):
    v1 = jnp.max(vals, axis=axis, keepdims=True)
    i1 = jnp.min(jnp.where(vals == v1, idx, n_idx), axis=axis, keepdims=True)
    rest = jnp.where(idx == i1, -jnp.inf, vals)
    v2 = jnp.max(rest, axis=axis, keepdims=True)
    i2 = jnp.min(jnp.where(rest == v2, idx, n_idx), axis=axis, keepdims=True)
    return i1, i2, v1, v2


def _pick(idx, sel, vals, axis):
    return jnp.sum(jnp.where(idx == sel, vals, 0.0), axis=axis, keepdims=True)


def _merge_router_kernel(*refs, seq_len, n_experts):
    mix = refs[:_N_MIX_REFS]
    (nf_ref, sh2_ref, sc2_ref, wr_hi_ref, wr_lo_ref,
     x1_ref, h2_ref, ms_ref, ml_ref, cnt_ref) = refs[_N_MIX_REFS:]
    x1 = _mixer_update(pl.program_id(0), *mix, seq_len=seq_len)
    x1_ref[...] = x1
    h2 = _modulate(x1, nf_ref[...], sh2_ref[...], sc2_ref[...])
    h_hi = h2.astype(_BF16)
    h2_ref[...] = h_hi
    tm = h2.shape[0]
    h_lo = (h2 - h_hi.astype(_F32)).astype(_BF16)
    wr_hi = wr_hi_ref[...]
    logits = _bdot(h_hi, wr_hi) + _bdot(h_lo, wr_hi) + _bdot(h_hi, wr_lo_ref[...])
    row = lax.broadcasted_iota(jnp.int32, (tm, tm), 0)
    col = lax.broadcasted_iota(jnp.int32, (tm, tm), 1)

    lane = lax.broadcasted_iota(jnp.int32, logits.shape, 1)
    i1, i2, v1, v2 = _top2(jnp.where(lane < n_experts, logits, -jnp.inf), lane, _LANES, -1)
    e2 = jnp.exp(v2 - v1)
    den = 1.0 + e2
    sel = _indicator(jnp.logical_or(lane == i1, lane == i2))
    ranks = _bdot(_indicator(col < row), sel)
    fields = (i1.astype(_F32), i2.astype(_F32), _pick(lane, i1, ranks, -1),
              _pick(lane, i2, ranks, -1), 1.0 / den, e2 / den)
    meta = jnp.zeros(logits.shape, _F32)
    for k, f in enumerate(fields):
        meta = jnp.where(lane == k, f, meta)
    ms_ref[...] = meta

    lt = logits.T[:n_experts]
    sub = lax.broadcasted_iota(jnp.int32, lt.shape, 0)
    j1, j2, _, _ = _top2(lt, sub, n_experts, 0)
    sel_t = _indicator(jnp.logical_or(sub == j1, sub == j2))
    ranks_t = _bdot(sel_t, _indicator(row < col))
    fields_t = (j1.astype(_F32), j2.astype(_F32), _pick(sub, j1, ranks_t, 0),
                _pick(sub, j2, ranks_t, 0))
    meta_t = jnp.zeros(lt.shape, _F32)
    for k, f in enumerate(fields_t):
        meta_t = jnp.where(sub == k, f, meta_t)
    ml_ref[...] = meta_t
    cnt = jnp.sum(sel_t.astype(_F32), axis=1, keepdims=True)
    cnt_ref[...] = jnp.broadcast_to(cnt, cnt_ref.shape)


def _mixer_specs(n, d, tm, p_cols, row_fn, weights):
    half = p_cols // 2
    sub = _BF16_SUBLANES
    n_sub = n // sub
    specs = [
        pl.BlockSpec((tm, d), lambda i: (i, 0)),
        pl.BlockSpec((tm, weights["wao"].shape[0]), lambda i: (i, 0)),
        pl.BlockSpec((tm, half), lambda i: (i, 1)),
        pl.BlockSpec((sub, half), lambda i: (jnp.maximum(i * (tm // sub) - 1, 0), 1)),
        pl.BlockSpec((sub, half), lambda i: (jnp.minimum((i + 1) * (tm // sub), n_sub - 1), 1)),
        _resident((1, d)),
        _mod_spec(d, 0, row_fn), _mod_spec(d, 1, row_fn), _mod_spec(d, 2, row_fn),
    ]
    for name in ("wconv", "wao", "wco", "wgate", "bgate", "wout"):
        specs.append(_resident(weights[name].shape))
    return specs


def _merge_ffn(x2d, ya, p, mods, row_fn, seq_len, wts, tm):
    n, d = x2d.shape
    specs = _mixer_specs(n, d, tm, p.shape[1], row_fn, wts)
    specs += [_resident((1, d)), _mod_spec(d, 3, row_fn), _mod_spec(d, 4, row_fn),
              _mod_spec(d, 5, row_fn),
              _resident(wts["ffn_gate"].shape), _resident(wts["ffn_up"].shape),
              _resident(wts["ffn_down"].shape)]
    return pl.pallas_call(
        functools.partial(_merge_ffn_kernel, seq_len=seq_len),
        grid=(n // tm,),
        in_specs=specs,
        out_specs=pl.BlockSpec((tm, d), lambda i: (i, 0)),
        out_shape=jax.ShapeDtypeStruct((n, d), _F32),
        compiler_params=_params(1),
        name="merge_ffn",
    )(x2d, ya, p, p, p, wts["norm_mix"], mods, mods, mods,
      wts["wconv"], wts["wao"], wts["wco"], wts["wgate"], wts["bgate"], wts["wout"],
      wts["norm_ffn"], mods, mods, mods, wts["ffn_gate"], wts["ffn_up"], wts["ffn_down"])


def _merge_router(x2d, ya, p, mods, row_fn, seq_len, wts, n_experts, tm):
    n, d = x2d.shape
    specs = _mixer_specs(n, d, tm, p.shape[1], row_fn, wts)
    specs += [_resident((1, d)), _mod_spec(d, 3, row_fn), _mod_spec(d, 4, row_fn),
              _resident(wts["router_hi"].shape), _resident(wts["router_lo"].shape)]
    tile = lambda w: pl.BlockSpec((tm, w), lambda i: (i, 0))
    n_tiles = n // tm
    return pl.pallas_call(
        functools.partial(_merge_router_kernel, seq_len=seq_len, n_experts=n_experts),
        grid=(n_tiles,),
        in_specs=specs,
        out_specs=[tile(d), tile(d), tile(_LANES),
                   pl.BlockSpec((n_experts, tm), lambda i: (i, 0)),
                   pl.BlockSpec((n_experts, _LANES), lambda i: (i, 0))],
        out_shape=[jax.ShapeDtypeStruct((n, d), _F32),
                   jax.ShapeDtypeStruct((n, d), _BF16),
                   jax.ShapeDtypeStruct((n, _LANES), _F32),
                   jax.ShapeDtypeStruct((n_tiles * n_experts, tm), _F32),
                   jax.ShapeDtypeStruct((n_tiles * n_experts, _LANES), _F32)],
        compiler_params=_params(1),
        name="merge_router",
    )(x2d, ya, p, p, p, wts["norm_mix"], mods, mods, mods,
      wts["wconv"], wts["wao"], wts["wco"], wts["wgate"], wts["bgate"], wts["wout"],
      wts["norm_ffn"], mods, mods, wts["router_hi"], wts["router_lo"])


_SEG_ALIGN = _BF16_SUBLANES
_FFN_ROWS = 256


def _round_up(v, m):
    return (v + m - 1) // m * m


def _local_slots(tm, n_experts):
    return _round_up(_TOP_K * tm + n_experts * (_SEG_ALIGN - 1), _SEG_ALIGN)


def _routing_plan(cnt, n_slots):
    pad = _round_up(cnt, _SEG_ALIGN)
    total = jnp.sum(pad, axis=0)
    region = _round_up(total, _FFN_ROWS)
    ends = jnp.cumsum(region)
    seg_start = (ends - region)[None, :] + jnp.cumsum(pad, axis=0) - pad
    loc_off = jnp.cumsum(pad, axis=1) - pad
    n_used = (ends[-1] // _FFN_ROWS).astype(jnp.int32).reshape(1)
    tails = jnp.concatenate([ends - region + total, region - total, n_used])
    tile_row = jnp.arange(n_slots // _FFN_ROWS, dtype=jnp.int32) * _FFN_ROWS
    tile_expert = jnp.minimum(jnp.sum(tile_row[:, None] >= ends[None, :], axis=1),
                              cnt.shape[1] - 1).astype(jnp.int32)
    flat = lambda a: a.reshape(-1).astype(jnp.int32)
    return (flat(seg_start), flat(pad), flat(loc_off)), flat(tails), tile_expert, n_used


def _segment_copies(i, n_experts, seg_ref, pad_ref, loc_ref, hbm_ref, buf_ref, sem, to_hbm):
    copies = []
    for e in range(n_experts):
        n = pl.multiple_of(pad_ref[i * n_experts + e], _SEG_ALIGN)
        loc = pl.multiple_of(loc_ref[i * n_experts + e], _SEG_ALIGN)
        seg = pl.multiple_of(seg_ref[i * n_experts + e], _SEG_ALIGN)
        local, remote = buf_ref.at[pl.ds(loc, n)], hbm_ref.at[pl.ds(seg, n)]
        src, dst = (local, remote) if to_hbm else (remote, local)
        copies.append((n, pltpu.make_async_copy(src, dst, sem)))
    return copies


def _run_copies(copies):
    for n, cp in copies:
        pl.when(n > 0)(cp.start)
    for n, cp in copies:
        pl.when(n > 0)(cp.wait)


def _expert_offset(expert, loc_ref, i, n_experts):
    off = jnp.zeros(expert.shape, jnp.int32)
    for e in range(n_experts):
        off = jnp.where(expert == e, loc_ref[i * n_experts + e], off)
    return off


def _dispatch_kernel(seg_ref, pad_ref, loc_ref, tail_ref, h_ref, ml_ref, xs_ref, buf, zeros, sem,
                     *, n_experts):
    i = pl.program_id(0)

    @pl.when(i == pl.num_programs(0) - 1)
    def _():
        zeros[...] = jnp.zeros_like(zeros)
        fills = []
        for e in range(n_experts):
            start = pl.multiple_of(tail_ref[e], _SEG_ALIGN)
            n = pl.multiple_of(tail_ref[n_experts + e], _SEG_ALIGN)
            fills.append((n, pltpu.make_async_copy(zeros.at[pl.ds(0, n)],
                                                   xs_ref.at[pl.ds(start, n)], sem)))
        _run_copies(fills)

        def fill_tile(j, carry):
            cp = pltpu.make_async_copy(
                zeros, xs_ref.at[pl.ds(pl.multiple_of(j * _FFN_ROWS, _FFN_ROWS), _FFN_ROWS)], sem)
            cp.start()
            cp.wait()
            return carry

        lax.fori_loop(tail_ref[2 * n_experts], xs_ref.shape[0] // _FFN_ROWS, fill_tile, 0)

    tm = h_ref.shape[0]
    meta = ml_ref[...]
    slot1 = _expert_offset(meta[0:1].astype(jnp.int32), loc_ref, i, n_experts) + meta[2:3].astype(jnp.int32)
    slot2 = _expert_offset(meta[1:2].astype(jnp.int32), loc_ref, i, n_experts) + meta[3:4].astype(jnp.int32)
    slot = lax.broadcasted_iota(jnp.int32, (buf.shape[0], tm), 0)
    perm = _indicator(jnp.logical_or(slot == slot1, slot == slot2))
    buf[...] = _bdot(perm, h_ref[...]).astype(buf.dtype)
    _run_copies(_segment_copies(i, n_experts, seg_ref, pad_ref, loc_ref, xs_ref, buf, sem, True))


def _dispatch(h2, meta_l, plan, tails, n_slots, n_experts, tm):
    n, d = h2.shape
    seg_start, pad, loc_off = plan
    return pl.pallas_call(
        functools.partial(_dispatch_kernel, n_experts=n_experts),
        grid_spec=pltpu.PrefetchScalarGridSpec(
            num_scalar_prefetch=4,
            grid=(n // tm,),
            in_specs=[pl.BlockSpec((tm, d), lambda i, *_: (i, 0)),
                      pl.BlockSpec((n_experts, tm), lambda i, *_: (i, 0))],
            out_specs=pl.BlockSpec(memory_space=pl.ANY),
            scratch_shapes=[pltpu.VMEM((_local_slots(tm, n_experts), d), _BF16),
                            pltpu.VMEM((_FFN_ROWS, d), _BF16),
                            pltpu.SemaphoreType.DMA(())],
        ),
        out_shape=jax.ShapeDtypeStruct((n_slots, d), _BF16),
        compiler_params=_params(1),
        name="dispatch",
    )(seg_start, pad, loc_off, tails, h2, meta_l)


def _expert_ffn_kernel(te_ref, nu_ref, x_ref, wg_ref, wu_ref, wd_ref, o_ref):
    del te_ref
    j = pl.program_id(0)

    @pl.when(j < nu_ref[0])
    def _():
        x = x_ref[...]
        gate = _bdot(x, wg_ref[...])
        act = (gate * _sigmoid(gate)) * _bdot(x, wu_ref[...])
        o_ref[...] = _bdot(act.astype(_BF16), wd_ref[...]).astype(o_ref.dtype)

    @pl.when(j >= nu_ref[0])
    def _():
        o_ref[...] = jnp.zeros_like(o_ref)


def _expert_ffn(xs, tile_expert, n_used, wg, wu, wd):
    n_slots, d = xs.shape
    f = wg.shape[2]
    return pl.pallas_call(
        _expert_ffn_kernel,
        grid_spec=pltpu.PrefetchScalarGridSpec(
            num_scalar_prefetch=2,
            grid=(n_slots // _FFN_ROWS,),
            in_specs=[
                pl.BlockSpec((_FFN_ROWS, d),
                             lambda j, te, nu: (jnp.maximum(jnp.minimum(j, nu[0] - 1), 0), 0)),
                pl.BlockSpec((None, d, f), lambda j, te, nu: (te[j], 0, 0)),
                pl.BlockSpec((None, d, f), lambda j, te, nu: (te[j], 0, 0)),
                pl.BlockSpec((None, f, d), lambda j, te, nu: (te[j], 0, 0)),
            ],
            out_specs=pl.BlockSpec((_FFN_ROWS, d), lambda j, te, nu: (j, 0)),
        ),
        out_shape=jax.ShapeDtypeStruct((n_slots, d), _BF16),
        compiler_params=_params(1),
        name="expert_ffn",
    )(tile_expert, n_used, xs, wg, wu, wd)


def _combine_kernel(seg_ref, pad_ref, loc_ref, ys_ref, ms_ref, x1_ref, g2_ref, nf_ref, o_ref,
                    buf, sem, *, n_experts):
    i = pl.program_id(0)
    tm = x1_ref.shape[0]

    @pl.when(i == 0)
    def _():
        buf[...] = jnp.zeros_like(buf)

    _run_copies(_segment_copies(i, n_experts, seg_ref, pad_ref, loc_ref, ys_ref, buf, sem, False))
    meta = ms_ref[...]
    col = lambda k: meta[:, k:k + 1]
    slot1 = _expert_offset(col(0).astype(jnp.int32), loc_ref, i, n_experts) + col(2).astype(jnp.int32)
    slot2 = _expert_offset(col(1).astype(jnp.int32), loc_ref, i, n_experts) + col(3).astype(jnp.int32)
    slot = lax.broadcasted_iota(jnp.int32, (tm, buf.shape[0]), 1)
    weights = jnp.where(slot == slot1, col(4), jnp.where(slot == slot2, col(5), 0.0))
    y = _bdot(weights.astype(_BF16), buf[...])
    o_ref[...] = _rmsnorm(x1_ref[...] + g2_ref[...] * y, nf_ref[...])


def _combine(ys, meta_s, x1, plan, mods, row_fn, norm_final, n_experts, tm):
    n, d = x1.shape
    seg_start, pad, loc_off = plan
    return pl.pallas_call(
        functools.partial(_combine_kernel, n_experts=n_experts),
        grid_spec=pltpu.PrefetchScalarGridSpec(
            num_scalar_prefetch=3,
            grid=(n // tm,),
            in_specs=[pl.BlockSpec(memory_space=pl.ANY),
                      pl.BlockSpec((tm, _LANES), lambda i, *_: (i, 0)),
                      pl.BlockSpec((tm, d), lambda i, *_: (i, 0)),
                      _mod_spec(d, 5, row_fn),
                      _resident((1, d))],
            out_specs=pl.BlockSpec((tm, d), lambda i, *_: (i, 0)),
            scratch_shapes=[pltpu.VMEM((_local_slots(tm, n_experts), d), _BF16),
                            pltpu.SemaphoreType.DMA(())],
        ),
        out_shape=jax.ShapeDtypeStruct((n, d), _F32),
        compiler_params=_params(1),
        name="combine",
    )(seg_start, pad, loc_off, ys, meta_s, x1, mods, norm_final)


def kernel(x, c, ctx, c_ctx, w_ada, b_ada, norm_mix, norm_ffn, w_in, w_conv, rpb, w_attn_out,
           w_conv_out, w_gate, b_gate, w_out, w_ffn_gate, w_ffn_up, w_ffn_down, w_router,
           w_exp_gate, w_exp_up, w_exp_down, norm_final):
    batch, seq, d = x.shape
    ctx_len = ctx.shape[1]
    depth = w_ada.shape[0]
    attn_w = _N_HEADS * _HEAD_DIM
    n_experts = w_router.shape[-1]
    rows = seq // _GRID_W
    n_pairs = attn_w // _LANES
    assert depth == 2 and seq % (_Q_ROWS * _GRID_W) == 0 and attn_w % _LANES == 0
    assert n_experts == _F32_SUBLANES

    ctx_row = batch
    cond_rows = -(-(batch + 1) // 8) * 8
    cond = jnp.zeros((cond_rows, d), _F32).at[:batch].set(c).at[ctx_row].set(c_ctx)
    mods_all = _ada(cond, w_ada, b_ada, tn=1536).reshape(depth, cond_rows, 1, 6 * d)

    union, starts, variant_of, patterns = _attn_plan(rows)

    x2d = x.reshape(batch * seq, d)
    c2d = ctx.reshape(batch * ctx_len, d)
    tm_lat = 256
    ctx_row_fn = lambda i: ctx_row

    def lat_rows(tm):
        return lambda i: (i * tm) // seq

    out = None
    for l in range(depth):
        last = l == depth - 1
        mods = mods_all[l]
        wts = {
            "norm_mix": norm_mix[l].reshape(1, d),
            "norm_ffn": norm_ffn[l].reshape(1, d),
            "wconv": w_conv[l],
            "wao": w_attn_out[l].astype(_BF16),
            "wco": w_conv_out[l].astype(_BF16),
            "wgate": w_gate[l].astype(_BF16),
            "bgate": b_gate[l].reshape(1, -1),
            "wout": w_out[l].astype(_BF16),
        }
        w_in_l = w_in[l].astype(_BF16)
        p = _proj(x2d, mods, lat_rows(512), wts["norm_mix"], w_in_l, tm=512)
        if last:
            pc = _proj(c2d, mods, ctx_row_fn, wts["norm_mix"], w_in_l[:, attn_w:3 * attn_w], tm=512)
            kc_col = 0
        else:
            pc = _proj(c2d, mods, ctx_row_fn, wts["norm_mix"], w_in_l, tm=512)
            kc_col = n_pairs
        table = _bias_table(_rpb_tiles(rpb[l]), union, patterns)
        ya, yca = _attention(p, pc, kc_col, table, starts, variant_of, union, batch, seq,
                             ctx_len, with_ctx=not last)
        if not last:
            wts["ffn_gate"] = w_ffn_gate[l // 2].astype(_BF16)
            wts["ffn_up"] = w_ffn_up[l // 2].astype(_BF16)
            wts["ffn_down"] = w_ffn_down[l // 2].astype(_BF16)
            x2d = _merge_ffn(x2d, ya, p, mods, lat_rows(tm_lat), seq, wts, tm=tm_lat)
            c2d = _merge_ffn(c2d, yca, pc, mods, ctx_row_fn, ctx_len, wts, tm=ctx_len)
        else:
            wr = jnp.pad(w_router[l // 2], ((0, 0), (0, _LANES - n_experts)))
            wts["router_hi"] = wr.astype(_BF16)
            wts["router_lo"] = (wr - wts["router_hi"].astype(_F32)).astype(_BF16)
            tm_moe = 512
            n_tok = batch * seq
            n_tiles = n_tok // tm_moe
            x1, h2, meta_s, meta_l, cnt = _merge_router(x2d, ya, p, mods, lat_rows(tm_moe), seq,
                                                        wts, n_experts, tm=tm_moe)
            n_slots = _round_up(_TOP_K * n_tok + n_tiles * n_experts * (_SEG_ALIGN - 1)
                                + n_experts * (_FFN_ROWS - 1), _FFN_ROWS)
            cnt = cnt[:, 0].reshape(n_tiles, n_experts).astype(jnp.int32)
            plan, tails, tile_expert, n_used = _routing_plan(cnt, n_slots)
            xs = _dispatch(h2, meta_l, plan, tails, n_slots, n_experts, tm=tm_moe)
            ys = _expert_ffn(xs, tile_expert, n_used, w_exp_gate[l // 2].astype(_BF16),
                             w_exp_up[l // 2].astype(_BF16), w_exp_down[l // 2].astype(_BF16))
            out = _combine(ys, meta_s, x1, plan, mods, lat_rows(tm_moe), norm_final.reshape(1, d),
                           n_experts, tm=tm_moe)
    return out.reshape(batch, seq, d)
```

```python
import functools

import numpy as np
import jax
import jax.numpy as jnp
from jax import lax
from jax.experimental import pallas as pl
from jax.experimental.pallas import tpu as pltpu

_F32 = jnp.float32
_BF16 = jnp.bfloat16

_GRID_W = 64
_N_HEADS = 8
_HEAD_DIM = 64
_NA_ROWS = 8
_NA_COLS = 16
_CONV_K = 3
_TOP_K = 2
_EPS = 1e-6
_NEG_INF = -1e30

_LANES = 128
_F32_SUBLANES = 8
_BF16_SUBLANES = 16
_VMEM_LIMIT_BYTES = 56 * 1024 * 1024

_Q_ROWS = 4
_ATTN_UNROLL = 2
_HEADS_PER_BLOCK = _LANES // _HEAD_DIM
_NT = (((1,), (1,)), ((), ()))
_TN = (((0,), (0,)), ((), ()))


def _params(n_axes):
    return pltpu.CompilerParams(
        dimension_semantics=("arbitrary",) * n_axes,
        vmem_limit_bytes=_VMEM_LIMIT_BYTES)


def _resident(shape):
    nd = len(shape)
    return pl.BlockSpec(shape, lambda *_: (0,) * nd, pipeline_mode=pl.Buffered(1))


def _sigmoid(x):
    return 1.0 / (1.0 + jnp.exp(-x))


def _rmsnorm(x, g):
    ms = jnp.mean(x * x, axis=-1, keepdims=True)
    return (x * lax.rsqrt(ms + _EPS)) * g


def _modulate(x, g, shift, scale):
    return _rmsnorm(x, g) * (1.0 + scale) + shift


def _bdot(a, b):
    return jnp.dot(a, b, preferred_element_type=_F32)


def _indicator(mask):
    return jnp.where(mask, 1.0, 0.0).astype(_BF16)


def _ada_kernel(cond_ref, w_ref, b_ref, o_ref):
    cond = cond_ref[...]
    act = cond * _sigmoid(cond)
    o_ref[...] = jnp.dot(act, w_ref[...], precision=lax.Precision.HIGHEST,
                         preferred_element_type=_F32) + b_ref[...]


def _ada(cond, w_ada, b_ada, tn):
    depth, d, n_out = w_ada.shape
    rows = cond.shape[0]
    return pl.pallas_call(
        _ada_kernel,
        grid=(depth, n_out // tn),
        in_specs=[
            pl.BlockSpec((rows, d), lambda l, j: (0, 0)),
            pl.BlockSpec((None, d, tn), lambda l, j: (l, 0, j)),
            pl.BlockSpec((None, 1, tn), lambda l, j: (l, 0, j)),
        ],
        out_specs=pl.BlockSpec((None, rows, tn), lambda l, j: (l, 0, j)),
        out_shape=jax.ShapeDtypeStruct((depth, rows, n_out), _F32),
        compiler_params=_params(2),
        name="ada",
    )(cond, w_ada, b_ada.reshape(depth, 1, n_out))


def _mod_spec(d, chunk, row_fn):
    return pl.BlockSpec((None, 1, d), lambda i, *_: (row_fn(i), 0, chunk))


def _proj_kernel(x_ref, g_ref, sh_ref, sc_ref, w_ref, o_ref):
    h = _modulate(x_ref[...], g_ref[...], sh_ref[...], sc_ref[...])
    o_ref[...] = _bdot(h.astype(_BF16), w_ref[...]).astype(o_ref.dtype)


def _proj(x2d, mods, row_fn, g, w, tm):
    n, d = x2d.shape
    n_out = w.shape[1]
    return pl.pallas_call(
        _proj_kernel,
        grid=(n // tm,),
        in_specs=[
            pl.BlockSpec((tm, d), lambda i: (i, 0)),
            _resident((1, d)),
            _mod_spec(d, 0, row_fn),
            _mod_spec(d, 1, row_fn),
            _resident(w.shape),
        ],
        out_specs=pl.BlockSpec((tm, n_out), lambda i: (i, 0)),
        out_shape=jax.ShapeDtypeStruct((n, n_out), _BF16),
        compiler_params=_params(1),
        name="proj",
    )(x2d, g, mods, mods, w)


def _rpb_kernel(rpb_ref, o_ref):
    rows, n = o_ref.shape
    idx = lax.broadcasted_iota(jnp.int32, (rows, n), 1)
    cq = jnp.right_shift(idx, _GRID_W.bit_length() - 1)
    ck = jnp.bitwise_and(idx, _GRID_W - 1)
    dc = jnp.clip(ck - cq, -(_NA_COLS - 1), _NA_COLS - 1) + (_NA_COLS - 1)
    acc = jnp.zeros((rows, n), _F32)
    for j in range(2 * _NA_COLS - 1):
        acc = jnp.where(dc == j, rpb_ref[:, j:j + 1], acc)
    cs = jnp.clip(cq - _NA_COLS // 2, 0, _GRID_W - _NA_COLS)
    in_win = jnp.logical_and(ck >= cs, ck < cs + _NA_COLS)
    o_ref[...] = jnp.where(in_win, acc, _NEG_INF)


def _rpb_tiles(rpb_l):
    h, n_dr, n_dc = rpb_l.shape
    rows = -(-(h * n_dr) // 8) * 8
    flat = jnp.pad(rpb_l.reshape(h * n_dr, n_dc), ((0, rows - h * n_dr), (0, _LANES - n_dc)))
    out = pl.pallas_call(
        _rpb_kernel,
        out_shape=jax.ShapeDtypeStruct((rows, _GRID_W * _GRID_W), _F32),
        compiler_params=pltpu.CompilerParams(vmem_limit_bytes=_VMEM_LIMIT_BYTES),
        name="rpb",
    )(flat)
    return out[:h * n_dr].reshape(h, n_dr, _GRID_W, _GRID_W)


def _attn_plan(rows):
    kh = min(_NA_ROWS, rows)
    union = min(_Q_ROWS + kh - 1, rows)
    n_blocks = rows // _Q_ROWS
    starts, variant_of, patterns = [], [], []
    for t in range(n_blocks):
        start = int(np.clip(_Q_ROWS * t - kh // 2, 0, rows - union))
        pat = []
        for a in range(_Q_ROWS):
            r = _Q_ROWS * t + a
            rs = int(np.clip(r - kh // 2, 0, rows - kh))
            for j in range(union):
                kr = start + j
                pat.append(kr - r if rs <= kr < rs + kh else None)
        pat = tuple(pat)
        if pat not in patterns:
            patterns.append(pat)
        starts.append(start)
        variant_of.append(patterns.index(pat))
    return union, starts, variant_of, patterns


def _bias_table(tiles, union, patterns):
    h, n_dr = tiles.shape[:2]
    tiles = jnp.concatenate([tiles, jnp.full((h, 1, _GRID_W, _GRID_W), _NEG_INF, _F32)], axis=1)
    idx = np.array([[n_dr if dr is None else dr + _NA_ROWS - 1 for dr in pat] for pat in patterns],
                   np.int32).reshape(len(patterns), _Q_ROWS, union)
    blocks = tiles[:, idx]
    return blocks.transpose(1, 0, 2, 4, 3, 5).reshape(
        len(patterns), h, _Q_ROWS * _GRID_W, union * _GRID_W)


def _first_head_lanes(shape):
    return lax.broadcasted_iota(jnp.int32, shape, len(shape) - 1) < _HEAD_DIM


def _values_with_ones(v):
    lo = _first_head_lanes(v.shape)
    ones = jnp.ones_like(v)
    return jnp.where(lo, v, ones), jnp.where(lo, ones, v)


def _attend(qb, parts):
    lo = _first_head_lanes(qb.shape)
    res = []
    for hh in range(_HEADS_PER_BLOCK):
        qm = jnp.where(lo if hh == 0 else jnp.logical_not(lo), qb, jnp.zeros_like(qb))
        scores = []
        for kk, _, bias_fn in parts:
            s = lax.dot_general(qm, kk, _NT, preferred_element_type=_F32)
            if bias_fn is not None:
                s = s + bias_fn(hh)
            scores.append(s)
        m = functools.reduce(jnp.maximum, [jnp.max(s, axis=-1, keepdims=True) for s in scores])
        acc = None
        for s, (_, v1, _) in zip(scores, parts):
            o = _bdot(jnp.exp(s - m).astype(_BF16), v1[hh])
            acc = o if acc is None else acc + o
        res.append(acc)
    num = jnp.where(lo, res[0], res[1])
    den = pltpu.roll(jnp.where(lo, res[1], res[0]), _HEAD_DIM, 1)
    return num / den


def _attn_kernel(start_ref, var_ref, q_ref, k_ref, v_ref, kc_ref, vc_ref, tab_ref, *rest,
                 n_blocks, union_tokens, with_ctx):
    if with_ctx:
        qc_ref, o_ref, oc_ref = rest
    else:
        (o_ref,) = rest
    qb_tokens = _Q_ROWS * _GRID_W
    scale = _HEAD_DIM ** -0.5
    kc = kc_ref[...]
    vc1 = _values_with_ones(vc_ref[...])

    def body(t, carry):
        q0 = pl.multiple_of(t * qb_tokens, qb_tokens)
        k0 = pl.multiple_of(start_ref[t] * _GRID_W, _GRID_W)
        var = var_ref[t]
        qb = q_ref[pl.ds(q0, qb_tokens), :] * scale
        kb = k_ref[pl.ds(k0, union_tokens), :]
        vb1 = _values_with_ones(v_ref[pl.ds(k0, union_tokens), :])
        o = _attend(qb, [(kb, vb1, lambda hh: tab_ref[var, hh]), (kc, vc1, None)])
        o_ref[pl.ds(q0, qb_tokens), :] = o.astype(o_ref.dtype)
        return carry

    lax.fori_loop(0, n_blocks, body, 0, unroll=_ATTN_UNROLL)

    if with_ctx:
        oc_ref[...] = _attend(qc_ref[...] * scale, [(kc, vc1, None)]).astype(oc_ref.dtype)


def _attention(p, pc, kc_col, table, starts, variant_of, union, batch, seq, ctx_len, with_ctx):
    attn_w = _N_HEADS * _HEAD_DIM
    n_pairs = attn_w // _LANES
    n_var = table.shape[0]
    qb_tokens = _Q_ROWS * _GRID_W
    union_tokens = union * _GRID_W
    kern = functools.partial(_attn_kernel, n_blocks=len(starts), union_tokens=union_tokens,
                             with_ctx=with_ctx)
    in_specs = [
        pl.BlockSpec((seq, _LANES), lambda hp, b, *_: (b, hp)),
        pl.BlockSpec((seq, _LANES), lambda hp, b, *_: (b, n_pairs + hp)),
        pl.BlockSpec((seq, _LANES), lambda hp, b, *_: (b, 2 * n_pairs + hp)),
        pl.BlockSpec((ctx_len, _LANES), lambda hp, b, *_: (b, kc_col + hp)),
        pl.BlockSpec((ctx_len, _LANES), lambda hp, b, *_: (b, kc_col + n_pairs + hp)),
        pl.BlockSpec((n_var, _HEADS_PER_BLOCK, qb_tokens, union_tokens),
                     lambda hp, b, *_: (0, hp, 0, 0)),
    ]
    args = [p, p, p, pc, pc, table]
    out_specs = [pl.BlockSpec((seq, _LANES), lambda hp, b, *_: (b, hp))]
    out_shape = [jax.ShapeDtypeStruct((batch * seq, attn_w), _BF16)]
    if with_ctx:
        in_specs.append(pl.BlockSpec((ctx_len, _LANES), lambda hp, b, *_: (b, hp)))
        args.append(pc)
        out_specs.append(pl.BlockSpec((ctx_len, _LANES), lambda hp, b, *_: (b, hp)))
        out_shape.append(jax.ShapeDtypeStruct((batch * ctx_len, attn_w), _BF16))
    outs = pl.pallas_call(
        kern,
        grid_spec=pltpu.PrefetchScalarGridSpec(
            num_scalar_prefetch=2,
            grid=(n_pairs, batch),
            in_specs=in_specs,
            out_specs=out_specs,
        ),
        out_shape=out_shape,
        compiler_params=_params(2),
        name="attn",
    )(jnp.asarray(starts, jnp.int32), jnp.asarray(variant_of, jnp.int32), *args)
    return outs if with_ctx else (outs[0], None)


def _mixer_update(i, x_ref, ya_ref, bcu_ref, prev_ref, next_ref, nm_ref, sh1_ref, sc1_ref, g1_ref,
                  wconv_ref, wao_ref, wco_ref, wgate_ref, bgate_ref, wout_ref, seq_len):
    tm, d = x_ref.shape
    cw = wco_ref.shape[0]
    x = x_ref[...]
    hb = _modulate(x, nm_ref[...], sh1_ref[...], sc1_ref[...]).astype(_BF16)
    gates = _sigmoid(_bdot(hb, wgate_ref[...]) + bgate_ref[...])

    bcu = bcu_ref[...].astype(_F32)
    bg, z = bcu[:, :cw], bcu[:, cw:2 * cw] * bcu[:, 2 * cw:]
    prev = prev_ref[...].astype(_F32)
    nxt = next_ref[...].astype(_F32)
    has_prev = ((i * tm) % seq_len != 0).astype(_F32)
    has_next = (((i + 1) * tm) % seq_len != 0).astype(_F32)
    last = prev.shape[0] - 1
    z_prev = prev[last:last + 1, cw:2 * cw] * prev[last:last + 1, 2 * cw:] * has_prev
    z_next = nxt[0:1, cw:2 * cw] * nxt[0:1, 2 * cw:] * has_next
    row = lax.broadcasted_iota(jnp.int32, z.shape, 0)
    z_dn = jnp.where(row == 0, z_prev, pltpu.roll(z, 1, 0))
    z_up = jnp.where(row == tm - 1, z_next, pltpu.roll(z, tm - 1, 0))
    wconv = wconv_ref[...]
    y_conv = bg * (z_dn * wconv[0:1] + z * wconv[1:2] + z_up * wconv[2:3])

    a_proj = _bdot(ya_ref[...], wao_ref[...])
    c_proj = _bdot(y_conv.astype(_BF16), wco_ref[...])
    merged = gates[:, :d] * a_proj + gates[:, d:] * c_proj
    return x + g1_ref[...] * _bdot(merged.astype(_BF16), wout_ref[...])


_N_MIX_REFS = 15


def _merge_ffn_kernel(*refs, seq_len):
    mix = refs[:_N_MIX_REFS]
    nf_ref, sh2_ref, sc2_ref, g2_ref, wg_ref, wu_ref, wd_ref, o_ref = refs[_N_MIX_REFS:]
    x1 = _mixer_update(pl.program_id(0), *mix, seq_len=seq_len)
    h2 = _modulate(x1, nf_ref[...], sh2_ref[...], sc2_ref[...]).astype(_BF16)
    gate = _bdot(h2, wg_ref[...])
    act = (gate * _sigmoid(gate)) * _bdot(h2, wu_ref[...])
    o_ref[...] = x1 + g2_ref[...] * _bdot(act.astype(_BF16), wd_ref[...])


def _top2(vals, idx, n_idx, axis):
    v1 = jnp.max(vals, axis=axis, keepdims=True)
    i1 = jnp.min(jnp.where(vals == v1, idx, n_idx), axis=axis, keepdims=True)
    rest = jnp.where(idx == i1, -jnp.inf, vals)
    v2 = jnp.max(rest, axis=axis, keepdims=True)
    i2 = jnp.min(jnp.where(rest == v2, idx, n_idx), axis=axis, keepdims=True)
    return i1, i2, v1, v2


def _pick(idx, sel, vals, axis):
    return jnp.sum(jnp.where(idx == sel, vals, 0.0), axis=axis, keepdims=True)


def _merge_router_kernel(*refs, seq_len, n_experts):
    mix = refs[:_N_MIX_REFS]
    (nf_ref, sh2_ref, sc2_ref, wr_hi_ref, wr_lo_ref,
     x1_ref, h2_ref, ms_ref, ml_ref, cnt_ref) = refs[_N_MIX_REFS:]
    x1 = _mixer_update(pl.program_id(0), *mix, seq_len=seq_len)
    x1_ref[...] = x1
    h2 = _modulate(x1, nf_ref[...], sh2_ref[...], sc2_ref[...])
    h_hi = h2.astype(_BF16)
    h2_ref[...] = h_hi
    tm = h2.shape[0]
    h_lo = (h2 - h_hi.astype(_F32)).astype(_BF16)
    wr_hi = wr_hi_ref[...]
    logits = _bdot(h_hi, wr_hi) + _bdot(h_lo, wr_hi) + _bdot(h_hi, wr_lo_ref[...])
    row = lax.broadcasted_iota(jnp.int32, (tm, tm), 0)
    col = lax.broadcasted_iota(jnp.int32, (tm, tm), 1)

    lane = lax.broadcasted_iota(jnp.int32, logits.shape, 1)
    i1, i2, v1, v2 = _top2(jnp.where(lane < n_experts, logits, -jnp.inf), lane, _LANES, -1)
    e2 = jnp.exp(v2 - v1)
    den = 1.0 + e2
    sel = _indicator(jnp.logical_or(lane == i1, lane == i2))
    ranks = _bdot(_indicator(col < row), sel)
    fields = (i1.astype(_F32), i2.astype(_F32), _pick(lane, i1, ranks, -1),
              _pick(lane, i2, ranks, -1), 1.0 / den, e2 / den)
    meta = jnp.zeros(logits.shape, _F32)
    for k, f in enumerate(fields):
        meta = jnp.where(lane == k, f, meta)
    ms_ref[...] = meta

    lt = logits.T[:n_experts]
    sub = lax.broadcasted_iota(jnp.int32, lt.shape, 0)
    j1, j2, _, _ = _top2(lt, sub, n_experts, 0)
    sel_t = _indicator(jnp.logical_or(sub == j1, sub == j2))
    ranks_t = _bdot(sel_t, _indicator(row < col))
    fields_t = (j1.astype(_F32), j2.astype(_F32), _pick(sub, j1, ranks_t, 0),
                _pick(sub, j2, ranks_t, 0))
    meta_t = jnp.zeros(lt.shape, _F32)
    for k, f in enumerate(fields_t):
        meta_t = jnp.where(sub == k, f, meta_t)
    ml_ref[...] = meta_t
    cnt = jnp.sum(sel_t.astype(_F32), axis=1, keepdims=True)
    cnt_ref[...] = jnp.broadcast_to(cnt, cnt_ref.shape)


def _mixer_specs(n, d, tm, p_cols, row_fn, weights):
    half = p_cols // 2
    sub = _BF16_SUBLANES
    n_sub = n // sub
    specs = [
        pl.BlockSpec((tm, d), lambda i: (i, 0)),
        pl.BlockSpec((tm, weights["wao"].shape[0]), lambda i: (i, 0)),
        pl.BlockSpec((tm, half), lambda i: (i, 1)),
        pl.BlockSpec((sub, half), lambda i: (jnp.maximum(i * (tm // sub) - 1, 0), 1)),
        pl.BlockSpec((sub, half), lambda i: (jnp.minimum((i + 1) * (tm // sub), n_sub - 1), 1)),
        _resident((1, d)),
        _mod_spec(d, 0, row_fn), _mod_spec(d, 1, row_fn), _mod_spec(d, 2, row_fn),
    ]
    for name in ("wconv", "wao", "wco", "wgate", "bgate", "wout"):
        specs.append(_resident(weights[name].shape))
    return specs


def _merge_ffn(x2d, ya, p, mods, row_fn, seq_len, wts, tm):
    n, d = x2d.shape
    specs = _mixer_specs(n, d, tm, p.shape[1], row_fn, wts)
    specs += [_resident((1, d)), _mod_spec(d, 3, row_fn), _mod_spec(d, 4, row_fn),
              _mod_spec(d, 5, row_fn),
              _resident(wts["ffn_gate"].shape), _resident(wts["ffn_up"].shape),
              _resident(wts["ffn_down"].shape)]
    return pl.pallas_call(
        functools.partial(_merge_ffn_kernel, seq_len=seq_len),
        grid=(n // tm,),
        in_specs=specs,
        out_specs=pl.BlockSpec((tm, d), lambda i: (i, 0)),
        out_shape=jax.ShapeDtypeStruct((n, d), _F32),
        compiler_params=_params(1),
        name="merge_ffn",
    )(x2d, ya, p, p, p, wts["norm_mix"], mods, mods, mods,
      wts["wconv"], wts["wao"], wts["wco"], wts["wgate"], wts["bgate"], wts["wout"],
      wts["norm_ffn"], mods, mods, mods, wts["ffn_gate"], wts["ffn_up"], wts["ffn_down"])


def _merge_router(x2d, ya, p, mods, row_fn, seq_len, wts, n_experts, tm):
    n, d = x2d.shape
    specs = _mixer_specs(n, d, tm, p.shape[1], row_fn, wts)
    specs += [_resident((1, d)), _mod_spec(d, 3, row_fn), _mod_spec(d, 4, row_fn),
              _resident(wts["router_hi"].shape), _resident(wts["router_lo"].shape)]
    tile = lambda w: pl.BlockSpec((tm, w), lambda i: (i, 0))
    n_tiles = n // tm
    return pl.pallas_call(
        functools.partial(_merge_router_kernel, seq_len=seq_len, n_experts=n_experts),
        grid=(n_tiles,),
        in_specs=specs,
        out_specs=[tile(d), tile(d), tile(_LANES),
                   pl.BlockSpec((n_experts, tm), lambda i: (i, 0)),
                   pl.BlockSpec((n_experts, _LANES), lambda i: (i, 0))],
        out_shape=[jax.ShapeDtypeStruct((n, d), _F32),
                   jax.ShapeDtypeStruct((n, d), _BF16),
                   jax.ShapeDtypeStruct((n, _LANES), _F32),
                   jax.ShapeDtypeStruct((n_tiles * n_experts, tm), _F32),
                   jax.ShapeDtypeStruct((n_tiles * n_experts, _LANES), _F32)],
        compiler_params=_params(1),
        name="merge_router",
    )(x2d, ya, p, p, p, wts["norm_mix"], mods, mods, mods,
      wts["wconv"], wts["wao"], wts["wco"], wts["wgate"], wts["bgate"], wts["wout"],
      wts["norm_ffn"], mods, mods, wts["router_hi"], wts["router_lo"])


_SEG_ALIGN = _BF16_SUBLANES
_FFN_ROWS = 256


def _round_up(v, m):
    return (v + m - 1) // m * m


def _local_slots(tm, n_experts):
    return _round_up(_TOP_K * tm + n_experts * (_SEG_ALIGN - 1), _SEG_ALIGN)


def _routing_plan(cnt, n_slots):
    pad = _round_up(cnt, _SEG_ALIGN)
    total = jnp.sum(pad, axis=0)
    region = _round_up(total, _FFN_ROWS)
    ends = jnp.cumsum(region)
    seg_start = (ends - region)[None, :] + jnp.cumsum(pad, axis=0) - pad
    loc_off = jnp.cumsum(pad, axis=1) - pad
    n_used = (ends[-1] // _FFN_ROWS).astype(jnp.int32).reshape(1)
    tails = jnp.concatenate([ends - region + total, region - total, n_used])
    tile_row = jnp.arange(n_slots // _FFN_ROWS, dtype=jnp.int32) * _FFN_ROWS
    tile_expert = jnp.minimum(jnp.sum(tile_row[:, None] >= ends[None, :], axis=1),
                              cnt.shape[1] - 1).astype(jnp.int32)
    flat = lambda a: a.reshape(-1).astype(jnp.int32)
    return (flat(seg_start), flat(pad), flat(loc_off)), flat(tails), tile_expert, n_used


def _segment_copies(i, n_experts, seg_ref, pad_ref, loc_ref, hbm_ref, buf_ref, sem, to_hbm):
    copies = []
    for e in range(n_experts):
        n = pl.multiple_of(pad_ref[i * n_experts + e], _SEG_ALIGN)
        loc = pl.multiple_of(loc_ref[i * n_experts + e], _SEG_ALIGN)
        seg = pl.multiple_of(seg_ref[i * n_experts + e], _SEG_ALIGN)
        local, remote = buf_ref.at[pl.ds(loc, n)], hbm_ref.at[pl.ds(seg, n)]
        src, dst = (local, remote) if to_hbm else (remote, local)
        copies.append((n, pltpu.make_async_copy(src, dst, sem)))
    return copies


def _start_copies(copies):
    for n, cp in copies:
        pl.when(n > 0)(cp.start)


def _wait_copies(copies):
    for n, cp in copies:
        pl.when(n > 0)(cp.wait)


def _run_copies(copies):
    _start_copies(copies)
    _wait_copies(copies)


def _expert_offset(expert, loc_ref, i, n_experts):
    off = jnp.zeros(expert.shape, jnp.int32)
    for e in range(n_experts):
        off = jnp.where(expert == e, loc_ref[i * n_experts + e], off)
    return off


def _dispatch_kernel(seg_ref, pad_ref, loc_ref, tail_ref, h_ref, ml_ref, xs_ref, bufs, zeros, sems,
                     *, n_experts):
    i = pl.program_id(0)
    last = pl.num_programs(0) - 1
    cur = lax.rem(i, 2)

    def copies(tile, b):
        return _segment_copies(tile, n_experts, seg_ref, pad_ref, loc_ref, xs_ref, bufs.at[b],
                               sems.at[b], True)

    @pl.when(i >= 2)
    def _():
        _wait_copies(copies(i - 2, cur))

    tm = h_ref.shape[0]
    meta = ml_ref[...]
    slot1 = _expert_offset(meta[0:1].astype(jnp.int32), loc_ref, i, n_experts) + meta[2:3].astype(jnp.int32)
    slot2 = _expert_offset(meta[1:2].astype(jnp.int32), loc_ref, i, n_experts) + meta[3:4].astype(jnp.int32)
    slot = lax.broadcasted_iota(jnp.int32, (bufs.shape[1], tm), 0)
    perm = _indicator(jnp.logical_or(slot == slot1, slot == slot2))
    bufs[cur] = _bdot(perm, h_ref[...]).astype(bufs.dtype)
    _start_copies(copies(i, cur))

    @pl.when(i == last)
    def _():
        @pl.when(i >= 1)
        def _():
            _wait_copies(copies(i - 1, 1 - cur))

        _wait_copies(copies(i, cur))

        zeros[...] = jnp.zeros_like(zeros)
        fills = []
        for e in range(n_experts):
            start = pl.multiple_of(tail_ref[e], _SEG_ALIGN)
            n = pl.multiple_of(tail_ref[n_experts + e], _SEG_ALIGN)
            fills.append((n, pltpu.make_async_copy(zeros.at[pl.ds(0, n)],
                                                   xs_ref.at[pl.ds(start, n)], sems.at[0])))
        _run_copies(fills)

        def fill_tile(j, carry):
            cp = pltpu.make_async_copy(
                zeros, xs_ref.at[pl.ds(pl.multiple_of(j * _FFN_ROWS, _FFN_ROWS), _FFN_ROWS)],
                sems.at[0])
            cp.start()
            cp.wait()
            return carry

        lax.fori_loop(tail_ref[2 * n_experts], xs_ref.shape[0] // _FFN_ROWS, fill_tile, 0)


def _dispatch(h2, meta_l, plan, tails, n_slots, n_experts, tm):
    n, d = h2.shape
    seg_start, pad, loc_off = plan
    return pl.pallas_call(
        functools.partial(_dispatch_kernel, n_experts=n_experts),
        grid_spec=pltpu.PrefetchScalarGridSpec(
            num_scalar_prefetch=4,
            grid=(n // tm,),
            in_specs=[pl.BlockSpec((tm, d), lambda i, *_: (i, 0)),
                      pl.BlockSpec((n_experts, tm), lambda i, *_: (i, 0))],
            out_specs=pl.BlockSpec(memory_space=pl.ANY),
            scratch_shapes=[pltpu.VMEM((2, _local_slots(tm, n_experts), d), _BF16),
                            pltpu.VMEM((_FFN_ROWS, d), _BF16),
                            pltpu.SemaphoreType.DMA((2,))],
        ),
        out_shape=jax.ShapeDtypeStruct((n_slots, d), _BF16),
        compiler_params=_params(1),
        name="dispatch",
    )(seg_start, pad, loc_off, tails, h2, meta_l)


def _expert_ffn_kernel(te_ref, nu_ref, x_ref, wg_ref, wu_ref, wd_ref, o_ref):
    del te_ref
    j = pl.program_id(0)

    @pl.when(j < nu_ref[0])
    def _():
        x = x_ref[...]
        gate = _bdot(x, wg_ref[...])
        act = (gate * _sigmoid(gate)) * _bdot(x, wu_ref[...])
        o_ref[...] = _bdot(act.astype(_BF16), wd_ref[...]).astype(o_ref.dtype)

    @pl.when(j >= nu_ref[0])
    def _():
        o_ref[...] = jnp.zeros_like(o_ref)


def _expert_ffn(xs, tile_expert, n_used, wg, wu, wd):
    n_slots, d = xs.shape
    f = wg.shape[2]
    return pl.pallas_call(
        _expert_ffn_kernel,
        grid_spec=pltpu.PrefetchScalarGridSpec(
            num_scalar_prefetch=2,
            grid=(n_slots // _FFN_ROWS,),
            in_specs=[
                pl.BlockSpec((_FFN_ROWS, d),
                             lambda j, te, nu: (jnp.maximum(jnp.minimum(j, nu[0] - 1), 0), 0)),
                pl.BlockSpec((None, d, f), lambda j, te, nu: (te[j], 0, 0)),
                pl.BlockSpec((None, d, f), lambda j, te, nu: (te[j], 0, 0)),
                pl.BlockSpec((None, f, d), lambda j, te, nu: (te[j], 0, 0)),
            ],
            out_specs=pl.BlockSpec((_FFN_ROWS, d), lambda j, te, nu: (j, 0)),
        ),
        out_shape=jax.ShapeDtypeStruct((n_slots, d), _BF16),
        compiler_params=_params(1),
        name="expert_ffn",
    )(tile_expert, n_used, xs, wg, wu, wd)


def _combine_kernel(seg_ref, pad_ref, loc_ref, ys_ref, ms_ref, x1_ref, g2_ref, nf_ref, o_ref,
                    bufs, sems, *, n_experts):
    i = pl.program_id(0)
    tm = x1_ref.shape[0]
    cur = lax.rem(i, 2)

    def copies(tile, b):
        return _segment_copies(tile, n_experts, seg_ref, pad_ref, loc_ref, ys_ref, bufs.at[b],
                               sems.at[b], False)

    @pl.when(i == 0)
    def _():
        bufs[...] = jnp.zeros_like(bufs)
        _start_copies(copies(0, 0))

    @pl.when(i + 1 < pl.num_programs(0))
    def _():
        _start_copies(copies(i + 1, 1 - cur))

    _wait_copies(copies(i, cur))
    meta = ms_ref[...]
    col = lambda k: meta[:, k:k + 1]
    slot1 = _expert_offset(col(0).astype(jnp.int32), loc_ref, i, n_experts) + col(2).astype(jnp.int32)
    slot2 = _expert_offset(col(1).astype(jnp.int32), loc_ref, i, n_experts) + col(3).astype(jnp.int32)
    slot = lax.broadcasted_iota(jnp.int32, (tm, bufs.shape[1]), 1)
    weights = jnp.where(slot == slot1, col(4), jnp.where(slot == slot2, col(5), 0.0))
    y = _bdot(weights.astype(_BF16), bufs[cur])
    o_ref[...] = _rmsnorm(x1_ref[...] + g2_ref[...] * y, nf_ref[...])


def _combine(ys, meta_s, x1, plan, mods, row_fn, norm_final, n_experts, tm):
    n, d = x1.shape
    seg_start, pad, loc_off = plan
    return pl.pallas_call(
        functools.partial(_combine_kernel, n_experts=n_experts),
        grid_spec=pltpu.PrefetchScalarGridSpec(
            num_scalar_prefetch=3,
            grid=(n // tm,),
            in_specs=[pl.BlockSpec(memory_space=pl.ANY),
                      pl.BlockSpec((tm, _LANES), lambda i, *_: (i, 0)),
                      pl.BlockSpec((tm, d), lambda i, *_: (i, 0)),
                      _mod_spec(d, 5, row_fn),
                      _resident((1, d))],
            out_specs=pl.BlockSpec((tm, d), lambda i, *_: (i, 0)),
            scratch_shapes=[pltpu.VMEM((2, _local_slots(tm, n_experts), d), _BF16),
                            pltpu.SemaphoreType.DMA((2,))],
        ),
        out_shape=jax.ShapeDtypeStruct((n, d), _F32),
        compiler_params=_params(1),
        name="combine",
    )(seg_start, pad, loc_off, ys, meta_s, x1, mods, norm_final)


def kernel(x, c, ctx, c_ctx, w_ada, b_ada, norm_mix, norm_ffn, w_in, w_conv, rpb, w_attn_out,
           w_conv_out, w_gate, b_gate, w_out, w_ffn_gate, w_ffn_up, w_ffn_down, w_router,
           w_exp_gate, w_exp_up, w_exp_down, norm_final):
    batch, seq, d = x.shape
    ctx_len = ctx.shape[1]
    depth = w_ada.shape[0]
    attn_w = _N_HEADS * _HEAD_DIM
    n_experts = w_router.shape[-1]
    rows = seq // _GRID_W
    n_pairs = attn_w // _LANES
    assert depth == 2 and seq % (_Q_ROWS * _GRID_W) == 0 and attn_w % _LANES == 0
    assert n_experts == _F32_SUBLANES

    ctx_row = batch
    cond_rows = -(-(batch + 1) // 8) * 8
    cond = jnp.zeros((cond_rows, d), _F32).at[:batch].set(c).at[ctx_row].set(c_ctx)
    mods_all = _ada(cond, w_ada, b_ada, tn=1536).reshape(depth, cond_rows, 1, 6 * d)

    union, starts, variant_of, patterns = _attn_plan(rows)

    x2d = x.reshape(batch * seq, d)
    c2d = ctx.reshape(batch * ctx_len, d)
    tm_lat = 512
    ctx_row_fn = lambda i: ctx_row

    def lat_rows(tm):
        return lambda i: (i * tm) // seq

    out = None
    for l in range(depth):
        last = l == depth - 1
        mods = mods_all[l]
        wts = {
            "norm_mix": norm_mix[l].reshape(1, d),
            "norm_ffn": norm_ffn[l].reshape(1, d),
            "wconv": w_conv[l],
            "wao": w_attn_out[l].astype(_BF16),
            "wco": w_conv_out[l].astype(_BF16),
            "wgate": w_gate[l].astype(_BF16),
            "bgate": b_gate[l].reshape(1, -1),
            "wout": w_out[l].astype(_BF16),
        }
        w_in_l = w_in[l].astype(_BF16)
        p = _proj(x2d, mods, lat_rows(512), wts["norm_mix"], w_in_l, tm=512)
        if last:
            pc = _proj(c2d, mods, ctx_row_fn, wts["norm_mix"], w_in_l[:, attn_w:3 * attn_w], tm=512)
            kc_col = 0
        else:
            pc = _proj(c2d, mods, ctx_row_fn, wts["norm_mix"], w_in_l, tm=512)
            kc_col = n_pairs
        table = _bias_table(_rpb_tiles(rpb[l]), union, patterns)
        ya, yca = _attention(p, pc, kc_col, table, starts, variant_of, union, batch, seq,
                             ctx_len, with_ctx=not last)
        if not last:
            wts["ffn_gate"] = w_ffn_gate[l // 2].astype(_BF16)
            wts["ffn_up"] = w_ffn_up[l // 2].astype(_BF16)
            wts["ffn_down"] = w_ffn_down[l // 2].astype(_BF16)
            x2d = _merge_ffn(x2d, ya, p, mods, lat_rows(tm_lat), seq, wts, tm=tm_lat)
            c2d = _merge_ffn(c2d, yca, pc, mods, ctx_row_fn, ctx_len, wts, tm=ctx_len)
        else:
            wr = jnp.pad(w_router[l // 2], ((0, 0), (0, _LANES - n_experts)))
            wts["router_hi"] = wr.astype(_BF16)
            wts["router_lo"] = (wr - wts["router_hi"].astype(_F32)).astype(_BF16)
            tm_moe = 512
            n_tok = batch * seq
            n_tiles = n_tok // tm_moe
            x1, h2, meta_s, meta_l, cnt = _merge_router(x2d, ya, p, mods, lat_rows(tm_moe), seq,
                                                        wts, n_experts, tm=tm_moe)
            n_slots = _round_up(_TOP_K * n_tok + n_tiles * n_experts * (_SEG_ALIGN - 1)
                                + n_experts * (_FFN_ROWS - 1), _FFN_ROWS)
            cnt = cnt[:, 0].reshape(n_tiles, n_experts).astype(jnp.int32)
            plan, tails, tile_expert, n_used = _routing_plan(cnt, n_slots)
            xs = _dispatch(h2, meta_l, plan, tails, n_slots, n_experts, tm=tm_moe)
            ys = _expert_ffn(xs, tile_expert, n_used, w_exp_gate[l // 2].astype(_BF16),
                             w_exp_up[l // 2].astype(_BF16), w_exp_down[l // 2].astype(_BF16))
            out = _combine(ys, meta_s, x1, plan, mods, lat_rows(tm_moe), norm_final.reshape(1, d),
                           n_experts, tm=tm_moe)
    return out.reshape(batch, seq, d)
```

```python
import functools

import numpy as np
import jax
import jax.numpy as jnp
from jax import lax
from jax.experimental import pallas as pl
from jax.experimental.pallas import tpu as pltpu

_F32 = jnp.float32
_BF16 = jnp.bfloat16

_GRID_W = 64
_N_HEADS = 8
_HEAD_DIM = 64
_NA_ROWS = 8
_NA_COLS = 16
_CONV_K = 3
_TOP_K = 2
_EPS = 1e-6
_NEG_INF = -1e30

_LANES = 128
_F32_SUBLANES = 8
_BF16_SUBLANES = 16
_VMEM_LIMIT_BYTES = 56 * 1024 * 1024

_Q_ROWS = 4
_ATTN_UNROLL = 2
_HEADS_PER_BLOCK = _LANES // _HEAD_DIM
_NT = (((1,), (1,)), ((), ()))
_TN = (((0,), (0,)), ((), ()))


def _params(n_axes):
    return pltpu.CompilerParams(
        dimension_semantics=("arbitrary",) * n_axes,
        vmem_limit_bytes=_VMEM_LIMIT_BYTES)


def _resident(shape):
    nd = len(shape)
    return pl.BlockSpec(shape, lambda *_: (0,) * nd, pipeline_mode=pl.Buffered(1))


def _sigmoid(x):
    return 1.0 / (1.0 + jnp.exp(-x))


def _rmsnorm(x, g):
    ms = jnp.mean(x * x, axis=-1, keepdims=True)
    return (x * lax.rsqrt(ms + _EPS)) * g


def _modulate(x, g, shift, scale):
    return _rmsnorm(x, g) * (1.0 + scale) + shift


def _bdot(a, b):
    return jnp.dot(a, b, preferred_element_type=_F32)


def _indicator(mask):
    return jnp.where(mask, 1.0, 0.0).astype(_BF16)


def _ada_kernel(cond_ref, w_ref, b_ref, o_ref):
    cond = cond_ref[...]
    act = cond * _sigmoid(cond)
    o_ref[...] = jnp.dot(act, w_ref[...], precision=lax.Precision.HIGHEST,
                         preferred_element_type=_F32) + b_ref[...]


def _ada(cond, w_ada, b_ada, tn):
    depth, d, n_out = w_ada.shape
    rows = cond.shape[0]
    return pl.pallas_call(
        _ada_kernel,
        grid=(depth, n_out // tn),
        in_specs=[
            pl.BlockSpec((rows, d), lambda l, j: (0, 0)),
            pl.BlockSpec((None, d, tn), lambda l, j: (l, 0, j)),
            pl.BlockSpec((None, 1, tn), lambda l, j: (l, 0, j)),
        ],
        out_specs=pl.BlockSpec((None, rows, tn), lambda l, j: (l, 0, j)),
        out_shape=jax.ShapeDtypeStruct((depth, rows, n_out), _F32),
        compiler_params=_params(2),
        name="ada",
    )(cond, w_ada, b_ada.reshape(depth, 1, n_out))


def _mod_spec(d, chunk, row_fn):
    return pl.BlockSpec((None, 1, d), lambda i, *_: (row_fn(i), 0, chunk))


def _proj_kernel(x_ref, g_ref, sh_ref, sc_ref, w_ref, o_ref):
    h = _modulate(x_ref[...], g_ref[...], sh_ref[...], sc_ref[...])
    o_ref[...] = _bdot(h.astype(_BF16), w_ref[...]).astype(o_ref.dtype)


def _proj(x2d, mods, row_fn, g, w, tm):
    n, d = x2d.shape
    n_out = w.shape[1]
    return pl.pallas_call(
        _proj_kernel,
        grid=(n // tm,),
        in_specs=[
            pl.BlockSpec((tm, d), lambda i: (i, 0)),
            _resident((1, d)),
            _mod_spec(d, 0, row_fn),
            _mod_spec(d, 1, row_fn),
            _resident(w.shape),
        ],
        out_specs=pl.BlockSpec((tm, n_out), lambda i: (i, 0)),
        out_shape=jax.ShapeDtypeStruct((n, n_out), _BF16),
        compiler_params=_params(1),
        name="proj",
    )(x2d, g, mods, mods, w)


def _bias_table_kernel(rpb_ref, o_ref, *, union, patterns):
    sub = _F32_SUBLANES
    span = _NA_COLS - 1
    rp = rpb_ref[...]
    lane = lax.broadcasted_iota(jnp.int32, (sub, _LANES), 1)
    srow = lax.broadcasted_iota(jnp.int32, (sub, _LANES), 0)
    ck = jnp.bitwise_and(lane, _GRID_W - 1)
    first = lane < _GRID_W
    in_win = []
    for g in range(_GRID_W // sub):
        cs = jnp.clip(srow + (sub * g - _NA_COLS // 2), 0, _GRID_W - _NA_COLS)
        in_win.append(jnp.logical_and(ck >= cs, ck < cs + _NA_COLS))
    first_part = jnp.logical_or(lane <= span, lane >= _LANES - span)
    second_part = jnp.logical_and(lane >= _GRID_W - span, lane <= _GRID_W + span)
    neg = jnp.full((sub, _LANES), _NEG_INF, _F32)

    def row_vec(dr):
        return jnp.broadcast_to(rp[dr + _NA_ROWS - 1:dr + _NA_ROWS, :], (sub, _LANES))

    for v, pat in enumerate(patterns):
        for a in range(_Q_ROWS):
            for jp in range(-(-union // 2)):
                j0, j1 = 2 * jp, 2 * jp + 1
                dr0 = pat[a * union + j0]
                dr1 = pat[a * union + j1] if j1 < union else None
                width = _LANES if j1 < union else _GRID_W
                base = jnp.zeros((sub, _LANES), _F32)
                if dr0 is not None:
                    base = jnp.where(first_part, pltpu.roll(row_vec(dr0), _LANES - span, 1), base)
                if dr1 is not None:
                    base = jnp.where(second_part, pltpu.roll(row_vec(dr1), _GRID_W - span, 1), base)
                for g in range(_GRID_W // sub):
                    rows = slice(a * _GRID_W + sub * g, a * _GRID_W + sub * (g + 1))
                    if dr0 is None and dr1 is None:
                        tile = neg
                    else:
                        tile = pltpu.roll(base, sub * g, 1, stride=1, stride_axis=0)
                        keep = in_win[g]
                        if dr0 is None:
                            keep = jnp.logical_and(keep, jnp.logical_not(first))
                        if dr1 is None:
                            keep = jnp.logical_and(keep, first)
                        tile = jnp.where(keep, tile, neg)
                    o_ref[v, rows, jp * _LANES:jp * _LANES + width] = tile[:, :width]


def _bias_table(rpb_l, union, patterns):
    h, n_dr, n_dc = rpb_l.shape
    assert _LANES == 2 * _GRID_W and n_dc <= _LANES
    n_dr_pad = _round_up(n_dr, _F32_SUBLANES)
    padded = jnp.pad(rpb_l, ((0, 0), (0, n_dr_pad - n_dr), (0, _LANES - n_dc)))
    n_var = len(patterns)
    qt, kt = _Q_ROWS * _GRID_W, union * _GRID_W
    return pl.pallas_call(
        functools.partial(_bias_table_kernel, union=union, patterns=patterns),
        grid=(h,),
        in_specs=[pl.BlockSpec((None, n_dr_pad, _LANES), lambda i: (i, 0, 0))],
        out_specs=pl.BlockSpec((n_var, None, qt, kt), lambda i: (0, i, 0, 0)),
        out_shape=jax.ShapeDtypeStruct((n_var, h, qt, kt), _F32),
        compiler_params=_params(1),
        name="bias_table",
    )(padded)


def _attn_plan(rows):
    kh = min(_NA_ROWS, rows)
    union = min(_Q_ROWS + kh - 1, rows)
    n_blocks = rows // _Q_ROWS
    starts, variant_of, patterns = [], [], []
    for t in range(n_blocks):
        start = int(np.clip(_Q_ROWS * t - kh // 2, 0, rows - union))
        pat = []
        for a in range(_Q_ROWS):
            r = _Q_ROWS * t + a
            rs = int(np.clip(r - kh // 2, 0, rows - kh))
            for j in range(union):
                kr = start + j
                pat.append(kr - r if rs <= kr < rs + kh else None)
        pat = tuple(pat)
        if pat not in patterns:
            patterns.append(pat)
        starts.append(start)
        variant_of.append(patterns.index(pat))
    return union, starts, variant_of, patterns


def _first_head_lanes(shape):
    return lax.broadcasted_iota(jnp.int32, shape, len(shape) - 1) < _HEAD_DIM


def _values_with_ones(v):
    lo = _first_head_lanes(v.shape)
    ones = jnp.ones_like(v)
    return jnp.where(lo, v, ones), jnp.where(lo, ones, v)


def _attend(qb, parts):
    lo = _first_head_lanes(qb.shape)
    res = []
    for hh in range(_HEADS_PER_BLOCK):
        qm = jnp.where(lo if hh == 0 else jnp.logical_not(lo), qb, jnp.zeros_like(qb))
        scores = []
        for kk, _, bias_fn in parts:
            s = lax.dot_general(qm, kk, _NT, preferred_element_type=_F32)
            if bias_fn is not None:
                s = s + bias_fn(hh)
            scores.append(s)
        m = functools.reduce(jnp.maximum, [jnp.max(s, axis=-1, keepdims=True) for s in scores])
        acc = None
        for s, (_, v1, _) in zip(scores, parts):
            o = _bdot(jnp.exp(s - m).astype(_BF16), v1[hh])
            acc = o if acc is None else acc + o
        res.append(acc)
    num = jnp.where(lo, res[0], res[1])
    den = pltpu.roll(jnp.where(lo, res[1], res[0]), _HEAD_DIM, 1)
    return num / den


def _attn_kernel(start_ref, var_ref, q_ref, k_ref, v_ref, kc_ref, vc_ref, tab_ref, *rest,
                 n_blocks, union_tokens, with_ctx):
    if with_ctx:
        qc_ref, o_ref, oc_ref = rest
    else:
        (o_ref,) = rest
    qb_tokens = _Q_ROWS * _GRID_W
    scale = _HEAD_DIM ** -0.5
    kc = kc_ref[...]
    vc1 = _values_with_ones(vc_ref[...])

    def body(t, carry):
        q0 = pl.multiple_of(t * qb_tokens, qb_tokens)
        k0 = pl.multiple_of(start_ref[t] * _GRID_W, _GRID_W)
        var = var_ref[t]
        qb = q_ref[pl.ds(q0, qb_tokens), :] * scale
        kb = k_ref[pl.ds(k0, union_tokens), :]
        vb1 = _values_with_ones(v_ref[pl.ds(k0, union_tokens), :])
        o = _attend(qb, [(kb, vb1, lambda hh: tab_ref[var, hh]), (kc, vc1, None)])
        o_ref[pl.ds(q0, qb_tokens), :] = o.astype(o_ref.dtype)
        return carry

    lax.fori_loop(0, n_blocks, body, 0, unroll=_ATTN_UNROLL)

    if with_ctx:
        oc_ref[...] = _attend(qc_ref[...] * scale, [(kc, vc1, None)]).astype(oc_ref.dtype)


def _attention(p, pc, kc_col, table, starts, variant_of, union, batch, seq, ctx_len, with_ctx):
    attn_w = _N_HEADS * _HEAD_DIM
    n_pairs = attn_w // _LANES
    n_var = table.shape[0]
    qb_tokens = _Q_ROWS * _GRID_W
    union_tokens = union * _GRID_W
    kern = functools.partial(_attn_kernel, n_blocks=len(starts), union_tokens=union_tokens,
                             with_ctx=with_ctx)
    in_specs = [
        pl.BlockSpec((seq, _LANES), lambda hp, b, *_: (b, hp)),
        pl.BlockSpec((seq, _LANES), lambda hp, b, *_: (b, n_pairs + hp)),
        pl.BlockSpec((seq, _LANES), lambda hp, b, *_: (b, 2 * n_pairs + hp)),
        pl.BlockSpec((ctx_len, _LANES), lambda hp, b, *_: (b, kc_col + hp)),
        pl.BlockSpec((ctx_len, _LANES), lambda hp, b, *_: (b, kc_col + n_pairs + hp)),
        pl.BlockSpec((n_var, _HEADS_PER_BLOCK, qb_tokens, union_tokens),
                     lambda hp, b, *_: (0, hp, 0, 0)),
    ]
    args = [p, p, p, pc, pc, table]
    out_specs = [pl.BlockSpec((seq, _LANES), lambda hp, b, *_: (b, hp))]
    out_shape = [jax.ShapeDtypeStruct((batch * seq, attn_w), _BF16)]
    if with_ctx:
        in_specs.append(pl.BlockSpec((ctx_len, _LANES), lambda hp, b, *_: (b, hp)))
        args.append(pc)
        out_specs.append(pl.BlockSpec((ctx_len, _LANES), lambda hp, b, *_: (b, hp)))
        out_shape.append(jax.ShapeDtypeStruct((batch * ctx_len, attn_w), _BF16))
    outs = pl.pallas_call(
        kern,
        grid_spec=pltpu.PrefetchScalarGridSpec(
            num_scalar_prefetch=2,
            grid=(n_pairs, batch),
            in_specs=in_specs,
            out_specs=out_specs,
        ),
        out_shape=out_shape,
        compiler_params=_params(2),
        name="attn",
    )(jnp.asarray(starts, jnp.int32), jnp.asarray(variant_of, jnp.int32), *args)
    return outs if with_ctx else (outs[0], None)


def _mixer_update(i, x_ref, ya_ref, bcu_ref, prev_ref, next_ref, nm_ref, sh1_ref, sc1_ref, g1_ref,
                  wconv_ref, wao_ref, wco_ref, wgate_ref, bgate_ref, wout_ref, seq_len):
    tm, d = x_ref.shape
    cw = wco_ref.shape[0]
    x = x_ref[...]
    hb = _modulate(x, nm_ref[...], sh1_ref[...], sc1_ref[...]).astype(_BF16)
    gates = _sigmoid(_bdot(hb, wgate_ref[...]) + bgate_ref[...])

    bcu = bcu_ref[...].astype(_F32)
    bg, z = bcu[:, :cw], bcu[:, cw:2 * cw] * bcu[:, 2 * cw:]
    prev = prev_ref[...].astype(_F32)
    nxt = next_ref[...].astype(_F32)
    has_prev = ((i * tm) % seq_len != 0).astype(_F32)
    has_next = (((i + 1) * tm) % seq_len != 0).astype(_F32)
    last = prev.shape[0] - 1
    z_prev = prev[last:last + 1, cw:2 * cw] * prev[last:last + 1, 2 * cw:] * has_prev
    z_next = nxt[0:1, cw:2 * cw] * nxt[0:1, 2 * cw:] * has_next
    row = lax.broadcasted_iota(jnp.int32, z.shape, 0)
    z_dn = jnp.where(row == 0, z_prev, pltpu.roll(z, 1, 0))
    z_up = jnp.where(row == tm - 1, z_next, pltpu.roll(z, tm - 1, 0))
    wconv = wconv_ref[...]
    y_conv = bg * (z_dn * wconv[0:1] + z * wconv[1:2] + z_up * wconv[2:3])

    a_proj = _bdot(ya_ref[...], wao_ref[...])
    c_proj = _bdot(y_conv.astype(_BF16), wco_ref[...])
    merged = gates[:, :d] * a_proj + gates[:, d:] * c_proj
    return x + g1_ref[...] * _bdot(merged.astype(_BF16), wout_ref[...])


_N_MIX_REFS = 15


def _merge_ffn_kernel(*refs, seq_len):
    mix = refs[:_N_MIX_REFS]
    nf_ref, sh2_ref, sc2_ref, g2_ref, wg_ref, wu_ref, wd_ref, o_ref = refs[_N_MIX_REFS:]
    x1 = _mixer_update(pl.program_id(0), *mix, seq_len=seq_len)
    h2 = _modulate(x1, nf_ref[...], sh2_ref[...], sc2_ref[...]).astype(_BF16)
    gate = _bdot(h2, wg_ref[...])
    act = (gate * _sigmoid(gate)) * _bdot(h2, wu_ref[...])
    o_ref[...] = x1 + g2_ref[...] * _bdot(act.astype(_BF16), wd_ref[...])


def _top2(vals, idx, n_idx, axis):
    v1 = jnp.max(vals, axis=axis, keepdims=True)
    i1 = jnp.min(jnp.where(vals == v1, idx, n_idx), axis=axis, keepdims=True)
    rest = jnp.where(idx == i1, -jnp.inf, vals)
    v2 = jnp.max(rest, axis=axis, keepdims=True)
    i2 = jnp.min(jnp.where(rest == v2, idx, n_idx), axis=axis, keepdims=True)
    return i1, i2, v1, v2


def _pick(idx, sel, vals, axis):
    return jnp.sum(jnp.where(idx == sel, vals, 0.0), axis=axis, keepdims=True)


def _merge_router_kernel(*refs, seq_len, n_experts):
    mix = refs[:_N_MIX_REFS]
    (nf_ref, sh2_ref, sc2_ref, wr_hi_ref, wr_lo_ref,
     x1_ref, h2_ref, ms_ref, ml_ref, cnt_ref) = refs[_N_MIX_REFS:]
    x1 = _mixer_update(pl.program_id(0), *mix, seq_len=seq_len)
    x1_ref[...] = x1
    h2 = _modulate(x1, nf_ref[...], sh2_ref[...], sc2_ref[...])
    h_hi = h2.astype(_BF16)
    h2_ref[...] = h_hi
    tm = h2.shape[0]
    h_lo = (h2 - h_hi.astype(_F32)).astype(_BF16)
    wr_hi = wr_hi_ref[...]
    logits = _bdot(h_hi, wr_hi) + _bdot(h_lo, wr_hi) + _bdot(h_hi, wr_lo_ref[...])
    row = lax.broadcasted_iota(jnp.int32, (tm, tm), 0)
    col = lax.broadcasted_iota(jnp.int32, (tm, tm), 1)

    lane = lax.broadcasted_iota(jnp.int32, logits.shape, 1)
    i1, i2, v1, v2 = _top2(jnp.where(lane < n_experts, logits, -jnp.inf), lane, _LANES, -1)
    e2 = jnp.exp(v2 - v1)
    den = 1.0 + e2
    sel = _indicator(jnp.logical_or(lane == i1, lane == i2))
    ranks = _bdot(_indicator(col < row), sel)
    fields = (i1.astype(_F32), i2.astype(_F32), _pick(lane, i1, ranks, -1),
              _pick(lane, i2, ranks, -1), 1.0 / den, e2 / den)
    meta = jnp.zeros(logits.shape, _F32)
    for k, f in enumerate(fields):
        meta = jnp.where(lane == k, f, meta)
    ms_ref[...] = meta

    lt = logits.T[:n_experts]
    sub = lax.broadcasted_iota(jnp.int32, lt.shape, 0)
    j1, j2, _, _ = _top2(lt, sub, n_experts, 0)
    sel_t = _indicator(jnp.logical_or(sub == j1, sub == j2))
    ranks_t = _bdot(sel_t, _indicator(row < col))
    fields_t = (j1.astype(_F32), j2.astype(_F32), _pick(sub, j1, ranks_t, 0),
                _pick(sub, j2, ranks_t, 0))
    meta_t = jnp.zeros(lt.shape, _F32)
    for k, f in enumerate(fields_t):
        meta_t = jnp.where(sub == k, f, meta_t)
    ml_ref[...] = meta_t
    cnt = jnp.sum(sel_t.astype(_F32), axis=1, keepdims=True)
    cnt_ref[...] = jnp.broadcast_to(cnt, cnt_ref.shape)


def _mixer_specs(n, d, tm, p_cols, row_fn, weights):
    half = p_cols // 2
    sub = _BF16_SUBLANES
    n_sub = n // sub
    specs = [
        pl.BlockSpec((tm, d), lambda i: (i, 0)),
        pl.BlockSpec((tm, weights["wao"].shape[0]), lambda i: (i, 0)),
        pl.BlockSpec((tm, half), lambda i: (i, 1)),
        pl.BlockSpec((sub, half), lambda i: (jnp.maximum(i * (tm // sub) - 1, 0), 1)),
        pl.BlockSpec((sub, half), lambda i: (jnp.minimum((i + 1) * (tm // sub), n_sub - 1), 1)),
        _resident((1, d)),
        _mod_spec(d, 0, row_fn), _mod_spec(d, 1, row_fn), _mod_spec(d, 2, row_fn),
    ]
    for name in ("wconv", "wao", "wco", "wgate", "bgate", "wout"):
        specs.append(_resident(weights[name].shape))
    return specs


def _merge_ffn(x2d, ya, p, mods, row_fn, seq_len, wts, tm):
    n, d = x2d.shape
    specs = _mixer_specs(n, d, tm, p.shape[1], row_fn, wts)
    specs += [_resident((1, d)), _mod_spec(d, 3, row_fn), _mod_spec(d, 4, row_fn),
              _mod_spec(d, 5, row_fn),
              _resident(wts["ffn_gate"].shape), _resident(wts["ffn_up"].shape),
              _resident(wts["ffn_down"].shape)]
    return pl.pallas_call(
        functools.partial(_merge_ffn_kernel, seq_len=seq_len),
        grid=(n // tm,),
        in_specs=specs,
        out_specs=pl.BlockSpec((tm, d), lambda i: (i, 0)),
        out_shape=jax.ShapeDtypeStruct((n, d), _F32),
        compiler_params=_params(1),
        name="merge_ffn",
    )(x2d, ya, p, p, p, wts["norm_mix"], mods, mods, mods,
      wts["wconv"], wts["wao"], wts["wco"], wts["wgate"], wts["bgate"], wts["wout"],
      wts["norm_ffn"], mods, mods, mods, wts["ffn_gate"], wts["ffn_up"], wts["ffn_down"])


def _merge_router(x2d, ya, p, mods, row_fn, seq_len, wts, n_experts, tm):
    n, d = x2d.shape
    specs = _mixer_specs(n, d, tm, p.shape[1], row_fn, wts)
    specs += [_resident((1, d)), _mod_spec(d, 3, row_fn), _mod_spec(d, 4, row_fn),
              _resident(wts["router_hi"].shape), _resident(wts["router_lo"].shape)]
    tile = lambda w: pl.BlockSpec((tm, w), lambda i: (i, 0))
    n_tiles = n // tm
    return pl.pallas_call(
        functools.partial(_merge_router_kernel, seq_len=seq_len, n_experts=n_experts),
        grid=(n_tiles,),
        in_specs=specs,
        out_specs=[tile(d), tile(d), tile(_LANES),
                   pl.BlockSpec((n_experts, tm), lambda i: (i, 0)),
                   pl.BlockSpec((n_experts, _LANES), lambda i: (i, 0))],
        out_shape=[jax.ShapeDtypeStruct((n, d), _F32),
                   jax.ShapeDtypeStruct((n, d), _BF16),
                   jax.ShapeDtypeStruct((n, _LANES), _F32),
                   jax.ShapeDtypeStruct((n_tiles * n_experts, tm), _F32),
                   jax.ShapeDtypeStruct((n_tiles * n_experts, _LANES), _F32)],
        compiler_params=_params(1),
        name="merge_router",
    )(x2d, ya, p, p, p, wts["norm_mix"], mods, mods, mods,
      wts["wconv"], wts["wao"], wts["wco"], wts["wgate"], wts["bgate"], wts["wout"],
      wts["norm_ffn"], mods, mods, wts["router_hi"], wts["router_lo"])


_SEG_ALIGN = _BF16_SUBLANES
_FFN_ROWS = 256


def _round_up(v, m):
    return (v + m - 1) // m * m


def _local_slots(tm, n_experts):
    return _round_up(_TOP_K * tm + n_experts * (_SEG_ALIGN - 1), _SEG_ALIGN)


def _routing_plan(cnt, n_slots):
    pad = _round_up(cnt, _SEG_ALIGN)
    total = jnp.sum(pad, axis=0)
    region = _round_up(total, _FFN_ROWS)
    ends = jnp.cumsum(region)
    seg_start = (ends - region)[None, :] + jnp.cumsum(pad, axis=0) - pad
    loc_off = jnp.cumsum(pad, axis=1) - pad
    n_used = (ends[-1] // _FFN_ROWS).astype(jnp.int32).reshape(1)
    tails = jnp.concatenate([ends - region + total, region - total, n_used])
    tile_row = jnp.arange(n_slots // _FFN_ROWS, dtype=jnp.int32) * _FFN_ROWS
    tile_expert = jnp.minimum(jnp.sum(tile_row[:, None] >= ends[None, :], axis=1),
                              cnt.shape[1] - 1).astype(jnp.int32)
    ids = jnp.arange(cnt.shape[1], dtype=jnp.int32)
    later = jnp.logical_and(region[None, :] > 0, ids[None, :] > ids[:, None])
    nxt = jnp.min(jnp.where(later, ids[None, :], cnt.shape[1]), axis=1)
    next_expert = jnp.where(nxt == cnt.shape[1], ids, nxt).astype(jnp.int32)
    flat = lambda a: a.reshape(-1).astype(jnp.int32)
    return ((flat(seg_start), flat(pad), flat(loc_off)), flat(tails), tile_expert, n_used,
            next_expert)


def _segment_copies(i, n_experts, seg_ref, pad_ref, loc_ref, hbm_ref, buf_ref, sem, to_hbm):
    copies = []
    for e in range(n_experts):
        n = pl.multiple_of(pad_ref[i * n_experts + e], _SEG_ALIGN)
        loc = pl.multiple_of(loc_ref[i * n_experts + e], _SEG_ALIGN)
        seg = pl.multiple_of(seg_ref[i * n_experts + e], _SEG_ALIGN)
        local, remote = buf_ref.at[pl.ds(loc, n)], hbm_ref.at[pl.ds(seg, n)]
        src, dst = (local, remote) if to_hbm else (remote, local)
        copies.append((n, pltpu.make_async_copy(src, dst, sem)))
    return copies


def _start_copies(copies):
    for n, cp in copies:
        pl.when(n > 0)(cp.start)


def _wait_copies(copies):
    for n, cp in copies:
        pl.when(n > 0)(cp.wait)


def _run_copies(copies):
    _start_copies(copies)
    _wait_copies(copies)


def _expert_offset(expert, loc_ref, i, n_experts):
    off = jnp.zeros(expert.shape, jnp.int32)
    for e in range(n_experts):
        off = jnp.where(expert == e, loc_ref[i * n_experts + e], off)
    return off


def _dispatch_kernel(seg_ref, pad_ref, loc_ref, tail_ref, h_ref, ml_ref, xs_ref, bufs, zeros, sems,
                     *, n_experts):
    i = pl.program_id(0)
    last = pl.num_programs(0) - 1
    cur = lax.rem(i, 2)

    def copies(tile, b):
        return _segment_copies(tile, n_experts, seg_ref, pad_ref, loc_ref, xs_ref, bufs.at[b],
                               sems.at[b], True)

    @pl.when(i >= 2)
    def _():
        _wait_copies(copies(i - 2, cur))

    tm = h_ref.shape[0]
    meta = ml_ref[...]
    slot1 = _expert_offset(meta[0:1].astype(jnp.int32), loc_ref, i, n_experts) + meta[2:3].astype(jnp.int32)
    slot2 = _expert_offset(meta[1:2].astype(jnp.int32), loc_ref, i, n_experts) + meta[3:4].astype(jnp.int32)
    slot = lax.broadcasted_iota(jnp.int32, (bufs.shape[1], tm), 0)
    perm = _indicator(jnp.logical_or(slot == slot1, slot == slot2))
    bufs[cur] = _bdot(perm, h_ref[...]).astype(bufs.dtype)
    _start_copies(copies(i, cur))

    @pl.when(i == last)
    def _():
        @pl.when(i >= 1)
        def _():
            _wait_copies(copies(i - 1, 1 - cur))

        _wait_copies(copies(i, cur))

        zeros[...] = jnp.zeros_like(zeros)
        fills = []
        for e in range(n_experts):
            start = pl.multiple_of(tail_ref[e], _SEG_ALIGN)
            n = pl.multiple_of(tail_ref[n_experts + e], _SEG_ALIGN)
            fills.append((n, pltpu.make_async_copy(zeros.at[pl.ds(0, n)],
                                                   xs_ref.at[pl.ds(start, n)], sems.at[0])))
        _run_copies(fills)

        def fill_tile(j, carry):
            cp = pltpu.make_async_copy(
                zeros, xs_ref.at[pl.ds(pl.multiple_of(j * _FFN_ROWS, _FFN_ROWS), _FFN_ROWS)],
                sems.at[0])
            cp.start()
            cp.wait()
            return carry

        lax.fori_loop(tail_ref[2 * n_experts], xs_ref.shape[0] // _FFN_ROWS, fill_tile, 0)


def _dispatch(h2, meta_l, plan, tails, n_slots, n_experts, tm):
    n, d = h2.shape
    seg_start, pad, loc_off = plan
    return pl.pallas_call(
        functools.partial(_dispatch_kernel, n_experts=n_experts),
        grid_spec=pltpu.PrefetchScalarGridSpec(
            num_scalar_prefetch=4,
            grid=(n // tm,),
            in_specs=[pl.BlockSpec((tm, d), lambda i, *_: (i, 0)),
                      pl.BlockSpec((n_experts, tm), lambda i, *_: (i, 0))],
            out_specs=pl.BlockSpec(memory_space=pl.ANY),
            scratch_shapes=[pltpu.VMEM((2, _local_slots(tm, n_experts), d), _BF16),
                            pltpu.VMEM((_FFN_ROWS, d), _BF16),
                            pltpu.SemaphoreType.DMA((2,))],
        ),
        out_shape=jax.ShapeDtypeStruct((n_slots, d), _BF16),
        compiler_params=_params(1),
        name="dispatch",
    )(seg_start, pad, loc_off, tails, h2, meta_l)


_W_ROUNDS = 16
_W_ROUNDS_PER_STEP = 2
_ST_CUR, _ST_LOADED, _ST_TARGET, _ST_DONE = range(4)


def _expert_ffn_kernel(te_ref, nu_ref, nxt_ref, x_ref, wg_hbm, wu_hbm, wd_hbm, o_ref,
                       wg_v, wu_v, wd_v, stage_gu, stage_d, sems, state):
    j = pl.program_id(0)
    rg = wg_v.shape[1] // _W_ROUNDS
    rd = wd_v.shape[1] // _W_ROUNDS

    def round_copies(e, r, s):
        g_rows = pl.ds(pl.multiple_of(r * rg, rg), rg)
        d_rows = pl.ds(pl.multiple_of(r * rd, rd), rd)
        return (pltpu.make_async_copy(wg_hbm.at[e, g_rows], stage_gu.at[s, 0], sems.at[s, 0]),
                pltpu.make_async_copy(wu_hbm.at[e, g_rows], stage_gu.at[s, 1], sems.at[s, 1]),
                pltpu.make_async_copy(wd_hbm.at[e, d_rows], stage_d.at[s], sems.at[s, 2]))

    def narrow_round(r, s, copy):
        g_rows = pl.ds(pl.multiple_of(r * rg, rg), rg)
        d_rows = pl.ds(pl.multiple_of(r * rd, rd), rd)
        wg_v[copy, g_rows, :] = stage_gu[s, 0].astype(_BF16)
        wu_v[copy, g_rows, :] = stage_gu[s, 1].astype(_BF16)
        wd_v[copy, d_rows, :] = stage_d[s].astype(_BF16)

    @pl.when(j == 0)
    def _():
        state[_ST_CUR] = 0
        state[_ST_LOADED] = -1
        state[_ST_TARGET] = -1
        state[_ST_DONE] = _W_ROUNDS

    active = j < nu_ref[0]
    e = te_ref[j]

    @pl.when(jnp.logical_and(active, e != state[_ST_LOADED]))
    def _():
        other = 1 - state[_ST_CUR]

        @pl.when(state[_ST_TARGET] != e)
        def _():
            state[_ST_TARGET] = e
            state[_ST_DONE] = 0

        def load_round(r, carry):
            cps = round_copies(e, r, 0)
            for cp in cps:
                cp.start()
            for cp in cps:
                cp.wait()
            narrow_round(r, 0, other)
            return carry

        lax.fori_loop(state[_ST_DONE], _W_ROUNDS, load_round, 0)
        nxt = nxt_ref[e]
        state[_ST_CUR] = other
        state[_ST_LOADED] = e
        state[_ST_TARGET] = nxt
        state[_ST_DONE] = jnp.where(nxt != e, 0, _W_ROUNDS)

    cur = state[_ST_CUR]
    target = state[_ST_TARGET]
    done = state[_ST_DONE]
    prefetch = jnp.logical_and(active, done < _W_ROUNDS)

    @pl.when(prefetch)
    def _():
        for s in range(_W_ROUNDS_PER_STEP):
            for cp in round_copies(target, done + s, s):
                cp.start()

    @pl.when(active)
    def _():
        x = x_ref[...]
        gate = _bdot(x, wg_v[cur])
        act = (gate * _sigmoid(gate)) * _bdot(x, wu_v[cur])
        o_ref[...] = _bdot(act.astype(_BF16), wd_v[cur]).astype(o_ref.dtype)

    @pl.when(jnp.logical_not(active))
    def _():
        o_ref[...] = jnp.zeros_like(o_ref)

    @pl.when(prefetch)
    def _():
        for s in range(_W_ROUNDS_PER_STEP):
            for cp in round_copies(target, done + s, s):
                cp.wait()
            narrow_round(done + s, s, 1 - cur)
        state[_ST_DONE] = done + _W_ROUNDS_PER_STEP


def _expert_ffn(xs, tile_expert, n_used, next_expert, wg, wu, wd):
    n_slots, d = xs.shape
    n_experts, _, f = wg.shape
    assert d % (_W_ROUNDS * _BF16_SUBLANES) == 0 and f % (_W_ROUNDS * _BF16_SUBLANES) == 0
    assert _W_ROUNDS % _W_ROUNDS_PER_STEP == 0
    any_spec = pl.BlockSpec(memory_space=pl.ANY)
    return pl.pallas_call(
        _expert_ffn_kernel,
        grid_spec=pltpu.PrefetchScalarGridSpec(
            num_scalar_prefetch=3,
            grid=(n_slots // _FFN_ROWS,),
            in_specs=[
                pl.BlockSpec((_FFN_ROWS, d),
                             lambda j, te, nu, nx: (jnp.maximum(jnp.minimum(j, nu[0] - 1), 0), 0)),
                any_spec, any_spec, any_spec,
            ],
            out_specs=pl.BlockSpec((_FFN_ROWS, d), lambda j, te, nu, nx: (j, 0)),
            scratch_shapes=[
                pltpu.VMEM((2, d, f), _BF16), pltpu.VMEM((2, d, f), _BF16),
                pltpu.VMEM((2, f, d), _BF16),
                pltpu.VMEM((_W_ROUNDS_PER_STEP, 2, d // _W_ROUNDS, f), _F32),
                pltpu.VMEM((_W_ROUNDS_PER_STEP, f // _W_ROUNDS, d), _F32),
                pltpu.SemaphoreType.DMA((_W_ROUNDS_PER_STEP, 3)),
                pltpu.SMEM((4,), jnp.int32),
            ],
        ),
        out_shape=jax.ShapeDtypeStruct((n_slots, d), _BF16),
        compiler_params=_params(1),
        name="expert_ffn",
    )(tile_expert, n_used, next_expert, xs, wg, wu, wd)


def _combine_kernel(seg_ref, pad_ref, loc_ref, ys_ref, ms_ref, x1_ref, g2_ref, nf_ref, o_ref,
                    bufs, sems, *, n_experts):
    i = pl.program_id(0)
    tm = x1_ref.shape[0]
    cur = lax.rem(i, 2)

    def copies(tile, b):
        return _segment_copies(tile, n_experts, seg_ref, pad_ref, loc_ref, ys_ref, bufs.at[b],
                               sems.at[b], False)

    @pl.when(i == 0)
    def _():
        bufs[...] = jnp.zeros_like(bufs)
        _start_copies(copies(0, 0))

    @pl.when(i + 1 < pl.num_programs(0))
    def _():
        _start_copies(copies(i + 1, 1 - cur))

    _wait_copies(copies(i, cur))
    meta = ms_ref[...]
    col = lambda k: meta[:, k:k + 1]
    slot1 = _expert_offset(col(0).astype(jnp.int32), loc_ref, i, n_experts) + col(2).astype(jnp.int32)
    slot2 = _expert_offset(col(1).astype(jnp.int32), loc_ref, i, n_experts) + col(3).astype(jnp.int32)
    slot = lax.broadcasted_iota(jnp.int32, (tm, bufs.shape[1]), 1)
    weights = jnp.where(slot == slot1, col(4), jnp.where(slot == slot2, col(5), 0.0))
    y = _bdot(weights.astype(_BF16), bufs[cur])
    o_ref[...] = _rmsnorm(x1_ref[...] + g2_ref[...] * y, nf_ref[...])


def _combine(ys, meta_s, x1, plan, mods, row_fn, norm_final, n_experts, tm):
    n, d = x1.shape
    seg_start, pad, loc_off = plan
    return pl.pallas_call(
        functools.partial(_combine_kernel, n_experts=n_experts),
        grid_spec=pltpu.PrefetchScalarGridSpec(
            num_scalar_prefetch=3,
            grid=(n // tm,),
            in_specs=[pl.BlockSpec(memory_space=pl.ANY),
                      pl.BlockSpec((tm, _LANES), lambda i, *_: (i, 0)),
                      pl.BlockSpec((tm, d), lambda i, *_: (i, 0)),
                      _mod_spec(d, 5, row_fn),
                      _resident((1, d))],
            out_specs=pl.BlockSpec((tm, d), lambda i, *_: (i, 0)),
            scratch_shapes=[pltpu.VMEM((2, _local_slots(tm, n_experts), d), _BF16),
                            pltpu.SemaphoreType.DMA((2,))],
        ),
        out_shape=jax.ShapeDtypeStruct((n, d), _F32),
        compiler_params=_params(1),
        name="combine",
    )(seg_start, pad, loc_off, ys, meta_s, x1, mods, norm_final)


def kernel(x, c, ctx, c_ctx, w_ada, b_ada, norm_mix, norm_ffn, w_in, w_conv, rpb, w_attn_out,
           w_conv_out, w_gate, b_gate, w_out, w_ffn_gate, w_ffn_up, w_ffn_down, w_router,
           w_exp_gate, w_exp_up, w_exp_down, norm_final):
    batch, seq, d = x.shape
    ctx_len = ctx.shape[1]
    depth = w_ada.shape[0]
    attn_w = _N_HEADS * _HEAD_DIM
    n_experts = w_router.shape[-1]
    rows = seq // _GRID_W
    n_pairs = attn_w // _LANES
    assert depth == 2 and seq % (_Q_ROWS * _GRID_W) == 0 and attn_w % _LANES == 0
    assert n_experts == _F32_SUBLANES

    ctx_row = batch
    cond_rows = -(-(batch + 1) // 8) * 8
    cond = jnp.zeros((cond_rows, d), _F32).at[:batch].set(c).at[ctx_row].set(c_ctx)
    mods_all = _ada(cond, w_ada, b_ada, tn=1536).reshape(depth, cond_rows, 1, 6 * d)

    union, starts, variant_of, patterns = _attn_plan(rows)

    x2d = x.reshape(batch * seq, d)
    c2d = ctx.reshape(batch * ctx_len, d)
    tm_lat = 512
    ctx_row_fn = lambda i: ctx_row

    def lat_rows(tm):
        return lambda i: (i * tm) // seq

    out = None
    for l in range(depth):
        last = l == depth - 1
        mods = mods_all[l]
        wts = {
            "norm_mix": norm_mix[l].reshape(1, d),
            "norm_ffn": norm_ffn[l].reshape(1, d),
            "wconv": w_conv[l],
            "wao": w_attn_out[l].astype(_BF16),
            "wco": w_conv_out[l].astype(_BF16),
            "wgate": w_gate[l].astype(_BF16),
            "bgate": b_gate[l].reshape(1, -1),
            "wout": w_out[l].astype(_BF16),
        }
        w_in_l = w_in[l].astype(_BF16)
        p = _proj(x2d, mods, lat_rows(512), wts["norm_mix"], w_in_l, tm=512)
        if last:
            pc = _proj(c2d, mods, ctx_row_fn, wts["norm_mix"], w_in_l[:, attn_w:3 * attn_w], tm=512)
            kc_col = 0
        else:
            pc = _proj(c2d, mods, ctx_row_fn, wts["norm_mix"], w_in_l, tm=512)
            kc_col = n_pairs
        table = _bias_table(rpb[l], union, tuple(patterns))
        ya, yca = _attention(p, pc, kc_col, table, starts, variant_of, union, batch, seq,
                             ctx_len, with_ctx=not last)
        if not last:
            wts["ffn_gate"] = w_ffn_gate[l // 2].astype(_BF16)
            wts["ffn_up"] = w_ffn_up[l // 2].astype(_BF16)
            wts["ffn_down"] = w_ffn_down[l // 2].astype(_BF16)
            x2d = _merge_ffn(x2d, ya, p, mods, lat_rows(tm_lat), seq, wts, tm=tm_lat)
            c2d = _merge_ffn(c2d, yca, pc, mods, ctx_row_fn, ctx_len, wts, tm=ctx_len)
        else:
            wr = jnp.pad(w_router[l // 2], ((0, 0), (0, _LANES - n_experts)))
            wts["router_hi"] = wr.astype(_BF16)
            wts["router_lo"] = (wr - wts["router_hi"].astype(_F32)).astype(_BF16)
            tm_moe = 512
            n_tok = batch * seq
            n_tiles = n_tok // tm_moe
            x1, h2, meta_s, meta_l, cnt = _merge_router(x2d, ya, p, mods, lat_rows(tm_moe), seq,
                                                        wts, n_experts, tm=tm_moe)
            n_slots = _round_up(_TOP_K * n_tok + n_tiles * n_experts * (_SEG_ALIGN - 1)
                                + n_experts * (_FFN_ROWS - 1), _FFN_ROWS)
            cnt = cnt[:, 0].reshape(n_tiles, n_experts).astype(jnp.int32)
            plan, tails, tile_expert, n_used, next_expert = _routing_plan(cnt, n_slots)
            xs = _dispatch(h2, meta_l, plan, tails, n_slots, n_experts, tm=tm_moe)
            ys = _expert_ffn(xs, tile_expert, n_used, next_expert, w_exp_gate[l // 2],
                             w_exp_up[l // 2], w_exp_down[l // 2])
            out = _combine(ys, meta_s, x1, plan, mods, lat_rows(tm_moe), norm_final.reshape(1, d),
                           n_experts, tm=tm_moe)
    return out.reshape(batch, seq, d)
```

```python
import functools

import numpy as np
import jax
import jax.numpy as jnp
from jax import lax
from jax.experimental import pallas as pl
from jax.experimental.pallas import tpu as pltpu

_F32 = jnp.float32
_BF16 = jnp.bfloat16

_GRID_W = 64
_N_HEADS = 8
_HEAD_DIM = 64
_NA_ROWS = 8
_NA_COLS = 16
_CONV_K = 3
_TOP_K = 2
_EPS = 1e-6
_NEG_INF = -1e30

_LANES = 128
_F32_SUBLANES = 8
_BF16_SUBLANES = 16
_VMEM_LIMIT_BYTES = 56 * 1024 * 1024

_Q_ROWS = 4
_ATTN_UNROLL = 2
_HEADS_PER_BLOCK = _LANES // _HEAD_DIM
_NT = (((1,), (1,)), ((), ()))
_TN = (((0,), (0,)), ((), ()))


def _params(n_axes):
    return pltpu.CompilerParams(
        dimension_semantics=("arbitrary",) * n_axes,
        vmem_limit_bytes=_VMEM_LIMIT_BYTES)


def _resident(shape):
    nd = len(shape)
    return pl.BlockSpec(shape, lambda *_: (0,) * nd, pipeline_mode=pl.Buffered(1))


def _sigmoid(x):
    return 1.0 / (1.0 + jnp.exp(-x))


def _rmsnorm(x, g):
    ms = jnp.mean(x * x, axis=-1, keepdims=True)
    return (x * lax.rsqrt(ms + _EPS)) * g


def _modulate(x, g, shift, scale):
    return _rmsnorm(x, g) * (1.0 + scale) + shift


def _bdot(a, b):
    return jnp.dot(a, b, preferred_element_type=_F32)


def _indicator(mask):
    return jnp.where(mask, 1.0, 0.0).astype(_BF16)


def _ada_kernel(cond_ref, w_ref, b_ref, o_ref):
    cond = cond_ref[...]
    act = cond * _sigmoid(cond)
    o_ref[...] = jnp.dot(act, w_ref[...], precision=lax.Precision.HIGHEST,
                         preferred_element_type=_F32) + b_ref[...]


def _ada(cond, w_ada, b_ada, tn):
    depth, d, n_out = w_ada.shape
    rows = cond.shape[0]
    return pl.pallas_call(
        _ada_kernel,
        grid=(depth, n_out // tn),
        in_specs=[
            pl.BlockSpec((rows, d), lambda l, j: (0, 0)),
            pl.BlockSpec((None, d, tn), lambda l, j: (l, 0, j)),
            pl.BlockSpec((None, 1, tn), lambda l, j: (l, 0, j)),
        ],
        out_specs=pl.BlockSpec((None, rows, tn), lambda l, j: (l, 0, j)),
        out_shape=jax.ShapeDtypeStruct((depth, rows, n_out), _F32),
        compiler_params=_params(2),
        name="ada",
    )(cond, w_ada, b_ada.reshape(depth, 1, n_out))


def _mod_spec(d, chunk, row_fn):
    return pl.BlockSpec((None, 1, d), lambda i, *_: (row_fn(i), 0, chunk))


def _proj_kernel(x_ref, g_ref, sh_ref, sc_ref, w_ref, o_ref):
    h = _modulate(x_ref[...], g_ref[...], sh_ref[...], sc_ref[...])
    o_ref[...] = _bdot(h.astype(_BF16), w_ref[...]).astype(o_ref.dtype)


def _proj(x2d, mods, row_fn, g, w, tm):
    n, d = x2d.shape
    n_out = w.shape[1]
    return pl.pallas_call(
        _proj_kernel,
        grid=(n // tm,),
        in_specs=[
            pl.BlockSpec((tm, d), lambda i: (i, 0)),
            _resident((1, d)),
            _mod_spec(d, 0, row_fn),
            _mod_spec(d, 1, row_fn),
            _resident(w.shape),
        ],
        out_specs=pl.BlockSpec((tm, n_out), lambda i: (i, 0)),
        out_shape=jax.ShapeDtypeStruct((n, n_out), _BF16),
        compiler_params=_params(1),
        name="proj",
    )(x2d, g, mods, mods, w)


def _bias_table_kernel(rpb_ref, o_ref, *, union, patterns):
    sub = _F32_SUBLANES
    span = _NA_COLS - 1
    rp = rpb_ref[...]
    lane = lax.broadcasted_iota(jnp.int32, (sub, _LANES), 1)
    srow = lax.broadcasted_iota(jnp.int32, (sub, _LANES), 0)
    ck = jnp.bitwise_and(lane, _GRID_W - 1)
    first = lane < _GRID_W
    in_win = []
    for g in range(_GRID_W // sub):
        cs = jnp.clip(srow + (sub * g - _NA_COLS // 2), 0, _GRID_W - _NA_COLS)
        in_win.append(jnp.logical_and(ck >= cs, ck < cs + _NA_COLS))
    first_part = jnp.logical_or(lane <= span, lane >= _LANES - span)
    second_part = jnp.logical_and(lane >= _GRID_W - span, lane <= _GRID_W + span)
    neg = jnp.full((sub, _LANES), _NEG_INF, _F32)

    def row_vec(dr):
        return jnp.broadcast_to(rp[dr + _NA_ROWS - 1:dr + _NA_ROWS, :], (sub, _LANES))

    for v, pat in enumerate(patterns):
        for a in range(_Q_ROWS):
            for jp in range(-(-union // 2)):
                j0, j1 = 2 * jp, 2 * jp + 1
                dr0 = pat[a * union + j0]
                dr1 = pat[a * union + j1] if j1 < union else None
                width = _LANES if j1 < union else _GRID_W
                base = jnp.zeros((sub, _LANES), _F32)
                if dr0 is not None:
                    base = jnp.where(first_part, pltpu.roll(row_vec(dr0), _LANES - span, 1), base)
                if dr1 is not None:
                    base = jnp.where(second_part, pltpu.roll(row_vec(dr1), _GRID_W - span, 1), base)
                for g in range(_GRID_W // sub):
                    rows = slice(a * _GRID_W + sub * g, a * _GRID_W + sub * (g + 1))
                    if dr0 is None and dr1 is None:
                        tile = neg
                    else:
                        tile = pltpu.roll(base, sub * g, 1, stride=1, stride_axis=0)
                        keep = in_win[g]
                        if dr0 is None:
                            keep = jnp.logical_and(keep, jnp.logical_not(first))
                        if dr1 is None:
                            keep = jnp.logical_and(keep, first)
                        tile = jnp.where(keep, tile, neg)
                    o_ref[v, rows, jp * _LANES:jp * _LANES + width] = tile[:, :width]


def _bias_table(rpb_l, union, patterns):
    h, n_dr, n_dc = rpb_l.shape
    assert _LANES == 2 * _GRID_W and n_dc <= _LANES
    n_dr_pad = _round_up(n_dr, _F32_SUBLANES)
    padded = jnp.pad(rpb_l, ((0, 0), (0, n_dr_pad - n_dr), (0, _LANES - n_dc)))
    n_var = len(patterns)
    qt, kt = _Q_ROWS * _GRID_W, union * _GRID_W
    return pl.pallas_call(
        functools.partial(_bias_table_kernel, union=union, patterns=patterns),
        grid=(h,),
        in_specs=[pl.BlockSpec((None, n_dr_pad, _LANES), lambda i: (i, 0, 0))],
        out_specs=pl.BlockSpec((n_var, None, qt, kt), lambda i: (0, i, 0, 0)),
        out_shape=jax.ShapeDtypeStruct((n_var, h, qt, kt), _F32),
        compiler_params=_params(1),
        name="bias_table",
    )(padded)


def _attn_plan(rows):
    kh = min(_NA_ROWS, rows)
    union = min(_Q_ROWS + kh - 1, rows)
    n_blocks = rows // _Q_ROWS
    starts, variant_of, patterns = [], [], []
    for t in range(n_blocks):
        start = int(np.clip(_Q_ROWS * t - kh // 2, 0, rows - union))
        pat = []
        for a in range(_Q_ROWS):
            r = _Q_ROWS * t + a
            rs = int(np.clip(r - kh // 2, 0, rows - kh))
            for j in range(union):
                kr = start + j
                pat.append(kr - r if rs <= kr < rs + kh else None)
        pat = tuple(pat)
        if pat not in patterns:
            patterns.append(pat)
        starts.append(start)
        variant_of.append(patterns.index(pat))
    return union, starts, variant_of, patterns


def _first_head_lanes(shape):
    return lax.broadcasted_iota(jnp.int32, shape, len(shape) - 1) < _HEAD_DIM


def _values_with_ones(v):
    lo = _first_head_lanes(v.shape)
    ones = jnp.ones_like(v)
    return jnp.where(lo, v, ones), jnp.where(lo, ones, v)


def _scores(qb, keys, bias_fns):
    lo = _first_head_lanes(qb.shape)
    out = []
    for hh in range(_HEADS_PER_BLOCK):
        qm = jnp.where(lo if hh == 0 else jnp.logical_not(lo), qb, jnp.zeros_like(qb))
        scores = []
        for kk, bias_fn in zip(keys, bias_fns):
            s = lax.dot_general(qm, kk, _NT, preferred_element_type=_F32)
            scores.append(s if bias_fn is None else s + bias_fn(hh))
        out.append(scores)
    return out


def _probabilities(scores):
    m = functools.reduce(jnp.maximum, [jnp.max(s, axis=-1, keepdims=True) for s in scores])
    return [jnp.exp(s - m).astype(_BF16) for s in scores]


def _weighted_values(probs, values):
    res = []
    for hh in range(_HEADS_PER_BLOCK):
        acc = None
        for e, v1 in zip(probs[hh], values):
            o = _bdot(e, v1[hh])
            acc = o if acc is None else acc + o
        res.append(acc)
    lo = _first_head_lanes(res[0].shape)
    num = jnp.where(lo, res[0], res[1])
    den = pltpu.roll(jnp.where(lo, res[1], res[0]), _HEAD_DIM, 1)
    return num / den


def _attn_kernel(start_ref, var_ref, q_ref, k_ref, v_ref, kc_ref, vc_ref, tab_ref, *rest,
                 n_blocks, union_tokens, with_ctx):
    if with_ctx:
        qc_ref, o_ref, oc_ref, s_loc, s_ctx, e_loc, e_ctx = rest
    else:
        o_ref, s_loc, s_ctx, e_loc, e_ctx = rest
    heads = range(_HEADS_PER_BLOCK)
    qb_tokens = _Q_ROWS * _GRID_W
    scale = _HEAD_DIM ** -0.5
    kc = kc_ref[...]
    vc1 = _values_with_ones(vc_ref[...])

    def key_start(t):
        return pl.multiple_of(start_ref[t] * _GRID_W, _GRID_W)

    def query_start(t):
        return pl.multiple_of(t * qb_tokens, qb_tokens)

    def scores_stage(t):
        slot, var = lax.rem(t, 2), var_ref[t]
        qb = q_ref[pl.ds(query_start(t), qb_tokens), :] * scale
        kb = k_ref[pl.ds(key_start(t), union_tokens), :]
        scores = _scores(qb, [kb, kc], [lambda hh: tab_ref[var, hh], None])
        for hh in heads:
            s_loc[slot, hh], s_ctx[slot, hh] = scores[hh]

    def softmax_stage(t):
        slot = lax.rem(t, 2)
        for hh in heads:
            e_loc[slot, hh], e_ctx[slot, hh] = _probabilities([s_loc[slot, hh], s_ctx[slot, hh]])

    def values_stage(t):
        slot = lax.rem(t, 2)
        vb1 = _values_with_ones(v_ref[pl.ds(key_start(t), union_tokens), :])
        probs = [[e_loc[slot, hh], e_ctx[slot, hh]] for hh in heads]
        o = _weighted_values(probs, [vb1, vc1])
        o_ref[pl.ds(query_start(t), qb_tokens), :] = o.astype(o_ref.dtype)

    assert n_blocks >= 2
    scores_stage(0)
    softmax_stage(0)
    scores_stage(1)

    def body(t, carry):
        values_stage(t - 2)
        softmax_stage(t - 1)
        scores_stage(t)
        return carry

    lax.fori_loop(2, n_blocks, body, 0, unroll=_ATTN_UNROLL)
    values_stage(n_blocks - 2)
    softmax_stage(n_blocks - 1)
    values_stage(n_blocks - 1)

    if with_ctx:
        scores = _scores(qc_ref[...] * scale, [kc], [None])
        probs = [_probabilities(scores[hh]) for hh in heads]
        oc_ref[...] = _weighted_values(probs, [vc1]).astype(oc_ref.dtype)


def _attention(p, pc, kc_col, table, starts, variant_of, union, batch, seq, ctx_len, with_ctx):
    attn_w = _N_HEADS * _HEAD_DIM
    n_pairs = attn_w // _LANES
    n_var = table.shape[0]
    qb_tokens = _Q_ROWS * _GRID_W
    union_tokens = union * _GRID_W
    kern = functools.partial(_attn_kernel, n_blocks=len(starts), union_tokens=union_tokens,
                             with_ctx=with_ctx)
    in_specs = [
        pl.BlockSpec((seq, _LANES), lambda hp, b, *_: (b, hp)),
        pl.BlockSpec((seq, _LANES), lambda hp, b, *_: (b, n_pairs + hp)),
        pl.BlockSpec((seq, _LANES), lambda hp, b, *_: (b, 2 * n_pairs + hp)),
        pl.BlockSpec((ctx_len, _LANES), lambda hp, b, *_: (b, kc_col + hp)),
        pl.BlockSpec((ctx_len, _LANES), lambda hp, b, *_: (b, kc_col + n_pairs + hp)),
        pl.BlockSpec((n_var, _HEADS_PER_BLOCK, qb_tokens, union_tokens),
                     lambda hp, b, *_: (0, hp, 0, 0)),
    ]
    args = [p, p, p, pc, pc, table]
    out_specs = [pl.BlockSpec((seq, _LANES), lambda hp, b, *_: (b, hp))]
    out_shape = [jax.ShapeDtypeStruct((batch * seq, attn_w), _BF16)]
    if with_ctx:
        in_specs.append(pl.BlockSpec((ctx_len, _LANES), lambda hp, b, *_: (b, hp)))
        args.append(pc)
        out_specs.append(pl.BlockSpec((ctx_len, _LANES), lambda hp, b, *_: (b, hp)))
        out_shape.append(jax.ShapeDtypeStruct((batch * ctx_len, attn_w), _BF16))
    outs = pl.pallas_call(
        kern,
        grid_spec=pltpu.PrefetchScalarGridSpec(
            num_scalar_prefetch=2,
            grid=(n_pairs, batch),
            in_specs=in_specs,
            out_specs=out_specs,
            scratch_shapes=[
                pltpu.VMEM((2, _HEADS_PER_BLOCK, qb_tokens, union_tokens), _F32),
                pltpu.VMEM((2, _HEADS_PER_BLOCK, qb_tokens, ctx_len), _F32),
                pltpu.VMEM((2, _HEADS_PER_BLOCK, qb_tokens, union_tokens), _BF16),
                pltpu.VMEM((2, _HEADS_PER_BLOCK, qb_tokens, ctx_len), _BF16),
            ],
        ),
        out_shape=out_shape,
        compiler_params=_params(2),
        name="attn",
    )(jnp.asarray(starts, jnp.int32), jnp.asarray(variant_of, jnp.int32), *args)
    return outs if with_ctx else (outs[0], None)


def _mixer_update(i, x_ref, ya_ref, bcu_ref, prev_ref, next_ref, nm_ref, sh1_ref, sc1_ref, g1_ref,
                  wconv_ref, wao_ref, wco_ref, wgate_ref, bgate_ref, wout_ref, seq_len):
    tm, d = x_ref.shape
    cw = wco_ref.shape[0]
    x = x_ref[...]
    hb = _modulate(x, nm_ref[...], sh1_ref[...], sc1_ref[...]).astype(_BF16)
    gates = _sigmoid(_bdot(hb, wgate_ref[...]) + bgate_ref[...])

    bcu = bcu_ref[...].astype(_F32)
    bg, z = bcu[:, :cw], bcu[:, cw:2 * cw] * bcu[:, 2 * cw:]
    prev = prev_ref[...].astype(_F32)
    nxt = next_ref[...].astype(_F32)
    has_prev = ((i * tm) % seq_len != 0).astype(_F32)
    has_next = (((i + 1) * tm) % seq_len != 0).astype(_F32)
    last = prev.shape[0] - 1
    z_prev = prev[last:last + 1, cw:2 * cw] * prev[last:last + 1, 2 * cw:] * has_prev
    z_next = nxt[0:1, cw:2 * cw] * nxt[0:1, 2 * cw:] * has_next
    row = lax.broadcasted_iota(jnp.int32, z.shape, 0)
    z_dn = jnp.where(row == 0, z_prev, pltpu.roll(z, 1, 0))
    z_up = jnp.where(row == tm - 1, z_next, pltpu.roll(z, tm - 1, 0))
    wconv = wconv_ref[...]
    y_conv = bg * (z_dn * wconv[0:1] + z * wconv[1:2] + z_up * wconv[2:3])

    a_proj = _bdot(ya_ref[...], wao_ref[...])
    c_proj = _bdot(y_conv.astype(_BF16), wco_ref[...])
    merged = gates[:, :d] * a_proj + gates[:, d:] * c_proj
    return x + g1_ref[...] * _bdot(merged.astype(_BF16), wout_ref[...])


_N_MIX_REFS = 15


def _merge_ffn_kernel(*refs, seq_len):
    mix = refs[:_N_MIX_REFS]
    nf_ref, sh2_ref, sc2_ref, g2_ref, wg_ref, wu_ref, wd_ref, o_ref = refs[_N_MIX_REFS:]
    x1 = _mixer_update(pl.program_id(0), *mix, seq_len=seq_len)
    h2 = _modulate(x1, nf_ref[...], sh2_ref[...], sc2_ref[...]).astype(_BF16)
    gate = _bdot(h2, wg_ref[...])
    act = (gate * _sigmoid(gate)) * _bdot(h2, wu_ref[...])
    o_ref[...] = x1 + g2_ref[...] * _bdot(act.astype(_BF16), wd_ref[...])


def _top2(vals, idx, n_idx, axis):
    v1 = jnp.max(vals, axis=axis, keepdims=True)
    i1 = jnp.min(jnp.where(vals == v1, idx, n_idx), axis=axis, keepdims=True)
    rest = jnp.where(idx == i1, -jnp.inf, vals)
    v2 = jnp.max(rest, axis=axis, keepdims=True)
    i2 = jnp.min(jnp.where(rest == v2, idx, n_idx), axis=axis, keepdims=True)
    return i1, i2, v1, v2


def _pick(idx, sel, vals, axis):
    return jnp.sum(jnp.where(idx == sel, vals, 0.0), axis=axis, keepdims=True)


def _merge_router_kernel(*refs, seq_len, n_experts):
    mix = refs[:_N_MIX_REFS]
    (nf_ref, sh2_ref, sc2_ref, wr_hi_ref, wr_lo_ref,
     x1_ref, h2_ref, ms_ref, ml_ref, cnt_ref) = refs[_N_MIX_REFS:]
    x1 = _mixer_update(pl.program_id(0), *mix, seq_len=seq_len)
    x1_ref[...] = x1
    h2 = _modulate(x1, nf_ref[...], sh2_ref[...], sc2_ref[...])
    h_hi = h2.astype(_BF16)
    h2_ref[...] = h_hi
    tm = h2.shape[0]
    h_lo = (h2 - h_hi.astype(_F32)).astype(_BF16)
    wr_hi = wr_hi_ref[...]
    logits = _bdot(h_hi, wr_hi) + _bdot(h_lo, wr_hi) + _bdot(h_hi, wr_lo_ref[...])
    row = lax.broadcasted_iota(jnp.int32, (tm, tm), 0)
    col = lax.broadcasted_iota(jnp.int32, (tm, tm), 1)

    lane = lax.broadcasted_iota(jnp.int32, logits.shape, 1)
    i1, i2, v1, v2 = _top2(jnp.where(lane < n_experts, logits, -jnp.inf), lane, _LANES, -1)
    e2 = jnp.exp(v2 - v1)
    den = 1.0 + e2
    sel = _indicator(jnp.logical_or(lane == i1, lane == i2))
    ranks = _bdot(_indicator(col < row), sel)
    fields = (i1.astype(_F32), i2.astype(_F32), _pick(lane, i1, ranks, -1),
              _pick(lane, i2, ranks, -1), 1.0 / den, e2 / den)
    meta = jnp.zeros(logits.shape, _F32)
    for k, f in enumerate(fields):
        meta = jnp.where(lane == k, f, meta)
    ms_ref[...] = meta

    lt = logits.T[:n_experts]
    sub = lax.broadcasted_iota(jnp.int32, lt.shape, 0)
    j1, j2, _, _ = _top2(lt, sub, n_experts, 0)
    sel_t = _indicator(jnp.logical_or(sub == j1, sub == j2))
    ranks_t = _bdot(sel_t, _indicator(row < col))
    fields_t = (j1.astype(_F32), j2.astype(_F32), _pick(sub, j1, ranks_t, 0),
                _pick(sub, j2, ranks_t, 0))
    meta_t = jnp.zeros(lt.shape, _F32)
    for k, f in enumerate(fields_t):
        meta_t = jnp.where(sub == k, f, meta_t)
    ml_ref[...] = meta_t
    cnt = jnp.sum(sel_t.astype(_F32), axis=1, keepdims=True)
    cnt_ref[...] = jnp.broadcast_to(cnt, cnt_ref.shape)


def _mixer_specs(n, d, tm, p_cols, row_fn, weights):
    half = p_cols // 2
    sub = _BF16_SUBLANES
    n_sub = n // sub
    specs = [
        pl.BlockSpec((tm, d), lambda i: (i, 0)),
        pl.BlockSpec((tm, weights["wao"].shape[0]), lambda i: (i, 0)),
        pl.BlockSpec((tm, half), lambda i: (i, 1)),
        pl.BlockSpec((sub, half), lambda i: (jnp.maximum(i * (tm // sub) - 1, 0), 1)),
        pl.BlockSpec((sub, half), lambda i: (jnp.minimum((i + 1) * (tm // sub), n_sub - 1), 1)),
        _resident((1, d)),
        _mod_spec(d, 0, row_fn), _mod_spec(d, 1, row_fn), _mod_spec(d, 2, row_fn),
    ]
    for name in ("wconv", "wao", "wco", "wgate", "bgate", "wout"):
        specs.append(_resident(weights[name].shape))
    return specs


def _merge_ffn(x2d, ya, p, mods, row_fn, seq_len, wts, tm):
    n, d = x2d.shape
    specs = _mixer_specs(n, d, tm, p.shape[1], row_fn, wts)
    specs += [_resident((1, d)), _mod_spec(d, 3, row_fn), _mod_spec(d, 4, row_fn),
              _mod_spec(d, 5, row_fn),
              _resident(wts["ffn_gate"].shape), _resident(wts["ffn_up"].shape),
              _resident(wts["ffn_down"].shape)]
    return pl.pallas_call(
        functools.partial(_merge_ffn_kernel, seq_len=seq_len),
        grid=(n // tm,),
        in_specs=specs,
        out_specs=pl.BlockSpec((tm, d), lambda i: (i, 0)),
        out_shape=jax.ShapeDtypeStruct((n, d), _F32),
        compiler_params=_params(1),
        name="merge_ffn",
    )(x2d, ya, p, p, p, wts["norm_mix"], mods, mods, mods,
      wts["wconv"], wts["wao"], wts["wco"], wts["wgate"], wts["bgate"], wts["wout"],
      wts["norm_ffn"], mods, mods, mods, wts["ffn_gate"], wts["ffn_up"], wts["ffn_down"])


def _merge_router(x2d, ya, p, mods, row_fn, seq_len, wts, n_experts, tm):
    n, d = x2d.shape
    specs = _mixer_specs(n, d, tm, p.shape[1], row_fn, wts)
    specs += [_resident((1, d)), _mod_spec(d, 3, row_fn), _mod_spec(d, 4, row_fn),
              _resident(wts["router_hi"].shape), _resident(wts["router_lo"].shape)]
    tile = lambda w: pl.BlockSpec((tm, w), lambda i: (i, 0))
    n_tiles = n // tm
    return pl.pallas_call(
        functools.partial(_merge_router_kernel, seq_len=seq_len, n_experts=n_experts),
        grid=(n_tiles,),
        in_specs=specs,
        out_specs=[tile(d), tile(d), tile(_LANES),
                   pl.BlockSpec((n_experts, tm), lambda i: (i, 0)),
                   pl.BlockSpec((n_experts, _LANES), lambda i: (i, 0))],
        out_shape=[jax.ShapeDtypeStruct((n, d), _F32),
                   jax.ShapeDtypeStruct((n, d), _BF16),
                   jax.ShapeDtypeStruct((n, _LANES), _F32),
                   jax.ShapeDtypeStruct((n_tiles * n_experts, tm), _F32),
                   jax.ShapeDtypeStruct((n_tiles * n_experts, _LANES), _F32)],
        compiler_params=_params(1),
        name="merge_router",
    )(x2d, ya, p, p, p, wts["norm_mix"], mods, mods, mods,
      wts["wconv"], wts["wao"], wts["wco"], wts["wgate"], wts["bgate"], wts["wout"],
      wts["norm_ffn"], mods, mods, wts["router_hi"], wts["router_lo"])


_SEG_ALIGN = _BF16_SUBLANES
_FFN_ROWS = 256


def _round_up(v, m):
    return (v + m - 1) // m * m


def _local_slots(tm, n_experts):
    return _round_up(_TOP_K * tm + n_experts * (_SEG_ALIGN - 1), _SEG_ALIGN)


def _routing_plan(cnt, n_slots):
    pad = _round_up(cnt, _SEG_ALIGN)
    total = jnp.sum(pad, axis=0)
    region = _round_up(total, _FFN_ROWS)
    ends = jnp.cumsum(region)
    seg_start = (ends - region)[None, :] + jnp.cumsum(pad, axis=0) - pad
    loc_off = jnp.cumsum(pad, axis=1) - pad
    n_used = (ends[-1] // _FFN_ROWS).astype(jnp.int32).reshape(1)
    tails = jnp.concatenate([ends - region + total, region - total, n_used])
    tile_row = jnp.arange(n_slots // _FFN_ROWS, dtype=jnp.int32) * _FFN_ROWS
    tile_expert = jnp.minimum(jnp.sum(tile_row[:, None] >= ends[None, :], axis=1),
                              cnt.shape[1] - 1).astype(jnp.int32)
    ids = jnp.arange(cnt.shape[1], dtype=jnp.int32)
    later = jnp.logical_and(region[None, :] > 0, ids[None, :] > ids[:, None])
    nxt = jnp.min(jnp.where(later, ids[None, :], cnt.shape[1]), axis=1)
    next_expert = jnp.where(nxt == cnt.shape[1], ids, nxt).astype(jnp.int32)
    flat = lambda a: a.reshape(-1).astype(jnp.int32)
    return ((flat(seg_start), flat(pad), flat(loc_off)), flat(tails), tile_expert, n_used,
            next_expert)


def _segment_copies(i, n_experts, seg_ref, pad_ref, loc_ref, hbm_ref, buf_ref, sem, to_hbm):
    copies = []
    for e in range(n_experts):
        n = pl.multiple_of(pad_ref[i * n_experts + e], _SEG_ALIGN)
        loc = pl.multiple_of(loc_ref[i * n_experts + e], _SEG_ALIGN)
        seg = pl.multiple_of(seg_ref[i * n_experts + e], _SEG_ALIGN)
        local, remote = buf_ref.at[pl.ds(loc, n)], hbm_ref.at[pl.ds(seg, n)]
        src, dst = (local, remote) if to_hbm else (remote, local)
        copies.append((n, pltpu.make_async_copy(src, dst, sem)))
    return copies


def _start_copies(copies):
    for n, cp in copies:
        pl.when(n > 0)(cp.start)


def _wait_copies(copies):
    for n, cp in copies:
        pl.when(n > 0)(cp.wait)


def _run_copies(copies):
    _start_copies(copies)
    _wait_copies(copies)


def _expert_offset(expert, loc_ref, i, n_experts):
    off = jnp.zeros(expert.shape, jnp.int32)
    for e in range(n_experts):
        off = jnp.where(expert == e, loc_ref[i * n_experts + e], off)
    return off


def _dispatch_kernel(seg_ref, pad_ref, loc_ref, tail_ref, h_ref, ml_ref, xs_ref, bufs, zeros, sems,
                     *, n_experts):
    i = pl.program_id(0)
    last = pl.num_programs(0) - 1
    cur = lax.rem(i, 2)

    def copies(tile, b):
        return _segment_copies(tile, n_experts, seg_ref, pad_ref, loc_ref, xs_ref, bufs.at[b],
                               sems.at[b], True)

    @pl.when(i >= 2)
    def _():
        _wait_copies(copies(i - 2, cur))

    tm = h_ref.shape[0]
    meta = ml_ref[...]
    slot1 = _expert_offset(meta[0:1].astype(jnp.int32), loc_ref, i, n_experts) + meta[2:3].astype(jnp.int32)
    slot2 = _expert_offset(meta[1:2].astype(jnp.int32), loc_ref, i, n_experts) + meta[3:4].astype(jnp.int32)
    slot = lax.broadcasted_iota(jnp.int32, (bufs.shape[1], tm), 0)
    perm = _indicator(jnp.logical_or(slot == slot1, slot == slot2))
    bufs[cur] = _bdot(perm, h_ref[...]).astype(bufs.dtype)
    _start_copies(copies(i, cur))

    @pl.when(i == last)
    def _():
        @pl.when(i >= 1)
        def _():
            _wait_copies(copies(i - 1, 1 - cur))

        _wait_copies(copies(i, cur))

        zeros[...] = jnp.zeros_like(zeros)
        fills = []
        for e in range(n_experts):
            start = pl.multiple_of(tail_ref[e], _SEG_ALIGN)
            n = pl.multiple_of(tail_ref[n_experts + e], _SEG_ALIGN)
            fills.append((n, pltpu.make_async_copy(zeros.at[pl.ds(0, n)],
                                                   xs_ref.at[pl.ds(start, n)], sems.at[0])))
        _run_copies(fills)

        def fill_tile(j, carry):
            cp = pltpu.make_async_copy(
                zeros, xs_ref.at[pl.ds(pl.multiple_of(j * _FFN_ROWS, _FFN_ROWS), _FFN_ROWS)],
                sems.at[0])
            cp.start()
            cp.wait()
            return carry

        lax.fori_loop(tail_ref[2 * n_experts], xs_ref.shape[0] // _FFN_ROWS, fill_tile, 0)


def _dispatch(h2, meta_l, plan, tails, n_slots, n_experts, tm):
    n, d = h2.shape
    seg_start, pad, loc_off = plan
    return pl.pallas_call(
        functools.partial(_dispatch_kernel, n_experts=n_experts),
        grid_spec=pltpu.PrefetchScalarGridSpec(
            num_scalar_prefetch=4,
            grid=(n // tm,),
            in_specs=[pl.BlockSpec((tm, d), lambda i, *_: (i, 0)),
                      pl.BlockSpec((n_experts, tm), lambda i, *_: (i, 0))],
            out_specs=pl.BlockSpec(memory_space=pl.ANY),
            scratch_shapes=[pltpu.VMEM((2, _local_slots(tm, n_experts), d), _BF16),
                            pltpu.VMEM((_FFN_ROWS, d), _BF16),
                            pltpu.SemaphoreType.DMA((2,))],
        ),
        out_shape=jax.ShapeDtypeStruct((n_slots, d), _BF16),
        compiler_params=_params(1),
        name="dispatch",
    )(seg_start, pad, loc_off, tails, h2, meta_l)


_W_ROUNDS = 16
_W_ROUNDS_PER_STEP = 2
_ST_CUR, _ST_LOADED, _ST_TARGET, _ST_DONE = range(4)


def _expert_ffn_kernel(te_ref, nu_ref, nxt_ref, x_ref, wg_hbm, wu_hbm, wd_hbm, o_ref,
                       wg_v, wu_v, wd_v, stage_gu, stage_d, sems, state):
    j = pl.program_id(0)
    rg = wg_v.shape[1] // _W_ROUNDS
    rd = wd_v.shape[1] // _W_ROUNDS

    def round_copies(e, r, s):
        g_rows = pl.ds(pl.multiple_of(r * rg, rg), rg)
        d_rows = pl.ds(pl.multiple_of(r * rd, rd), rd)
        return (pltpu.make_async_copy(wg_hbm.at[e, g_rows], stage_gu.at[s, 0], sems.at[s, 0]),
                pltpu.make_async_copy(wu_hbm.at[e, g_rows], stage_gu.at[s, 1], sems.at[s, 1]),
                pltpu.make_async_copy(wd_hbm.at[e, d_rows], stage_d.at[s], sems.at[s, 2]))

    def narrow_round(r, s, copy):
        g_rows = pl.ds(pl.multiple_of(r * rg, rg), rg)
        d_rows = pl.ds(pl.multiple_of(r * rd, rd), rd)
        wg_v[copy, g_rows, :] = stage_gu[s, 0].astype(_BF16)
        wu_v[copy, g_rows, :] = stage_gu[s, 1].astype(_BF16)
        wd_v[copy, d_rows, :] = stage_d[s].astype(_BF16)

    @pl.when(j == 0)
    def _():
        state[_ST_CUR] = 0
        state[_ST_LOADED] = -1
        state[_ST_TARGET] = -1
        state[_ST_DONE] = _W_ROUNDS

    active = j < nu_ref[0]
    e = te_ref[j]

    @pl.when(jnp.logical_and(active, e != state[_ST_LOADED]))
    def _():
        other = 1 - state[_ST_CUR]

        @pl.when(state[_ST_TARGET] != e)
        def _():
            state[_ST_TARGET] = e
            state[_ST_DONE] = 0

        def load_round(r, carry):
            cps = round_copies(e, r, 0)
            for cp in cps:
                cp.start()
            for cp in cps:
                cp.wait()
            narrow_round(r, 0, other)
            return carry

        lax.fori_loop(state[_ST_DONE], _W_ROUNDS, load_round, 0)
        nxt = nxt_ref[e]
        state[_ST_CUR] = other
        state[_ST_LOADED] = e
        state[_ST_TARGET] = nxt
        state[_ST_DONE] = jnp.where(nxt != e, 0, _W_ROUNDS)

    cur = state[_ST_CUR]
    target = state[_ST_TARGET]
    done = state[_ST_DONE]
    prefetch = jnp.logical_and(active, done < _W_ROUNDS)

    @pl.when(prefetch)
    def _():
        for s in range(_W_ROUNDS_PER_STEP):
            for cp in round_copies(target, done + s, s):
                cp.start()

    @pl.when(active)
    def _():
        x = x_ref[...]
        gate = _bdot(x, wg_v[cur])
        act = (gate * _sigmoid(gate)) * _bdot(x, wu_v[cur])
        o_ref[...] = _bdot(act.astype(_BF16), wd_v[cur]).astype(o_ref.dtype)

    @pl.when(jnp.logical_not(active))
    def _():
        o_ref[...] = jnp.zeros_like(o_ref)

    @pl.when(prefetch)
    def _():
        for s in range(_W_ROUNDS_PER_STEP):
            for cp in round_copies(target, done + s, s):
                cp.wait()
            narrow_round(done + s, s, 1 - cur)
        state[_ST_DONE] = done + _W_ROUNDS_PER_STEP


def _expert_ffn(xs, tile_expert, n_used, next_expert, wg, wu, wd):
    n_slots, d = xs.shape
    n_experts, _, f = wg.shape
    assert d % (_W_ROUNDS * _BF16_SUBLANES) == 0 and f % (_W_ROUNDS * _BF16_SUBLANES) == 0
    assert _W_ROUNDS % _W_ROUNDS_PER_STEP == 0
    any_spec = pl.BlockSpec(memory_space=pl.ANY)
    return pl.pallas_call(
        _expert_ffn_kernel,
        grid_spec=pltpu.PrefetchScalarGridSpec(
            num_scalar_prefetch=3,
            grid=(n_slots // _FFN_ROWS,),
            in_specs=[
                pl.BlockSpec((_FFN_ROWS, d),
                             lambda j, te, nu, nx: (jnp.maximum(jnp.minimum(j, nu[0] - 1), 0), 0)),
                any_spec, any_spec, any_spec,
            ],
            out_specs=pl.BlockSpec((_FFN_ROWS, d), lambda j, te, nu, nx: (j, 0)),
            scratch_shapes=[
                pltpu.VMEM((2, d, f), _BF16), pltpu.VMEM((2, d, f), _BF16),
                pltpu.VMEM((2, f, d), _BF16),
                pltpu.VMEM((_W_ROUNDS_PER_STEP, 2, d // _W_ROUNDS, f), _F32),
                pltpu.VMEM((_W_ROUNDS_PER_STEP, f // _W_ROUNDS, d), _F32),
                pltpu.SemaphoreType.DMA((_W_ROUNDS_PER_STEP, 3)),
                pltpu.SMEM((4,), jnp.int32),
            ],
        ),
        out_shape=jax.ShapeDtypeStruct((n_slots, d), _BF16),
        compiler_params=_params(1),
        name="expert_ffn",
    )(tile_expert, n_used, next_expert, xs, wg, wu, wd)


def _combine_kernel(seg_ref, pad_ref, loc_ref, ys_ref, ms_ref, x1_ref, g2_ref, nf_ref, o_ref,
                    bufs, sems, *, n_experts):
    i = pl.program_id(0)
    tm = x1_ref.shape[0]
    cur = lax.rem(i, 2)

    def copies(tile, b):
        return _segment_copies(tile, n_experts, seg_ref, pad_ref, loc_ref, ys_ref, bufs.at[b],
                               sems.at[b], False)

    @pl.when(i == 0)
    def _():
        bufs[...] = jnp.zeros_like(bufs)
        _start_copies(copies(0, 0))

    @pl.when(i + 1 < pl.num_programs(0))
    def _():
        _start_copies(copies(i + 1, 1 - cur))

    _wait_copies(copies(i, cur))
    meta = ms_ref[...]
    col = lambda k: meta[:, k:k + 1]
    slot1 = _expert_offset(col(0).astype(jnp.int32), loc_ref, i, n_experts) + col(2).astype(jnp.int32)
    slot2 = _expert_offset(col(1).astype(jnp.int32), loc_ref, i, n_experts) + col(3).astype(jnp.int32)
    slot = lax.broadcasted_iota(jnp.int32, (tm, bufs.shape[1]), 1)
    weights = jnp.where(slot == slot1, col(4), jnp.where(slot == slot2, col(5), 0.0))
    y = _bdot(weights.astype(_BF16), bufs[cur])
    o_ref[...] = _rmsnorm(x1_ref[...] + g2_ref[...] * y, nf_ref[...])


def _combine(ys, meta_s, x1, plan, mods, row_fn, norm_final, n_experts, tm):
    n, d = x1.shape
    seg_start, pad, loc_off = plan
    return pl.pallas_call(
        functools.partial(_combine_kernel, n_experts=n_experts),
        grid_spec=pltpu.PrefetchScalarGridSpec(
            num_scalar_prefetch=3,
            grid=(n // tm,),
            in_specs=[pl.BlockSpec(memory_space=pl.ANY),
                      pl.BlockSpec((tm, _LANES), lambda i, *_: (i, 0)),
                      pl.BlockSpec((tm, d), lambda i, *_: (i, 0)),
                      _mod_spec(d, 5, row_fn),
                      _resident((1, d))],
            out_specs=pl.BlockSpec((tm, d), lambda i, *_: (i, 0)),
            scratch_shapes=[pltpu.VMEM((2, _local_slots(tm, n_experts), d), _BF16),
                            pltpu.SemaphoreType.DMA((2,))],
        ),
        out_shape=jax.ShapeDtypeStruct((n, d), _F32),
        compiler_params=_params(1),
        name="combine",
    )(seg_start, pad, loc_off, ys, meta_s, x1, mods, norm_final)


def kernel(x, c, ctx, c_ctx, w_ada, b_ada, norm_mix, norm_ffn, w_in, w_conv, rpb, w_attn_out,
           w_conv_out, w_gate, b_gate, w_out, w_ffn_gate, w_ffn_up, w_ffn_down, w_router,
           w_exp_gate, w_exp_up, w_exp_down, norm_final):
    batch, seq, d = x.shape
    ctx_len = ctx.shape[1]
    depth = w_ada.shape[0]
    attn_w = _N_HEADS * _HEAD_DIM
    n_experts = w_router.shape[-1]
    rows = seq // _GRID_W
    n_pairs = attn_w // _LANES
    assert depth == 2 and seq % (_Q_ROWS * _GRID_W) == 0 and attn_w % _LANES == 0
    assert n_experts == _F32_SUBLANES

    ctx_row = batch
    cond_rows = -(-(batch + 1) // 8) * 8
    cond = jnp.zeros((cond_rows, d), _F32).at[:batch].set(c).at[ctx_row].set(c_ctx)
    mods_all = _ada(cond, w_ada, b_ada, tn=1536).reshape(depth, cond_rows, 1, 6 * d)

    union, starts, variant_of, patterns = _attn_plan(rows)

    x2d = x.reshape(batch * seq, d)
    c2d = ctx.reshape(batch * ctx_len, d)
    tm_lat = 512
    ctx_row_fn = lambda i: ctx_row

    def lat_rows(tm):
        return lambda i: (i * tm) // seq

    out = None
    for l in range(depth):
        last = l == depth - 1
        mods = mods_all[l]
        wts = {
            "norm_mix": norm_mix[l].reshape(1, d),
            "norm_ffn": norm_ffn[l].reshape(1, d),
            "wconv": w_conv[l],
            "wao": w_attn_out[l].astype(_BF16),
            "wco": w_conv_out[l].astype(_BF16),
            "wgate": w_gate[l].astype(_BF16),
            "bgate": b_gate[l].reshape(1, -1),
            "wout": w_out[l].astype(_BF16),
        }
        w_in_l = w_in[l].astype(_BF16)
        p = _proj(x2d, mods, lat_rows(512), wts["norm_mix"], w_in_l, tm=512)
        if last:
            pc = _proj(c2d, mods, ctx_row_fn, wts["norm_mix"], w_in_l[:, attn_w:3 * attn_w], tm=512)
            kc_col = 0
        else:
            pc = _proj(c2d, mods, ctx_row_fn, wts["norm_mix"], w_in_l, tm=512)
            kc_col = n_pairs
        table = _bias_table(rpb[l], union, tuple(patterns))
        ya, yca = _attention(p, pc, kc_col, table, starts, variant_of, union, batch, seq,
                             ctx_len, with_ctx=not last)
        if not last:
            wts["ffn_gate"] = w_ffn_gate[l // 2].astype(_BF16)
            wts["ffn_up"] = w_ffn_up[l // 2].astype(_BF16)
            wts["ffn_down"] = w_ffn_down[l // 2].astype(_BF16)
            x2d = _merge_ffn(x2d, ya, p, mods, lat_rows(tm_lat), seq, wts, tm=tm_lat)
            c2d = _merge_ffn(c2d, yca, pc, mods, ctx_row_fn, ctx_len, wts, tm=ctx_len)
        else:
            wr = jnp.pad(w_router[l // 2], ((0, 0), (0, _LANES - n_experts)))
            wts["router_hi"] = wr.astype(_BF16)
            wts["router_lo"] = (wr - wts["router_hi"].astype(_F32)).astype(_BF16)
            tm_moe = 512
            n_tok = batch * seq
            n_tiles = n_tok // tm_moe
            x1, h2, meta_s, meta_l, cnt = _merge_router(x2d, ya, p, mods, lat_rows(tm_moe), seq,
                                                        wts, n_experts, tm=tm_moe)
            n_slots = _round_up(_TOP_K * n_tok + n_tiles * n_experts * (_SEG_ALIGN - 1)
                                + n_experts * (_FFN_ROWS - 1), _FFN_ROWS)
            cnt = cnt[:, 0].reshape(n_tiles, n_experts).astype(jnp.int32)
            plan, tails, tile_expert, n_used, next_expert = _routing_plan(cnt, n_slots)
            xs = _dispatch(h2, meta_l, plan, tails, n_slots, n_experts, tm=tm_moe)
            ys = _expert_ffn(xs, tile_expert, n_used, next_expert, w_exp_gate[l // 2],
                             w_exp_up[l // 2], w_exp_down[l // 2])
            out = _combine(ys, meta_s, x1, plan, mods, lat_rows(tm_moe), norm_final.reshape(1, d),
                           n_experts, tm=tm_moe)
    return out.reshape(batch, seq, d)
```

```python
import functools

import numpy as np
import jax
import jax.numpy as jnp
from jax import lax
from jax.experimental import pallas as pl
from jax.experimental.pallas import tpu as pltpu

_F32 = jnp.float32
_BF16 = jnp.bfloat16

_GRID_W = 64
_N_HEADS = 8
_HEAD_DIM = 64
_NA_ROWS = 8
_NA_COLS = 16
_CONV_K = 3
_TOP_K = 2
_EPS = 1e-6
_NEG_INF = -1e30

_LANES = 128
_F32_SUBLANES = 8
_BF16_SUBLANES = 16
_VMEM_LIMIT_BYTES = 56 * 1024 * 1024

_Q_ROWS = 4
_ATTN_UNROLL = 2
_MIX_CHUNK_ROWS = 256
_HEADS_PER_BLOCK = _LANES // _HEAD_DIM
_NT = (((1,), (1,)), ((), ()))
_TN = (((0,), (0,)), ((), ()))


def _params(n_axes):
    return pltpu.CompilerParams(
        dimension_semantics=("arbitrary",) * n_axes,
        vmem_limit_bytes=_VMEM_LIMIT_BYTES)


def _resident(shape):
    nd = len(shape)
    return pl.BlockSpec(shape, lambda *_: (0,) * nd, pipeline_mode=pl.Buffered(1))


def _sigmoid(x):
    return 1.0 / (1.0 + jnp.exp(-x))


def _rmsnorm(x, g):
    ms = jnp.mean(x * x, axis=-1, keepdims=True)
    return (x * lax.rsqrt(ms + _EPS)) * g


def _modulate(x, g, shift, scale):
    return _rmsnorm(x, g) * (1.0 + scale) + shift


def _bdot(a, b):
    return jnp.dot(a, b, preferred_element_type=_F32)


def _indicator(mask):
    return jnp.where(mask, 1.0, 0.0).astype(_BF16)


def _ada_kernel(cond_ref, w_ref, b_ref, o_ref):
    cond = cond_ref[...]
    act = cond * _sigmoid(cond)
    o_ref[...] = jnp.dot(act, w_ref[...], precision=lax.Precision.HIGHEST,
                         preferred_element_type=_F32) + b_ref[...]


def _ada(cond, w_ada, b_ada, tn):
    depth, d, n_out = w_ada.shape
    rows = cond.shape[0]
    return pl.pallas_call(
        _ada_kernel,
        grid=(depth, n_out // tn),
        in_specs=[
            pl.BlockSpec((rows, d), lambda l, j: (0, 0)),
            pl.BlockSpec((None, d, tn), lambda l, j: (l, 0, j)),
            pl.BlockSpec((None, 1, tn), lambda l, j: (l, 0, j)),
        ],
        out_specs=pl.BlockSpec((None, rows, tn), lambda l, j: (l, 0, j)),
        out_shape=jax.ShapeDtypeStruct((depth, rows, n_out), _F32),
        compiler_params=_params(2),
        name="ada",
    )(cond, w_ada, b_ada.reshape(depth, 1, n_out))


def _mod_spec(d, chunk, row_fn):
    return pl.BlockSpec((None, 1, d), lambda i, *_: (row_fn(i), 0, chunk))


def _proj_kernel(x_ref, g_ref, sh_ref, sc_ref, w_ref, o_ref):
    h = _modulate(x_ref[...], g_ref[...], sh_ref[...], sc_ref[...])
    o_ref[...] = _bdot(h.astype(_BF16), w_ref[...]).astype(o_ref.dtype)


def _proj(x2d, mods, row_fn, g, w, tm):
    n, d = x2d.shape
    n_out = w.shape[1]
    return pl.pallas_call(
        _proj_kernel,
        grid=(n // tm,),
        in_specs=[
            pl.BlockSpec((tm, d), lambda i: (i, 0)),
            _resident((1, d)),
            _mod_spec(d, 0, row_fn),
            _mod_spec(d, 1, row_fn),
            _resident(w.shape),
        ],
        out_specs=pl.BlockSpec((tm, n_out), lambda i: (i, 0)),
        out_shape=jax.ShapeDtypeStruct((n, n_out), _BF16),
        compiler_params=_params(1),
        name="proj",
    )(x2d, g, mods, mods, w)


def _bias_table_kernel(rpb_ref, o_ref, *, union, patterns):
    sub = _F32_SUBLANES
    span = _NA_COLS - 1
    rp = rpb_ref[...]
    lane = lax.broadcasted_iota(jnp.int32, (sub, _LANES), 1)
    srow = lax.broadcasted_iota(jnp.int32, (sub, _LANES), 0)
    ck = jnp.bitwise_and(lane, _GRID_W - 1)
    first = lane < _GRID_W
    in_win = []
    for g in range(_GRID_W // sub):
        cs = jnp.clip(srow + (sub * g - _NA_COLS // 2), 0, _GRID_W - _NA_COLS)
        in_win.append(jnp.logical_and(ck >= cs, ck < cs + _NA_COLS))
    first_part = jnp.logical_or(lane <= span, lane >= _LANES - span)
    second_part = jnp.logical_and(lane >= _GRID_W - span, lane <= _GRID_W + span)
    neg = jnp.full((sub, _LANES), _NEG_INF, _F32)

    def row_vec(dr):
        return jnp.broadcast_to(rp[dr + _NA_ROWS - 1:dr + _NA_ROWS, :], (sub, _LANES))

    for v, pat in enumerate(patterns):
        for a in range(_Q_ROWS):
            for jp in range(-(-union // 2)):
                j0, j1 = 2 * jp, 2 * jp + 1
                dr0 = pat[a * union + j0]
                dr1 = pat[a * union + j1] if j1 < union else None
                width = _LANES if j1 < union else _GRID_W
                base = jnp.zeros((sub, _LANES), _F32)
                if dr0 is not None:
                    base = jnp.where(first_part, pltpu.roll(row_vec(dr0), _LANES - span, 1), base)
                if dr1 is not None:
                    base = jnp.where(second_part, pltpu.roll(row_vec(dr1), _GRID_W - span, 1), base)
                for g in range(_GRID_W // sub):
                    rows = slice(a * _GRID_W + sub * g, a * _GRID_W + sub * (g + 1))
                    if dr0 is None and dr1 is None:
                        tile = neg
                    else:
                        tile = pltpu.roll(base, sub * g, 1, stride=1, stride_axis=0)
                        keep = in_win[g]
                        if dr0 is None:
                            keep = jnp.logical_and(keep, jnp.logical_not(first))
                        if dr1 is None:
                            keep = jnp.logical_and(keep, first)
                        tile = jnp.where(keep, tile, neg)
                    o_ref[v, rows, jp * _LANES:jp * _LANES + width] = tile[:, :width]


def _bias_table(rpb_l, union, patterns):
    h, n_dr, n_dc = rpb_l.shape
    assert _LANES == 2 * _GRID_W and n_dc <= _LANES
    n_dr_pad = _round_up(n_dr, _F32_SUBLANES)
    padded = jnp.pad(rpb_l, ((0, 0), (0, n_dr_pad - n_dr), (0, _LANES - n_dc)))
    n_var = len(patterns)
    qt, kt = _Q_ROWS * _GRID_W, union * _GRID_W
    return pl.pallas_call(
        functools.partial(_bias_table_kernel, union=union, patterns=patterns),
        grid=(h,),
        in_specs=[pl.BlockSpec((None, n_dr_pad, _LANES), lambda i: (i, 0, 0))],
        out_specs=pl.BlockSpec((n_var, None, qt, kt), lambda i: (0, i, 0, 0)),
        out_shape=jax.ShapeDtypeStruct((n_var, h, qt, kt), _F32),
        compiler_params=_params(1),
        name="bias_table",
    )(padded)


def _attn_plan(rows):
    kh = min(_NA_ROWS, rows)
    union = min(_Q_ROWS + kh - 1, rows)
    n_blocks = rows // _Q_ROWS
    starts, variant_of, patterns = [], [], []
    for t in range(n_blocks):
        start = int(np.clip(_Q_ROWS * t - kh // 2, 0, rows - union))
        pat = []
        for a in range(_Q_ROWS):
            r = _Q_ROWS * t + a
            rs = int(np.clip(r - kh // 2, 0, rows - kh))
            for j in range(union):
                kr = start + j
                pat.append(kr - r if rs <= kr < rs + kh else None)
        pat = tuple(pat)
        if pat not in patterns:
            patterns.append(pat)
        starts.append(start)
        variant_of.append(patterns.index(pat))
    return union, starts, variant_of, patterns


def _first_head_lanes(shape):
    return lax.broadcasted_iota(jnp.int32, shape, len(shape) - 1) < _HEAD_DIM


def _values_with_ones(v):
    lo = _first_head_lanes(v.shape)
    ones = jnp.ones_like(v)
    return jnp.where(lo, v, ones), jnp.where(lo, ones, v)


def _scores(qb, keys, bias_fns):
    lo = _first_head_lanes(qb.shape)
    out = []
    for hh in range(_HEADS_PER_BLOCK):
        qm = jnp.where(lo if hh == 0 else jnp.logical_not(lo), qb, jnp.zeros_like(qb))
        scores = []
        for kk, bias_fn in zip(keys, bias_fns):
            s = lax.dot_general(qm, kk, _NT, preferred_element_type=_F32)
            scores.append(s if bias_fn is None else s + bias_fn(hh))
        out.append(scores)
    return out


def _probabilities(scores):
    m = functools.reduce(jnp.maximum, [jnp.max(s, axis=-1, keepdims=True) for s in scores])
    return [jnp.exp(s - m).astype(_BF16) for s in scores]


def _weighted_values(probs, values):
    res = []
    for hh in range(_HEADS_PER_BLOCK):
        acc = None
        for e, v1 in zip(probs[hh], values):
            o = _bdot(e, v1[hh])
            acc = o if acc is None else acc + o
        res.append(acc)
    lo = _first_head_lanes(res[0].shape)
    num = jnp.where(lo, res[0], res[1])
    den = pltpu.roll(jnp.where(lo, res[1], res[0]), _HEAD_DIM, 1)
    return num / den


def _attn_kernel(start_ref, var_ref, q_ref, k_ref, v_ref, kc_ref, vc_ref, tab_ref, *rest,
                 n_blocks, union_tokens, with_ctx):
    if with_ctx:
        qc_ref, o_ref, oc_ref, s_loc, s_ctx, e_loc, e_ctx = rest
    else:
        o_ref, s_loc, s_ctx, e_loc, e_ctx = rest
    heads = range(_HEADS_PER_BLOCK)
    qb_tokens = _Q_ROWS * _GRID_W
    scale = _HEAD_DIM ** -0.5
    kc = kc_ref[...]
    vc1 = _values_with_ones(vc_ref[...])

    def key_start(t):
        return pl.multiple_of(start_ref[t] * _GRID_W, _GRID_W)

    def query_start(t):
        return pl.multiple_of(t * qb_tokens, qb_tokens)

    def scores_stage(t):
        slot, var = lax.rem(t, 2), var_ref[t]
        qb = q_ref[pl.ds(query_start(t), qb_tokens), :] * scale
        kb = k_ref[pl.ds(key_start(t), union_tokens), :]
        scores = _scores(qb, [kb, kc], [lambda hh: tab_ref[var, hh], None])
        for hh in heads:
            s_loc[slot, hh], s_ctx[slot, hh] = scores[hh]

    def softmax_stage(t):
        slot = lax.rem(t, 2)
        for hh in heads:
            e_loc[slot, hh], e_ctx[slot, hh] = _probabilities([s_loc[slot, hh], s_ctx[slot, hh]])

    def values_stage(t):
        slot = lax.rem(t, 2)
        vb1 = _values_with_ones(v_ref[pl.ds(key_start(t), union_tokens), :])
        probs = [[e_loc[slot, hh], e_ctx[slot, hh]] for hh in heads]
        o = _weighted_values(probs, [vb1, vc1])
        o_ref[pl.ds(query_start(t), qb_tokens), :] = o.astype(o_ref.dtype)

    assert n_blocks >= 2
    scores_stage(0)
    softmax_stage(0)
    scores_stage(1)

    def body(t, carry):
        values_stage(t - 2)
        softmax_stage(t - 1)
        scores_stage(t)
        return carry

    lax.fori_loop(2, n_blocks, body, 0, unroll=_ATTN_UNROLL)
    values_stage(n_blocks - 2)
    softmax_stage(n_blocks - 1)
    values_stage(n_blocks - 1)

    if with_ctx:
        scores = _scores(qc_ref[...] * scale, [kc], [None])
        probs = [_probabilities(scores[hh]) for hh in heads]
        oc_ref[...] = _weighted_values(probs, [vc1]).astype(oc_ref.dtype)


def _attention(p, pc, kc_col, table, starts, variant_of, union, batch, seq, ctx_len, with_ctx):
    attn_w = _N_HEADS * _HEAD_DIM
    n_pairs = attn_w // _LANES
    n_var = table.shape[0]
    qb_tokens = _Q_ROWS * _GRID_W
    union_tokens = union * _GRID_W
    kern = functools.partial(_attn_kernel, n_blocks=len(starts), union_tokens=union_tokens,
                             with_ctx=with_ctx)
    in_specs = [
        pl.BlockSpec((seq, _LANES), lambda hp, b, *_: (b, hp)),
        pl.BlockSpec((seq, _LANES), lambda hp, b, *_: (b, n_pairs + hp)),
        pl.BlockSpec((seq, _LANES), lambda hp, b, *_: (b, 2 * n_pairs + hp)),
        pl.BlockSpec((ctx_len, _LANES), lambda hp, b, *_: (b, kc_col + hp)),
        pl.BlockSpec((ctx_len, _LANES), lambda hp, b, *_: (b, kc_col + n_pairs + hp)),
        pl.BlockSpec((n_var, _HEADS_PER_BLOCK, qb_tokens, union_tokens),
                     lambda hp, b, *_: (0, hp, 0, 0)),
    ]
    args = [p, p, p, pc, pc, table]
    out_specs = [pl.BlockSpec((seq, _LANES), lambda hp, b, *_: (b, hp))]
    out_shape = [jax.ShapeDtypeStruct((batch * seq, attn_w), _BF16)]
    if with_ctx:
        in_specs.append(pl.BlockSpec((ctx_len, _LANES), lambda hp, b, *_: (b, hp)))
        args.append(pc)
        out_specs.append(pl.BlockSpec((ctx_len, _LANES), lambda hp, b, *_: (b, hp)))
        out_shape.append(jax.ShapeDtypeStruct((batch * ctx_len, attn_w), _BF16))
    outs = pl.pallas_call(
        kern,
        grid_spec=pltpu.PrefetchScalarGridSpec(
            num_scalar_prefetch=2,
            grid=(n_pairs, batch),
            in_specs=in_specs,
            out_specs=out_specs,
            scratch_shapes=[
                pltpu.VMEM((2, _HEADS_PER_BLOCK, qb_tokens, union_tokens), _F32),
                pltpu.VMEM((2, _HEADS_PER_BLOCK, qb_tokens, ctx_len), _F32),
                pltpu.VMEM((2, _HEADS_PER_BLOCK, qb_tokens, union_tokens), _BF16),
                pltpu.VMEM((2, _HEADS_PER_BLOCK, qb_tokens, ctx_len), _BF16),
            ],
        ),
        out_shape=out_shape,
        compiler_params=_params(2),
        name="attn",
    )(jnp.asarray(starts, jnp.int32), jnp.asarray(variant_of, jnp.int32), *args)
    return outs if with_ctx else (outs[0], None)


def _mixer_update(i, x_ref, ya_ref, bcu_ref, prev_ref, next_ref, nm_ref, sh1_ref, sc1_ref, g1_ref,
                  wconv_ref, wao_ref, wco_ref, wgate_ref, bgate_ref, wout_ref, seq_len):
    tm, d = x_ref.shape
    cw = wco_ref.shape[0]

    bcu = bcu_ref[...].astype(_F32)
    bg, z = bcu[:, :cw], bcu[:, cw:2 * cw] * bcu[:, 2 * cw:]
    prev = prev_ref[...].astype(_F32)
    nxt = next_ref[...].astype(_F32)
    has_prev = ((i * tm) % seq_len != 0).astype(_F32)
    has_next = (((i + 1) * tm) % seq_len != 0).astype(_F32)
    last = prev.shape[0] - 1
    z_prev = prev[last:last + 1, cw:2 * cw] * prev[last:last + 1, 2 * cw:] * has_prev
    z_next = nxt[0:1, cw:2 * cw] * nxt[0:1, 2 * cw:] * has_next
    row = lax.broadcasted_iota(jnp.int32, z.shape, 0)
    z_dn = jnp.where(row == 0, z_prev, pltpu.roll(z, 1, 0))
    z_up = jnp.where(row == tm - 1, z_next, pltpu.roll(z, tm - 1, 0))
    wconv = wconv_ref[...]
    y_conv = (bg * (z_dn * wconv[0:1] + z * wconv[1:2] + z_up * wconv[2:3])).astype(_BF16)

    chunks = []
    rc = min(tm, _MIX_CHUNK_ROWS)
    for c in range(tm // rc):
        rows = slice(c * rc, (c + 1) * rc)
        x = x_ref[rows, :]
        hb = _modulate(x, nm_ref[...], sh1_ref[...], sc1_ref[...]).astype(_BF16)
        gates = _sigmoid(_bdot(hb, wgate_ref[...]) + bgate_ref[...])
        a_proj = _bdot(ya_ref[rows, :], wao_ref[...])
        c_proj = _bdot(y_conv[rows, :], wco_ref[...])
        merged = gates[:, :d] * a_proj + gates[:, d:] * c_proj
        chunks.append(x + g1_ref[...] * _bdot(merged.astype(_BF16), wout_ref[...]))
    return jnp.concatenate(chunks, axis=0)


_N_MIX_REFS = 15


def _merge_ffn_kernel(*refs, seq_len):
    mix = refs[:_N_MIX_REFS]
    nf_ref, sh2_ref, sc2_ref, g2_ref, wg_ref, wu_ref, wd_ref, o_ref = refs[_N_MIX_REFS:]
    x1 = _mixer_update(pl.program_id(0), *mix, seq_len=seq_len)
    h2 = _modulate(x1, nf_ref[...], sh2_ref[...], sc2_ref[...]).astype(_BF16)
    gate = _bdot(h2, wg_ref[...])
    act = (gate * _sigmoid(gate)) * _bdot(h2, wu_ref[...])
    o_ref[...] = x1 + g2_ref[...] * _bdot(act.astype(_BF16), wd_ref[...])


def _top2(vals, idx, n_idx, axis):
    v1 = jnp.max(vals, axis=axis, keepdims=True)
    i1 = jnp.min(jnp.where(vals == v1, idx, n_idx), axis=axis, keepdims=True)
    rest = jnp.where(idx == i1, -jnp.inf, vals)
    v2 = jnp.max(rest, axis=axis, keepdims=True)
    i2 = jnp.min(jnp.where(rest == v2, idx, n_idx), axis=axis, keepdims=True)
    return i1, i2, v1, v2


def _pick(idx, sel, vals, axis):
    return jnp.sum(jnp.where(idx == sel, vals, 0.0), axis=axis, keepdims=True)


def _merge_router_kernel(*refs, seq_len, n_experts):
    mix = refs[:_N_MIX_REFS]
    (nf_ref, sh2_ref, sc2_ref, wr_ref,
     x1_ref, h2_ref, ms_ref, ml_ref, cnt_ref) = refs[_N_MIX_REFS:]
    x1 = _mixer_update(pl.program_id(0), *mix, seq_len=seq_len)
    x1_ref[...] = x1
    h2 = _modulate(x1, nf_ref[...], sh2_ref[...], sc2_ref[...])
    h_hi = h2.astype(_BF16)
    h2_ref[...] = h_hi
    tm = h2.shape[0]
    h_lo = (h2 - h_hi.astype(_F32)).astype(_BF16)
    wr = wr_ref[...]
    by_hi = lax.dot_general(wr, h_hi, _NT, preferred_element_type=_F32)
    by_lo = lax.dot_general(wr, h_lo, _NT, preferred_element_type=_F32)
    lt = by_hi[:n_experts] + by_hi[n_experts:] + by_lo[:n_experts]
    row = lax.broadcasted_iota(jnp.int32, (tm, tm), 0)
    col = lax.broadcasted_iota(jnp.int32, (tm, tm), 1)

    logits = jnp.concatenate(
        [lt, jnp.full((_LANES - n_experts, tm), -jnp.inf, _F32)], axis=0).T
    lane = lax.broadcasted_iota(jnp.int32, logits.shape, 1)
    i1, i2, v1, v2 = _top2(logits, lane, _LANES, -1)
    e2 = jnp.exp(v2 - v1)
    den = 1.0 + e2
    sel = _indicator(jnp.logical_or(lane == i1, lane == i2))
    ranks = _bdot(_indicator(col < row), sel)
    fields = (i1.astype(_F32), i2.astype(_F32), _pick(lane, i1, ranks, -1),
              _pick(lane, i2, ranks, -1), 1.0 / den, e2 / den)
    meta = jnp.zeros(logits.shape, _F32)
    for k, f in enumerate(fields):
        meta = jnp.where(lane == k, f, meta)
    ms_ref[...] = meta

    sub = lax.broadcasted_iota(jnp.int32, lt.shape, 0)
    j1, j2, _, _ = _top2(lt, sub, n_experts, 0)
    sel_t = _indicator(jnp.logical_or(sub == j1, sub == j2))
    ranks_t = _bdot(sel_t, _indicator(row < col))
    fields_t = (j1.astype(_F32), j2.astype(_F32), _pick(sub, j1, ranks_t, 0),
                _pick(sub, j2, ranks_t, 0))
    meta_t = jnp.zeros(lt.shape, _F32)
    for k, f in enumerate(fields_t):
        meta_t = jnp.where(sub == k, f, meta_t)
    ml_ref[...] = meta_t
    cnt = jnp.sum(sel_t.astype(_F32), axis=1, keepdims=True)
    cnt_ref[...] = jnp.broadcast_to(cnt, cnt_ref.shape)


def _mixer_specs(n, d, tm, p_cols, row_fn, weights):
    half = p_cols // 2
    sub = _BF16_SUBLANES
    n_sub = n // sub
    specs = [
        pl.BlockSpec((tm, d), lambda i: (i, 0)),
        pl.BlockSpec((tm, weights["wao"].shape[0]), lambda i: (i, 0)),
        pl.BlockSpec((tm, half), lambda i: (i, 1)),
        pl.BlockSpec((sub, half), lambda i: (jnp.maximum(i * (tm // sub) - 1, 0), 1)),
        pl.BlockSpec((sub, half), lambda i: (jnp.minimum((i + 1) * (tm // sub), n_sub - 1), 1)),
        _resident((1, d)),
        _mod_spec(d, 0, row_fn), _mod_spec(d, 1, row_fn), _mod_spec(d, 2, row_fn),
    ]
    for name in ("wconv", "wao", "wco", "wgate", "bgate", "wout"):
        specs.append(_resident(weights[name].shape))
    return specs


def _merge_ffn(x2d, ya, p, mods, row_fn, seq_len, wts, tm):
    n, d = x2d.shape
    specs = _mixer_specs(n, d, tm, p.shape[1], row_fn, wts)
    specs += [_resident((1, d)), _mod_spec(d, 3, row_fn), _mod_spec(d, 4, row_fn),
              _mod_spec(d, 5, row_fn),
              _resident(wts["ffn_gate"].shape), _resident(wts["ffn_up"].shape),
              _resident(wts["ffn_down"].shape)]
    return pl.pallas_call(
        functools.partial(_merge_ffn_kernel, seq_len=seq_len),
        grid=(n // tm,),
        in_specs=specs,
        out_specs=pl.BlockSpec((tm, d), lambda i: (i, 0)),
        out_shape=jax.ShapeDtypeStruct((n, d), _F32),
        compiler_params=_params(1),
        name="merge_ffn",
    )(x2d, ya, p, p, p, wts["norm_mix"], mods, mods, mods,
      wts["wconv"], wts["wao"], wts["wco"], wts["wgate"], wts["bgate"], wts["wout"],
      wts["norm_ffn"], mods, mods, mods, wts["ffn_gate"], wts["ffn_up"], wts["ffn_down"])


def _merge_router(x2d, ya, p, mods, row_fn, seq_len, wts, n_experts, tm):
    n, d = x2d.shape
    specs = _mixer_specs(n, d, tm, p.shape[1], row_fn, wts)
    specs += [_resident((1, d)), _mod_spec(d, 3, row_fn), _mod_spec(d, 4, row_fn),
              _resident(wts["router"].shape)]
    tile = lambda w: pl.BlockSpec((tm, w), lambda i: (i, 0))
    n_tiles = n // tm
    return pl.pallas_call(
        functools.partial(_merge_router_kernel, seq_len=seq_len, n_experts=n_experts),
        grid=(n_tiles,),
        in_specs=specs,
        out_specs=[tile(d), tile(d), tile(_LANES),
                   pl.BlockSpec((n_experts, tm), lambda i: (i, 0)),
                   pl.BlockSpec((n_experts, _LANES), lambda i: (i, 0))],
        out_shape=[jax.ShapeDtypeStruct((n, d), _F32),
                   jax.ShapeDtypeStruct((n, d), _BF16),
                   jax.ShapeDtypeStruct((n, _LANES), _F32),
                   jax.ShapeDtypeStruct((n_tiles * n_experts, tm), _F32),
                   jax.ShapeDtypeStruct((n_tiles * n_experts, _LANES), _F32)],
        compiler_params=_params(1),
        name="merge_router",
    )(x2d, ya, p, p, p, wts["norm_mix"], mods, mods, mods,
      wts["wconv"], wts["wao"], wts["wco"], wts["wgate"], wts["bgate"], wts["wout"],
      wts["norm_ffn"], mods, mods, wts["router"])


_SEG_ALIGN = _BF16_SUBLANES
_FFN_ROWS = 256


def _round_up(v, m):
    return (v + m - 1) // m * m


def _local_slots(tm, n_experts):
    return _round_up(_TOP_K * tm + n_experts * (_SEG_ALIGN - 1), _SEG_ALIGN)


def _routing_plan(cnt, n_slots):
    pad = _round_up(cnt, _SEG_ALIGN)
    total = jnp.sum(pad, axis=0)
    region = _round_up(total, _FFN_ROWS)
    ends = jnp.cumsum(region)
    seg_start = (ends - region)[None, :] + jnp.cumsum(pad, axis=0) - pad
    loc_off = jnp.cumsum(pad, axis=1) - pad
    n_used = (ends[-1] // _FFN_ROWS).astype(jnp.int32).reshape(1)
    tails = jnp.concatenate([ends - region + total, region - total, n_used])
    tile_row = jnp.arange(n_slots // _FFN_ROWS, dtype=jnp.int32) * _FFN_ROWS
    tile_expert = jnp.minimum(jnp.sum(tile_row[:, None] >= ends[None, :], axis=1),
                              cnt.shape[1] - 1).astype(jnp.int32)
    ids = jnp.arange(cnt.shape[1], dtype=jnp.int32)
    later = jnp.logical_and(region[None, :] > 0, ids[None, :] > ids[:, None])
    nxt = jnp.min(jnp.where(later, ids[None, :], cnt.shape[1]), axis=1)
    next_expert = jnp.where(nxt == cnt.shape[1], ids, nxt).astype(jnp.int32)
    flat = lambda a: a.reshape(-1).astype(jnp.int32)
    return ((flat(seg_start), flat(pad), flat(loc_off)), flat(tails), tile_expert, n_used,
            next_expert)


def _segment_copies(i, n_experts, seg_ref, pad_ref, loc_ref, hbm_ref, buf_ref, sem, to_hbm):
    copies = []
    for e in range(n_experts):
        n = pl.multiple_of(pad_ref[i * n_experts + e], _SEG_ALIGN)
        loc = pl.multiple_of(loc_ref[i * n_experts + e], _SEG_ALIGN)
        seg = pl.multiple_of(seg_ref[i * n_experts + e], _SEG_ALIGN)
        local, remote = buf_ref.at[pl.ds(loc, n)], hbm_ref.at[pl.ds(seg, n)]
        src, dst = (local, remote) if to_hbm else (remote, local)
        copies.append((n, pltpu.make_async_copy(src, dst, sem)))
    return copies


def _start_copies(copies):
    for n, cp in copies:
        pl.when(n > 0)(cp.start)


def _wait_copies(copies):
    for n, cp in copies:
        pl.when(n > 0)(cp.wait)


def _run_copies(copies):
    _start_copies(copies)
    _wait_copies(copies)


def _expert_offset(expert, loc_ref, i, n_experts):
    off = jnp.zeros(expert.shape, jnp.int32)
    for e in range(n_experts):
        off = jnp.where(expert == e, loc_ref[i * n_experts + e], off)
    return off


def _dispatch_kernel(seg_ref, pad_ref, loc_ref, tail_ref, h_ref, ml_ref, xs_ref, bufs, zeros, sems,
                     *, n_experts):
    i = pl.program_id(0)
    last = pl.num_programs(0) - 1
    cur = lax.rem(i, 2)

    def copies(tile, b):
        return _segment_copies(tile, n_experts, seg_ref, pad_ref, loc_ref, xs_ref, bufs.at[b],
                               sems.at[b], True)

    @pl.when(i >= 2)
    def _():
        _wait_copies(copies(i - 2, cur))

    tm = h_ref.shape[0]
    meta = ml_ref[...]
    slot1 = _expert_offset(meta[0:1].astype(jnp.int32), loc_ref, i, n_experts) + meta[2:3].astype(jnp.int32)
    slot2 = _expert_offset(meta[1:2].astype(jnp.int32), loc_ref, i, n_experts) + meta[3:4].astype(jnp.int32)
    slot = lax.broadcasted_iota(jnp.int32, (bufs.shape[1], tm), 0)
    perm = _indicator(jnp.logical_or(slot == slot1, slot == slot2))
    bufs[cur] = _bdot(perm, h_ref[...]).astype(bufs.dtype)
    _start_copies(copies(i, cur))

    @pl.when(i == last)
    def _():
        @pl.when(i >= 1)
        def _():
            _wait_copies(copies(i - 1, 1 - cur))

        _wait_copies(copies(i, cur))

        zeros[...] = jnp.zeros_like(zeros)
        fills = []
        for e in range(n_experts):
            start = pl.multiple_of(tail_ref[e], _SEG_ALIGN)
            n = pl.multiple_of(tail_ref[n_experts + e], _SEG_ALIGN)
            fills.append((n, pltpu.make_async_copy(zeros.at[pl.ds(0, n)],
                                                   xs_ref.at[pl.ds(start, n)], sems.at[0])))
        _run_copies(fills)

        def fill_tile(j, carry):
            cp = pltpu.make_async_copy(
                zeros, xs_ref.at[pl.ds(pl.multiple_of(j * _FFN_ROWS, _FFN_ROWS), _FFN_ROWS)],
                sems.at[0])
            cp.start()
            cp.wait()
            return carry

        lax.fori_loop(tail_ref[2 * n_experts], xs_ref.shape[0] // _FFN_ROWS, fill_tile, 0)


def _dispatch(h2, meta_l, plan, tails, n_slots, n_experts, tm):
    n, d = h2.shape
    seg_start, pad, loc_off = plan
    return pl.pallas_call(
        functools.partial(_dispatch_kernel, n_experts=n_experts),
        grid_spec=pltpu.PrefetchScalarGridSpec(
            num_scalar_prefetch=4,
            grid=(n // tm,),
            in_specs=[pl.BlockSpec((tm, d), lambda i, *_: (i, 0)),
                      pl.BlockSpec((n_experts, tm), lambda i, *_: (i, 0))],
            out_specs=pl.BlockSpec(memory_space=pl.ANY),
            scratch_shapes=[pltpu.VMEM((2, _local_slots(tm, n_experts), d), _BF16),
                            pltpu.VMEM((_FFN_ROWS, d), _BF16),
                            pltpu.SemaphoreType.DMA((2,))],
        ),
        out_shape=jax.ShapeDtypeStruct((n_slots, d), _BF16),
        compiler_params=_params(1),
        name="dispatch",
    )(seg_start, pad, loc_off, tails, h2, meta_l)


_W_ROUNDS = 16
_W_ROUNDS_PER_STEP = 2
_ST_CUR, _ST_LOADED, _ST_TARGET, _ST_DONE = range(4)


def _expert_ffn_kernel(te_ref, nu_ref, nxt_ref, x_ref, wg_hbm, wu_hbm, wd_hbm, o_ref,
                       wg_v, wu_v, wd_v, stage_gu, stage_d, sems, state):
    j = pl.program_id(0)
    rg = wg_v.shape[1] // _W_ROUNDS
    rd = wd_v.shape[1] // _W_ROUNDS

    def round_copies(e, r, s):
        g_rows = pl.ds(pl.multiple_of(r * rg, rg), rg)
        d_rows = pl.ds(pl.multiple_of(r * rd, rd), rd)
        return (pltpu.make_async_copy(wg_hbm.at[e, g_rows], stage_gu.at[s, 0], sems.at[s, 0]),
                pltpu.make_async_copy(wu_hbm.at[e, g_rows], stage_gu.at[s, 1], sems.at[s, 1]),
                pltpu.make_async_copy(wd_hbm.at[e, d_rows], stage_d.at[s], sems.at[s, 2]))

    def narrow_round(r, s, copy):
        g_rows = pl.ds(pl.multiple_of(r * rg, rg), rg)
        d_rows = pl.ds(pl.multiple_of(r * rd, rd), rd)
        wg_v[copy, g_rows, :] = stage_gu[s, 0].astype(_BF16)
        wu_v[copy, g_rows, :] = stage_gu[s, 1].astype(_BF16)
        wd_v[copy, d_rows, :] = stage_d[s].astype(_BF16)

    @pl.when(j == 0)
    def _():
        state[_ST_CUR] = 0
        state[_ST_LOADED] = -1
        state[_ST_TARGET] = -1
        state[_ST_DONE] = _W_ROUNDS

    active = j < nu_ref[0]
    e = te_ref[j]

    @pl.when(jnp.logical_and(active, e != state[_ST_LOADED]))
    def _():
        other = 1 - state[_ST_CUR]

        @pl.when(state[_ST_TARGET] != e)
        def _():
            state[_ST_TARGET] = e
            state[_ST_DONE] = 0

        def load_round(r, carry):
            cps = round_copies(e, r, 0)
            for cp in cps:
                cp.start()
            for cp in cps:
                cp.wait()
            narrow_round(r, 0, other)
            return carry

        lax.fori_loop(state[_ST_DONE], _W_ROUNDS, load_round, 0)
        nxt = nxt_ref[e]
        state[_ST_CUR] = other
        state[_ST_LOADED] = e
        state[_ST_TARGET] = nxt
        state[_ST_DONE] = jnp.where(nxt != e, 0, _W_ROUNDS)

    cur = state[_ST_CUR]
    target = state[_ST_TARGET]
    done = state[_ST_DONE]
    prefetch = jnp.logical_and(active, done < _W_ROUNDS)

    @pl.when(prefetch)
    def _():
        for s in range(_W_ROUNDS_PER_STEP):
            for cp in round_copies(target, done + s, s):
                cp.start()

    @pl.when(active)
    def _():
        x = x_ref[...]
        gate = _bdot(x, wg_v[cur])
        act = (gate * _sigmoid(gate)) * _bdot(x, wu_v[cur])
        o_ref[...] = _bdot(act.astype(_BF16), wd_v[cur]).astype(o_ref.dtype)

    @pl.when(jnp.logical_not(active))
    def _():
        o_ref[...] = jnp.zeros_like(o_ref)

    @pl.when(prefetch)
    def _():
        for s in range(_W_ROUNDS_PER_STEP):
            for cp in round_copies(target, done + s, s):
                cp.wait()
            narrow_round(done + s, s, 1 - cur)
        state[_ST_DONE] = done + _W_ROUNDS_PER_STEP


def _expert_ffn(xs, tile_expert, n_used, next_expert, wg, wu, wd):
    n_slots, d = xs.shape
    n_experts, _, f = wg.shape
    assert d % (_W_ROUNDS * _BF16_SUBLANES) == 0 and f % (_W_ROUNDS * _BF16_SUBLANES) == 0
    assert _W_ROUNDS % _W_ROUNDS_PER_STEP == 0
    any_spec = pl.BlockSpec(memory_space=pl.ANY)
    return pl.pallas_call(
        _expert_ffn_kernel,
        grid_spec=pltpu.PrefetchScalarGridSpec(
            num_scalar_prefetch=3,
            grid=(n_slots // _FFN_ROWS,),
            in_specs=[
                pl.BlockSpec((_FFN_ROWS, d),
                             lambda j, te, nu, nx: (jnp.maximum(jnp.minimum(j, nu[0] - 1), 0), 0)),
                any_spec, any_spec, any_spec,
            ],
            out_specs=pl.BlockSpec((_FFN_ROWS, d), lambda j, te, nu, nx: (j, 0)),
            scratch_shapes=[
                pltpu.VMEM((2, d, f), _BF16), pltpu.VMEM((2, d, f), _BF16),
                pltpu.VMEM((2, f, d), _BF16),
                pltpu.VMEM((_W_ROUNDS_PER_STEP, 2, d // _W_ROUNDS, f), _F32),
                pltpu.VMEM((_W_ROUNDS_PER_STEP, f // _W_ROUNDS, d), _F32),
                pltpu.SemaphoreType.DMA((_W_ROUNDS_PER_STEP, 3)),
                pltpu.SMEM((4,), jnp.int32),
            ],
        ),
        out_shape=jax.ShapeDtypeStruct((n_slots, d), _BF16),
        compiler_params=_params(1),
        name="expert_ffn",
    )(tile_expert, n_used, next_expert, xs, wg, wu, wd)


def _combine_kernel(seg_ref, pad_ref, loc_ref, ys_ref, ms_ref, x1_ref, g2_ref, nf_ref, o_ref,
                    bufs, sems, *, n_experts):
    i = pl.program_id(0)
    tm = x1_ref.shape[0]
    cur = lax.rem(i, 2)

    def copies(tile, b):
        return _segment_copies(tile, n_experts, seg_ref, pad_ref, loc_ref, ys_ref, bufs.at[b],
                               sems.at[b], False)

    @pl.when(i == 0)
    def _():
        bufs[...] = jnp.zeros_like(bufs)
        _start_copies(copies(0, 0))

    @pl.when(i + 1 < pl.num_programs(0))
    def _():
        _start_copies(copies(i + 1, 1 - cur))

    _wait_copies(copies(i, cur))
    rc = min(tm, _MIX_CHUNK_ROWS)
    slot = lax.broadcasted_iota(jnp.int32, (rc, bufs.shape[1]), 1)
    for c in range(tm // rc):
        rows = slice(c * rc, (c + 1) * rc)
        meta = ms_ref[rows, :]
        col = lambda k: meta[:, k:k + 1]
        slot1 = _expert_offset(col(0).astype(jnp.int32), loc_ref, i, n_experts) + col(2).astype(jnp.int32)
        slot2 = _expert_offset(col(1).astype(jnp.int32), loc_ref, i, n_experts) + col(3).astype(jnp.int32)
        weights = jnp.where(slot == slot1, col(4), jnp.where(slot == slot2, col(5), 0.0))
        y = _bdot(weights.astype(_BF16), bufs[cur])
        o_ref[rows, :] = _rmsnorm(x1_ref[rows, :] + g2_ref[...] * y, nf_ref[...])


def _combine(ys, meta_s, x1, plan, mods, row_fn, norm_final, n_experts, tm):
    n, d = x1.shape
    seg_start, pad, loc_off = plan
    return pl.pallas_call(
        functools.partial(_combine_kernel, n_experts=n_experts),
        grid_spec=pltpu.PrefetchScalarGridSpec(
            num_scalar_prefetch=3,
            grid=(n // tm,),
            in_specs=[pl.BlockSpec(memory_space=pl.ANY),
                      pl.BlockSpec((tm, _LANES), lambda i, *_: (i, 0)),
                      pl.BlockSpec((tm, d), lambda i, *_: (i, 0)),
                      _mod_spec(d, 5, row_fn),
                      _resident((1, d))],
            out_specs=pl.BlockSpec((tm, d), lambda i, *_: (i, 0)),
            scratch_shapes=[pltpu.VMEM((2, _local_slots(tm, n_experts), d), _BF16),
                            pltpu.SemaphoreType.DMA((2,))],
        ),
        out_shape=jax.ShapeDtypeStruct((n, d), _F32),
        compiler_params=_params(1),
        name="combine",
    )(seg_start, pad, loc_off, ys, meta_s, x1, mods, norm_final)


def kernel(x, c, ctx, c_ctx, w_ada, b_ada, norm_mix, norm_ffn, w_in, w_conv, rpb, w_attn_out,
           w_conv_out, w_gate, b_gate, w_out, w_ffn_gate, w_ffn_up, w_ffn_down, w_router,
           w_exp_gate, w_exp_up, w_exp_down, norm_final):
    batch, seq, d = x.shape
    ctx_len = ctx.shape[1]
    depth = w_ada.shape[0]
    attn_w = _N_HEADS * _HEAD_DIM
    n_experts = w_router.shape[-1]
    rows = seq // _GRID_W
    n_pairs = attn_w // _LANES
    assert depth == 2 and seq % (_Q_ROWS * _GRID_W) == 0 and attn_w % _LANES == 0
    assert n_experts == _F32_SUBLANES

    ctx_row = batch
    cond_rows = -(-(batch + 1) // 8) * 8
    cond = jnp.zeros((cond_rows, d), _F32).at[:batch].set(c).at[ctx_row].set(c_ctx)
    mods_all = _ada(cond, w_ada, b_ada, tn=1536).reshape(depth, cond_rows, 1, 6 * d)

    union, starts, variant_of, patterns = _attn_plan(rows)

    x2d = x.reshape(batch * seq, d)
    c2d = ctx.reshape(batch * ctx_len, d)
    tm_lat = 512
    ctx_row_fn = lambda i: ctx_row

    def lat_rows(tm):
        return lambda i: (i * tm) // seq

    out = None
    for l in range(depth):
        last = l == depth - 1
        mods = mods_all[l]
        wts = {
            "norm_mix": norm_mix[l].reshape(1, d),
            "norm_ffn": norm_ffn[l].reshape(1, d),
            "wconv": w_conv[l],
            "wao": w_attn_out[l].astype(_BF16),
            "wco": w_conv_out[l].astype(_BF16),
            "wgate": w_gate[l].astype(_BF16),
            "bgate": b_gate[l].reshape(1, -1),
            "wout": w_out[l].astype(_BF16),
        }
        w_in_l = w_in[l].astype(_BF16)
        p = _proj(x2d, mods, lat_rows(512), wts["norm_mix"], w_in_l, tm=512)
        if last:
            pc = _proj(c2d, mods, ctx_row_fn, wts["norm_mix"], w_in_l[:, attn_w:3 * attn_w], tm=512)
            kc_col = 0
        else:
            pc = _proj(c2d, mods, ctx_row_fn, wts["norm_mix"], w_in_l, tm=512)
            kc_col = n_pairs
        table = _bias_table(rpb[l], union, tuple(patterns))
        ya, yca = _attention(p, pc, kc_col, table, starts, variant_of, union, batch, seq,
                             ctx_len, with_ctx=not last)
        if not last:
            wts["ffn_gate"] = w_ffn_gate[l // 2].astype(_BF16)
            wts["ffn_up"] = w_ffn_up[l // 2].astype(_BF16)
            wts["ffn_down"] = w_ffn_down[l // 2].astype(_BF16)
            x2d = _merge_ffn(x2d, ya, p, mods, lat_rows(tm_lat), seq, wts, tm=tm_lat)
            c2d = _merge_ffn(c2d, yca, pc, mods, ctx_row_fn, ctx_len, wts, tm=ctx_len)
        else:
            wr_t = w_router[l // 2].T
            wr_hi = wr_t.astype(_BF16)
            wr_lo = (wr_t - wr_hi.astype(_F32)).astype(_BF16)
            wts["router"] = jnp.concatenate([wr_hi, wr_lo], axis=0)
            tm_moe = 512
            n_tok = batch * seq
            n_tiles = n_tok // tm_moe
            x1, h2, meta_s, meta_l, cnt = _merge_router(x2d, ya, p, mods, lat_rows(tm_moe), seq,
                                                        wts, n_experts, tm=tm_moe)
            n_slots = _round_up(_TOP_K * n_tok + n_tiles * n_experts * (_SEG_ALIGN - 1)
                                + n_experts * (_FFN_ROWS - 1), _FFN_ROWS)
            cnt = cnt[:, 0].reshape(n_tiles, n_experts).astype(jnp.int32)
            plan, tails, tile_expert, n_used, next_expert = _routing_plan(cnt, n_slots)
            xs = _dispatch(h2, meta_l, plan, tails, n_slots, n_experts, tm=tm_moe)
            ys = _expert_ffn(xs, tile_expert, n_used, next_expert, w_exp_gate[l // 2],
                             w_exp_up[l // 2], w_exp_down[l // 2])
            out = _combine(ys, meta_s, x1, plan, mods, lat_rows(tm_moe), norm_final.reshape(1, d),
                           n_experts, tm=tm_moe)
    return out.reshape(batch, seq, d)
```

```python
import functools

import numpy as np
import jax
import jax.numpy as jnp
from jax import lax
from jax.experimental import pallas as pl
from jax.experimental.pallas import tpu as pltpu

_F32 = jnp.float32
_BF16 = jnp.bfloat16

_GRID_W = 64
_N_HEADS = 8
_HEAD_DIM = 64
_NA_ROWS = 8
_NA_COLS = 16
_CONV_K = 3
_TOP_K = 2
_EPS = 1e-6
_NEG_INF = -1e30

_LANES = 128
_F32_SUBLANES = 8
_BF16_SUBLANES = 16
_VMEM_LIMIT_BYTES = 56 * 1024 * 1024

_Q_ROWS = 8
_Q_COLS = 16
_K_COLS = 32
_MIX_CHUNK_ROWS = 256
_HEADS_PER_BLOCK = _LANES // _HEAD_DIM
_NT = (((1,), (1,)), ((), ()))
_TN = (((0,), (0,)), ((), ()))


def _params(n_axes):
    return pltpu.CompilerParams(
        dimension_semantics=("arbitrary",) * n_axes,
        vmem_limit_bytes=_VMEM_LIMIT_BYTES)


def _resident(shape):
    nd = len(shape)
    return pl.BlockSpec(shape, lambda *_: (0,) * nd, pipeline_mode=pl.Buffered(1))


def _sigmoid(x):
    return 1.0 / (1.0 + jnp.exp(-x))


def _rmsnorm(x, g):
    ms = jnp.mean(x * x, axis=-1, keepdims=True)
    return (x * lax.rsqrt(ms + _EPS)) * g


def _modulate(x, g, shift, scale):
    return _rmsnorm(x, g) * (1.0 + scale) + shift


def _bdot(a, b):
    return jnp.dot(a, b, preferred_element_type=_F32)


def _indicator(mask):
    return jnp.where(mask, 1.0, 0.0).astype(_BF16)


def _ada_kernel(cond_ref, w_ref, b_ref, o_ref):
    cond = cond_ref[...]
    act = cond * _sigmoid(cond)
    o_ref[...] = jnp.dot(act, w_ref[...], precision=lax.Precision.HIGHEST,
                         preferred_element_type=_F32) + b_ref[...]


def _ada(cond, w_ada, b_ada, tn):
    depth, d, n_out = w_ada.shape
    rows = cond.shape[0]
    return pl.pallas_call(
        _ada_kernel,
        grid=(depth, n_out // tn),
        in_specs=[
            pl.BlockSpec((rows, d), lambda l, j: (0, 0)),
            pl.BlockSpec((None, d, tn), lambda l, j: (l, 0, j)),
            pl.BlockSpec((None, 1, tn), lambda l, j: (l, 0, j)),
        ],
        out_specs=pl.BlockSpec((None, rows, tn), lambda l, j: (l, 0, j)),
        out_shape=jax.ShapeDtypeStruct((depth, rows, n_out), _F32),
        compiler_params=_params(2),
        name="ada",
    )(cond, w_ada, b_ada.reshape(depth, 1, n_out))


def _mod_spec(d, chunk, row_fn):
    return pl.BlockSpec((None, 1, d), lambda i, *_: (row_fn(i), 0, chunk))


def _proj_kernel(x_ref, g_ref, sh_ref, sc_ref, w_ref, o_ref):
    h = _modulate(x_ref[...], g_ref[...], sh_ref[...], sc_ref[...])
    o_ref[...] = _bdot(h.astype(_BF16), w_ref[...]).astype(o_ref.dtype)


def _proj(x2d, mods, row_fn, g, w, tm):
    n, d = x2d.shape
    n_out = w.shape[1]
    return pl.pallas_call(
        _proj_kernel,
        grid=(n // tm,),
        in_specs=[
            pl.BlockSpec((tm, d), lambda i: (i, 0)),
            _resident((1, d)),
            _mod_spec(d, 0, row_fn),
            _mod_spec(d, 1, row_fn),
            _resident(w.shape),
        ],
        out_specs=pl.BlockSpec((tm, n_out), lambda i: (i, 0)),
        out_shape=jax.ShapeDtypeStruct((n, n_out), _BF16),
        compiler_params=_params(1),
        name="proj",
    )(x2d, g, mods, mods, w)


def _bias_table_kernel(rpb_ref, o_ref, *, union, patterns, col_starts):
    sub = _F32_SUBLANES
    span = _NA_COLS - 1
    slots = _LANES // _K_COLS
    rp = rpb_ref[...]
    lane = lax.broadcasted_iota(jnp.int32, (sub, _LANES), 1)
    srow = lax.broadcasted_iota(jnp.int32, (sub, _LANES), 0)
    ck_local = jnp.bitwise_and(lane, _K_COLS - 1)
    slot_of_lane = jnp.right_shift(lane, _K_COLS.bit_length() - 1)
    neg = jnp.full((sub, _LANES), _NEG_INF, _F32)

    def row_vec(dr):
        return jnp.broadcast_to(rp[dr + _NA_ROWS - 1:dr + _NA_ROWS, :], (sub, _LANES))

    for g, c0 in enumerate(col_starts):
        for gg in range(_Q_COLS // sub):
            k0 = c0 - _Q_COLS * g - sub * gg + span
            lo, hi = max(-(sub - 1), -k0), min(_K_COLS - 1, 2 * span - k0)
            cq = srow + (_Q_COLS * g + sub * gg)
            cs = jnp.clip(cq - _NA_COLS // 2, 0, _GRID_W - _NA_COLS)
            ck = ck_local + c0
            in_win = jnp.logical_and(ck >= cs, ck < cs + _NA_COLS)
            for v, pat in enumerate(patterns):
                for a in range(_Q_ROWS):
                    rows = slice(a * _Q_COLS + sub * gg, a * _Q_COLS + sub * (gg + 1))
                    for quad in range(-(-union // slots)):
                        js = [quad * slots + jq for jq in range(slots)]
                        drs = [pat[a * union + j] if j < union else None for j in js]
                        width = min(slots, union - quad * slots) * _K_COLS
                        cols = slice(quad * _LANES, quad * _LANES + width)
                        if all(dr is None for dr in drs):
                            o_ref[v, g, rows, cols] = neg[:, :width]
                            continue
                        base = jnp.zeros((sub, _LANES), _F32)
                        keep = None
                        for jq, dr in enumerate(drs):
                            if dr is None:
                                continue
                            rel = jnp.bitwise_and(lane - (_K_COLS * jq + lo), _LANES - 1)
                            shifted = pltpu.roll(row_vec(dr), (_K_COLS * jq - k0) % _LANES, 1)
                            base = jnp.where(rel <= hi - lo, shifted, base)
                            here = slot_of_lane == jq
                            keep = here if keep is None else jnp.logical_or(keep, here)
                        tile = pltpu.roll(base, 0, 1, stride=1, stride_axis=0)
                        tile = jnp.where(jnp.logical_and(in_win, keep), tile, neg)
                        o_ref[v, g, rows, cols] = tile[:, :width]


def _bias_table(rpb_l, union, patterns, col_starts):
    h, n_dr, n_dc = rpb_l.shape
    assert n_dc <= _LANES and _LANES % _K_COLS == 0
    n_dr_pad = _round_up(n_dr, _F32_SUBLANES)
    padded = jnp.pad(rpb_l, ((0, 0), (0, n_dr_pad - n_dr), (0, _LANES - n_dc)))
    n_var, n_grp = len(patterns), len(col_starts)
    qt, kt = _Q_ROWS * _Q_COLS, union * _K_COLS
    return pl.pallas_call(
        functools.partial(_bias_table_kernel, union=union, patterns=patterns,
                          col_starts=col_starts),
        grid=(h,),
        in_specs=[pl.BlockSpec((None, n_dr_pad, _LANES), lambda i: (i, 0, 0))],
        out_specs=pl.BlockSpec((n_var, n_grp, None, qt, kt), lambda i: (0, 0, i, 0, 0)),
        out_shape=jax.ShapeDtypeStruct((n_var, n_grp, h, qt, kt), _F32),
        compiler_params=_params(1),
        name="bias_table",
    )(padded)


def _attn_plan(rows):
    kh = min(_NA_ROWS, rows)
    union = min(_Q_ROWS + kh - 1, rows)
    n_blocks = rows // _Q_ROWS
    starts, variant_of, patterns = [], [], []
    for t in range(n_blocks):
        start = int(np.clip(_Q_ROWS * t - kh // 2, 0, rows - union))
        pat = []
        for a in range(_Q_ROWS):
            r = _Q_ROWS * t + a
            rs = int(np.clip(r - kh // 2, 0, rows - kh))
            for j in range(union):
                kr = start + j
                pat.append(kr - r if rs <= kr < rs + kh else None)
        pat = tuple(pat)
        if pat not in patterns:
            patterns.append(pat)
        starts.append(start)
        variant_of.append(patterns.index(pat))
    col_starts = []
    for g in range(_GRID_W // _Q_COLS):
        c0 = int(np.clip(_Q_COLS * g - _NA_COLS // 2, 0, _GRID_W - _K_COLS))
        for cq in range(_Q_COLS * g, _Q_COLS * (g + 1)):
            cs = int(np.clip(cq - _NA_COLS // 2, 0, _GRID_W - _NA_COLS))
            assert c0 <= cs and cs + _NA_COLS <= c0 + _K_COLS
        col_starts.append(c0)
    return union, starts, variant_of, tuple(patterns), tuple(col_starts)


def _first_head_lanes(shape):
    return lax.broadcasted_iota(jnp.int32, shape, len(shape) - 1) < _HEAD_DIM


def _values_with_ones(v):
    lo = _first_head_lanes(v.shape)
    ones = jnp.ones_like(v)
    return jnp.where(lo, v, ones), jnp.where(lo, ones, v)


def _scores(qb, keys, bias_fns):
    lo = _first_head_lanes(qb.shape)
    out = []
    for hh in range(_HEADS_PER_BLOCK):
        qm = jnp.where(lo if hh == 0 else jnp.logical_not(lo), qb, jnp.zeros_like(qb))
        scores = []
        for kk, bias_fn in zip(keys, bias_fns):
            s = lax.dot_general(qm, kk, _NT, preferred_element_type=_F32)
            scores.append(s if bias_fn is None else s + bias_fn(hh))
        out.append(scores)
    return out


def _probabilities(scores):
    m = functools.reduce(jnp.maximum, [jnp.max(s, axis=-1, keepdims=True) for s in scores])
    return [jnp.exp(s - m).astype(_BF16) for s in scores]


def _weighted_values(probs, values):
    res = []
    for hh in range(_HEADS_PER_BLOCK):
        acc = None
        for e, v1 in zip(probs[hh], values):
            o = _bdot(e, v1[hh])
            acc = o if acc is None else acc + o
        res.append(acc)
    lo = _first_head_lanes(res[0].shape)
    num = jnp.where(lo, res[0], res[1])
    den = pltpu.roll(jnp.where(lo, res[1], res[0]), _HEAD_DIM, 1)
    return num / den


def _attn_kernel(start_ref, var_ref, q_ref, k_ref, v_ref, kc_ref, vc_ref, tab_ref, *rest,
                 n_row_blocks, union, col_starts, with_ctx):
    if with_ctx:
        qc_ref, o_ref, oc_ref, s_loc, s_ctx, e_loc, e_ctx = rest
    else:
        o_ref, s_loc, s_ctx, e_loc, e_ctx = rest
    heads = range(_HEADS_PER_BLOCK)
    n_groups = len(col_starts)
    assert n_groups % 2 == 0 and n_groups >= 2 and n_row_blocks >= 1
    scale = _HEAD_DIM ** -0.5
    kc = kc_ref[...]
    vc1 = _values_with_ones(vc_ref[...])

    def gather(ref, first_row, n_rows, col, width):
        return jnp.concatenate(
            [ref[pl.ds(first_row + r * _GRID_W + col, width), :] for r in range(n_rows)], axis=0)

    def key_rows(t):
        return pl.multiple_of(start_ref[t] * _GRID_W, _GRID_W)

    def query_rows(t):
        return pl.multiple_of(t * (_Q_ROWS * _GRID_W), _Q_ROWS * _GRID_W)

    def scores_stage(t, g):
        slot, var = g % 2, var_ref[t]
        qb = gather(q_ref, query_rows(t), _Q_ROWS, _Q_COLS * g, _Q_COLS) * scale
        kb = gather(k_ref, key_rows(t), union, col_starts[g], _K_COLS)
        scores = _scores(qb, [kb, kc], [lambda hh: tab_ref[var, g, hh], None])
        for hh in heads:
            s_loc[slot, hh], s_ctx[slot, hh] = scores[hh]

    def softmax_stage(g):
        slot = g % 2
        for hh in heads:
            e_loc[slot, hh], e_ctx[slot, hh] = _probabilities([s_loc[slot, hh], s_ctx[slot, hh]])

    def values_stage(t, g):
        slot = g % 2
        vb1 = _values_with_ones(gather(v_ref, key_rows(t), union, col_starts[g], _K_COLS))
        probs = [[e_loc[slot, hh], e_ctx[slot, hh]] for hh in heads]
        o = _weighted_values(probs, [vb1, vc1]).astype(o_ref.dtype)
        for a in range(_Q_ROWS):
            o_ref[pl.ds(query_rows(t) + a * _GRID_W + _Q_COLS * g, _Q_COLS), :] = (
                o[a * _Q_COLS:(a + 1) * _Q_COLS])

    def issue(t, k):
        values_stage(t + k // n_groups, k % n_groups)
        softmax_stage((k + 1) % n_groups)
        scores_stage(t + (k + 2) // n_groups, (k + 2) % n_groups)

    scores_stage(0, 0)
    softmax_stage(0)
    scores_stage(0, 1)

    def body(t, carry):
        for k in range(n_groups):
            issue(t, k)
        return carry

    last = n_row_blocks - 1
    lax.fori_loop(0, last, body, 0)
    for k in range(n_groups - 2):
        issue(last, k)
    values_stage(last, n_groups - 2)
    softmax_stage(n_groups - 1)
    values_stage(last, n_groups - 1)

    if with_ctx:
        scores = _scores(qc_ref[...] * scale, [kc], [None])
        probs = [_probabilities(scores[hh]) for hh in heads]
        oc_ref[...] = _weighted_values(probs, [vc1]).astype(oc_ref.dtype)


def _attention(p, pc, kc_col, table, starts, variant_of, union, col_starts, batch, seq, ctx_len,
               with_ctx):
    attn_w = _N_HEADS * _HEAD_DIM
    n_pairs = attn_w // _LANES
    n_var, n_grp = table.shape[:2]
    qb_tokens = _Q_ROWS * _Q_COLS
    union_tokens = union * _K_COLS
    kern = functools.partial(_attn_kernel, n_row_blocks=len(starts), union=union,
                             col_starts=col_starts, with_ctx=with_ctx)
    in_specs = [
        pl.BlockSpec((seq, _LANES), lambda hp, b, *_: (b, hp)),
        pl.BlockSpec((seq, _LANES), lambda hp, b, *_: (b, n_pairs + hp)),
        pl.BlockSpec((seq, _LANES), lambda hp, b, *_: (b, 2 * n_pairs + hp)),
        pl.BlockSpec((ctx_len, _LANES), lambda hp, b, *_: (b, kc_col + hp)),
        pl.BlockSpec((ctx_len, _LANES), lambda hp, b, *_: (b, kc_col + n_pairs + hp)),
        pl.BlockSpec((n_var, n_grp, _HEADS_PER_BLOCK, qb_tokens, union_tokens),
                     lambda hp, b, *_: (0, 0, hp, 0, 0)),
    ]
    args = [p, p, p, pc, pc, table]
    out_specs = [pl.BlockSpec((seq, _LANES), lambda hp, b, *_: (b, hp))]
    out_shape = [jax.ShapeDtypeStruct((batch * seq, attn_w), _BF16)]
    if with_ctx:
        in_specs.append(pl.BlockSpec((ctx_len, _LANES), lambda hp, b, *_: (b, hp)))
        args.append(pc)
        out_specs.append(pl.BlockSpec((ctx_len, _LANES), lambda hp, b, *_: (b, hp)))
        out_shape.append(jax.ShapeDtypeStruct((batch * ctx_len, attn_w), _BF16))
    outs = pl.pallas_call(
        kern,
        grid_spec=pltpu.PrefetchScalarGridSpec(
            num_scalar_prefetch=2,
            grid=(n_pairs, batch),
            in_specs=in_specs,
            out_specs=out_specs,
            scratch_shapes=[
                pltpu.VMEM((2, _HEADS_PER_BLOCK, qb_tokens, union_tokens), _F32),
                pltpu.VMEM((2, _HEADS_PER_BLOCK, qb_tokens, ctx_len), _F32),
                pltpu.VMEM((2, _HEADS_PER_BLOCK, qb_tokens, union_tokens), _BF16),
                pltpu.VMEM((2, _HEADS_PER_BLOCK, qb_tokens, ctx_len), _BF16),
            ],
        ),
        out_shape=out_shape,
        compiler_params=_params(2),
        name="attn",
    )(jnp.asarray(starts, jnp.int32), jnp.asarray(variant_of, jnp.int32), *args)
    return outs if with_ctx else (outs[0], None)


def _mixer_update(i, x_ref, ya_ref, bcu_ref, prev_ref, next_ref, nm_ref, sh1_ref, sc1_ref, g1_ref,
                  wconv_ref, wao_ref, wco_ref, wgate_ref, bgate_ref, wout_ref, seq_len):
    tm, d = x_ref.shape
    cw = wco_ref.shape[0]

    bcu = bcu_ref[...].astype(_F32)
    bg, z = bcu[:, :cw], bcu[:, cw:2 * cw] * bcu[:, 2 * cw:]
    prev = prev_ref[...].astype(_F32)
    nxt = next_ref[...].astype(_F32)
    has_prev = ((i * tm) % seq_len != 0).astype(_F32)
    has_next = (((i + 1) * tm) % seq_len != 0).astype(_F32)
    last = prev.shape[0] - 1
    z_prev = prev[last:last + 1, cw:2 * cw] * prev[last:last + 1, 2 * cw:] * has_prev
    z_next = nxt[0:1, cw:2 * cw] * nxt[0:1, 2 * cw:] * has_next
    row = lax.broadcasted_iota(jnp.int32, z.shape, 0)
    z_dn = jnp.where(row == 0, z_prev, pltpu.roll(z, 1, 0))
    z_up = jnp.where(row == tm - 1, z_next, pltpu.roll(z, tm - 1, 0))
    wconv = wconv_ref[...]
    y_conv = (bg * (z_dn * wconv[0:1] + z * wconv[1:2] + z_up * wconv[2:3])).astype(_BF16)

    chunks = []
    rc = min(tm, _MIX_CHUNK_ROWS)
    for c in range(tm // rc):
        rows = slice(c * rc, (c + 1) * rc)
        x = x_ref[rows, :]
        hb = _modulate(x, nm_ref[...], sh1_ref[...], sc1_ref[...]).astype(_BF16)
        gates = _sigmoid(_bdot(hb, wgate_ref[...]) + bgate_ref[...])
        a_proj = _bdot(ya_ref[rows, :], wao_ref[...])
        c_proj = _bdot(y_conv[rows, :], wco_ref[...])
        merged = gates[:, :d] * a_proj + gates[:, d:] * c_proj
        chunks.append(x + g1_ref[...] * _bdot(merged.astype(_BF16), wout_ref[...]))
    return jnp.concatenate(chunks, axis=0)


_N_MIX_REFS = 15


def _merge_ffn_kernel(*refs, seq_len):
    mix = refs[:_N_MIX_REFS]
    nf_ref, sh2_ref, sc2_ref, g2_ref, wg_ref, wu_ref, wd_ref, o_ref = refs[_N_MIX_REFS:]
    x1 = _mixer_update(pl.program_id(0), *mix, seq_len=seq_len)
    h2 = _modulate(x1, nf_ref[...], sh2_ref[...], sc2_ref[...]).astype(_BF16)
    gate = _bdot(h2, wg_ref[...])
    act = (gate * _sigmoid(gate)) * _bdot(h2, wu_ref[...])
    o_ref[...] = x1 + g2_ref[...] * _bdot(act.astype(_BF16), wd_ref[...])


def _top2(vals, idx, n_idx, axis):
    v1 = jnp.max(vals, axis=axis, keepdims=True)
    i1 = jnp.min(jnp.where(vals == v1, idx, n_idx), axis=axis, keepdims=True)
    rest = jnp.where(idx == i1, -jnp.inf, vals)
    v2 = jnp.max(rest, axis=axis, keepdims=True)
    i2 = jnp.min(jnp.where(rest == v2, idx, n_idx), axis=axis, keepdims=True)
    return i1, i2, v1, v2


def _pick(idx, sel, vals, axis):
    return jnp.sum(jnp.where(idx == sel, vals, 0.0), axis=axis, keepdims=True)


def _merge_router_kernel(*refs, seq_len, n_experts):
    mix = refs[:_N_MIX_REFS]
    (nf_ref, sh2_ref, sc2_ref, wr_ref,
     x1_ref, h2_ref, ms_ref, ml_ref, cnt_ref) = refs[_N_MIX_REFS:]
    x1 = _mixer_update(pl.program_id(0), *mix, seq_len=seq_len)
    x1_ref[...] = x1
    h2 = _modulate(x1, nf_ref[...], sh2_ref[...], sc2_ref[...])
    h_hi = h2.astype(_BF16)
    h2_ref[...] = h_hi
    tm = h2.shape[0]
    h_lo = (h2 - h_hi.astype(_F32)).astype(_BF16)
    wr = wr_ref[...]
    by_hi = lax.dot_general(wr, h_hi, _NT, preferred_element_type=_F32)
    by_lo = lax.dot_general(wr, h_lo, _NT, preferred_element_type=_F32)
    lt = by_hi[:n_experts] + by_hi[n_experts:] + by_lo[:n_experts]
    row = lax.broadcasted_iota(jnp.int32, (tm, tm), 0)
    col = lax.broadcasted_iota(jnp.int32, (tm, tm), 1)

    logits = jnp.concatenate(
        [lt, jnp.full((_LANES - n_experts, tm), -jnp.inf, _F32)], axis=0).T
    lane = lax.broadcasted_iota(jnp.int32, logits.shape, 1)
    i1, i2, v1, v2 = _top2(logits, lane, _LANES, -1)
    e2 = jnp.exp(v2 - v1)
    den = 1.0 + e2
    sel = _indicator(jnp.logical_or(lane == i1, lane == i2))
    ranks = _bdot(_indicator(col < row), sel)
    fields = (i1.astype(_F32), i2.astype(_F32), _pick(lane, i1, ranks, -1),
              _pick(lane, i2, ranks, -1), 1.0 / den, e2 / den)
    meta = jnp.zeros(logits.shape, _F32)
    for k, f in enumerate(fields):
        meta = jnp.where(lane == k, f, meta)
    ms_ref[...] = meta

    sub = lax.broadcasted_iota(jnp.int32, lt.shape, 0)
    j1, j2, _, _ = _top2(lt, sub, n_experts, 0)
    sel_t = _indicator(jnp.logical_or(sub == j1, sub == j2))
    ranks_t = _bdot(sel_t, _indicator(row < col))
    fields_t = (j1.astype(_F32), j2.astype(_F32), _pick(sub, j1, ranks_t, 0),
                _pick(sub, j2, ranks_t, 0))
    meta_t = jnp.zeros(lt.shape, _F32)
    for k, f in enumerate(fields_t):
        meta_t = jnp.where(sub == k, f, meta_t)
    ml_ref[...] = meta_t
    cnt = jnp.sum(sel_t.astype(_F32), axis=1, keepdims=True)
    cnt_ref[...] = jnp.broadcast_to(cnt, cnt_ref.shape)


def _mixer_specs(n, d, tm, p_cols, row_fn, weights):
    half = p_cols // 2
    sub = _BF16_SUBLANES
    n_sub = n // sub
    specs = [
        pl.BlockSpec((tm, d), lambda i: (i, 0)),
        pl.BlockSpec((tm, weights["wao"].shape[0]), lambda i: (i, 0)),
        pl.BlockSpec((tm, half), lambda i: (i, 1)),
        pl.BlockSpec((sub, half), lambda i: (jnp.maximum(i * (tm // sub) - 1, 0), 1)),
        pl.BlockSpec((sub, half), lambda i: (jnp.minimum((i + 1) * (tm // sub), n_sub - 1), 1)),
        _resident((1, d)),
        _mod_spec(d, 0, row_fn), _mod_spec(d, 1, row_fn), _mod_spec(d, 2, row_fn),
    ]
    for name in ("wconv", "wao", "wco", "wgate", "bgate", "wout"):
        specs.append(_resident(weights[name].shape))
    return specs


def _merge_ffn(x2d, ya, p, mods, row_fn, seq_len, wts, tm):
    n, d = x2d.shape
    specs = _mixer_specs(n, d, tm, p.shape[1], row_fn, wts)
    specs += [_resident((1, d)), _mod_spec(d, 3, row_fn), _mod_spec(d, 4, row_fn),
              _mod_spec(d, 5, row_fn),
              _resident(wts["ffn_gate"].shape), _resident(wts["ffn_up"].shape),
              _resident(wts["ffn_down"].shape)]
    return pl.pallas_call(
        functools.partial(_merge_ffn_kernel, seq_len=seq_len),
        grid=(n // tm,),
        in_specs=specs,
        out_specs=pl.BlockSpec((tm, d), lambda i: (i, 0)),
        out_shape=jax.ShapeDtypeStruct((n, d), _F32),
        compiler_params=_params(1),
        name="merge_ffn",
    )(x2d, ya, p, p, p, wts["norm_mix"], mods, mods, mods,
      wts["wconv"], wts["wao"], wts["wco"], wts["wgate"], wts["bgate"], wts["wout"],
      wts["norm_ffn"], mods, mods, mods, wts["ffn_gate"], wts["ffn_up"], wts["ffn_down"])


def _merge_router(x2d, ya, p, mods, row_fn, seq_len, wts, n_experts, tm):
    n, d = x2d.shape
    specs = _mixer_specs(n, d, tm, p.shape[1], row_fn, wts)
    specs += [_resident((1, d)), _mod_spec(d, 3, row_fn), _mod_spec(d, 4, row_fn),
              _resident(wts["router"].shape)]
    tile = lambda w: pl.BlockSpec((tm, w), lambda i: (i, 0))
    n_tiles = n // tm
    return pl.pallas_call(
        functools.partial(_merge_router_kernel, seq_len=seq_len, n_experts=n_experts),
        grid=(n_tiles,),
        in_specs=specs,
        out_specs=[tile(d), tile(d), tile(_LANES),
                   pl.BlockSpec((n_experts, tm), lambda i: (i, 0)),
                   pl.BlockSpec((n_experts, _LANES), lambda i: (i, 0))],
        out_shape=[jax.ShapeDtypeStruct((n, d), _F32),
                   jax.ShapeDtypeStruct((n, d), _BF16),
                   jax.ShapeDtypeStruct((n, _LANES), _F32),
                   jax.ShapeDtypeStruct((n_tiles * n_experts, tm), _F32),
                   jax.ShapeDtypeStruct((n_tiles * n_experts, _LANES), _F32)],
        compiler_params=_params(1),
        name="merge_router",
    )(x2d, ya, p, p, p, wts["norm_mix"], mods, mods, mods,
      wts["wconv"], wts["wao"], wts["wco"], wts["wgate"], wts["bgate"], wts["wout"],
      wts["norm_ffn"], mods, mods, wts["router"])


_SEG_ALIGN = _BF16_SUBLANES
_FFN_ROWS = 256


def _round_up(v, m):
    return (v + m - 1) // m * m


def _local_slots(tm, n_experts):
    return _round_up(_TOP_K * tm + n_experts * (_SEG_ALIGN - 1), _SEG_ALIGN)


def _routing_plan(cnt, n_slots):
    pad = _round_up(cnt, _SEG_ALIGN)
    total = jnp.sum(pad, axis=0)
    region = _round_up(total, _FFN_ROWS)
    ends = jnp.cumsum(region)
    seg_start = (ends - region)[None, :] + jnp.cumsum(pad, axis=0) - pad
    loc_off = jnp.cumsum(pad, axis=1) - pad
    n_used = (ends[-1] // _FFN_ROWS).astype(jnp.int32).reshape(1)
    tails = jnp.concatenate([ends - region + total, region - total, n_used])
    tile_row = jnp.arange(n_slots // _FFN_ROWS, dtype=jnp.int32) * _FFN_ROWS
    tile_expert = jnp.minimum(jnp.sum(tile_row[:, None] >= ends[None, :], axis=1),
                              cnt.shape[1] - 1).astype(jnp.int32)
    ids = jnp.arange(cnt.shape[1], dtype=jnp.int32)
    later = jnp.logical_and(region[None, :] > 0, ids[None, :] > ids[:, None])
    nxt = jnp.min(jnp.where(later, ids[None, :], cnt.shape[1]), axis=1)
    next_expert = jnp.where(nxt == cnt.shape[1], ids, nxt).astype(jnp.int32)
    flat = lambda a: a.reshape(-1).astype(jnp.int32)
    return ((flat(seg_start), flat(pad), flat(loc_off)), flat(tails), tile_expert, n_used,
            next_expert)


def _segment_copies(i, n_experts, seg_ref, pad_ref, loc_ref, hbm_ref, buf_ref, sem, to_hbm):
    copies = []
    for e in range(n_experts):
        n = pl.multiple_of(pad_ref[i * n_experts + e], _SEG_ALIGN)
        loc = pl.multiple_of(loc_ref[i * n_experts + e], _SEG_ALIGN)
        seg = pl.multiple_of(seg_ref[i * n_experts + e], _SEG_ALIGN)
        local, remote = buf_ref.at[pl.ds(loc, n)], hbm_ref.at[pl.ds(seg, n)]
        src, dst = (local, remote) if to_hbm else (remote, local)
        copies.append((n, pltpu.make_async_copy(src, dst, sem)))
    return copies


def _start_copies(copies):
    for n, cp in copies:
        pl.when(n > 0)(cp.start)


def _wait_copies(copies):
    for n, cp in copies:
        pl.when(n > 0)(cp.wait)


def _run_copies(copies):
    _start_copies(copies)
    _wait_copies(copies)


def _expert_offset(expert, loc_ref, i, n_experts):
    off = jnp.zeros(expert.shape, jnp.int32)
    for e in range(n_experts):
        off = jnp.where(expert == e, loc_ref[i * n_experts + e], off)
    return off


def _dispatch_kernel(seg_ref, pad_ref, loc_ref, tail_ref, h_ref, ml_ref, xs_ref, bufs, zeros, sems,
                     *, n_experts):
    i = pl.program_id(0)
    last = pl.num_programs(0) - 1
    cur = lax.rem(i, 2)

    def copies(tile, b):
        return _segment_copies(tile, n_experts, seg_ref, pad_ref, loc_ref, xs_ref, bufs.at[b],
                               sems.at[b], True)

    @pl.when(i >= 2)
    def _():
        _wait_copies(copies(i - 2, cur))

    tm = h_ref.shape[0]
    meta = ml_ref[...]
    slot1 = _expert_offset(meta[0:1].astype(jnp.int32), loc_ref, i, n_experts) + meta[2:3].astype(jnp.int32)
    slot2 = _expert_offset(meta[1:2].astype(jnp.int32), loc_ref, i, n_experts) + meta[3:4].astype(jnp.int32)
    slot = lax.broadcasted_iota(jnp.int32, (bufs.shape[1], tm), 0)
    perm = _indicator(jnp.logical_or(slot == slot1, slot == slot2))
    bufs[cur] = _bdot(perm, h_ref[...]).astype(bufs.dtype)
    _start_copies(copies(i, cur))

    @pl.when(i == last)
    def _():
        @pl.when(i >= 1)
        def _():
            _wait_copies(copies(i - 1, 1 - cur))

        _wait_copies(copies(i, cur))

        zeros[...] = jnp.zeros_like(zeros)
        fills = []
        for e in range(n_experts):
            start = pl.multiple_of(tail_ref[e], _SEG_ALIGN)
            n = pl.multiple_of(tail_ref[n_experts + e], _SEG_ALIGN)
            fills.append((n, pltpu.make_async_copy(zeros.at[pl.ds(0, n)],
                                                   xs_ref.at[pl.ds(start, n)], sems.at[0])))
        _run_copies(fills)

        def fill_tile(j, carry):
            cp = pltpu.make_async_copy(
                zeros, xs_ref.at[pl.ds(pl.multiple_of(j * _FFN_ROWS, _FFN_ROWS), _FFN_ROWS)],
                sems.at[0])
            cp.start()
            cp.wait()
            return carry

        lax.fori_loop(tail_ref[2 * n_experts], xs_ref.shape[0] // _FFN_ROWS, fill_tile, 0)


def _dispatch(h2, meta_l, plan, tails, n_slots, n_experts, tm):
    n, d = h2.shape
    seg_start, pad, loc_off = plan
    return pl.pallas_call(
        functools.partial(_dispatch_kernel, n_experts=n_experts),
        grid_spec=pltpu.PrefetchScalarGridSpec(
            num_scalar_prefetch=4,
            grid=(n // tm,),
            in_specs=[pl.BlockSpec((tm, d), lambda i, *_: (i, 0)),
                      pl.BlockSpec((n_experts, tm), lambda i, *_: (i, 0))],
            out_specs=pl.BlockSpec(memory_space=pl.ANY),
            scratch_shapes=[pltpu.VMEM((2, _local_slots(tm, n_experts), d), _BF16),
                            pltpu.VMEM((_FFN_ROWS, d), _BF16),
                            pltpu.SemaphoreType.DMA((2,))],
        ),
        out_shape=jax.ShapeDtypeStruct((n_slots, d), _BF16),
        compiler_params=_params(1),
        name="dispatch",
    )(seg_start, pad, loc_off, tails, h2, meta_l)


_W_ROUNDS = 16
_W_ROUNDS_PER_STEP = 2
_ST_CUR, _ST_LOADED, _ST_TARGET, _ST_DONE = range(4)


def _expert_ffn_kernel(te_ref, nu_ref, nxt_ref, x_ref, wg_hbm, wu_hbm, wd_hbm, o_ref,
                       wg_v, wu_v, wd_v, stage_gu, stage_d, sems, state):
    j = pl.program_id(0)
    rg = wg_v.shape[1] // _W_ROUNDS
    rd = wd_v.shape[1] // _W_ROUNDS

    def round_copies(e, r, s):
        g_rows = pl.ds(pl.multiple_of(r * rg, rg), rg)
        d_rows = pl.ds(pl.multiple_of(r * rd, rd), rd)
        return (pltpu.make_async_copy(wg_hbm.at[e, g_rows], stage_gu.at[s, 0], sems.at[s, 0]),
                pltpu.make_async_copy(wu_hbm.at[e, g_rows], stage_gu.at[s, 1], sems.at[s, 1]),
                pltpu.make_async_copy(wd_hbm.at[e, d_rows], stage_d.at[s], sems.at[s, 2]))

    def narrow_round(r, s, copy):
        g_rows = pl.ds(pl.multiple_of(r * rg, rg), rg)
        d_rows = pl.ds(pl.multiple_of(r * rd, rd), rd)
        wg_v[copy, g_rows, :] = stage_gu[s, 0].astype(_BF16)
        wu_v[copy, g_rows, :] = stage_gu[s, 1].astype(_BF16)
        wd_v[copy, d_rows, :] = stage_d[s].astype(_BF16)

    @pl.when(j == 0)
    def _():
        state[_ST_CUR] = 0
        state[_ST_LOADED] = -1
        state[_ST_TARGET] = -1
        state[_ST_DONE] = _W_ROUNDS

    active = j < nu_ref[0]
    e = te_ref[j]

    @pl.when(jnp.logical_and(active, e != state[_ST_LOADED]))
    def _():
        other = 1 - state[_ST_CUR]

        @pl.when(state[_ST_TARGET] != e)
        def _():
            state[_ST_TARGET] = e
            state[_ST_DONE] = 0

        def load_round(r, carry):
            cps = round_copies(e, r, 0)
            for cp in cps:
                cp.start()
            for cp in cps:
                cp.wait()
            narrow_round(r, 0, other)
            return carry

        lax.fori_loop(state[_ST_DONE], _W_ROUNDS, load_round, 0)
        nxt = nxt_ref[e]
        state[_ST_CUR] = other
        state[_ST_LOADED] = e
        state[_ST_TARGET] = nxt
        state[_ST_DONE] = jnp.where(nxt != e, 0, _W_ROUNDS)

    cur = state[_ST_CUR]
    target = state[_ST_TARGET]
    done = state[_ST_DONE]
    prefetch = jnp.logical_and(active, done < _W_ROUNDS)

    @pl.when(prefetch)
    def _():
        for s in range(_W_ROUNDS_PER_STEP):
            for cp in round_copies(target, done + s, s):
                cp.start()

    @pl.when(active)
    def _():
        x = x_ref[...]
        gate = _bdot(x, wg_v[cur])
        act = (gate * _sigmoid(gate)) * _bdot(x, wu_v[cur])
        o_ref[...] = _bdot(act.astype(_BF16), wd_v[cur]).astype(o_ref.dtype)

    @pl.when(jnp.logical_not(active))
    def _():
        o_ref[...] = jnp.zeros_like(o_ref)

    @pl.when(prefetch)
    def _():
        for s in range(_W_ROUNDS_PER_STEP):
            for cp in round_copies(target, done + s, s):
                cp.wait()
            narrow_round(done + s, s, 1 - cur)
        state[_ST_DONE] = done + _W_ROUNDS_PER_STEP


def _expert_ffn(xs, tile_expert, n_used, next_expert, wg, wu, wd):
    n_slots, d = xs.shape
    n_experts, _, f = wg.shape
    assert d % (_W_ROUNDS * _BF16_SUBLANES) == 0 and f % (_W_ROUNDS * _BF16_SUBLANES) == 0
    assert _W_ROUNDS % _W_ROUNDS_PER_STEP == 0
    any_spec = pl.BlockSpec(memory_space=pl.ANY)
    return pl.pallas_call(
        _expert_ffn_kernel,
        grid_spec=pltpu.PrefetchScalarGridSpec(
            num_scalar_prefetch=3,
            grid=(n_slots // _FFN_ROWS,),
            in_specs=[
                pl.BlockSpec((_FFN_ROWS, d),
                             lambda j, te, nu, nx: (jnp.maximum(jnp.minimum(j, nu[0] - 1), 0), 0)),
                any_spec, any_spec, any_spec,
            ],
            out_specs=pl.BlockSpec((_FFN_ROWS, d), lambda j, te, nu, nx: (j, 0)),
            scratch_shapes=[
                pltpu.VMEM((2, d, f), _BF16), pltpu.VMEM((2, d, f), _BF16),
                pltpu.VMEM((2, f, d), _BF16),
                pltpu.VMEM((_W_ROUNDS_PER_STEP, 2, d // _W_ROUNDS, f), _F32),
                pltpu.VMEM((_W_ROUNDS_PER_STEP, f // _W_ROUNDS, d), _F32),
                pltpu.SemaphoreType.DMA((_W_ROUNDS_PER_STEP, 3)),
                pltpu.SMEM((4,), jnp.int32),
            ],
        ),
        out_shape=jax.ShapeDtypeStruct((n_slots, d), _BF16),
        compiler_params=_params(1),
        name="expert_ffn",
    )(tile_expert, n_used, next_expert, xs, wg, wu, wd)


def _combine_kernel(seg_ref, pad_ref, loc_ref, ys_ref, ms_ref, x1_ref, g2_ref, nf_ref, o_ref,
                    bufs, sems, *, n_experts):
    i = pl.program_id(0)
    tm = x1_ref.shape[0]
    cur = lax.rem(i, 2)

    def copies(tile, b):
        return _segment_copies(tile, n_experts, seg_ref, pad_ref, loc_ref, ys_ref, bufs.at[b],
                               sems.at[b], False)

    @pl.when(i == 0)
    def _():
        bufs[...] = jnp.zeros_like(bufs)
        _start_copies(copies(0, 0))

    @pl.when(i + 1 < pl.num_programs(0))
    def _():
        _start_copies(copies(i + 1, 1 - cur))

    _wait_copies(copies(i, cur))
    rc = min(tm, _MIX_CHUNK_ROWS)
    slot = lax.broadcasted_iota(jnp.int32, (rc, bufs.shape[1]), 1)
    for c in range(tm // rc):
        rows = slice(c * rc, (c + 1) * rc)
        meta = ms_ref[rows, :]
        col = lambda k: meta[:, k:k + 1]
        slot1 = _expert_offset(col(0).astype(jnp.int32), loc_ref, i, n_experts) + col(2).astype(jnp.int32)
        slot2 = _expert_offset(col(1).astype(jnp.int32), loc_ref, i, n_experts) + col(3).astype(jnp.int32)
        weights = jnp.where(slot == slot1, col(4), jnp.where(slot == slot2, col(5), 0.0))
        y = _bdot(weights.astype(_BF16), bufs[cur])
        o_ref[rows, :] = _rmsnorm(x1_ref[rows, :] + g2_ref[...] * y, nf_ref[...])


def _combine(ys, meta_s, x1, plan, mods, row_fn, norm_final, n_experts, tm):
    n, d = x1.shape
    seg_start, pad, loc_off = plan
    return pl.pallas_call(
        functools.partial(_combine_kernel, n_experts=n_experts),
        grid_spec=pltpu.PrefetchScalarGridSpec(
            num_scalar_prefetch=3,
            grid=(n // tm,),
            in_specs=[pl.BlockSpec(memory_space=pl.ANY),
                      pl.BlockSpec((tm, _LANES), lambda i, *_: (i, 0)),
                      pl.BlockSpec((tm, d), lambda i, *_: (i, 0)),
                      _mod_spec(d, 5, row_fn),
                      _resident((1, d))],
            out_specs=pl.BlockSpec((tm, d), lambda i, *_: (i, 0)),
            scratch_shapes=[pltpu.VMEM((2, _local_slots(tm, n_experts), d), _BF16),
                            pltpu.SemaphoreType.DMA((2,))],
        ),
        out_shape=jax.ShapeDtypeStruct((n, d), _F32),
        compiler_params=_params(1),
        name="combine",
    )(seg_start, pad, loc_off, ys, meta_s, x1, mods, norm_final)


def kernel(x, c, ctx, c_ctx, w_ada, b_ada, norm_mix, norm_ffn, w_in, w_conv, rpb, w_attn_out,
           w_conv_out, w_gate, b_gate, w_out, w_ffn_gate, w_ffn_up, w_ffn_down, w_router,
           w_exp_gate, w_exp_up, w_exp_down, norm_final):
    batch, seq, d = x.shape
    ctx_len = ctx.shape[1]
    depth = w_ada.shape[0]
    attn_w = _N_HEADS * _HEAD_DIM
    n_experts = w_router.shape[-1]
    rows = seq // _GRID_W
    n_pairs = attn_w // _LANES
    assert depth == 2 and seq % (_Q_ROWS * _GRID_W) == 0 and attn_w % _LANES == 0
    assert n_experts == _F32_SUBLANES

    ctx_row = batch
    cond_rows = -(-(batch + 1) // 8) * 8
    cond = jnp.zeros((cond_rows, d), _F32).at[:batch].set(c).at[ctx_row].set(c_ctx)
    mods_all = _ada(cond, w_ada, b_ada, tn=1536).reshape(depth, cond_rows, 1, 6 * d)

    union, starts, variant_of, patterns, col_starts = _attn_plan(rows)

    x2d = x.reshape(batch * seq, d)
    c2d = ctx.reshape(batch * ctx_len, d)
    tm_lat = 512
    ctx_row_fn = lambda i: ctx_row

    def lat_rows(tm):
        return lambda i: (i * tm) // seq

    out = None
    for l in range(depth):
        last = l == depth - 1
        mods = mods_all[l]
        wts = {
            "norm_mix": norm_mix[l].reshape(1, d),
            "norm_ffn": norm_ffn[l].reshape(1, d),
            "wconv": w_conv[l],
            "wao": w_attn_out[l].astype(_BF16),
            "wco": w_conv_out[l].astype(_BF16),
            "wgate": w_gate[l].astype(_BF16),
            "bgate": b_gate[l].reshape(1, -1),
            "wout": w_out[l].astype(_BF16),
        }
        w_in_l = w_in[l].astype(_BF16)
        p = _proj(x2d, mods, lat_rows(512), wts["norm_mix"], w_in_l, tm=512)
        if last:
            pc = _proj(c2d, mods, ctx_row_fn, wts["norm_mix"], w_in_l[:, attn_w:3 * attn_w], tm=512)
            kc_col = 0
        else:
            pc = _proj(c2d, mods, ctx_row_fn, wts["norm_mix"], w_in_l, tm=512)
            kc_col = n_pairs
        table = _bias_table(rpb[l], union, patterns, col_starts)
        ya, yca = _attention(p, pc, kc_col, table, starts, variant_of, union, col_starts, batch,
                             seq, ctx_len, with_ctx=not last)
        if not last:
            wts["ffn_gate"] = w_ffn_gate[l // 2].astype(_BF16)
            wts["ffn_up"] = w_ffn_up[l // 2].astype(_BF16)
            wts["ffn_down"] = w_ffn_down[l // 2].astype(_BF16)
            x2d = _merge_ffn(x2d, ya, p, mods, lat_rows(tm_lat), seq, wts, tm=tm_lat)
            c2d = _merge_ffn(c2d, yca, pc, mods, ctx_row_fn, ctx_len, wts, tm=ctx_len)
        else:
            wr_t = w_router[l // 2].T
            wr_hi = wr_t.astype(_BF16)
            wr_lo = (wr_t - wr_hi.astype(_F32)).astype(_BF16)
            wts["router"] = jnp.concatenate([wr_hi, wr_lo], axis=0)
            tm_moe = 512
            n_tok = batch * seq
            n_tiles = n_tok // tm_moe
            x1, h2, meta_s, meta_l, cnt = _merge_router(x2d, ya, p, mods, lat_rows(tm_moe), seq,
                                                        wts, n_experts, tm=tm_moe)
            n_slots = _round_up(_TOP_K * n_tok + n_tiles * n_experts * (_SEG_ALIGN - 1)
                                + n_experts * (_FFN_ROWS - 1), _FFN_ROWS)
            cnt = cnt[:, 0].reshape(n_tiles, n_experts).astype(jnp.int32)
            plan, tails, tile_expert, n_used, next_expert = _routing_plan(cnt, n_slots)
            xs = _dispatch(h2, meta_l, plan, tails, n_slots, n_experts, tm=tm_moe)
            ys = _expert_ffn(xs, tile_expert, n_used, next_expert, w_exp_gate[l // 2],
                             w_exp_up[l // 2], w_exp_down[l // 2])
            out = _combine(ys, meta_s, x1, plan, mods, lat_rows(tm_moe), norm_final.reshape(1, d),
                           n_experts, tm=tm_moe)
    return out.reshape(batch, seq, d)
```

```python
import functools

import numpy as np
import jax
import jax.numpy as jnp
from jax import lax
from jax.experimental import pallas as pl
from jax.experimental.pallas import tpu as pltpu

_F32 = jnp.float32
_BF16 = jnp.bfloat16

_GRID_W = 64
_N_HEADS = 8
_HEAD_DIM = 64
_NA_ROWS = 8
_NA_COLS = 16
_CONV_K = 3
_TOP_K = 2
_EPS = 1e-6
_NEG_INF = -1e30

_LANES = 128
_F32_SUBLANES = 8
_BF16_SUBLANES = 16
_VMEM_LIMIT_BYTES = 56 * 1024 * 1024

_Q_ROWS = 8
_Q_COLS = 16
_K_COLS = 32
_MIX_CHUNK_ROWS = 256
_HEADS_PER_BLOCK = _LANES // _HEAD_DIM
_NT = (((1,), (1,)), ((), ()))
_TN = (((0,), (0,)), ((), ()))


def _params(n_axes):
    return pltpu.CompilerParams(
        dimension_semantics=("arbitrary",) * n_axes,
        vmem_limit_bytes=_VMEM_LIMIT_BYTES)


def _resident(shape):
    nd = len(shape)
    return pl.BlockSpec(shape, lambda *_: (0,) * nd, pipeline_mode=pl.Buffered(1))


def _sigmoid(x):
    return 1.0 / (1.0 + jnp.exp(-x))


def _rmsnorm(x, g):
    ms = jnp.mean(x * x, axis=-1, keepdims=True)
    return (x * lax.rsqrt(ms + _EPS)) * g


def _modulate(x, g, shift, scale):
    return _rmsnorm(x, g) * (1.0 + scale) + shift


def _bdot(a, b):
    return jnp.dot(a, b, preferred_element_type=_F32)


def _indicator(mask):
    return jnp.where(mask, 1.0, 0.0).astype(_BF16)


def _ada_kernel(cond_ref, w_ref, b_ref, o_ref):
    cond = cond_ref[...]
    act = cond * _sigmoid(cond)
    o_ref[...] = jnp.dot(act, w_ref[...], precision=lax.Precision.HIGHEST,
                         preferred_element_type=_F32) + b_ref[...]


def _ada(cond, w_ada, b_ada, tn):
    depth, d, n_out = w_ada.shape
    rows = cond.shape[0]
    return pl.pallas_call(
        _ada_kernel,
        grid=(depth, n_out // tn),
        in_specs=[
            pl.BlockSpec((rows, d), lambda l, j: (0, 0)),
            pl.BlockSpec((None, d, tn), lambda l, j: (l, 0, j)),
            pl.BlockSpec((None, 1, tn), lambda l, j: (l, 0, j)),
        ],
        out_specs=pl.BlockSpec((None, rows, tn), lambda l, j: (l, 0, j)),
        out_shape=jax.ShapeDtypeStruct((depth, rows, n_out), _F32),
        compiler_params=_params(2),
        name="ada",
    )(cond, w_ada, b_ada.reshape(depth, 1, n_out))


def _mod_spec(d, chunk, row_fn):
    return pl.BlockSpec((None, 1, d), lambda i, *_: (row_fn(i), 0, chunk))


def _proj_kernel(x_ref, g_ref, sh_ref, sc_ref, w_ref, o_ref):
    h = _modulate(x_ref[...], g_ref[...], sh_ref[...], sc_ref[...])
    o_ref[...] = _bdot(h.astype(_BF16), w_ref[...]).astype(o_ref.dtype)


def _proj(x2d, mods, row_fn, g, w, tm):
    n, d = x2d.shape
    n_out = w.shape[1]
    return pl.pallas_call(
        _proj_kernel,
        grid=(n // tm,),
        in_specs=[
            pl.BlockSpec((tm, d), lambda i: (i, 0)),
            _resident((1, d)),
            _mod_spec(d, 0, row_fn),
            _mod_spec(d, 1, row_fn),
            _resident(w.shape),
        ],
        out_specs=pl.BlockSpec((tm, n_out), lambda i: (i, 0)),
        out_shape=jax.ShapeDtypeStruct((n, n_out), _BF16),
        compiler_params=_params(1),
        name="proj",
    )(x2d, g, mods, mods, w)


def _bias_table_kernel(rpb_ref, o_ref, *, union, patterns, col_starts):
    sub = _F32_SUBLANES
    span = _NA_COLS - 1
    slots = _LANES // _K_COLS
    rp = rpb_ref[...]
    lane = lax.broadcasted_iota(jnp.int32, (sub, _LANES), 1)
    srow = lax.broadcasted_iota(jnp.int32, (sub, _LANES), 0)
    ck_local = jnp.bitwise_and(lane, _K_COLS - 1)
    slot_of_lane = jnp.right_shift(lane, _K_COLS.bit_length() - 1)
    neg = jnp.full((sub, _LANES), _NEG_INF, _F32)

    def row_vec(dr):
        return jnp.broadcast_to(rp[dr + _NA_ROWS - 1:dr + _NA_ROWS, :], (sub, _LANES))

    for g, c0 in enumerate(col_starts):
        for gg in range(_Q_COLS // sub):
            k0 = c0 - _Q_COLS * g - sub * gg + span
            lo, hi = max(-(sub - 1), -k0), min(_K_COLS - 1, 2 * span - k0)
            cq = srow + (_Q_COLS * g + sub * gg)
            cs = jnp.clip(cq - _NA_COLS // 2, 0, _GRID_W - _NA_COLS)
            ck = ck_local + c0
            in_win = jnp.logical_and(ck >= cs, ck < cs + _NA_COLS)
            part = [jnp.bitwise_and(lane - (_K_COLS * jq + lo), _LANES - 1) <= hi - lo
                    for jq in range(slots)]
            tiles = {}

            def make_tile(drs):
                base = jnp.zeros((sub, _LANES), _F32)
                keep = None
                for jq, dr in enumerate(drs):
                    if dr is None:
                        continue
                    shifted = pltpu.roll(row_vec(dr), (_K_COLS * jq - k0) % _LANES, 1)
                    base = jnp.where(part[jq], shifted, base)
                    here = slot_of_lane == jq
                    keep = here if keep is None else jnp.logical_or(keep, here)
                if keep is None:
                    return neg
                tile = pltpu.roll(base, 0, 1, stride=1, stride_axis=0)
                return jnp.where(jnp.logical_and(in_win, keep), tile, neg)

            for v, pat in enumerate(patterns):
                for a in range(_Q_ROWS):
                    rows = slice(a * _Q_COLS + sub * gg, a * _Q_COLS + sub * (gg + 1))
                    for quad in range(-(-union // slots)):
                        js = [quad * slots + jq for jq in range(slots)]
                        drs = tuple(pat[a * union + j] if j < union else None for j in js)
                        width = min(slots, union - quad * slots) * _K_COLS
                        if drs not in tiles:
                            tiles[drs] = make_tile(drs)
                        o_ref[v, g, rows, quad * _LANES:quad * _LANES + width] = (
                            tiles[drs][:, :width])


def _bias_table(rpb_l, union, patterns, col_starts):
    h, n_dr, n_dc = rpb_l.shape
    assert n_dc <= _LANES and _LANES % _K_COLS == 0
    n_dr_pad = _round_up(n_dr, _F32_SUBLANES)
    padded = jnp.pad(rpb_l, ((0, 0), (0, n_dr_pad - n_dr), (0, _LANES - n_dc)))
    n_var, n_grp = len(patterns), len(col_starts)
    qt, kt = _Q_ROWS * _Q_COLS, union * _K_COLS
    return pl.pallas_call(
        functools.partial(_bias_table_kernel, union=union, patterns=patterns,
                          col_starts=col_starts),
        grid=(h,),
        in_specs=[pl.BlockSpec((None, n_dr_pad, _LANES), lambda i: (i, 0, 0))],
        out_specs=pl.BlockSpec((n_var, n_grp, None, qt, kt), lambda i: (0, 0, i, 0, 0)),
        out_shape=jax.ShapeDtypeStruct((n_var, n_grp, h, qt, kt), _F32),
        compiler_params=_params(1),
        name="bias_table",
    )(padded)


def _attn_plan(rows):
    kh = min(_NA_ROWS, rows)
    union = min(_Q_ROWS + kh - 1, rows)
    n_blocks = rows // _Q_ROWS
    starts, variant_of, patterns = [], [], []
    for t in range(n_blocks):
        start = int(np.clip(_Q_ROWS * t - kh // 2, 0, rows - union))
        pat = []
        for a in range(_Q_ROWS):
            r = _Q_ROWS * t + a
            rs = int(np.clip(r - kh // 2, 0, rows - kh))
            for j in range(union):
                kr = start + j
                pat.append(kr - r if rs <= kr < rs + kh else None)
        pat = tuple(pat)
        if pat not in patterns:
            patterns.append(pat)
        starts.append(start)
        variant_of.append(patterns.index(pat))
    col_starts = []
    for g in range(_GRID_W // _Q_COLS):
        c0 = int(np.clip(_Q_COLS * g - _NA_COLS // 2, 0, _GRID_W - _K_COLS))
        for cq in range(_Q_COLS * g, _Q_COLS * (g + 1)):
            cs = int(np.clip(cq - _NA_COLS // 2, 0, _GRID_W - _NA_COLS))
            assert c0 <= cs and cs + _NA_COLS <= c0 + _K_COLS
        col_starts.append(c0)
    return union, starts, variant_of, tuple(patterns), tuple(col_starts)


def _first_head_lanes(shape):
    return lax.broadcasted_iota(jnp.int32, shape, len(shape) - 1) < _HEAD_DIM


def _values_with_ones(v):
    lo = _first_head_lanes(v.shape)
    ones = jnp.ones_like(v)
    return jnp.where(lo, v, ones), jnp.where(lo, ones, v)


def _scores(qb, keys, bias_fns):
    lo = _first_head_lanes(qb.shape)
    out = []
    for hh in range(_HEADS_PER_BLOCK):
        qm = jnp.where(lo if hh == 0 else jnp.logical_not(lo), qb, jnp.zeros_like(qb))
        scores = []
        for kk, bias_fn in zip(keys, bias_fns):
            s = lax.dot_general(qm, kk, _NT, preferred_element_type=_F32)
            scores.append(s if bias_fn is None else s + bias_fn(hh))
        out.append(scores)
    return out


def _probabilities(scores):
    m = functools.reduce(jnp.maximum, [jnp.max(s, axis=-1, keepdims=True) for s in scores])
    return [jnp.exp(s - m).astype(_BF16) for s in scores]


def _weighted_values(probs, values):
    res = []
    for hh in range(_HEADS_PER_BLOCK):
        acc = None
        for e, v1 in zip(probs[hh], values):
            o = _bdot(e, v1[hh])
            acc = o if acc is None else acc + o
        res.append(acc)
    lo = _first_head_lanes(res[0].shape)
    num = jnp.where(lo, res[0], res[1])
    den = pltpu.roll(jnp.where(lo, res[1], res[0]), _HEAD_DIM, 1)
    return num / den


def _attn_kernel(start_ref, var_ref, q_ref, k_ref, v_ref, kc_ref, vc_ref, tab_ref, *rest,
                 n_row_blocks, union, col_starts, with_ctx):
    if with_ctx:
        qc_ref, o_ref, oc_ref, s_loc, s_ctx, e_loc, e_ctx = rest
    else:
        o_ref, s_loc, s_ctx, e_loc, e_ctx = rest
    heads = range(_HEADS_PER_BLOCK)
    n_groups = len(col_starts)
    assert n_groups % 2 == 0 and n_groups >= 2 and n_row_blocks >= 1
    scale = _HEAD_DIM ** -0.5
    kc = kc_ref[...]
    vc1 = _values_with_ones(vc_ref[...])

    def gather(ref, first_row, n_rows, col, width):
        return jnp.concatenate(
            [ref[pl.ds(first_row + r * _GRID_W + col, width), :] for r in range(n_rows)], axis=0)

    def key_rows(t):
        return pl.multiple_of(start_ref[t] * _GRID_W, _GRID_W)

    def query_rows(t):
        return pl.multiple_of(t * (_Q_ROWS * _GRID_W), _Q_ROWS * _GRID_W)

    def scores_stage(t, g):
        slot, var = g % 2, var_ref[t]
        qb = gather(q_ref, query_rows(t), _Q_ROWS, _Q_COLS * g, _Q_COLS) * scale
        kb = gather(k_ref, key_rows(t), union, col_starts[g], _K_COLS)
        scores = _scores(qb, [kb, kc], [lambda hh: tab_ref[var, g, hh], None])
        for hh in heads:
            s_loc[slot, hh], s_ctx[slot, hh] = scores[hh]

    def softmax_stage(g):
        slot = g % 2
        for hh in heads:
            e_loc[slot, hh], e_ctx[slot, hh] = _probabilities([s_loc[slot, hh], s_ctx[slot, hh]])

    def values_stage(t, g):
        slot = g % 2
        vb1 = _values_with_ones(gather(v_ref, key_rows(t), union, col_starts[g], _K_COLS))
        probs = [[e_loc[slot, hh], e_ctx[slot, hh]] for hh in heads]
        o = _weighted_values(probs, [vb1, vc1]).astype(o_ref.dtype)
        for a in range(_Q_ROWS):
            o_ref[pl.ds(query_rows(t) + a * _GRID_W + _Q_COLS * g, _Q_COLS), :] = (
                o[a * _Q_COLS:(a + 1) * _Q_COLS])

    def issue(t, k):
        values_stage(t + k // n_groups, k % n_groups)
        softmax_stage((k + 1) % n_groups)
        scores_stage(t + (k + 2) // n_groups, (k + 2) % n_groups)

    scores_stage(0, 0)
    softmax_stage(0)
    scores_stage(0, 1)

    def body(t, carry):
        for k in range(n_groups):
            issue(t, k)
        return carry

    last = n_row_blocks - 1
    lax.fori_loop(0, last, body, 0)
    for k in range(n_groups - 2):
        issue(last, k)
    values_stage(last, n_groups - 2)
    softmax_stage(n_groups - 1)
    values_stage(last, n_groups - 1)

    if with_ctx:
        scores = _scores(qc_ref[...] * scale, [kc], [None])
        probs = [_probabilities(scores[hh]) for hh in heads]
        oc_ref[...] = _weighted_values(probs, [vc1]).astype(oc_ref.dtype)


def _attention(p, pc, kc_col, table, starts, variant_of, union, col_starts, batch, seq, ctx_len,
               with_ctx):
    attn_w = _N_HEADS * _HEAD_DIM
    n_pairs = attn_w // _LANES
    n_var, n_grp = table.shape[:2]
    qb_tokens = _Q_ROWS * _Q_COLS
    union_tokens = union * _K_COLS
    kern = functools.partial(_attn_kernel, n_row_blocks=len(starts), union=union,
                             col_starts=col_starts, with_ctx=with_ctx)
    in_specs = [
        pl.BlockSpec((seq, _LANES), lambda hp, b, *_: (b, hp)),
        pl.BlockSpec((seq, _LANES), lambda hp, b, *_: (b, n_pairs + hp)),
        pl.BlockSpec((seq, _LANES), lambda hp, b, *_: (b, 2 * n_pairs + hp)),
        pl.BlockSpec((ctx_len, _LANES), lambda hp, b, *_: (b, kc_col + hp)),
        pl.BlockSpec((ctx_len, _LANES), lambda hp, b, *_: (b, kc_col + n_pairs + hp)),
        pl.BlockSpec((n_var, n_grp, _HEADS_PER_BLOCK, qb_tokens, union_tokens),
                     lambda hp, b, *_: (0, 0, hp, 0, 0)),
    ]
    args = [p, p, p, pc, pc, table]
    out_specs = [pl.BlockSpec((seq, _LANES), lambda hp, b, *_: (b, hp))]
    out_shape = [jax.ShapeDtypeStruct((batch * seq, attn_w), _BF16)]
    if with_ctx:
        in_specs.append(pl.BlockSpec((ctx_len, _LANES), lambda hp, b, *_: (b, hp)))
        args.append(pc)
        out_specs.append(pl.BlockSpec((ctx_len, _LANES), lambda hp, b, *_: (b, hp)))
        out_shape.append(jax.ShapeDtypeStruct((batch * ctx_len, attn_w), _BF16))
    outs = pl.pallas_call(
        kern,
        grid_spec=pltpu.PrefetchScalarGridSpec(
            num_scalar_prefetch=2,
            grid=(n_pairs, batch),
            in_specs=in_specs,
            out_specs=out_specs,
            scratch_shapes=[
                pltpu.VMEM((2, _HEADS_PER_BLOCK, qb_tokens, union_tokens), _F32),
                pltpu.VMEM((2, _HEADS_PER_BLOCK, qb_tokens, ctx_len), _F32),
                pltpu.VMEM((2, _HEADS_PER_BLOCK, qb_tokens, union_tokens), _BF16),
                pltpu.VMEM((2, _HEADS_PER_BLOCK, qb_tokens, ctx_len), _BF16),
            ],
        ),
        out_shape=out_shape,
        compiler_params=_params(2),
        name="attn",
    )(jnp.asarray(starts, jnp.int32), jnp.asarray(variant_of, jnp.int32), *args)
    return outs if with_ctx else (outs[0], None)


def _mixer_update(i, x_ref, ya_ref, bcu_ref, prev_ref, next_ref, nm_ref, sh1_ref, sc1_ref, g1_ref,
                  wconv_ref, wao_ref, wco_ref, wgate_ref, bgate_ref, wout_ref, seq_len):
    tm, d = x_ref.shape
    cw = wco_ref.shape[0]

    bcu = bcu_ref[...].astype(_F32)
    bg, z = bcu[:, :cw], bcu[:, cw:2 * cw] * bcu[:, 2 * cw:]
    prev = prev_ref[...].astype(_F32)
    nxt = next_ref[...].astype(_F32)
    has_prev = ((i * tm) % seq_len != 0).astype(_F32)
    has_next = (((i + 1) * tm) % seq_len != 0).astype(_F32)
    last = prev.shape[0] - 1
    z_prev = prev[last:last + 1, cw:2 * cw] * prev[last:last + 1, 2 * cw:] * has_prev
    z_next = nxt[0:1, cw:2 * cw] * nxt[0:1, 2 * cw:] * has_next
    row = lax.broadcasted_iota(jnp.int32, z.shape, 0)
    z_dn = jnp.where(row == 0, z_prev, pltpu.roll(z, 1, 0))
    z_up = jnp.where(row == tm - 1, z_next, pltpu.roll(z, tm - 1, 0))
    wconv = wconv_ref[...]
    y_conv = (bg * (z_dn * wconv[0:1] + z * wconv[1:2] + z_up * wconv[2:3])).astype(_BF16)

    chunks = []
    rc = min(tm, _MIX_CHUNK_ROWS)
    for c in range(tm // rc):
        rows = slice(c * rc, (c + 1) * rc)
        x = x_ref[rows, :]
        hb = _modulate(x, nm_ref[...], sh1_ref[...], sc1_ref[...]).astype(_BF16)
        gates = _sigmoid(_bdot(hb, wgate_ref[...]) + bgate_ref[...])
        a_proj = _bdot(ya_ref[rows, :], wao_ref[...])
        c_proj = _bdot(y_conv[rows, :], wco_ref[...])
        merged = gates[:, :d] * a_proj + gates[:, d:] * c_proj
        chunks.append(x + g1_ref[...] * _bdot(merged.astype(_BF16), wout_ref[...]))
    return jnp.concatenate(chunks, axis=0)


_N_MIX_REFS = 15


def _merge_ffn_kernel(*refs, seq_len):
    mix = refs[:_N_MIX_REFS]
    nf_ref, sh2_ref, sc2_ref, g2_ref, wg_ref, wu_ref, wd_ref, o_ref = refs[_N_MIX_REFS:]
    x1 = _mixer_update(pl.program_id(0), *mix, seq_len=seq_len)
    h2 = _modulate(x1, nf_ref[...], sh2_ref[...], sc2_ref[...]).astype(_BF16)
    gate = _bdot(h2, wg_ref[...])
    act = (gate * _sigmoid(gate)) * _bdot(h2, wu_ref[...])
    o_ref[...] = x1 + g2_ref[...] * _bdot(act.astype(_BF16), wd_ref[...])


def _top2(vals, idx, n_idx, axis):
    v1 = jnp.max(vals, axis=axis, keepdims=True)
    i1 = jnp.min(jnp.where(vals == v1, idx, n_idx), axis=axis, keepdims=True)
    rest = jnp.where(idx == i1, -jnp.inf, vals)
    v2 = jnp.max(rest, axis=axis, keepdims=True)
    i2 = jnp.min(jnp.where(rest == v2, idx, n_idx), axis=axis, keepdims=True)
    return i1, i2, v1, v2


def _pick(idx, sel, vals, axis):
    return jnp.sum(jnp.where(idx == sel, vals, 0.0), axis=axis, keepdims=True)


def _merge_router_kernel(*refs, seq_len, n_experts, route_rows):
    mix = refs[:_N_MIX_REFS]
    (nf_ref, sh2_ref, sc2_ref, wr_ref,
     x1_ref, h2_ref, ms_ref, ml_ref, cnt_ref) = refs[_N_MIX_REFS:]
    x1 = _mixer_update(pl.program_id(0), *mix, seq_len=seq_len)
    x1_ref[...] = x1
    h2 = _modulate(x1, nf_ref[...], sh2_ref[...], sc2_ref[...])
    h_hi = h2.astype(_BF16)
    h2_ref[...] = h_hi
    tm = h2.shape[0]
    h_lo = (h2 - h_hi.astype(_F32)).astype(_BF16)
    wr = wr_ref[...]
    by_hi = lax.dot_general(wr, h_hi, _NT, preferred_element_type=_F32)
    by_lo = lax.dot_general(wr, h_lo, _NT, preferred_element_type=_F32)
    lt_all = by_hi[:n_experts] + by_hi[n_experts:] + by_lo[:n_experts]
    row = lax.broadcasted_iota(jnp.int32, (route_rows, route_rows), 0)
    col = lax.broadcasted_iota(jnp.int32, (route_rows, route_rows), 1)
    earlier_rows = _indicator(col < row)
    earlier_cols = _indicator(row < col)
    lane = lax.broadcasted_iota(jnp.int32, (route_rows, _LANES), 1)
    sub = lax.broadcasted_iota(jnp.int32, (n_experts, route_rows), 0)

    for u in range(tm // route_rows):
        lt = lt_all[:, u * route_rows:(u + 1) * route_rows]

        logits = jnp.concatenate(
            [lt, jnp.full((_LANES - n_experts, route_rows), -jnp.inf, _F32)], axis=0).T
        i1, i2, v1, v2 = _top2(logits, lane, _LANES, -1)
        e2 = jnp.exp(v2 - v1)
        den = 1.0 + e2
        sel = _indicator(jnp.logical_or(lane == i1, lane == i2))
        ranks = _bdot(earlier_rows, sel)
        fields = (i1.astype(_F32), i2.astype(_F32), _pick(lane, i1, ranks, -1),
                  _pick(lane, i2, ranks, -1), 1.0 / den, e2 / den)
        meta = jnp.zeros(logits.shape, _F32)
        for k, f in enumerate(fields):
            meta = jnp.where(lane == k, f, meta)
        ms_ref[u * route_rows:(u + 1) * route_rows, :] = meta

        j1, j2, _, _ = _top2(lt, sub, n_experts, 0)
        sel_t = _indicator(jnp.logical_or(sub == j1, sub == j2))
        ranks_t = _bdot(sel_t, earlier_cols)
        fields_t = (j1.astype(_F32), j2.astype(_F32), _pick(sub, j1, ranks_t, 0),
                    _pick(sub, j2, ranks_t, 0))
        meta_t = jnp.zeros(lt.shape, _F32)
        for k, f in enumerate(fields_t):
            meta_t = jnp.where(sub == k, f, meta_t)
        experts = slice(u * n_experts, (u + 1) * n_experts)
        ml_ref[experts, :] = meta_t
        cnt = jnp.sum(sel_t.astype(_F32), axis=1, keepdims=True)
        cnt_ref[experts, :] = jnp.broadcast_to(cnt, (n_experts, _LANES))


def _mixer_specs(n, d, tm, p_cols, row_fn, weights):
    half = p_cols // 2
    sub = _BF16_SUBLANES
    n_sub = n // sub
    specs = [
        pl.BlockSpec((tm, d), lambda i: (i, 0)),
        pl.BlockSpec((tm, weights["wao"].shape[0]), lambda i: (i, 0)),
        pl.BlockSpec((tm, half), lambda i: (i, 1)),
        pl.BlockSpec((sub, half), lambda i: (jnp.maximum(i * (tm // sub) - 1, 0), 1)),
        pl.BlockSpec((sub, half), lambda i: (jnp.minimum((i + 1) * (tm // sub), n_sub - 1), 1)),
        _resident((1, d)),
        _mod_spec(d, 0, row_fn), _mod_spec(d, 1, row_fn), _mod_spec(d, 2, row_fn),
    ]
    for name in ("wconv", "wao", "wco", "wgate", "bgate", "wout"):
        specs.append(_resident(weights[name].shape))
    return specs


def _merge_ffn(x2d, ya, p, mods, row_fn, seq_len, wts, tm):
    n, d = x2d.shape
    specs = _mixer_specs(n, d, tm, p.shape[1], row_fn, wts)
    specs += [_resident((1, d)), _mod_spec(d, 3, row_fn), _mod_spec(d, 4, row_fn),
              _mod_spec(d, 5, row_fn),
              _resident(wts["ffn_gate"].shape), _resident(wts["ffn_up"].shape),
              _resident(wts["ffn_down"].shape)]
    return pl.pallas_call(
        functools.partial(_merge_ffn_kernel, seq_len=seq_len),
        grid=(n // tm,),
        in_specs=specs,
        out_specs=pl.BlockSpec((tm, d), lambda i: (i, 0)),
        out_shape=jax.ShapeDtypeStruct((n, d), _F32),
        compiler_params=_params(1),
        name="merge_ffn",
    )(x2d, ya, p, p, p, wts["norm_mix"], mods, mods, mods,
      wts["wconv"], wts["wao"], wts["wco"], wts["wgate"], wts["bgate"], wts["wout"],
      wts["norm_ffn"], mods, mods, mods, wts["ffn_gate"], wts["ffn_up"], wts["ffn_down"])


def _merge_router(x2d, ya, p, mods, row_fn, seq_len, wts, n_experts, tm, route_rows):
    n, d = x2d.shape
    assert tm % route_rows == 0
    specs = _mixer_specs(n, d, tm, p.shape[1], row_fn, wts)
    specs += [_resident((1, d)), _mod_spec(d, 3, row_fn), _mod_spec(d, 4, row_fn),
              _resident(wts["router"].shape)]
    tile = lambda w: pl.BlockSpec((tm, w), lambda i: (i, 0))
    meta_rows = n_experts * (tm // route_rows)
    n_route = n // route_rows
    return pl.pallas_call(
        functools.partial(_merge_router_kernel, seq_len=seq_len, n_experts=n_experts,
                          route_rows=route_rows),
        grid=(n // tm,),
        in_specs=specs,
        out_specs=[tile(d), tile(d), tile(_LANES),
                   pl.BlockSpec((meta_rows, route_rows), lambda i: (i, 0)),
                   pl.BlockSpec((meta_rows, _LANES), lambda i: (i, 0))],
        out_shape=[jax.ShapeDtypeStruct((n, d), _F32),
                   jax.ShapeDtypeStruct((n, d), _BF16),
                   jax.ShapeDtypeStruct((n, _LANES), _F32),
                   jax.ShapeDtypeStruct((n_route * n_experts, route_rows), _F32),
                   jax.ShapeDtypeStruct((n_route * n_experts, _LANES), _F32)],
        compiler_params=_params(1),
        name="merge_router",
    )(x2d, ya, p, p, p, wts["norm_mix"], mods, mods, mods,
      wts["wconv"], wts["wao"], wts["wco"], wts["wgate"], wts["bgate"], wts["wout"],
      wts["norm_ffn"], mods, mods, wts["router"])


_SEG_ALIGN = _BF16_SUBLANES
_FFN_ROWS = 256


def _round_up(v, m):
    return (v + m - 1) // m * m


def _local_slots(tm, n_experts):
    return _round_up(_TOP_K * tm + n_experts * (_SEG_ALIGN - 1), _SEG_ALIGN)


def _routing_plan(cnt, n_slots):
    pad = _round_up(cnt, _SEG_ALIGN)
    total = jnp.sum(pad, axis=0)
    region = _round_up(total, _FFN_ROWS)
    ends = jnp.cumsum(region)
    seg_start = (ends - region)[None, :] + jnp.cumsum(pad, axis=0) - pad
    loc_off = jnp.cumsum(pad, axis=1) - pad
    n_used = (ends[-1] // _FFN_ROWS).astype(jnp.int32).reshape(1)
    tails = jnp.concatenate([ends - region + total, region - total, n_used])
    tile_row = jnp.arange(n_slots // _FFN_ROWS, dtype=jnp.int32) * _FFN_ROWS
    tile_expert = jnp.minimum(jnp.sum(tile_row[:, None] >= ends[None, :], axis=1),
                              cnt.shape[1] - 1).astype(jnp.int32)
    ids = jnp.arange(cnt.shape[1], dtype=jnp.int32)
    later = jnp.logical_and(region[None, :] > 0, ids[None, :] > ids[:, None])
    nxt = jnp.min(jnp.where(later, ids[None, :], cnt.shape[1]), axis=1)
    next_expert = jnp.where(nxt == cnt.shape[1], ids, nxt).astype(jnp.int32)
    flat = lambda a: a.reshape(-1).astype(jnp.int32)
    return ((flat(seg_start), flat(pad), flat(loc_off)), flat(tails), tile_expert, n_used,
            next_expert)


def _segment_copies(i, n_experts, seg_ref, pad_ref, loc_ref, hbm_ref, buf_ref, sem, to_hbm):
    copies = []
    for e in range(n_experts):
        n = pl.multiple_of(pad_ref[i * n_experts + e], _SEG_ALIGN)
        loc = pl.multiple_of(loc_ref[i * n_experts + e], _SEG_ALIGN)
        seg = pl.multiple_of(seg_ref[i * n_experts + e], _SEG_ALIGN)
        local, remote = buf_ref.at[pl.ds(loc, n)], hbm_ref.at[pl.ds(seg, n)]
        src, dst = (local, remote) if to_hbm else (remote, local)
        copies.append((n, pltpu.make_async_copy(src, dst, sem)))
    return copies


def _start_copies(copies):
    for n, cp in copies:
        pl.when(n > 0)(cp.start)


def _wait_copies(copies):
    for n, cp in copies:
        pl.when(n > 0)(cp.wait)


def _run_copies(copies):
    _start_copies(copies)
    _wait_copies(copies)


def _expert_offset(expert, loc_ref, i, n_experts):
    off = jnp.zeros(expert.shape, jnp.int32)
    for e in range(n_experts):
        off = jnp.where(expert == e, loc_ref[i * n_experts + e], off)
    return off


def _dispatch_kernel(seg_ref, pad_ref, loc_ref, tail_ref, h_ref, ml_ref, xs_ref, bufs, zeros, sems,
                     *, n_experts):
    i = pl.program_id(0)
    last = pl.num_programs(0) - 1
    cur = lax.rem(i, 2)

    def copies(tile, b):
        return _segment_copies(tile, n_experts, seg_ref, pad_ref, loc_ref, xs_ref, bufs.at[b],
                               sems.at[b], True)

    @pl.when(i >= 2)
    def _():
        _wait_copies(copies(i - 2, cur))

    tm = h_ref.shape[0]
    meta = ml_ref[...]
    slot1 = _expert_offset(meta[0:1].astype(jnp.int32), loc_ref, i, n_experts) + meta[2:3].astype(jnp.int32)
    slot2 = _expert_offset(meta[1:2].astype(jnp.int32), loc_ref, i, n_experts) + meta[3:4].astype(jnp.int32)
    slot = lax.broadcasted_iota(jnp.int32, (bufs.shape[1], tm), 0)
    perm = jnp.where(slot == slot1, 1.0, jnp.where(slot == slot2, 1.0, 0.0)).astype(_BF16)
    bufs[cur] = _bdot(perm, h_ref[...]).astype(bufs.dtype)
    _start_copies(copies(i, cur))

    @pl.when(i == last)
    def _():
        @pl.when(i >= 1)
        def _():
            _wait_copies(copies(i - 1, 1 - cur))

        _wait_copies(copies(i, cur))

        zeros[...] = jnp.zeros_like(zeros)
        fills = []
        for e in range(n_experts):
            start = pl.multiple_of(tail_ref[e], _SEG_ALIGN)
            n = pl.multiple_of(tail_ref[n_experts + e], _SEG_ALIGN)
            fills.append((n, pltpu.make_async_copy(zeros.at[pl.ds(0, n)],
                                                   xs_ref.at[pl.ds(start, n)], sems.at[0])))
        _run_copies(fills)

        def fill_tile(j, carry):
            cp = pltpu.make_async_copy(
                zeros, xs_ref.at[pl.ds(pl.multiple_of(j * _FFN_ROWS, _FFN_ROWS), _FFN_ROWS)],
                sems.at[0])
            cp.start()
            cp.wait()
            return carry

        lax.fori_loop(tail_ref[2 * n_experts], xs_ref.shape[0] // _FFN_ROWS, fill_tile, 0)


def _dispatch(h2, meta_l, plan, tails, n_slots, n_experts, tm):
    n, d = h2.shape
    seg_start, pad, loc_off = plan
    return pl.pallas_call(
        functools.partial(_dispatch_kernel, n_experts=n_experts),
        grid_spec=pltpu.PrefetchScalarGridSpec(
            num_scalar_prefetch=4,
            grid=(n // tm,),
            in_specs=[pl.BlockSpec((tm, d), lambda i, *_: (i, 0)),
                      pl.BlockSpec((n_experts, tm), lambda i, *_: (i, 0))],
            out_specs=pl.BlockSpec(memory_space=pl.ANY),
            scratch_shapes=[pltpu.VMEM((2, _local_slots(tm, n_experts), d), _BF16),
                            pltpu.VMEM((_FFN_ROWS, d), _BF16),
                            pltpu.SemaphoreType.DMA((2,))],
        ),
        out_shape=jax.ShapeDtypeStruct((n_slots, d), _BF16),
        compiler_params=_params(1),
        name="dispatch",
    )(seg_start, pad, loc_off, tails, h2, meta_l)


_W_ROUNDS = 16
_W_ROUNDS_PER_STEP = 2
_ST_CUR, _ST_LOADED, _ST_TARGET, _ST_DONE = range(4)


def _expert_ffn_kernel(te_ref, nu_ref, nxt_ref, x_ref, wg_hbm, wu_hbm, wd_hbm, o_ref,
                       wg_v, wu_v, wd_v, stage_gu, stage_d, sems, state):
    j = pl.program_id(0)
    rg = wg_v.shape[1] // _W_ROUNDS
    rd = wd_v.shape[1] // _W_ROUNDS

    def round_copies(e, r, s):
        g_rows = pl.ds(pl.multiple_of(r * rg, rg), rg)
        d_rows = pl.ds(pl.multiple_of(r * rd, rd), rd)
        return (pltpu.make_async_copy(wg_hbm.at[e, g_rows], stage_gu.at[s, 0], sems.at[s, 0]),
                pltpu.make_async_copy(wu_hbm.at[e, g_rows], stage_gu.at[s, 1], sems.at[s, 1]),
                pltpu.make_async_copy(wd_hbm.at[e, d_rows], stage_d.at[s], sems.at[s, 2]))

    def narrow_round(r, s, copy):
        g_rows = pl.ds(pl.multiple_of(r * rg, rg), rg)
        d_rows = pl.ds(pl.multiple_of(r * rd, rd), rd)
        wg_v[copy, g_rows, :] = stage_gu[s, 0].astype(_BF16)
        wu_v[copy, g_rows, :] = stage_gu[s, 1].astype(_BF16)
        wd_v[copy, d_rows, :] = stage_d[s].astype(_BF16)

    @pl.when(j == 0)
    def _():
        state[_ST_CUR] = 0
        state[_ST_LOADED] = -1
        state[_ST_TARGET] = -1
        state[_ST_DONE] = _W_ROUNDS

    active = j < nu_ref[0]
    e = te_ref[j]

    @pl.when(jnp.logical_and(active, e != state[_ST_LOADED]))
    def _():
        other = 1 - state[_ST_CUR]

        @pl.when(state[_ST_TARGET] != e)
        def _():
            state[_ST_TARGET] = e
            state[_ST_DONE] = 0

        def load_round(r, carry):
            cps = round_copies(e, r, 0)
            for cp in cps:
                cp.start()
            for cp in cps:
                cp.wait()
            narrow_round(r, 0, other)
            return carry

        lax.fori_loop(state[_ST_DONE], _W_ROUNDS, load_round, 0)
        nxt = nxt_ref[e]
        state[_ST_CUR] = other
        state[_ST_LOADED] = e
        state[_ST_TARGET] = nxt
        state[_ST_DONE] = jnp.where(nxt != e, 0, _W_ROUNDS)

    cur = state[_ST_CUR]
    target = state[_ST_TARGET]
    done = state[_ST_DONE]
    prefetch = jnp.logical_and(active, done < _W_ROUNDS)

    @pl.when(prefetch)
    def _():
        for s in range(_W_ROUNDS_PER_STEP):
            for cp in round_copies(target, done + s, s):
                cp.start()

    @pl.when(active)
    def _():
        x = x_ref[...]
        gate = _bdot(x, wg_v[cur])
        act = (gate * _sigmoid(gate)) * _bdot(x, wu_v[cur])
        o_ref[...] = _bdot(act.astype(_BF16), wd_v[cur]).astype(o_ref.dtype)

    @pl.when(jnp.logical_not(active))
    def _():
        o_ref[...] = jnp.zeros_like(o_ref)

    @pl.when(prefetch)
    def _():
        for s in range(_W_ROUNDS_PER_STEP):
            for cp in round_copies(target, done + s, s):
                cp.wait()
            narrow_round(done + s, s, 1 - cur)
        state[_ST_DONE] = done + _W_ROUNDS_PER_STEP


def _expert_ffn(xs, tile_expert, n_used, next_expert, wg, wu, wd):
    n_slots, d = xs.shape
    n_experts, _, f = wg.shape
    assert d % (_W_ROUNDS * _BF16_SUBLANES) == 0 and f % (_W_ROUNDS * _BF16_SUBLANES) == 0
    assert _W_ROUNDS % _W_ROUNDS_PER_STEP == 0
    any_spec = pl.BlockSpec(memory_space=pl.ANY)
    return pl.pallas_call(
        _expert_ffn_kernel,
        grid_spec=pltpu.PrefetchScalarGridSpec(
            num_scalar_prefetch=3,
            grid=(n_slots // _FFN_ROWS,),
            in_specs=[
                pl.BlockSpec((_FFN_ROWS, d),
                             lambda j, te, nu, nx: (jnp.maximum(jnp.minimum(j, nu[0] - 1), 0), 0)),
                any_spec, any_spec, any_spec,
            ],
            out_specs=pl.BlockSpec((_FFN_ROWS, d), lambda j, te, nu, nx: (j, 0)),
            scratch_shapes=[
                pltpu.VMEM((2, d, f), _BF16), pltpu.VMEM((2, d, f), _BF16),
                pltpu.VMEM((2, f, d), _BF16),
                pltpu.VMEM((_W_ROUNDS_PER_STEP, 2, d // _W_ROUNDS, f), _F32),
                pltpu.VMEM((_W_ROUNDS_PER_STEP, f // _W_ROUNDS, d), _F32),
                pltpu.SemaphoreType.DMA((_W_ROUNDS_PER_STEP, 3)),
                pltpu.SMEM((4,), jnp.int32),
            ],
        ),
        out_shape=jax.ShapeDtypeStruct((n_slots, d), _BF16),
        compiler_params=_params(1),
        name="expert_ffn",
    )(tile_expert, n_used, next_expert, xs, wg, wu, wd)


def _combine_kernel(seg_ref, pad_ref, loc_ref, ys_ref, ms_ref, x1_ref, g2_ref, nf_ref, o_ref,
                    bufs, sems, *, n_experts):
    i = pl.program_id(0)
    tm = x1_ref.shape[0]
    cur = lax.rem(i, 2)

    def copies(tile, b):
        return _segment_copies(tile, n_experts, seg_ref, pad_ref, loc_ref, ys_ref, bufs.at[b],
                               sems.at[b], False)

    @pl.when(i == 0)
    def _():
        bufs[...] = jnp.zeros_like(bufs)
        _start_copies(copies(0, 0))

    @pl.when(i + 1 < pl.num_programs(0))
    def _():
        _start_copies(copies(i + 1, 1 - cur))

    _wait_copies(copies(i, cur))
    rc = min(tm, _MIX_CHUNK_ROWS)
    slot = lax.broadcasted_iota(jnp.int32, (rc, bufs.shape[1]), 1)
    for c in range(tm // rc):
        rows = slice(c * rc, (c + 1) * rc)
        meta = ms_ref[rows, :]
        col = lambda k: meta[:, k:k + 1]
        slot1 = _expert_offset(col(0).astype(jnp.int32), loc_ref, i, n_experts) + col(2).astype(jnp.int32)
        slot2 = _expert_offset(col(1).astype(jnp.int32), loc_ref, i, n_experts) + col(3).astype(jnp.int32)
        weights = jnp.where(slot == slot1, col(4), jnp.where(slot == slot2, col(5), 0.0))
        y = _bdot(weights.astype(_BF16), bufs[cur])
        o_ref[rows, :] = _rmsnorm(x1_ref[rows, :] + g2_ref[...] * y, nf_ref[...])


def _combine(ys, meta_s, x1, plan, mods, row_fn, norm_final, n_experts, tm):
    n, d = x1.shape
    seg_start, pad, loc_off = plan
    return pl.pallas_call(
        functools.partial(_combine_kernel, n_experts=n_experts),
        grid_spec=pltpu.PrefetchScalarGridSpec(
            num_scalar_prefetch=3,
            grid=(n // tm,),
            in_specs=[pl.BlockSpec(memory_space=pl.ANY),
                      pl.BlockSpec((tm, _LANES), lambda i, *_: (i, 0)),
                      pl.BlockSpec((tm, d), lambda i, *_: (i, 0)),
                      _mod_spec(d, 5, row_fn),
                      _resident((1, d))],
            out_specs=pl.BlockSpec((tm, d), lambda i, *_: (i, 0)),
            scratch_shapes=[pltpu.VMEM((2, _local_slots(tm, n_experts), d), _BF16),
                            pltpu.SemaphoreType.DMA((2,))],
        ),
        out_shape=jax.ShapeDtypeStruct((n, d), _F32),
        compiler_params=_params(1),
        name="combine",
    )(seg_start, pad, loc_off, ys, meta_s, x1, mods, norm_final)


def kernel(x, c, ctx, c_ctx, w_ada, b_ada, norm_mix, norm_ffn, w_in, w_conv, rpb, w_attn_out,
           w_conv_out, w_gate, b_gate, w_out, w_ffn_gate, w_ffn_up, w_ffn_down, w_router,
           w_exp_gate, w_exp_up, w_exp_down, norm_final):
    batch, seq, d = x.shape
    ctx_len = ctx.shape[1]
    depth = w_ada.shape[0]
    attn_w = _N_HEADS * _HEAD_DIM
    n_experts = w_router.shape[-1]
    rows = seq // _GRID_W
    n_pairs = attn_w // _LANES
    assert depth == 2 and seq % (_Q_ROWS * _GRID_W) == 0 and attn_w % _LANES == 0
    assert n_experts == _F32_SUBLANES

    ctx_row = batch
    cond_rows = -(-(batch + 1) // 8) * 8
    cond = jnp.zeros((cond_rows, d), _F32).at[:batch].set(c).at[ctx_row].set(c_ctx)
    mods_all = _ada(cond, w_ada, b_ada, tn=1536).reshape(depth, cond_rows, 1, 6 * d)

    union, starts, variant_of, patterns, col_starts = _attn_plan(rows)

    x2d = x.reshape(batch * seq, d)
    c2d = ctx.reshape(batch * ctx_len, d)
    tm_lat = 512
    ctx_row_fn = lambda i: ctx_row

    def lat_rows(tm):
        return lambda i: (i * tm) // seq

    out = None
    for l in range(depth):
        last = l == depth - 1
        mods = mods_all[l]
        wts = {
            "norm_mix": norm_mix[l].reshape(1, d),
            "norm_ffn": norm_ffn[l].reshape(1, d),
            "wconv": w_conv[l],
            "wao": w_attn_out[l].astype(_BF16),
            "wco": w_conv_out[l].astype(_BF16),
            "wgate": w_gate[l].astype(_BF16),
            "bgate": b_gate[l].reshape(1, -1),
            "wout": w_out[l].astype(_BF16),
        }
        w_in_l = w_in[l].astype(_BF16)
        p = _proj(x2d, mods, lat_rows(512), wts["norm_mix"], w_in_l, tm=512)
        if last:
            pc = _proj(c2d, mods, ctx_row_fn, wts["norm_mix"], w_in_l[:, attn_w:3 * attn_w], tm=512)
            kc_col = 0
        else:
            pc = _proj(c2d, mods, ctx_row_fn, wts["norm_mix"], w_in_l, tm=512)
            kc_col = n_pairs
        table = _bias_table(rpb[l], union, patterns, col_starts)
        ya, yca = _attention(p, pc, kc_col, table, starts, variant_of, union, col_starts, batch,
                             seq, ctx_len, with_ctx=not last)
        if not last:
            wts["ffn_gate"] = w_ffn_gate[l // 2].astype(_BF16)
            wts["ffn_up"] = w_ffn_up[l // 2].astype(_BF16)
            wts["ffn_down"] = w_ffn_down[l // 2].astype(_BF16)
            x2d = _merge_ffn(x2d, ya, p, mods, lat_rows(tm_lat), seq, wts, tm=tm_lat)
            c2d = _merge_ffn(c2d, yca, pc, mods, ctx_row_fn, ctx_len, wts, tm=ctx_len)
        else:
            wr_t = w_router[l // 2].T
            wr_hi = wr_t.astype(_BF16)
            wr_lo = (wr_t - wr_hi.astype(_F32)).astype(_BF16)
            wts["router"] = jnp.concatenate([wr_hi, wr_lo], axis=0)
            tm_moe = 512
            n_tok = batch * seq
            n_tiles = n_tok // tm_moe
            tm_mix = 2 * tm_moe
            x1, h2, meta_s, meta_l, cnt = _merge_router(
                x2d, ya, p, mods, lat_rows(tm_mix), seq, wts, n_experts, tm=tm_mix,
                route_rows=tm_moe)
            n_slots = _round_up(_TOP_K * n_tok + n_tiles * n_experts * (_SEG_ALIGN - 1)
                                + n_experts * (_FFN_ROWS - 1), _FFN_ROWS)
            cnt = cnt[:, 0].reshape(n_tiles, n_experts).astype(jnp.int32)
            plan, tails, tile_expert, n_used, next_expert = _routing_plan(cnt, n_slots)
            xs = _dispatch(h2, meta_l, plan, tails, n_slots, n_experts, tm=tm_moe)
            ys = _expert_ffn(xs, tile_expert, n_used, next_expert, w_exp_gate[l // 2],
                             w_exp_up[l // 2], w_exp_down[l // 2])
            out = _combine(ys, meta_s, x1, plan, mods, lat_rows(tm_moe), norm_final.reshape(1, d),
                           n_experts, tm=tm_moe)
    return out.reshape(batch, seq, d)
```

```python
import functools

import numpy as np
import jax
import jax.numpy as jnp
from jax import lax
from jax.experimental import pallas as pl
from jax.experimental.pallas import tpu as pltpu

_F32 = jnp.float32
_BF16 = jnp.bfloat16

_GRID_W = 64
_N_HEADS = 8
_HEAD_DIM = 64
_NA_ROWS = 8
_NA_COLS = 16
_CONV_K = 3
_TOP_K = 2
_EPS = 1e-6
_NEG_INF = -1e30

_LANES = 128
_F32_SUBLANES = 8
_BF16_SUBLANES = 16
_VMEM_LIMIT_BYTES = 56 * 1024 * 1024

_Q_ROWS = 8
_Q_COLS = 16
_K_COLS = 32
_ATTN_BATCHES_PER_STEP = 4
_MIX_CHUNK_ROWS = 256
_HEADS_PER_BLOCK = _LANES // _HEAD_DIM
_NT = (((1,), (1,)), ((), ()))
_TN = (((0,), (0,)), ((), ()))


def _params(n_axes):
    return pltpu.CompilerParams(
        dimension_semantics=("arbitrary",) * n_axes,
        vmem_limit_bytes=_VMEM_LIMIT_BYTES)


def _resident(shape):
    nd = len(shape)
    return pl.BlockSpec(shape, lambda *_: (0,) * nd, pipeline_mode=pl.Buffered(1))


def _sigmoid(x):
    return 1.0 / (1.0 + jnp.exp(-x))


def _rmsnorm(x, g):
    ms = jnp.mean(x * x, axis=-1, keepdims=True)
    return (x * lax.rsqrt(ms + _EPS)) * g


def _modulate(x, g, shift, scale):
    return _rmsnorm(x, g) * (1.0 + scale) + shift


def _bdot(a, b):
    return jnp.dot(a, b, preferred_element_type=_F32)


def _indicator(mask):
    return jnp.where(mask, 1.0, 0.0).astype(_BF16)


def _ada_kernel(cond_ref, w_ref, b_ref, o_ref):
    cond = cond_ref[...]
    act = cond * _sigmoid(cond)
    o_ref[...] = jnp.dot(act, w_ref[...], precision=lax.Precision.HIGHEST,
                         preferred_element_type=_F32) + b_ref[...]


def _ada(cond, w_ada, b_ada, tn):
    depth, d, n_out = w_ada.shape
    rows = cond.shape[0]
    return pl.pallas_call(
        _ada_kernel,
        grid=(depth, n_out // tn),
        in_specs=[
            pl.BlockSpec((rows, d), lambda l, j: (0, 0)),
            pl.BlockSpec((None, d, tn), lambda l, j: (l, 0, j)),
            pl.BlockSpec((None, 1, tn), lambda l, j: (l, 0, j)),
        ],
        out_specs=pl.BlockSpec((None, rows, tn), lambda l, j: (l, 0, j)),
        out_shape=jax.ShapeDtypeStruct((depth, rows, n_out), _F32),
        compiler_params=_params(2),
        name="ada",
    )(cond, w_ada, b_ada.reshape(depth, 1, n_out))


def _mod_spec(d, chunk, row_fn):
    return pl.BlockSpec((None, 1, d), lambda i, *_: (row_fn(i), 0, chunk))


def _proj_kernel(x_ref, g_ref, sh_ref, sc_ref, w_ref, o_ref):
    h = _modulate(x_ref[...], g_ref[...], sh_ref[...], sc_ref[...])
    o_ref[...] = _bdot(h.astype(_BF16), w_ref[...]).astype(o_ref.dtype)


def _proj(x2d, mods, row_fn, g, w, tm):
    n, d = x2d.shape
    n_out = w.shape[1]
    return pl.pallas_call(
        _proj_kernel,
        grid=(n // tm,),
        in_specs=[
            pl.BlockSpec((tm, d), lambda i: (i, 0)),
            _resident((1, d)),
            _mod_spec(d, 0, row_fn),
            _mod_spec(d, 1, row_fn),
            _resident(w.shape),
        ],
        out_specs=pl.BlockSpec((tm, n_out), lambda i: (i, 0)),
        out_shape=jax.ShapeDtypeStruct((n, n_out), _BF16),
        compiler_params=_params(1),
        name="proj",
    )(x2d, g, mods, mods, w)


def _bias_table_kernel(rpb_ref, o_ref, *, union, patterns, col_starts):
    sub = _F32_SUBLANES
    span = _NA_COLS - 1
    slots = _LANES // _K_COLS
    rp = rpb_ref[...]
    lane = lax.broadcasted_iota(jnp.int32, (sub, _LANES), 1)
    srow = lax.broadcasted_iota(jnp.int32, (sub, _LANES), 0)
    ck_local = jnp.bitwise_and(lane, _K_COLS - 1)
    slot_of_lane = jnp.right_shift(lane, _K_COLS.bit_length() - 1)
    neg = jnp.full((sub, _LANES), _NEG_INF, _F32)

    def row_vec(dr):
        return jnp.broadcast_to(rp[dr + _NA_ROWS - 1:dr + _NA_ROWS, :], (sub, _LANES))

    for g, c0 in enumerate(col_starts):
        for gg in range(_Q_COLS // sub):
            k0 = c0 - _Q_COLS * g - sub * gg + span
            lo, hi = max(-(sub - 1), -k0), min(_K_COLS - 1, 2 * span - k0)
            cq = srow + (_Q_COLS * g + sub * gg)
            cs = jnp.clip(cq - _NA_COLS // 2, 0, _GRID_W - _NA_COLS)
            ck = ck_local + c0
            in_win = jnp.logical_and(ck >= cs, ck < cs + _NA_COLS)
            part = [jnp.bitwise_and(lane - (_K_COLS * jq + lo), _LANES - 1) <= hi - lo
                    for jq in range(slots)]
            tiles = {}

            def make_tile(drs):
                base = jnp.zeros((sub, _LANES), _F32)
                keep = None
                for jq, dr in enumerate(drs):
                    if dr is None:
                        continue
                    shifted = pltpu.roll(row_vec(dr), (_K_COLS * jq - k0) % _LANES, 1)
                    base = jnp.where(part[jq], shifted, base)
                    here = slot_of_lane == jq
                    keep = here if keep is None else jnp.logical_or(keep, here)
                if keep is None:
                    return neg
                tile = pltpu.roll(base, 0, 1, stride=1, stride_axis=0)
                return jnp.where(jnp.logical_and(in_win, keep), tile, neg)

            for v, pat in enumerate(patterns):
                for a in range(_Q_ROWS):
                    rows = slice(a * _Q_COLS + sub * gg, a * _Q_COLS + sub * (gg + 1))
                    for quad in range(-(-union // slots)):
                        js = [quad * slots + jq for jq in range(slots)]
                        drs = tuple(pat[a * union + j] if j < union else None for j in js)
                        width = min(slots, union - quad * slots) * _K_COLS
                        if drs not in tiles:
                            tiles[drs] = make_tile(drs)
                        o_ref[v, g, rows, quad * _LANES:quad * _LANES + width] = (
                            tiles[drs][:, :width])


def _bias_table(rpb_l, union, patterns, col_starts):
    h, n_dr, n_dc = rpb_l.shape
    assert n_dc <= _LANES and _LANES % _K_COLS == 0
    n_dr_pad = _round_up(n_dr, _F32_SUBLANES)
    padded = jnp.pad(rpb_l, ((0, 0), (0, n_dr_pad - n_dr), (0, _LANES - n_dc)))
    n_var, n_grp = len(patterns), len(col_starts)
    qt, kt = _Q_ROWS * _Q_COLS, union * _K_COLS
    return pl.pallas_call(
        functools.partial(_bias_table_kernel, union=union, patterns=patterns,
                          col_starts=col_starts),
        grid=(h,),
        in_specs=[pl.BlockSpec((None, n_dr_pad, _LANES), lambda i: (i, 0, 0))],
        out_specs=pl.BlockSpec((n_var, n_grp, None, qt, kt), lambda i: (0, 0, i, 0, 0)),
        out_shape=jax.ShapeDtypeStruct((n_var, n_grp, h, qt, kt), _F32),
        compiler_params=_params(1),
        name="bias_table",
    )(padded)


def _attn_plan(rows):
    kh = min(_NA_ROWS, rows)
    union = min(_Q_ROWS + kh - 1, rows)
    n_blocks = rows // _Q_ROWS
    starts, variant_of, patterns = [], [], []
    for t in range(n_blocks):
        start = int(np.clip(_Q_ROWS * t - kh // 2, 0, rows - union))
        pat = []
        for a in range(_Q_ROWS):
            r = _Q_ROWS * t + a
            rs = int(np.clip(r - kh // 2, 0, rows - kh))
            for j in range(union):
                kr = start + j
                pat.append(kr - r if rs <= kr < rs + kh else None)
        pat = tuple(pat)
        if pat not in patterns:
            patterns.append(pat)
        starts.append(start)
        variant_of.append(patterns.index(pat))
    col_starts = []
    for g in range(_GRID_W // _Q_COLS):
        c0 = int(np.clip(_Q_COLS * g - _NA_COLS // 2, 0, _GRID_W - _K_COLS))
        for cq in range(_Q_COLS * g, _Q_COLS * (g + 1)):
            cs = int(np.clip(cq - _NA_COLS // 2, 0, _GRID_W - _NA_COLS))
            assert c0 <= cs and cs + _NA_COLS <= c0 + _K_COLS
        col_starts.append(c0)
    return union, starts, variant_of, tuple(patterns), tuple(col_starts)


def _first_head_lanes(shape):
    return lax.broadcasted_iota(jnp.int32, shape, len(shape) - 1) < _HEAD_DIM


def _values_with_ones(v):
    lo = _first_head_lanes(v.shape)
    ones = jnp.ones_like(v)
    return jnp.where(lo, v, ones), jnp.where(lo, ones, v)


def _scores(qb, keys, bias_fns):
    lo = _first_head_lanes(qb.shape)
    out = []
    for hh in range(_HEADS_PER_BLOCK):
        qm = jnp.where(lo if hh == 0 else jnp.logical_not(lo), qb, jnp.zeros_like(qb))
        scores = []
        for kk, bias_fn in zip(keys, bias_fns):
            s = lax.dot_general(qm, kk, _NT, preferred_element_type=_F32)
            scores.append(s if bias_fn is None else s + bias_fn(hh))
        out.append(scores)
    return out


def _probabilities(scores):
    m = functools.reduce(jnp.maximum, [jnp.max(s, axis=-1, keepdims=True) for s in scores])
    return [jnp.exp(s - m).astype(_BF16) for s in scores]


def _weighted_values(probs, values):
    res = []
    for hh in range(_HEADS_PER_BLOCK):
        acc = None
        for e, v1 in zip(probs[hh], values):
            o = _bdot(e, v1[hh])
            acc = o if acc is None else acc + o
        res.append(acc)
    lo = _first_head_lanes(res[0].shape)
    num = jnp.where(lo, res[0], res[1])
    den = pltpu.roll(jnp.where(lo, res[1], res[0]), _HEAD_DIM, 1)
    return num / den


def _attn_kernel(start_ref, var_ref, q_ref, k_ref, v_ref, kc_ref, vc_ref, tab_ref, *rest,
                 n_row_blocks, n_batch, seq, ctx_len, union, col_starts, with_ctx):
    if with_ctx:
        qc_ref, o_ref, oc_ref, s_loc, s_ctx, e_loc, e_ctx = rest
    else:
        o_ref, s_loc, s_ctx, e_loc, e_ctx = rest
    heads = range(_HEADS_PER_BLOCK)
    n_groups = len(col_starts)
    assert n_groups % 2 == 0 and n_groups >= 2 and n_row_blocks >= 1
    scale = _HEAD_DIM ** -0.5
    rb_tokens = _Q_ROWS * _GRID_W

    def gather(ref, first_row, n_rows, col, width):
        return jnp.concatenate(
            [ref[pl.ds(first_row + r * _GRID_W + col, width), :] for r in range(n_rows)], axis=0)

    def ctx_rows(b):
        return pl.ds(pl.multiple_of(b * ctx_len, ctx_len), ctx_len)

    def key_rows(rb):
        b, t = rb // n_row_blocks, rb % n_row_blocks
        return pl.multiple_of(b * seq + start_ref[t] * _GRID_W, _GRID_W)

    def query_rows(rb):
        return pl.multiple_of(rb * rb_tokens, rb_tokens)

    def scores_stage(rb, g):
        slot, var = g % 2, var_ref[rb % n_row_blocks]
        qb = gather(q_ref, query_rows(rb), _Q_ROWS, _Q_COLS * g, _Q_COLS) * scale
        kb = gather(k_ref, key_rows(rb), union, col_starts[g], _K_COLS)
        kc = kc_ref[ctx_rows(rb // n_row_blocks), :]
        scores = _scores(qb, [kb, kc], [lambda hh: tab_ref[var, g, hh], None])
        for hh in heads:
            s_loc[slot, hh], s_ctx[slot, hh] = scores[hh]

    def softmax_stage(g):
        slot = g % 2
        for hh in heads:
            e_loc[slot, hh], e_ctx[slot, hh] = _probabilities([s_loc[slot, hh], s_ctx[slot, hh]])

    def values_stage(rb, g):
        slot = g % 2
        vb1 = _values_with_ones(gather(v_ref, key_rows(rb), union, col_starts[g], _K_COLS))
        vc1 = _values_with_ones(vc_ref[ctx_rows(rb // n_row_blocks), :])
        probs = [[e_loc[slot, hh], e_ctx[slot, hh]] for hh in heads]
        o = _weighted_values(probs, [vb1, vc1]).astype(o_ref.dtype)
        for a in range(_Q_ROWS):
            o_ref[pl.ds(query_rows(rb) + a * _GRID_W + _Q_COLS * g, _Q_COLS), :] = (
                o[a * _Q_COLS:(a + 1) * _Q_COLS])

    def issue(rb, k):
        values_stage(rb + k // n_groups, k % n_groups)
        softmax_stage((k + 1) % n_groups)
        scores_stage(rb + (k + 2) // n_groups, (k + 2) % n_groups)

    scores_stage(0, 0)
    softmax_stage(0)
    scores_stage(0, 1)

    def body(rb, carry):
        for k in range(n_groups):
            issue(rb, k)
        return carry

    last = n_batch * n_row_blocks - 1
    lax.fori_loop(0, last, body, 0)
    for k in range(n_groups - 2):
        issue(last, k)
    values_stage(last, n_groups - 2)
    softmax_stage(n_groups - 1)
    values_stage(last, n_groups - 1)

    if with_ctx:
        def ctx_body(b, carry):
            rows = ctx_rows(b)
            scores = _scores(qc_ref[rows, :] * scale, [kc_ref[rows, :]], [None])
            probs = [_probabilities(scores[hh]) for hh in heads]
            vc1 = _values_with_ones(vc_ref[rows, :])
            oc_ref[rows, :] = _weighted_values(probs, [vc1]).astype(oc_ref.dtype)
            return carry

        lax.fori_loop(0, n_batch, ctx_body, 0)


def _attention(p, pc, kc_col, table, starts, variant_of, union, col_starts, batch, seq, ctx_len,
               with_ctx):
    attn_w = _N_HEADS * _HEAD_DIM
    n_pairs = attn_w // _LANES
    n_var, n_grp = table.shape[:2]
    qb_tokens = _Q_ROWS * _Q_COLS
    union_tokens = union * _K_COLS
    nb = _ATTN_BATCHES_PER_STEP
    assert batch % nb == 0
    kern = functools.partial(_attn_kernel, n_row_blocks=len(starts), n_batch=nb, seq=seq,
                             ctx_len=ctx_len, union=union, col_starts=col_starts,
                             with_ctx=with_ctx)
    lat = lambda col: pl.BlockSpec((nb * seq, _LANES), lambda hp, b, *_: (b, col(hp)))
    cx = lambda col: pl.BlockSpec((nb * ctx_len, _LANES), lambda hp, b, *_: (b, col(hp)))
    in_specs = [
        lat(lambda hp: hp), lat(lambda hp: n_pairs + hp), lat(lambda hp: 2 * n_pairs + hp),
        cx(lambda hp: kc_col + hp), cx(lambda hp: kc_col + n_pairs + hp),
        pl.BlockSpec((n_var, n_grp, _HEADS_PER_BLOCK, qb_tokens, union_tokens),
                     lambda hp, b, *_: (0, 0, hp, 0, 0)),
    ]
    args = [p, p, p, pc, pc, table]
    out_specs = [lat(lambda hp: hp)]
    out_shape = [jax.ShapeDtypeStruct((batch * seq, attn_w), _BF16)]
    if with_ctx:
        in_specs.append(cx(lambda hp: hp))
        args.append(pc)
        out_specs.append(cx(lambda hp: hp))
        out_shape.append(jax.ShapeDtypeStruct((batch * ctx_len, attn_w), _BF16))
    outs = pl.pallas_call(
        kern,
        grid_spec=pltpu.PrefetchScalarGridSpec(
            num_scalar_prefetch=2,
            grid=(n_pairs, batch // nb),
            in_specs=in_specs,
            out_specs=out_specs,
            scratch_shapes=[
                pltpu.VMEM((2, _HEADS_PER_BLOCK, qb_tokens, union_tokens), _F32),
                pltpu.VMEM((2, _HEADS_PER_BLOCK, qb_tokens, ctx_len), _F32),
                pltpu.VMEM((2, _HEADS_PER_BLOCK, qb_tokens, union_tokens), _BF16),
                pltpu.VMEM((2, _HEADS_PER_BLOCK, qb_tokens, ctx_len), _BF16),
            ],
        ),
        out_shape=out_shape,
        compiler_params=_params(2),
        name="attn",
    )(jnp.asarray(starts, jnp.int32), jnp.asarray(variant_of, jnp.int32), *args)
    return outs if with_ctx else (outs[0], None)


def _mixer_update(i, x_ref, ya_ref, bcu_ref, prev_ref, next_ref, nm_ref, sh1_ref, sc1_ref, g1_ref,
                  wconv_ref, wao_ref, wco_ref, wgate_ref, bgate_ref, wout_ref, seq_len):
    tm, d = x_ref.shape
    cw = wco_ref.shape[0]

    bcu = bcu_ref[...].astype(_F32)
    bg, z = bcu[:, :cw], bcu[:, cw:2 * cw] * bcu[:, 2 * cw:]
    prev = prev_ref[...].astype(_F32)
    nxt = next_ref[...].astype(_F32)
    has_prev = ((i * tm) % seq_len != 0).astype(_F32)
    has_next = (((i + 1) * tm) % seq_len != 0).astype(_F32)
    last = prev.shape[0] - 1
    z_prev = prev[last:last + 1, cw:2 * cw] * prev[last:last + 1, 2 * cw:] * has_prev
    z_next = nxt[0:1, cw:2 * cw] * nxt[0:1, 2 * cw:] * has_next
    row = lax.broadcasted_iota(jnp.int32, z.shape, 0)
    z_dn = jnp.where(row == 0, z_prev, pltpu.roll(z, 1, 0))
    z_up = jnp.where(row == tm - 1, z_next, pltpu.roll(z, tm - 1, 0))
    wconv = wconv_ref[...]
    y_conv = (bg * (z_dn * wconv[0:1] + z * wconv[1:2] + z_up * wconv[2:3])).astype(_BF16)

    chunks = []
    rc = min(tm, _MIX_CHUNK_ROWS)
    for c in range(tm // rc):
        rows = slice(c * rc, (c + 1) * rc)
        x = x_ref[rows, :]
        hb = _modulate(x, nm_ref[...], sh1_ref[...], sc1_ref[...]).astype(_BF16)
        gates = _sigmoid(_bdot(hb, wgate_ref[...]) + bgate_ref[...])
        a_proj = _bdot(ya_ref[rows, :], wao_ref[...])
        c_proj = _bdot(y_conv[rows, :], wco_ref[...])
        merged = gates[:, :d] * a_proj + gates[:, d:] * c_proj
        chunks.append(x + g1_ref[...] * _bdot(merged.astype(_BF16), wout_ref[...]))
    return jnp.concatenate(chunks, axis=0)


_N_MIX_REFS = 15


def _merge_ffn_kernel(*refs, seq_len):
    mix = refs[:_N_MIX_REFS]
    nf_ref, sh2_ref, sc2_ref, g2_ref, wg_ref, wu_ref, wd_ref, o_ref = refs[_N_MIX_REFS:]
    x1 = _mixer_update(pl.program_id(0), *mix, seq_len=seq_len)
    h2 = _modulate(x1, nf_ref[...], sh2_ref[...], sc2_ref[...]).astype(_BF16)
    gate = _bdot(h2, wg_ref[...])
    act = (gate * _sigmoid(gate)) * _bdot(h2, wu_ref[...])
    o_ref[...] = x1 + g2_ref[...] * _bdot(act.astype(_BF16), wd_ref[...])


def _top2(vals, idx, n_idx, axis):
    v1 = jnp.max(vals, axis=axis, keepdims=True)
    i1 = jnp.min(jnp.where(vals == v1, idx, n_idx), axis=axis, keepdims=True)
    rest = jnp.where(idx == i1, -jnp.inf, vals)
    v2 = jnp.max(rest, axis=axis, keepdims=True)
    i2 = jnp.min(jnp.where(rest == v2, idx, n_idx), axis=axis, keepdims=True)
    return i1, i2, v1, v2


def _pick(idx, sel, vals, axis):
    return jnp.sum(jnp.where(idx == sel, vals, 0.0), axis=axis, keepdims=True)


def _merge_router_kernel(*refs, seq_len, n_experts, route_rows):
    mix = refs[:_N_MIX_REFS]
    (nf_ref, sh2_ref, sc2_ref, wr_ref,
     x1_ref, h2_ref, ms_ref, ml_ref, cnt_ref) = refs[_N_MIX_REFS:]
    x1 = _mixer_update(pl.program_id(0), *mix, seq_len=seq_len)
    x1_ref[...] = x1
    h2 = _modulate(x1, nf_ref[...], sh2_ref[...], sc2_ref[...])
    h_hi = h2.astype(_BF16)
    h2_ref[...] = h_hi
    tm = h2.shape[0]
    h_lo = (h2 - h_hi.astype(_F32)).astype(_BF16)
    wr = wr_ref[...]
    by_hi = lax.dot_general(wr, h_hi, _NT, preferred_element_type=_F32)
    by_lo = lax.dot_general(wr, h_lo, _NT, preferred_element_type=_F32)
    lt_all = by_hi[:n_experts] + by_hi[n_experts:] + by_lo[:n_experts]
    row = lax.broadcasted_iota(jnp.int32, (route_rows, route_rows), 0)
    col = lax.broadcasted_iota(jnp.int32, (route_rows, route_rows), 1)
    earlier_rows = _indicator(col < row)
    earlier_cols = _indicator(row < col)
    lane = lax.broadcasted_iota(jnp.int32, (route_rows, _LANES), 1)
    sub = lax.broadcasted_iota(jnp.int32, (n_experts, route_rows), 0)

    for u in range(tm // route_rows):
        lt = lt_all[:, u * route_rows:(u + 1) * route_rows]

        logits = jnp.concatenate(
            [lt, jnp.full((_LANES - n_experts, route_rows), -jnp.inf, _F32)], axis=0).T
        i1, i2, v1, v2 = _top2(logits, lane, _LANES, -1)
        e2 = jnp.exp(v2 - v1)
        den = 1.0 + e2
        sel = _indicator(jnp.logical_or(lane == i1, lane == i2))
        ranks = _bdot(earlier_rows, sel)
        fields = (i1.astype(_F32), i2.astype(_F32), _pick(lane, i1, ranks, -1),
                  _pick(lane, i2, ranks, -1), 1.0 / den, e2 / den)
        meta = jnp.zeros(logits.shape, _F32)
        for k, f in enumerate(fields):
            meta = jnp.where(lane == k, f, meta)
        ms_ref[u * route_rows:(u + 1) * route_rows, :] = meta

        j1, j2, _, _ = _top2(lt, sub, n_experts, 0)
        sel_t = _indicator(jnp.logical_or(sub == j1, sub == j2))
        ranks_t = _bdot(sel_t, earlier_cols)
        fields_t = (j1.astype(_F32), j2.astype(_F32), _pick(sub, j1, ranks_t, 0),
                    _pick(sub, j2, ranks_t, 0))
        meta_t = jnp.zeros(lt.shape, _F32)
        for k, f in enumerate(fields_t):
            meta_t = jnp.where(sub == k, f, meta_t)
        experts = slice(u * n_experts, (u + 1) * n_experts)
        ml_ref[experts, :] = meta_t
        cnt = jnp.sum(sel_t.astype(_F32), axis=1, keepdims=True)
        cnt_ref[experts, :] = jnp.broadcast_to(cnt, (n_experts, _LANES))


def _mixer_specs(n, d, tm, p_cols, row_fn, weights):
    half = p_cols // 2
    sub = _BF16_SUBLANES
    n_sub = n // sub
    specs = [
        pl.BlockSpec((tm, d), lambda i: (i, 0)),
        pl.BlockSpec((tm, weights["wao"].shape[0]), lambda i: (i, 0)),
        pl.BlockSpec((tm, half), lambda i: (i, 1)),
        pl.BlockSpec((sub, half), lambda i: (jnp.maximum(i * (tm // sub) - 1, 0), 1)),
        pl.BlockSpec((sub, half), lambda i: (jnp.minimum((i + 1) * (tm // sub), n_sub - 1), 1)),
        _resident((1, d)),
        _mod_spec(d, 0, row_fn), _mod_spec(d, 1, row_fn), _mod_spec(d, 2, row_fn),
    ]
    for name in ("wconv", "wao", "wco", "wgate", "bgate", "wout"):
        specs.append(_resident(weights[name].shape))
    return specs


def _merge_ffn(x2d, ya, p, mods, row_fn, seq_len, wts, tm):
    n, d = x2d.shape
    specs = _mixer_specs(n, d, tm, p.shape[1], row_fn, wts)
    specs += [_resident((1, d)), _mod_spec(d, 3, row_fn), _mod_spec(d, 4, row_fn),
              _mod_spec(d, 5, row_fn),
              _resident(wts["ffn_gate"].shape), _resident(wts["ffn_up"].shape),
              _resident(wts["ffn_down"].shape)]
    return pl.pallas_call(
        functools.partial(_merge_ffn_kernel, seq_len=seq_len),
        grid=(n // tm,),
        in_specs=specs,
        out_specs=pl.BlockSpec((tm, d), lambda i: (i, 0)),
        out_shape=jax.ShapeDtypeStruct((n, d), _F32),
        compiler_params=_params(1),
        name="merge_ffn",
    )(x2d, ya, p, p, p, wts["norm_mix"], mods, mods, mods,
      wts["wconv"], wts["wao"], wts["wco"], wts["wgate"], wts["bgate"], wts["wout"],
      wts["norm_ffn"], mods, mods, mods, wts["ffn_gate"], wts["ffn_up"], wts["ffn_down"])


def _merge_router(x2d, ya, p, mods, row_fn, seq_len, wts, n_experts, tm, route_rows):
    n, d = x2d.shape
    assert tm % route_rows == 0
    specs = _mixer_specs(n, d, tm, p.shape[1], row_fn, wts)
    specs += [_resident((1, d)), _mod_spec(d, 3, row_fn), _mod_spec(d, 4, row_fn),
              _resident(wts["router"].shape)]
    tile = lambda w: pl.BlockSpec((tm, w), lambda i: (i, 0))
    meta_rows = n_experts * (tm // route_rows)
    n_route = n // route_rows
    return pl.pallas_call(
        functools.partial(_merge_router_kernel, seq_len=seq_len, n_experts=n_experts,
                          route_rows=route_rows),
        grid=(n // tm,),
        in_specs=specs,
        out_specs=[tile(d), tile(d), tile(_LANES),
                   pl.BlockSpec((meta_rows, route_rows), lambda i: (i, 0)),
                   pl.BlockSpec((meta_rows, _LANES), lambda i: (i, 0))],
        out_shape=[jax.ShapeDtypeStruct((n, d), _F32),
                   jax.ShapeDtypeStruct((n, d), _BF16),
                   jax.ShapeDtypeStruct((n, _LANES), _F32),
                   jax.ShapeDtypeStruct((n_route * n_experts, route_rows), _F32),
                   jax.ShapeDtypeStruct((n_route * n_experts, _LANES), _F32)],
        compiler_params=_params(1),
        name="merge_router",
    )(x2d, ya, p, p, p, wts["norm_mix"], mods, mods, mods,
      wts["wconv"], wts["wao"], wts["wco"], wts["wgate"], wts["bgate"], wts["wout"],
      wts["norm_ffn"], mods, mods, wts["router"])


_SEG_ALIGN = _BF16_SUBLANES
_FFN_ROWS = 256


def _round_up(v, m):
    return (v + m - 1) // m * m


def _local_slots(tm, n_experts):
    return _round_up(_TOP_K * tm + n_experts * (_SEG_ALIGN - 1), _SEG_ALIGN)


def _routing_plan(cnt, n_slots):
    pad = _round_up(cnt, _SEG_ALIGN)
    total = jnp.sum(pad, axis=0)
    region = _round_up(total, _FFN_ROWS)
    ends = jnp.cumsum(region)
    seg_start = (ends - region)[None, :] + jnp.cumsum(pad, axis=0) - pad
    loc_off = jnp.cumsum(pad, axis=1) - pad
    n_used = (ends[-1] // _FFN_ROWS).astype(jnp.int32).reshape(1)
    tails = jnp.concatenate([ends - region + total, region - total, n_used])
    tile_row = jnp.arange(n_slots // _FFN_ROWS, dtype=jnp.int32) * _FFN_ROWS
    tile_expert = jnp.minimum(jnp.sum(tile_row[:, None] >= ends[None, :], axis=1),
                              cnt.shape[1] - 1).astype(jnp.int32)
    ids = jnp.arange(cnt.shape[1], dtype=jnp.int32)
    later = jnp.logical_and(region[None, :] > 0, ids[None, :] > ids[:, None])
    nxt = jnp.min(jnp.where(later, ids[None, :], cnt.shape[1]), axis=1)
    next_expert = jnp.where(nxt == cnt.shape[1], ids, nxt).astype(jnp.int32)
    flat = lambda a: a.reshape(-1).astype(jnp.int32)
    return ((flat(seg_start), flat(pad), flat(loc_off)), flat(tails), tile_expert, n_used,
            next_expert)


def _segment_copies(i, n_experts, seg_ref, pad_ref, loc_ref, hbm_ref, buf_ref, sem, to_hbm):
    copies = []
    for e in range(n_experts):
        n = pl.multiple_of(pad_ref[i * n_experts + e], _SEG_ALIGN)
        loc = pl.multiple_of(loc_ref[i * n_experts + e], _SEG_ALIGN)
        seg = pl.multiple_of(seg_ref[i * n_experts + e], _SEG_ALIGN)
        local, remote = buf_ref.at[pl.ds(loc, n)], hbm_ref.at[pl.ds(seg, n)]
        src, dst = (local, remote) if to_hbm else (remote, local)
        copies.append((n, pltpu.make_async_copy(src, dst, sem)))
    return copies


def _start_copies(copies):
    for n, cp in copies:
        pl.when(n > 0)(cp.start)


def _wait_copies(copies):
    for n, cp in copies:
        pl.when(n > 0)(cp.wait)


def _run_copies(copies):
    _start_copies(copies)
    _wait_copies(copies)


def _expert_offset(expert, loc_ref, i, n_experts):
    off = jnp.zeros(expert.shape, jnp.int32)
    for e in range(n_experts):
        off = jnp.where(expert == e, loc_ref[i * n_experts + e], off)
    return off


def _dispatch_kernel(seg_ref, pad_ref, loc_ref, tail_ref, h_ref, ml_ref, xs_ref, bufs, zeros, sems,
                     *, n_experts):
    i = pl.program_id(0)
    last = pl.num_programs(0) - 1
    cur = lax.rem(i, 2)

    def copies(tile, b):
        return _segment_copies(tile, n_experts, seg_ref, pad_ref, loc_ref, xs_ref, bufs.at[b],
                               sems.at[b], True)

    @pl.when(i >= 2)
    def _():
        _wait_copies(copies(i - 2, cur))

    tm = h_ref.shape[0]
    meta = ml_ref[...]
    slot1 = _expert_offset(meta[0:1].astype(jnp.int32), loc_ref, i, n_experts) + meta[2:3].astype(jnp.int32)
    slot2 = _expert_offset(meta[1:2].astype(jnp.int32), loc_ref, i, n_experts) + meta[3:4].astype(jnp.int32)
    slot = lax.broadcasted_iota(jnp.int32, (bufs.shape[1], tm), 0)
    perm = jnp.where(slot == slot1, 1.0, jnp.where(slot == slot2, 1.0, 0.0)).astype(_BF16)
    bufs[cur] = _bdot(perm, h_ref[...]).astype(bufs.dtype)
    _start_copies(copies(i, cur))

    @pl.when(i == last)
    def _():
        @pl.when(i >= 1)
        def _():
            _wait_copies(copies(i - 1, 1 - cur))

        _wait_copies(copies(i, cur))

        zeros[...] = jnp.zeros_like(zeros)
        fills = []
        for e in range(n_experts):
            start = pl.multiple_of(tail_ref[e], _SEG_ALIGN)
            n = pl.multiple_of(tail_ref[n_experts + e], _SEG_ALIGN)
            fills.append((n, pltpu.make_async_copy(zeros.at[pl.ds(0, n)],
                                                   xs_ref.at[pl.ds(start, n)], sems.at[0])))
        _run_copies(fills)

        def fill_tile(j, carry):
            cp = pltpu.make_async_copy(
                zeros, xs_ref.at[pl.ds(pl.multiple_of(j * _FFN_ROWS, _FFN_ROWS), _FFN_ROWS)],
                sems.at[0])
            cp.start()
            cp.wait()
            return carry

        lax.fori_loop(tail_ref[2 * n_experts], xs_ref.shape[0] // _FFN_ROWS, fill_tile, 0)


def _dispatch(h2, meta_l, plan, tails, n_slots, n_experts, tm):
    n, d = h2.shape
    seg_start, pad, loc_off = plan
    return pl.pallas_call(
        functools.partial(_dispatch_kernel, n_experts=n_experts),
        grid_spec=pltpu.PrefetchScalarGridSpec(
            num_scalar_prefetch=4,
            grid=(n // tm,),
            in_specs=[pl.BlockSpec((tm, d), lambda i, *_: (i, 0)),
                      pl.BlockSpec((n_experts, tm), lambda i, *_: (i, 0))],
            out_specs=pl.BlockSpec(memory_space=pl.ANY),
            scratch_shapes=[pltpu.VMEM((2, _local_slots(tm, n_experts), d), _BF16),
                            pltpu.VMEM((_FFN_ROWS, d), _BF16),
                            pltpu.SemaphoreType.DMA((2,))],
        ),
        out_shape=jax.ShapeDtypeStruct((n_slots, d), _BF16),
        compiler_params=_params(1),
        name="dispatch",
    )(seg_start, pad, loc_off, tails, h2, meta_l)


_W_ROUNDS = 16
_W_ROUNDS_PER_STEP = 2
_ST_CUR, _ST_LOADED, _ST_TARGET, _ST_DONE = range(4)


def _expert_ffn_kernel(te_ref, nu_ref, nxt_ref, x_ref, wg_hbm, wu_hbm, wd_hbm, o_ref,
                       wg_v, wu_v, wd_v, stage_gu, stage_d, sems, state):
    j = pl.program_id(0)
    rg = wg_v.shape[1] // _W_ROUNDS
    rd = wd_v.shape[1] // _W_ROUNDS

    def round_copies(e, r, s):
        g_rows = pl.ds(pl.multiple_of(r * rg, rg), rg)
        d_rows = pl.ds(pl.multiple_of(r * rd, rd), rd)
        return (pltpu.make_async_copy(wg_hbm.at[e, g_rows], stage_gu.at[s, 0], sems.at[s, 0]),
                pltpu.make_async_copy(wu_hbm.at[e, g_rows], stage_gu.at[s, 1], sems.at[s, 1]),
                pltpu.make_async_copy(wd_hbm.at[e, d_rows], stage_d.at[s], sems.at[s, 2]))

    def narrow_round(r, s, copy):
        g_rows = pl.ds(pl.multiple_of(r * rg, rg), rg)
        d_rows = pl.ds(pl.multiple_of(r * rd, rd), rd)
        wg_v[copy, g_rows, :] = stage_gu[s, 0].astype(_BF16)
        wu_v[copy, g_rows, :] = stage_gu[s, 1].astype(_BF16)
        wd_v[copy, d_rows, :] = stage_d[s].astype(_BF16)

    @pl.when(j == 0)
    def _():
        state[_ST_CUR] = 0
        state[_ST_LOADED] = -1
        state[_ST_TARGET] = -1
        state[_ST_DONE] = _W_ROUNDS

    active = j < nu_ref[0]
    e = te_ref[j]

    @pl.when(jnp.logical_and(active, e != state[_ST_LOADED]))
    def _():
        other = 1 - state[_ST_CUR]

        @pl.when(state[_ST_TARGET] != e)
        def _():
            state[_ST_TARGET] = e
            state[_ST_DONE] = 0

        def load_round(r, carry):
            cps = round_copies(e, r, 0)
            for cp in cps:
                cp.start()
            for cp in cps:
                cp.wait()
            narrow_round(r, 0, other)
            return carry

        lax.fori_loop(state[_ST_DONE], _W_ROUNDS, load_round, 0)
        nxt = nxt_ref[e]
        state[_ST_CUR] = other
        state[_ST_LOADED] = e
        state[_ST_TARGET] = nxt
        state[_ST_DONE] = jnp.where(nxt != e, 0, _W_ROUNDS)

    cur = state[_ST_CUR]
    target = state[_ST_TARGET]
    done = state[_ST_DONE]
    prefetch = jnp.logical_and(active, done < _W_ROUNDS)

    @pl.when(prefetch)
    def _():
        for s in range(_W_ROUNDS_PER_STEP):
            for cp in round_copies(target, done + s, s):
                cp.start()

    @pl.when(active)
    def _():
        x = x_ref[...]
        gate = _bdot(x, wg_v[cur])
        act = (gate * _sigmoid(gate)) * _bdot(x, wu_v[cur])
        o_ref[...] = _bdot(act.astype(_BF16), wd_v[cur]).astype(o_ref.dtype)

    @pl.when(jnp.logical_not(active))
    def _():
        o_ref[...] = jnp.zeros_like(o_ref)

    @pl.when(prefetch)
    def _():
        for s in range(_W_ROUNDS_PER_STEP):
            for cp in round_copies(target, done + s, s):
                cp.wait()
            narrow_round(done + s, s, 1 - cur)
        state[_ST_DONE] = done + _W_ROUNDS_PER_STEP


def _expert_ffn(xs, tile_expert, n_used, next_expert, wg, wu, wd):
    n_slots, d = xs.shape
    n_experts, _, f = wg.shape
    assert d % (_W_ROUNDS * _BF16_SUBLANES) == 0 and f % (_W_ROUNDS * _BF16_SUBLANES) == 0
    assert _W_ROUNDS % _W_ROUNDS_PER_STEP == 0
    any_spec = pl.BlockSpec(memory_space=pl.ANY)
    return pl.pallas_call(
        _expert_ffn_kernel,
        grid_spec=pltpu.PrefetchScalarGridSpec(
            num_scalar_prefetch=3,
            grid=(n_slots // _FFN_ROWS,),
            in_specs=[
                pl.BlockSpec((_FFN_ROWS, d),
                             lambda j, te, nu, nx: (jnp.maximum(jnp.minimum(j, nu[0] - 1), 0), 0)),
                any_spec, any_spec, any_spec,
            ],
            out_specs=pl.BlockSpec((_FFN_ROWS, d), lambda j, te, nu, nx: (j, 0)),
            scratch_shapes=[
                pltpu.VMEM((2, d, f), _BF16), pltpu.VMEM((2, d, f), _BF16),
                pltpu.VMEM((2, f, d), _BF16),
                pltpu.VMEM((_W_ROUNDS_PER_STEP, 2, d // _W_ROUNDS, f), _F32),
                pltpu.VMEM((_W_ROUNDS_PER_STEP, f // _W_ROUNDS, d), _F32),
                pltpu.SemaphoreType.DMA((_W_ROUNDS_PER_STEP, 3)),
                pltpu.SMEM((4,), jnp.int32),
            ],
        ),
        out_shape=jax.ShapeDtypeStruct((n_slots, d), _BF16),
        compiler_params=_params(1),
        name="expert_ffn",
    )(tile_expert, n_used, next_expert, xs, wg, wu, wd)


def _combine_kernel(seg_ref, pad_ref, loc_ref, ys_ref, ms_ref, x1_ref, g2_ref, nf_ref, o_ref,
                    bufs, sems, *, n_experts):
    i = pl.program_id(0)
    tm = x1_ref.shape[0]
    cur = lax.rem(i, 2)

    def copies(tile, b):
        return _segment_copies(tile, n_experts, seg_ref, pad_ref, loc_ref, ys_ref, bufs.at[b],
                               sems.at[b], False)

    @pl.when(i == 0)
    def _():
        bufs[...] = jnp.zeros_like(bufs)
        _start_copies(copies(0, 0))

    @pl.when(i + 1 < pl.num_programs(0))
    def _():
        _start_copies(copies(i + 1, 1 - cur))

    _wait_copies(copies(i, cur))
    rc = min(tm, _MIX_CHUNK_ROWS)
    slot = lax.broadcasted_iota(jnp.int32, (rc, bufs.shape[1]), 1)
    for c in range(tm // rc):
        rows = slice(c * rc, (c + 1) * rc)
        meta = ms_ref[rows, :]
        col = lambda k: meta[:, k:k + 1]
        slot1 = _expert_offset(col(0).astype(jnp.int32), loc_ref, i, n_experts) + col(2).astype(jnp.int32)
        slot2 = _expert_offset(col(1).astype(jnp.int32), loc_ref, i, n_experts) + col(3).astype(jnp.int32)
        weights = jnp.where(slot == slot1, col(4), jnp.where(slot == slot2, col(5), 0.0))
        y = _bdot(weights.astype(_BF16), bufs[cur])
        o_ref[rows, :] = _rmsnorm(x1_ref[rows, :] + g2_ref[...] * y, nf_ref[...])


def _combine(ys, meta_s, x1, plan, mods, row_fn, norm_final, n_experts, tm):
    n, d = x1.shape
    seg_start, pad, loc_off = plan
    return pl.pallas_call(
        functools.partial(_combine_kernel, n_experts=n_experts),
        grid_spec=pltpu.PrefetchScalarGridSpec(
            num_scalar_prefetch=3,
            grid=(n // tm,),
            in_specs=[pl.BlockSpec(memory_space=pl.ANY),
                      pl.BlockSpec((tm, _LANES), lambda i, *_: (i, 0)),
                      pl.BlockSpec((tm, d), lambda i, *_: (i, 0)),
                      _mod_spec(d, 5, row_fn),
                      _resident((1, d))],
            out_specs=pl.BlockSpec((tm, d), lambda i, *_: (i, 0)),
            scratch_shapes=[pltpu.VMEM((2, _local_slots(tm, n_experts), d), _BF16),
                            pltpu.SemaphoreType.DMA((2,))],
        ),
        out_shape=jax.ShapeDtypeStruct((n, d), _F32),
        compiler_params=_params(1),
        name="combine",
    )(seg_start, pad, loc_off, ys, meta_s, x1, mods, norm_final)


def kernel(x, c, ctx, c_ctx, w_ada, b_ada, norm_mix, norm_ffn, w_in, w_conv, rpb, w_attn_out,
           w_conv_out, w_gate, b_gate, w_out, w_ffn_gate, w_ffn_up, w_ffn_down, w_router,
           w_exp_gate, w_exp_up, w_exp_down, norm_final):
    batch, seq, d = x.shape
    ctx_len = ctx.shape[1]
    depth = w_ada.shape[0]
    attn_w = _N_HEADS * _HEAD_DIM
    n_experts = w_router.shape[-1]
    rows = seq // _GRID_W
    n_pairs = attn_w // _LANES
    assert depth == 2 and seq % (_Q_ROWS * _GRID_W) == 0 and attn_w % _LANES == 0
    assert n_experts == _F32_SUBLANES

    ctx_row = batch
    cond_rows = -(-(batch + 1) // 8) * 8
    cond = jnp.zeros((cond_rows, d), _F32).at[:batch].set(c).at[ctx_row].set(c_ctx)
    mods_all = _ada(cond, w_ada, b_ada, tn=1536).reshape(depth, cond_rows, 1, 6 * d)

    union, starts, variant_of, patterns, col_starts = _attn_plan(rows)

    x2d = x.reshape(batch * seq, d)
    c2d = ctx.reshape(batch * ctx_len, d)
    tm_lat = 512
    ctx_row_fn = lambda i: ctx_row

    def lat_rows(tm):
        return lambda i: (i * tm) // seq

    out = None
    for l in range(depth):
        last = l == depth - 1
        mods = mods_all[l]
        wts = {
            "norm_mix": norm_mix[l].reshape(1, d),
            "norm_ffn": norm_ffn[l].reshape(1, d),
            "wconv": w_conv[l],
            "wao": w_attn_out[l].astype(_BF16),
            "wco": w_conv_out[l].astype(_BF16),
            "wgate": w_gate[l].astype(_BF16),
            "bgate": b_gate[l].reshape(1, -1),
            "wout": w_out[l].astype(_BF16),
        }
        w_in_l = w_in[l].astype(_BF16)
        p = _proj(x2d, mods, lat_rows(512), wts["norm_mix"], w_in_l, tm=512)
        if last:
            pc = _proj(c2d, mods, ctx_row_fn, wts["norm_mix"], w_in_l[:, attn_w:3 * attn_w], tm=512)
            kc_col = 0
        else:
            pc = _proj(c2d, mods, ctx_row_fn, wts["norm_mix"], w_in_l, tm=512)
            kc_col = n_pairs
        table = _bias_table(rpb[l], union, patterns, col_starts)
        ya, yca = _attention(p, pc, kc_col, table, starts, variant_of, union, col_starts, batch,
                             seq, ctx_len, with_ctx=not last)
        if not last:
            wts["ffn_gate"] = w_ffn_gate[l // 2].astype(_BF16)
            wts["ffn_up"] = w_ffn_up[l // 2].astype(_BF16)
            wts["ffn_down"] = w_ffn_down[l // 2].astype(_BF16)
            x2d = _merge_ffn(x2d, ya, p, mods, lat_rows(tm_lat), seq, wts, tm=tm_lat)
            c2d = _merge_ffn(c2d, yca, pc, mods, ctx_row_fn, ctx_len, wts, tm=ctx_len)
        else:
            wr_t = w_router[l // 2].T
            wr_hi = wr_t.astype(_BF16)
            wr_lo = (wr_t - wr_hi.astype(_F32)).astype(_BF16)
            wts["router"] = jnp.concatenate([wr_hi, wr_lo], axis=0)
            tm_moe = 512
            n_tok = batch * seq
            n_tiles = n_tok // tm_moe
            tm_mix = 2 * tm_moe
            x1, h2, meta_s, meta_l, cnt = _merge_router(
                x2d, ya, p, mods, lat_rows(tm_mix), seq, wts, n_experts, tm=tm_mix,
                route_rows=tm_moe)
            n_slots = _round_up(_TOP_K * n_tok + n_tiles * n_experts * (_SEG_ALIGN - 1)
                                + n_experts * (_FFN_ROWS - 1), _FFN_ROWS)
            cnt = cnt[:, 0].reshape(n_tiles, n_experts).astype(jnp.int32)
            plan, tails, tile_expert, n_used, next_expert = _routing_plan(cnt, n_slots)
            xs = _dispatch(h2, meta_l, plan, tails, n_slots, n_experts, tm=tm_moe)
            ys = _expert_ffn(xs, tile_expert, n_used, next_expert, w_exp_gate[l // 2],
                             w_exp_up[l // 2], w_exp_down[l // 2])
            out = _combine(ys, meta_s, x1, plan, mods, lat_rows(tm_moe), norm_final.reshape(1, d),
                           n_experts, tm=tm_moe)
    return out.reshape(batch, seq, d)
```

```python
import functools

import numpy as np
import jax
import jax.numpy as jnp
from jax import lax
from jax.experimental import pallas as pl
from jax.experimental.pallas import tpu as pltpu

_F32 = jnp.float32
_BF16 = jnp.bfloat16

_GRID_W = 64
_N_HEADS = 8
_HEAD_DIM = 64
_NA_ROWS = 8
_NA_COLS = 16
_CONV_K = 3
_TOP_K = 2
_EPS = 1e-6
_NEG_INF = -1e30

_LANES = 128
_F32_SUBLANES = 8
_BF16_SUBLANES = 16
_VMEM_LIMIT_BYTES = 56 * 1024 * 1024

_Q_ROWS = 8
_Q_COLS = 16
_K_COLS = 32
_MIX_CHUNK_ROWS = 256
_HEADS_PER_BLOCK = _LANES // _HEAD_DIM
_NT = (((1,), (1,)), ((), ()))
_TN = (((0,), (0,)), ((), ()))


def _params(n_axes):
    return pltpu.CompilerParams(
        dimension_semantics=("arbitrary",) * n_axes,
        vmem_limit_bytes=_VMEM_LIMIT_BYTES)


def _resident(shape):
    nd = len(shape)
    return pl.BlockSpec(shape, lambda *_: (0,) * nd, pipeline_mode=pl.Buffered(1))


def _sigmoid(x):
    return 1.0 / (1.0 + jnp.exp(-x))


def _rmsnorm(x, g):
    ms = jnp.mean(x * x, axis=-1, keepdims=True)
    return (x * lax.rsqrt(ms + _EPS)) * g


def _modulate(x, g, shift, scale):
    return _rmsnorm(x, g) * (1.0 + scale) + shift


def _bdot(a, b):
    return jnp.dot(a, b, preferred_element_type=_F32)


def _indicator(mask):
    return jnp.where(mask, 1.0, 0.0).astype(_BF16)


def _ada_kernel(cond_ref, w_ref, b_ref, o_ref):
    cond = cond_ref[...]
    act = cond * _sigmoid(cond)
    o_ref[...] = jnp.dot(act, w_ref[...], precision=lax.Precision.HIGHEST,
                         preferred_element_type=_F32) + b_ref[...]


def _ada(cond, w_ada, b_ada, tn):
    depth, d, n_out = w_ada.shape
    rows = cond.shape[0]
    return pl.pallas_call(
        _ada_kernel,
        grid=(depth, n_out // tn),
        in_specs=[
            pl.BlockSpec((rows, d), lambda l, j: (0, 0)),
            pl.BlockSpec((None, d, tn), lambda l, j: (l, 0, j)),
            pl.BlockSpec((None, 1, tn), lambda l, j: (l, 0, j)),
        ],
        out_specs=pl.BlockSpec((None, rows, tn), lambda l, j: (l, 0, j)),
        out_shape=jax.ShapeDtypeStruct((depth, rows, n_out), _F32),
        compiler_params=_params(2),
        name="ada",
    )(cond, w_ada, b_ada.reshape(depth, 1, n_out))


def _mod_spec(d, chunk, row_fn):
    return pl.BlockSpec((None, 1, d), lambda i, *_: (row_fn(i), 0, chunk))


def _proj_kernel(x_ref, g_ref, sh_ref, sc_ref, w_ref, o_ref):
    h = _modulate(x_ref[...], g_ref[...], sh_ref[...], sc_ref[...])
    o_ref[...] = _bdot(h.astype(_BF16), w_ref[...]).astype(o_ref.dtype)


def _proj(x2d, mods, row_fn, g, w, tm):
    n, d = x2d.shape
    n_out = w.shape[1]
    return pl.pallas_call(
        _proj_kernel,
        grid=(n // tm,),
        in_specs=[
            pl.BlockSpec((tm, d), lambda i: (i, 0)),
            _resident((1, d)),
            _mod_spec(d, 0, row_fn),
            _mod_spec(d, 1, row_fn),
            _resident(w.shape),
        ],
        out_specs=pl.BlockSpec((tm, n_out), lambda i: (i, 0)),
        out_shape=jax.ShapeDtypeStruct((n, n_out), _BF16),
        compiler_params=_params(1),
        name="proj",
    )(x2d, g, mods, mods, w)


def _bias_table_kernel(rpb_ref, o_ref, *, union, patterns, col_starts):
    sub = _F32_SUBLANES
    span = _NA_COLS - 1
    slots = _LANES // _K_COLS
    rp = rpb_ref[...]
    lane = lax.broadcasted_iota(jnp.int32, (sub, _LANES), 1)
    srow = lax.broadcasted_iota(jnp.int32, (sub, _LANES), 0)
    ck_local = jnp.bitwise_and(lane, _K_COLS - 1)
    slot_of_lane = jnp.right_shift(lane, _K_COLS.bit_length() - 1)
    neg = jnp.full((sub, _LANES), _NEG_INF, _F32)

    def row_vec(dr):
        return jnp.broadcast_to(rp[dr + _NA_ROWS - 1:dr + _NA_ROWS, :], (sub, _LANES))

    for g, c0 in enumerate(col_starts):
        for gg in range(_Q_COLS // sub):
            k0 = c0 - _Q_COLS * g - sub * gg + span
            lo, hi = max(-(sub - 1), -k0), min(_K_COLS - 1, 2 * span - k0)
            cq = srow + (_Q_COLS * g + sub * gg)
            cs = jnp.clip(cq - _NA_COLS // 2, 0, _GRID_W - _NA_COLS)
            ck = ck_local + c0
            in_win = jnp.logical_and(ck >= cs, ck < cs + _NA_COLS)
            part = [jnp.bitwise_and(lane - (_K_COLS * jq + lo), _LANES - 1) <= hi - lo
                    for jq in range(slots)]
            tiles = {}

            def make_tile(drs):
                base = jnp.zeros((sub, _LANES), _F32)
                keep = None
                for jq, dr in enumerate(drs):
                    if dr is None:
                        continue
                    shifted = pltpu.roll(row_vec(dr), (_K_COLS * jq - k0) % _LANES, 1)
                    base = jnp.where(part[jq], shifted, base)
                    here = slot_of_lane == jq
                    keep = here if keep is None else jnp.logical_or(keep, here)
                if keep is None:
                    return neg
                tile = pltpu.roll(base, 0, 1, stride=1, stride_axis=0)
                return jnp.where(jnp.logical_and(in_win, keep), tile, neg)

            for v, pat in enumerate(patterns):
                for a in range(_Q_ROWS):
                    rows = slice(a * _Q_COLS + sub * gg, a * _Q_COLS + sub * (gg + 1))
                    for quad in range(-(-union // slots)):
                        js = [quad * slots + jq for jq in range(slots)]
                        drs = tuple(pat[a * union + j] if j < union else None for j in js)
                        width = min(slots, union - quad * slots) * _K_COLS
                        if drs not in tiles:
                            tiles[drs] = make_tile(drs)
                        o_ref[v, g, rows, quad * _LANES:quad * _LANES + width] = (
                            tiles[drs][:, :width])


def _bias_table(rpb_l, union, patterns, col_starts):
    h, n_dr, n_dc = rpb_l.shape
    assert n_dc <= _LANES and _LANES % _K_COLS == 0
    n_dr_pad = _round_up(n_dr, _F32_SUBLANES)
    padded = jnp.pad(rpb_l, ((0, 0), (0, n_dr_pad - n_dr), (0, _LANES - n_dc)))
    n_var, n_grp = len(patterns), len(col_starts)
    qt, kt = _Q_ROWS * _Q_COLS, union * _K_COLS
    return pl.pallas_call(
        functools.partial(_bias_table_kernel, union=union, patterns=patterns,
                          col_starts=col_starts),
        grid=(h,),
        in_specs=[pl.BlockSpec((None, n_dr_pad, _LANES), lambda i: (i, 0, 0))],
        out_specs=pl.BlockSpec((n_var, n_grp, None, qt, kt), lambda i: (0, 0, i, 0, 0)),
        out_shape=jax.ShapeDtypeStruct((n_var, n_grp, h, qt, kt), _F32),
        compiler_params=_params(1),
        name="bias_table",
    )(padded)


def _attn_plan(rows):
    kh = min(_NA_ROWS, rows)
    union = min(_Q_ROWS + kh - 1, rows)
    n_blocks = rows // _Q_ROWS
    starts, variant_of, patterns = [], [], []
    for t in range(n_blocks):
        start = int(np.clip(_Q_ROWS * t - kh // 2, 0, rows - union))
        pat = []
        for a in range(_Q_ROWS):
            r = _Q_ROWS * t + a
            rs = int(np.clip(r - kh // 2, 0, rows - kh))
            for j in range(union):
                kr = start + j
                pat.append(kr - r if rs <= kr < rs + kh else None)
        pat = tuple(pat)
        if pat not in patterns:
            patterns.append(pat)
        starts.append(start)
        variant_of.append(patterns.index(pat))
    col_starts = []
    for g in range(_GRID_W // _Q_COLS):
        c0 = int(np.clip(_Q_COLS * g - _NA_COLS // 2, 0, _GRID_W - _K_COLS))
        for cq in range(_Q_COLS * g, _Q_COLS * (g + 1)):
            cs = int(np.clip(cq - _NA_COLS // 2, 0, _GRID_W - _NA_COLS))
            assert c0 <= cs and cs + _NA_COLS <= c0 + _K_COLS
        col_starts.append(c0)
    return union, starts, variant_of, tuple(patterns), tuple(col_starts)


def _first_head_lanes(shape):
    return lax.broadcasted_iota(jnp.int32, shape, len(shape) - 1) < _HEAD_DIM


def _values_with_ones(v):
    lo = _first_head_lanes(v.shape)
    ones = jnp.ones_like(v)
    return jnp.where(lo, v, ones), jnp.where(lo, ones, v)


def _scores(qb, keys, bias_fns):
    lo = _first_head_lanes(qb.shape)
    out = []
    for hh in range(_HEADS_PER_BLOCK):
        qm = jnp.where(lo if hh == 0 else jnp.logical_not(lo), qb, jnp.zeros_like(qb))
        scores = []
        for kk, bias_fn in zip(keys, bias_fns):
            s = lax.dot_general(qm, kk, _NT, preferred_element_type=_F32)
            scores.append(s if bias_fn is None else s + bias_fn(hh))
        out.append(scores)
    return out


def _probabilities(scores):
    m = functools.reduce(jnp.maximum, [jnp.max(s, axis=-1, keepdims=True) for s in scores])
    return [jnp.exp(s - m).astype(_BF16) for s in scores]


def _weighted_values(probs, values):
    res = []
    for hh in range(_HEADS_PER_BLOCK):
        acc = None
        for e, v1 in zip(probs[hh], values):
            o = _bdot(e, v1[hh])
            acc = o if acc is None else acc + o
        res.append(acc)
    lo = _first_head_lanes(res[0].shape)
    num = jnp.where(lo, res[0], res[1])
    den = pltpu.roll(jnp.where(lo, res[1], res[0]), _HEAD_DIM, 1)
    return num / den


def _attn_kernel(start_ref, var_ref, q_ref, k_ref, v_ref, kc_ref, vc_ref, tab_ref, *rest,
                 n_row_blocks, union, col_starts, with_ctx):
    if with_ctx:
        qc_ref, o_ref, oc_ref, s_loc, s_ctx, e_loc, e_ctx = rest
    else:
        o_ref, s_loc, s_ctx, e_loc, e_ctx = rest
    heads = range(_HEADS_PER_BLOCK)
    n_groups = len(col_starts)
    assert n_groups % 2 == 0 and n_groups >= 2 and n_row_blocks >= 1
    scale = _HEAD_DIM ** -0.5
    kc = kc_ref[...]
    vc1 = _values_with_ones(vc_ref[...])

    def gather(ref, first_row, n_rows, col, width):
        return jnp.concatenate(
            [ref[pl.ds(first_row + r * _GRID_W + col, width), :] for r in range(n_rows)], axis=0)

    def key_rows(t):
        return pl.multiple_of(start_ref[t] * _GRID_W, _GRID_W)

    def query_rows(t):
        return pl.multiple_of(t * (_Q_ROWS * _GRID_W), _Q_ROWS * _GRID_W)

    def scores_stage(t, g):
        slot, var = g % 2, var_ref[t]
        qb = gather(q_ref, query_rows(t), _Q_ROWS, _Q_COLS * g, _Q_COLS) * scale
        kb = gather(k_ref, key_rows(t), union, col_starts[g], _K_COLS)
        scores = _scores(qb, [kb, kc], [lambda hh: tab_ref[var, g, hh], None])
        for hh in heads:
            s_loc[slot, hh], s_ctx[slot, hh] = scores[hh]

    def softmax_stage(g):
        slot = g % 2
        for hh in heads:
            e_loc[slot, hh], e_ctx[slot, hh] = _probabilities([s_loc[slot, hh], s_ctx[slot, hh]])

    def values_stage(t, g):
        slot = g % 2
        vb1 = _values_with_ones(gather(v_ref, key_rows(t), union, col_starts[g], _K_COLS))
        probs = [[e_loc[slot, hh], e_ctx[slot, hh]] for hh in heads]
        o = _weighted_values(probs, [vb1, vc1]).astype(o_ref.dtype)
        for a in range(_Q_ROWS):
            o_ref[pl.ds(query_rows(t) + a * _GRID_W + _Q_COLS * g, _Q_COLS), :] = (
                o[a * _Q_COLS:(a + 1) * _Q_COLS])

    def issue(t, k):
        values_stage(t + k // n_groups, k % n_groups)
        softmax_stage((k + 1) % n_groups)
        scores_stage(t + (k + 2) // n_groups, (k + 2) % n_groups)

    scores_stage(0, 0)
    softmax_stage(0)
    scores_stage(0, 1)

    def body(t, carry):
        for k in range(n_groups):
            issue(t, k)
        return carry

    last = n_row_blocks - 1
    lax.fori_loop(0, last, body, 0)
    for k in range(n_groups - 2):
        issue(last, k)
    values_stage(last, n_groups - 2)
    softmax_stage(n_groups - 1)
    values_stage(last, n_groups - 1)

    if with_ctx:
        scores = _scores(qc_ref[...] * scale, [kc], [None])
        probs = [_probabilities(scores[hh]) for hh in heads]
        oc_ref[...] = _weighted_values(probs, [vc1]).astype(oc_ref.dtype)


def _attention(p, pc, kc_col, table, starts, variant_of, union, col_starts, batch, seq, ctx_len,
               with_ctx):
    attn_w = _N_HEADS * _HEAD_DIM
    n_pairs = attn_w // _LANES
    n_var, n_grp = table.shape[:2]
    qb_tokens = _Q_ROWS * _Q_COLS
    union_tokens = union * _K_COLS
    kern = functools.partial(_attn_kernel, n_row_blocks=len(starts), union=union,
                             col_starts=col_starts, with_ctx=with_ctx)
    in_specs = [
        pl.BlockSpec((seq, _LANES), lambda hp, b, *_: (b, hp)),
        pl.BlockSpec((seq, _LANES), lambda hp, b, *_: (b, n_pairs + hp)),
        pl.BlockSpec((seq, _LANES), lambda hp, b, *_: (b, 2 * n_pairs + hp)),
        pl.BlockSpec((ctx_len, _LANES), lambda hp, b, *_: (b, kc_col + hp)),
        pl.BlockSpec((ctx_len, _LANES), lambda hp, b, *_: (b, kc_col + n_pairs + hp)),
        pl.BlockSpec((n_var, n_grp, _HEADS_PER_BLOCK, qb_tokens, union_tokens),
                     lambda hp, b, *_: (0, 0, hp, 0, 0)),
    ]
    args = [p, p, p, pc, pc, table]
    out_specs = [pl.BlockSpec((seq, _LANES), lambda hp, b, *_: (b, hp))]
    out_shape = [jax.ShapeDtypeStruct((batch * seq, attn_w), _BF16)]
    if with_ctx:
        in_specs.append(pl.BlockSpec((ctx_len, _LANES), lambda hp, b, *_: (b, hp)))
        args.append(pc)
        out_specs.append(pl.BlockSpec((ctx_len, _LANES), lambda hp, b, *_: (b, hp)))
        out_shape.append(jax.ShapeDtypeStruct((batch * ctx_len, attn_w), _BF16))
    outs = pl.pallas_call(
        kern,
        grid_spec=pltpu.PrefetchScalarGridSpec(
            num_scalar_prefetch=2,
            grid=(n_pairs, batch),
            in_specs=in_specs,
            out_specs=out_specs,
            scratch_shapes=[
                pltpu.VMEM((2, _HEADS_PER_BLOCK, qb_tokens, union_tokens), _F32),
                pltpu.VMEM((2, _HEADS_PER_BLOCK, qb_tokens, ctx_len), _F32),
                pltpu.VMEM((2, _HEADS_PER_BLOCK, qb_tokens, union_tokens), _BF16),
                pltpu.VMEM((2, _HEADS_PER_BLOCK, qb_tokens, ctx_len), _BF16),
            ],
        ),
        out_shape=out_shape,
        compiler_params=_params(2),
        name="attn",
    )(jnp.asarray(starts, jnp.int32), jnp.asarray(variant_of, jnp.int32), *args)
    return outs if with_ctx else (outs[0], None)


def _mixer_update(i, x_ref, ya_ref, bcu_ref, prev_ref, next_ref, nm_ref, sh1_ref, sc1_ref, g1_ref,
                  wconv_ref, wao_ref, wco_ref, wgate_ref, bgate_ref, wout_ref, seq_len):
    tm, d = x_ref.shape
    cw = wco_ref.shape[0]

    bcu = bcu_ref[...].astype(_F32)
    bg, z = bcu[:, :cw], bcu[:, cw:2 * cw] * bcu[:, 2 * cw:]
    prev = prev_ref[...].astype(_F32)
    nxt = next_ref[...].astype(_F32)
    has_prev = ((i * tm) % seq_len != 0).astype(_F32)
    has_next = (((i + 1) * tm) % seq_len != 0).astype(_F32)
    last = prev.shape[0] - 1
    z_prev = prev[last:last + 1, cw:2 * cw] * prev[last:last + 1, 2 * cw:] * has_prev
    z_next = nxt[0:1, cw:2 * cw] * nxt[0:1, 2 * cw:] * has_next
    row = lax.broadcasted_iota(jnp.int32, z.shape, 0)
    z_dn = jnp.where(row == 0, z_prev, pltpu.roll(z, 1, 0))
    z_up = jnp.where(row == tm - 1, z_next, pltpu.roll(z, tm - 1, 0))
    wconv = wconv_ref[...]
    y_conv = (bg * (z_dn * wconv[0:1] + z * wconv[1:2] + z_up * wconv[2:3])).astype(_BF16)

    chunks = []
    rc = min(tm, _MIX_CHUNK_ROWS)
    for c in range(tm // rc):
        rows = slice(c * rc, (c + 1) * rc)
        x = x_ref[rows, :]
        hb = _modulate(x, nm_ref[...], sh1_ref[...], sc1_ref[...]).astype(_BF16)
        gates = _sigmoid(_bdot(hb, wgate_ref[...]) + bgate_ref[...])
        a_proj = _bdot(ya_ref[rows, :], wao_ref[...])
        c_proj = _bdot(y_conv[rows, :], wco_ref[...])
        merged = gates[:, :d] * a_proj + gates[:, d:] * c_proj
        chunks.append(x + g1_ref[...] * _bdot(merged.astype(_BF16), wout_ref[...]))
    return jnp.concatenate(chunks, axis=0)


_N_MIX_REFS = 15


def _merge_ffn_kernel(*refs, seq_len):
    mix = refs[:_N_MIX_REFS]
    nf_ref, sh2_ref, sc2_ref, g2_ref, wg_ref, wu_ref, wd_ref, o_ref = refs[_N_MIX_REFS:]
    x1 = _mixer_update(pl.program_id(0), *mix, seq_len=seq_len)
    h2 = _modulate(x1, nf_ref[...], sh2_ref[...], sc2_ref[...]).astype(_BF16)
    gate = _bdot(h2, wg_ref[...])
    act = (gate * _sigmoid(gate)) * _bdot(h2, wu_ref[...])
    o_ref[...] = x1 + g2_ref[...] * _bdot(act.astype(_BF16), wd_ref[...])


def _top2(vals, idx, n_idx, axis):
    v1 = jnp.max(vals, axis=axis, keepdims=True)
    i1 = jnp.min(jnp.where(vals == v1, idx, n_idx), axis=axis, keepdims=True)
    rest = jnp.where(idx == i1, -jnp.inf, vals)
    v2 = jnp.max(rest, axis=axis, keepdims=True)
    i2 = jnp.min(jnp.where(rest == v2, idx, n_idx), axis=axis, keepdims=True)
    return i1, i2, v1, v2


def _pick(idx, sel, vals, axis):
    return jnp.sum(jnp.where(idx == sel, vals, 0.0), axis=axis, keepdims=True)


def _merge_router_kernel(*refs, seq_len, n_experts, route_rows):
    mix = refs[:_N_MIX_REFS]
    (nf_ref, sh2_ref, sc2_ref, wr_ref,
     x1_ref, h2_ref, ms_ref, ml_ref, cnt_ref) = refs[_N_MIX_REFS:]
    x1 = _mixer_update(pl.program_id(0), *mix, seq_len=seq_len)
    x1_ref[...] = x1
    h2 = _modulate(x1, nf_ref[...], sh2_ref[...], sc2_ref[...])
    h_hi = h2.astype(_BF16)
    h2_ref[...] = h_hi
    tm = h2.shape[0]
    h_lo = (h2 - h_hi.astype(_F32)).astype(_BF16)
    wr = wr_ref[...]
    by_hi = lax.dot_general(wr, h_hi, _NT, preferred_element_type=_F32)
    by_lo = lax.dot_general(wr, h_lo, _NT, preferred_element_type=_F32)
    lt_all = by_hi[:n_experts] + by_hi[n_experts:] + by_lo[:n_experts]
    row = lax.broadcasted_iota(jnp.int32, (route_rows, route_rows), 0)
    col = lax.broadcasted_iota(jnp.int32, (route_rows, route_rows), 1)
    earlier_cols = _indicator(row < col)
    sub = lax.broadcasted_iota(jnp.int32, (n_experts, route_rows), 0)

    for u in range(tm // route_rows):
        lt = lt_all[:, u * route_rows:(u + 1) * route_rows]
        j1, j2, v1, v2 = _top2(lt, sub, n_experts, 0)
        e2 = jnp.exp(v2 - v1)
        den = 1.0 + e2
        sel_t = _indicator(jnp.logical_or(sub == j1, sub == j2))
        ranks_t = _bdot(sel_t, earlier_cols)
        fields = (j1.astype(_F32), j2.astype(_F32), _pick(sub, j1, ranks_t, 0),
                  _pick(sub, j2, ranks_t, 0), 1.0 / den, e2 / den)
        assert len(fields) <= n_experts
        meta_t = jnp.zeros(lt.shape, _F32)
        for k, f in enumerate(fields):
            meta_t = jnp.where(sub == k, f, meta_t)
        experts = slice(u * n_experts, (u + 1) * n_experts)
        ml_ref[experts, :] = meta_t
        ms_ref[u * route_rows:(u + 1) * route_rows, :] = jnp.concatenate(
            [meta_t, jnp.zeros((_LANES - n_experts, route_rows), _F32)], axis=0).T
        cnt = jnp.sum(sel_t.astype(_F32), axis=1, keepdims=True)
        cnt_ref[experts, :] = jnp.broadcast_to(cnt, (n_experts, _LANES))


def _mixer_specs(n, d, tm, p_cols, row_fn, weights):
    half = p_cols // 2
    sub = _BF16_SUBLANES
    n_sub = n // sub
    specs = [
        pl.BlockSpec((tm, d), lambda i: (i, 0)),
        pl.BlockSpec((tm, weights["wao"].shape[0]), lambda i: (i, 0)),
        pl.BlockSpec((tm, half), lambda i: (i, 1)),
        pl.BlockSpec((sub, half), lambda i: (jnp.maximum(i * (tm // sub) - 1, 0), 1)),
        pl.BlockSpec((sub, half), lambda i: (jnp.minimum((i + 1) * (tm // sub), n_sub - 1), 1)),
        _resident((1, d)),
        _mod_spec(d, 0, row_fn), _mod_spec(d, 1, row_fn), _mod_spec(d, 2, row_fn),
    ]
    for name in ("wconv", "wao", "wco", "wgate", "bgate", "wout"):
        specs.append(_resident(weights[name].shape))
    return specs


def _merge_ffn(x2d, ya, p, mods, row_fn, seq_len, wts, tm):
    n, d = x2d.shape
    specs = _mixer_specs(n, d, tm, p.shape[1], row_fn, wts)
    specs += [_resident((1, d)), _mod_spec(d, 3, row_fn), _mod_spec(d, 4, row_fn),
              _mod_spec(d, 5, row_fn),
              _resident(wts["ffn_gate"].shape), _resident(wts["ffn_up"].shape),
              _resident(wts["ffn_down"].shape)]
    return pl.pallas_call(
        functools.partial(_merge_ffn_kernel, seq_len=seq_len),
        grid=(n // tm,),
        in_specs=specs,
        out_specs=pl.BlockSpec((tm, d), lambda i: (i, 0)),
        out_shape=jax.ShapeDtypeStruct((n, d), _F32),
        compiler_params=_params(1),
        name="merge_ffn",
    )(x2d, ya, p, p, p, wts["norm_mix"], mods, mods, mods,
      wts["wconv"], wts["wao"], wts["wco"], wts["wgate"], wts["bgate"], wts["wout"],
      wts["norm_ffn"], mods, mods, mods, wts["ffn_gate"], wts["ffn_up"], wts["ffn_down"])


def _merge_router(x2d, ya, p, mods, row_fn, seq_len, wts, n_experts, tm, route_rows):
    n, d = x2d.shape
    assert tm % route_rows == 0
    specs = _mixer_specs(n, d, tm, p.shape[1], row_fn, wts)
    specs += [_resident((1, d)), _mod_spec(d, 3, row_fn), _mod_spec(d, 4, row_fn),
              _resident(wts["router"].shape)]
    tile = lambda w: pl.BlockSpec((tm, w), lambda i: (i, 0))
    meta_rows = n_experts * (tm // route_rows)
    n_route = n // route_rows
    return pl.pallas_call(
        functools.partial(_merge_router_kernel, seq_len=seq_len, n_experts=n_experts,
                          route_rows=route_rows),
        grid=(n // tm,),
        in_specs=specs,
        out_specs=[tile(d), tile(d), tile(_LANES),
                   pl.BlockSpec((meta_rows, route_rows), lambda i: (i, 0)),
                   pl.BlockSpec((meta_rows, _LANES), lambda i: (i, 0))],
        out_shape=[jax.ShapeDtypeStruct((n, d), _F32),
                   jax.ShapeDtypeStruct((n, d), _BF16),
                   jax.ShapeDtypeStruct((n, _LANES), _F32),
                   jax.ShapeDtypeStruct((n_route * n_experts, route_rows), _F32),
                   jax.ShapeDtypeStruct((n_route * n_experts, _LANES), _F32)],
        compiler_params=_params(1),
        name="merge_router",
    )(x2d, ya, p, p, p, wts["norm_mix"], mods, mods, mods,
      wts["wconv"], wts["wao"], wts["wco"], wts["wgate"], wts["bgate"], wts["wout"],
      wts["norm_ffn"], mods, mods, wts["router"])


_SEG_ALIGN = _BF16_SUBLANES
_FFN_ROWS = 256


def _round_up(v, m):
    return (v + m - 1) // m * m


def _local_slots(tm, n_experts):
    return _round_up(_TOP_K * tm + n_experts * (_SEG_ALIGN - 1), _SEG_ALIGN)


def _routing_plan(cnt, n_slots):
    pad = _round_up(cnt, _SEG_ALIGN)
    total = jnp.sum(pad, axis=0)
    region = _round_up(total, _FFN_ROWS)
    ends = jnp.cumsum(region)
    seg_start = (ends - region)[None, :] + jnp.cumsum(pad, axis=0) - pad
    loc_off = jnp.cumsum(pad, axis=1) - pad
    n_used = (ends[-1] // _FFN_ROWS).astype(jnp.int32).reshape(1)
    tails = jnp.concatenate([ends - region + total, region - total, n_used])
    tile_row = jnp.arange(n_slots // _FFN_ROWS, dtype=jnp.int32) * _FFN_ROWS
    tile_expert = jnp.minimum(jnp.sum(tile_row[:, None] >= ends[None, :], axis=1),
                              cnt.shape[1] - 1).astype(jnp.int32)
    ids = jnp.arange(cnt.shape[1], dtype=jnp.int32)
    later = jnp.logical_and(region[None, :] > 0, ids[None, :] > ids[:, None])
    nxt = jnp.min(jnp.where(later, ids[None, :], cnt.shape[1]), axis=1)
    next_expert = jnp.where(nxt == cnt.shape[1], ids, nxt).astype(jnp.int32)
    flat = lambda a: a.reshape(-1).astype(jnp.int32)
    return ((flat(seg_start), flat(pad), flat(loc_off)), flat(tails), tile_expert, n_used,
            next_expert)


def _segment_copies(i, n_experts, seg_ref, pad_ref, loc_ref, hbm_ref, buf_ref, sem, to_hbm):
    copies = []
    for e in range(n_experts):
        n = pl.multiple_of(pad_ref[i * n_experts + e], _SEG_ALIGN)
        loc = pl.multiple_of(loc_ref[i * n_experts + e], _SEG_ALIGN)
        seg = pl.multiple_of(seg_ref[i * n_experts + e], _SEG_ALIGN)
        local, remote = buf_ref.at[pl.ds(loc, n)], hbm_ref.at[pl.ds(seg, n)]
        src, dst = (local, remote) if to_hbm else (remote, local)
        copies.append((n, pltpu.make_async_copy(src, dst, sem)))
    return copies


def _start_copies(copies):
    for n, cp in copies:
        pl.when(n > 0)(cp.start)


def _wait_copies(copies):
    for n, cp in copies:
        pl.when(n > 0)(cp.wait)


def _run_copies(copies):
    _start_copies(copies)
    _wait_copies(copies)


def _expert_offset(expert, loc_ref, i, n_experts):
    off = jnp.zeros(expert.shape, jnp.int32)
    for e in range(n_experts):
        off = jnp.where(expert == e, loc_ref[i * n_experts + e], off)
    return off


def _dispatch_kernel(seg_ref, pad_ref, loc_ref, tail_ref, h_ref, ml_ref, xs_ref, bufs, zeros, sems,
                     *, n_experts):
    i = pl.program_id(0)
    last = pl.num_programs(0) - 1
    cur = lax.rem(i, 2)

    def copies(tile, b):
        return _segment_copies(tile, n_experts, seg_ref, pad_ref, loc_ref, xs_ref, bufs.at[b],
                               sems.at[b], True)

    @pl.when(i >= 2)
    def _():
        _wait_copies(copies(i - 2, cur))

    tm = h_ref.shape[0]
    meta = ml_ref[...]
    slot1 = _expert_offset(meta[0:1].astype(jnp.int32), loc_ref, i, n_experts) + meta[2:3].astype(jnp.int32)
    slot2 = _expert_offset(meta[1:2].astype(jnp.int32), loc_ref, i, n_experts) + meta[3:4].astype(jnp.int32)
    slot = lax.broadcasted_iota(jnp.int32, (bufs.shape[1], tm), 0)
    perm = jnp.where(slot == slot1, 1.0, jnp.where(slot == slot2, 1.0, 0.0)).astype(_BF16)
    bufs[cur] = _bdot(perm, h_ref[...]).astype(bufs.dtype)
    _start_copies(copies(i, cur))

    @pl.when(i == last)
    def _():
        @pl.when(i >= 1)
        def _():
            _wait_copies(copies(i - 1, 1 - cur))

        _wait_copies(copies(i, cur))

        zeros[...] = jnp.zeros_like(zeros)
        fills = []
        for e in range(n_experts):
            start = pl.multiple_of(tail_ref[e], _SEG_ALIGN)
            n = pl.multiple_of(tail_ref[n_experts + e], _SEG_ALIGN)
            fills.append((n, pltpu.make_async_copy(zeros.at[pl.ds(0, n)],
                                                   xs_ref.at[pl.ds(start, n)], sems.at[0])))
        _run_copies(fills)

        def fill_tile(j, carry):
            cp = pltpu.make_async_copy(
                zeros, xs_ref.at[pl.ds(pl.multiple_of(j * _FFN_ROWS, _FFN_ROWS), _FFN_ROWS)],
                sems.at[0])
            cp.start()
            cp.wait()
            return carry

        lax.fori_loop(tail_ref[2 * n_experts], xs_ref.shape[0] // _FFN_ROWS, fill_tile, 0)


def _dispatch(h2, meta_l, plan, tails, n_slots, n_experts, tm):
    n, d = h2.shape
    seg_start, pad, loc_off = plan
    return pl.pallas_call(
        functools.partial(_dispatch_kernel, n_experts=n_experts),
        grid_spec=pltpu.PrefetchScalarGridSpec(
            num_scalar_prefetch=4,
            grid=(n // tm,),
            in_specs=[pl.BlockSpec((tm, d), lambda i, *_: (i, 0)),
                      pl.BlockSpec((n_experts, tm), lambda i, *_: (i, 0))],
            out_specs=pl.BlockSpec(memory_space=pl.ANY),
            scratch_shapes=[pltpu.VMEM((2, _local_slots(tm, n_experts), d), _BF16),
                            pltpu.VMEM((_FFN_ROWS, d), _BF16),
                            pltpu.SemaphoreType.DMA((2,))],
        ),
        out_shape=jax.ShapeDtypeStruct((n_slots, d), _BF16),
        compiler_params=_params(1),
        name="dispatch",
    )(seg_start, pad, loc_off, tails, h2, meta_l)


_W_ROUNDS = 16
_W_ROUNDS_PER_STEP = 2
_ST_CUR, _ST_LOADED, _ST_TARGET, _ST_DONE = range(4)


def _expert_ffn_kernel(te_ref, nu_ref, nxt_ref, x_ref, wg_hbm, wu_hbm, wd_hbm, o_ref,
                       wg_v, wu_v, wd_v, stage_gu, stage_d, sems, state):
    j = pl.program_id(0)
    rg = wg_v.shape[1] // _W_ROUNDS
    rd = wd_v.shape[1] // _W_ROUNDS

    def round_copies(e, r, s):
        g_rows = pl.ds(pl.multiple_of(r * rg, rg), rg)
        d_rows = pl.ds(pl.multiple_of(r * rd, rd), rd)
        return (pltpu.make_async_copy(wg_hbm.at[e, g_rows], stage_gu.at[s, 0], sems.at[s, 0]),
                pltpu.make_async_copy(wu_hbm.at[e, g_rows], stage_gu.at[s, 1], sems.at[s, 1]),
                pltpu.make_async_copy(wd_hbm.at[e, d_rows], stage_d.at[s], sems.at[s, 2]))

    def narrow_round(r, s, copy):
        g_rows = pl.ds(pl.multiple_of(r * rg, rg), rg)
        d_rows = pl.ds(pl.multiple_of(r * rd, rd), rd)
        wg_v[copy, g_rows, :] = stage_gu[s, 0].astype(_BF16)
        wu_v[copy, g_rows, :] = stage_gu[s, 1].astype(_BF16)
        wd_v[copy, d_rows, :] = stage_d[s].astype(_BF16)

    @pl.when(j == 0)
    def _():
        state[_ST_CUR] = 0
        state[_ST_LOADED] = -1
        state[_ST_TARGET] = -1
        state[_ST_DONE] = _W_ROUNDS

    active = j < nu_ref[0]
    e = te_ref[j]

    @pl.when(jnp.logical_and(active, e != state[_ST_LOADED]))
    def _():
        other = 1 - state[_ST_CUR]

        @pl.when(state[_ST_TARGET] != e)
        def _():
            state[_ST_TARGET] = e
            state[_ST_DONE] = 0

        def load_round(r, carry):
            cps = round_copies(e, r, 0)
            for cp in cps:
                cp.start()
            for cp in cps:
                cp.wait()
            narrow_round(r, 0, other)
            return carry

        lax.fori_loop(state[_ST_DONE], _W_ROUNDS, load_round, 0)
        nxt = nxt_ref[e]
        state[_ST_CUR] = other
        state[_ST_LOADED] = e
        state[_ST_TARGET] = nxt
        state[_ST_DONE] = jnp.where(nxt != e, 0, _W_ROUNDS)

    cur = state[_ST_CUR]
    target = state[_ST_TARGET]
    done = state[_ST_DONE]
    prefetch = jnp.logical_and(active, done < _W_ROUNDS)

    @pl.when(prefetch)
    def _():
        for s in range(_W_ROUNDS_PER_STEP):
            for cp in round_copies(target, done + s, s):
                cp.start()

    @pl.when(active)
    def _():
        x = x_ref[...]
        gate = _bdot(x, wg_v[cur])
        act = (gate * _sigmoid(gate)) * _bdot(x, wu_v[cur])
        o_ref[...] = _bdot(act.astype(_BF16), wd_v[cur]).astype(o_ref.dtype)

    @pl.when(jnp.logical_not(active))
    def _():
        o_ref[...] = jnp.zeros_like(o_ref)

    @pl.when(prefetch)
    def _():
        for s in range(_W_ROUNDS_PER_STEP):
            for cp in round_copies(target, done + s, s):
                cp.wait()
            narrow_round(done + s, s, 1 - cur)
        state[_ST_DONE] = done + _W_ROUNDS_PER_STEP


def _expert_ffn(xs, tile_expert, n_used, next_expert, wg, wu, wd):
    n_slots, d = xs.shape
    n_experts, _, f = wg.shape
    assert d % (_W_ROUNDS * _BF16_SUBLANES) == 0 and f % (_W_ROUNDS * _BF16_SUBLANES) == 0
    assert _W_ROUNDS % _W_ROUNDS_PER_STEP == 0
    any_spec = pl.BlockSpec(memory_space=pl.ANY)
    return pl.pallas_call(
        _expert_ffn_kernel,
        grid_spec=pltpu.PrefetchScalarGridSpec(
            num_scalar_prefetch=3,
            grid=(n_slots // _FFN_ROWS,),
            in_specs=[
                pl.BlockSpec((_FFN_ROWS, d),
                             lambda j, te, nu, nx: (jnp.maximum(jnp.minimum(j, nu[0] - 1), 0), 0)),
                any_spec, any_spec, any_spec,
            ],
            out_specs=pl.BlockSpec((_FFN_ROWS, d), lambda j, te, nu, nx: (j, 0)),
            scratch_shapes=[
                pltpu.VMEM((2, d, f), _BF16), pltpu.VMEM((2, d, f), _BF16),
                pltpu.VMEM((2, f, d), _BF16),
                pltpu.VMEM((_W_ROUNDS_PER_STEP, 2, d // _W_ROUNDS, f), _F32),
                pltpu.VMEM((_W_ROUNDS_PER_STEP, f // _W_ROUNDS, d), _F32),
                pltpu.SemaphoreType.DMA((_W_ROUNDS_PER_STEP, 3)),
                pltpu.SMEM((4,), jnp.int32),
            ],
        ),
        out_shape=jax.ShapeDtypeStruct((n_slots, d), _BF16),
        compiler_params=_params(1),
        name="expert_ffn",
    )(tile_expert, n_used, next_expert, xs, wg, wu, wd)


def _combine_kernel(seg_ref, pad_ref, loc_ref, ys_ref, ms_ref, x1_ref, g2_ref, nf_ref, o_ref,
                    bufs, sems, *, n_experts):
    i = pl.program_id(0)
    tm = x1_ref.shape[0]
    cur = lax.rem(i, 2)

    def copies(tile, b):
        return _segment_copies(tile, n_experts, seg_ref, pad_ref, loc_ref, ys_ref, bufs.at[b],
                               sems.at[b], False)

    @pl.when(i == 0)
    def _():
        bufs[...] = jnp.zeros_like(bufs)
        _start_copies(copies(0, 0))

    @pl.when(i + 1 < pl.num_programs(0))
    def _():
        _start_copies(copies(i + 1, 1 - cur))

    _wait_copies(copies(i, cur))
    rc = min(tm, _MIX_CHUNK_ROWS)
    slot = lax.broadcasted_iota(jnp.int32, (rc, bufs.shape[1]), 1)
    for c in range(tm // rc):
        rows = slice(c * rc, (c + 1) * rc)
        meta = ms_ref[rows, :]
        col = lambda k: meta[:, k:k + 1]
        slot1 = _expert_offset(col(0).astype(jnp.int32), loc_ref, i, n_experts) + col(2).astype(jnp.int32)
        slot2 = _expert_offset(col(1).astype(jnp.int32), loc_ref, i, n_experts) + col(3).astype(jnp.int32)
        weights = jnp.where(slot == slot1, col(4), jnp.where(slot == slot2, col(5), 0.0))
        y = _bdot(weights.astype(_BF16), bufs[cur])
        o_ref[rows, :] = _rmsnorm(x1_ref[rows, :] + g2_ref[...] * y, nf_ref[...])


def _combine(ys, meta_s, x1, plan, mods, row_fn, norm_final, n_experts, tm):
    n, d = x1.shape
    seg_start, pad, loc_off = plan
    return pl.pallas_call(
        functools.partial(_combine_kernel, n_experts=n_experts),
        grid_spec=pltpu.PrefetchScalarGridSpec(
            num_scalar_prefetch=3,
            grid=(n // tm,),
            in_specs=[pl.BlockSpec(memory_space=pl.ANY),
                      pl.BlockSpec((tm, _LANES), lambda i, *_: (i, 0)),
                      pl.BlockSpec((tm, d), lambda i, *_: (i, 0)),
                      _mod_spec(d, 5, row_fn),
                      _resident((1, d))],
            out_specs=pl.BlockSpec((tm, d), lambda i, *_: (i, 0)),
            scratch_shapes=[pltpu.VMEM((2, _local_slots(tm, n_experts), d), _BF16),
                            pltpu.SemaphoreType.DMA((2,))],
        ),
        out_shape=jax.ShapeDtypeStruct((n, d), _F32),
        compiler_params=_params(1),
        name="combine",
    )(seg_start, pad, loc_off, ys, meta_s, x1, mods, norm_final)


def kernel(x, c, ctx, c_ctx, w_ada, b_ada, norm_mix, norm_ffn, w_in, w_conv, rpb, w_attn_out,
           w_conv_out, w_gate, b_gate, w_out, w_ffn_gate, w_ffn_up, w_ffn_down, w_router,
           w_exp_gate, w_exp_up, w_exp_down, norm_final):
    batch, seq, d = x.shape
    ctx_len = ctx.shape[1]
    depth = w_ada.shape[0]
    attn_w = _N_HEADS * _HEAD_DIM
    n_experts = w_router.shape[-1]
    rows = seq // _GRID_W
    n_pairs = attn_w // _LANES
    assert depth == 2 and seq % (_Q_ROWS * _GRID_W) == 0 and attn_w % _LANES == 0
    assert n_experts == _F32_SUBLANES

    ctx_row = batch
    cond_rows = -(-(batch + 1) // 8) * 8
    cond = jnp.zeros((cond_rows, d), _F32).at[:batch].set(c).at[ctx_row].set(c_ctx)
    mods_all = _ada(cond, w_ada, b_ada, tn=1536).reshape(depth, cond_rows, 1, 6 * d)

    union, starts, variant_of, patterns, col_starts = _attn_plan(rows)

    x2d = x.reshape(batch * seq, d)
    c2d = ctx.reshape(batch * ctx_len, d)
    tm_lat = 512
    ctx_row_fn = lambda i: ctx_row

    def lat_rows(tm):
        return lambda i: (i * tm) // seq

    out = None
    for l in range(depth):
        last = l == depth - 1
        mods = mods_all[l]
        wts = {
            "norm_mix": norm_mix[l].reshape(1, d),
            "norm_ffn": norm_ffn[l].reshape(1, d),
            "wconv": w_conv[l],
            "wao": w_attn_out[l].astype(_BF16),
            "wco": w_conv_out[l].astype(_BF16),
            "wgate": w_gate[l].astype(_BF16),
            "bgate": b_gate[l].reshape(1, -1),
            "wout": w_out[l].astype(_BF16),
        }
        w_in_l = w_in[l].astype(_BF16)
        p = _proj(x2d, mods, lat_rows(512), wts["norm_mix"], w_in_l, tm=512)
        if last:
            pc = _proj(c2d, mods, ctx_row_fn, wts["norm_mix"], w_in_l[:, attn_w:3 * attn_w], tm=512)
            kc_col = 0
        else:
            pc = _proj(c2d, mods, ctx_row_fn, wts["norm_mix"], w_in_l, tm=512)
            kc_col = n_pairs
        table = _bias_table(rpb[l], union, patterns, col_starts)
        ya, yca = _attention(p, pc, kc_col, table, starts, variant_of, union, col_starts, batch,
                             seq, ctx_len, with_ctx=not last)
        if not last:
            wts["ffn_gate"] = w_ffn_gate[l // 2].astype(_BF16)
            wts["ffn_up"] = w_ffn_up[l // 2].astype(_BF16)
            wts["ffn_down"] = w_ffn_down[l // 2].astype(_BF16)
            x2d = _merge_ffn(x2d, ya, p, mods, lat_rows(tm_lat), seq, wts, tm=tm_lat)
            c2d = _merge_ffn(c2d, yca, pc, mods, ctx_row_fn, ctx_len, wts, tm=ctx_len)
        else:
            wr_t = w_router[l // 2].T
            wr_hi = wr_t.astype(_BF16)
            wr_lo = (wr_t - wr_hi.astype(_F32)).astype(_BF16)
            wts["router"] = jnp.concatenate([wr_hi, wr_lo], axis=0)
            tm_moe = 512
            n_tok = batch * seq
            n_tiles = n_tok // tm_moe
            tm_mix = 2 * tm_moe
            x1, h2, meta_s, meta_l, cnt = _merge_router(
                x2d, ya, p, mods, lat_rows(tm_mix), seq, wts, n_experts, tm=tm_mix,
                route_rows=tm_moe)
            n_slots = _round_up(_TOP_K * n_tok + n_tiles * n_experts * (_SEG_ALIGN - 1)
                                + n_experts * (_FFN_ROWS - 1), _FFN_ROWS)
            cnt = cnt[:, 0].reshape(n_tiles, n_experts).astype(jnp.int32)
            plan, tails, tile_expert, n_used, next_expert = _routing_plan(cnt, n_slots)
            xs = _dispatch(h2, meta_l, plan, tails, n_slots, n_experts, tm=tm_moe)
            ys = _expert_ffn(xs, tile_expert, n_used, next_expert, w_exp_gate[l // 2],
                             w_exp_up[l // 2], w_exp_down[l // 2])
            out = _combine(ys, meta_s, x1, plan, mods, lat_rows(tm_moe), norm_final.reshape(1, d),
                           n_experts, tm=tm_moe)
    return out.reshape(batch, seq, d)
```

```python
import functools

import numpy as np
import jax
import jax.numpy as jnp
from jax import lax
from jax.experimental import pallas as pl
from jax.experimental.pallas import tpu as pltpu

_F32 = jnp.float32
_BF16 = jnp.bfloat16

_GRID_W = 64
_N_HEADS = 8
_HEAD_DIM = 64
_NA_ROWS = 8
_NA_COLS = 16
_CONV_K = 3
_TOP_K = 2
_EPS = 1e-6
_NEG_INF = -1e30

_LANES = 128
_F32_SUBLANES = 8
_BF16_SUBLANES = 16
_VMEM_LIMIT_BYTES = 56 * 1024 * 1024

_Q_ROWS = 8
_Q_COLS = 16
_K_COLS = 32
_MIX_CHUNK_ROWS = 256
_HEADS_PER_BLOCK = _LANES // _HEAD_DIM
_NT = (((1,), (1,)), ((), ()))
_TN = (((0,), (0,)), ((), ()))


def _params(n_axes):
    return pltpu.CompilerParams(
        dimension_semantics=("arbitrary",) * n_axes,
        vmem_limit_bytes=_VMEM_LIMIT_BYTES)


def _resident(shape):
    nd = len(shape)
    return pl.BlockSpec(shape, lambda *_: (0,) * nd, pipeline_mode=pl.Buffered(1))


def _sigmoid(x):
    return 1.0 / (1.0 + jnp.exp(-x))


def _rmsnorm(x, g):
    ms = jnp.mean(x * x, axis=-1, keepdims=True)
    return (x * lax.rsqrt(ms + _EPS)) * g


def _modulate(x, g, shift, scale):
    return _rmsnorm(x, g) * (1.0 + scale) + shift


def _bdot(a, b):
    return jnp.dot(a, b, preferred_element_type=_F32)


def _indicator(mask):
    return jnp.where(mask, 1.0, 0.0).astype(_BF16)


def _ada_kernel(cond_ref, w_ref, b_ref, o_ref):
    cond = cond_ref[...]
    act = cond * _sigmoid(cond)
    rows = act.shape[0]
    a_hi = act.astype(_BF16)
    a_lo = (act - a_hi.astype(_F32)).astype(_BF16)
    r = _bdot(jnp.concatenate([a_hi, a_lo], axis=0), w_ref[...].astype(_BF16))
    o_ref[...] = r[:rows] + r[rows:] + b_ref[...]


def _ada(cond, w_ada, b_ada, tn):
    depth, d, n_out = w_ada.shape
    rows = cond.shape[0]
    return pl.pallas_call(
        _ada_kernel,
        grid=(depth, n_out // tn),
        in_specs=[
            pl.BlockSpec((rows, d), lambda l, j: (0, 0)),
            pl.BlockSpec((None, d, tn), lambda l, j: (l, 0, j)),
            pl.BlockSpec((None, 1, tn), lambda l, j: (l, 0, j)),
        ],
        out_specs=pl.BlockSpec((None, rows, tn), lambda l, j: (l, 0, j)),
        out_shape=jax.ShapeDtypeStruct((depth, rows, n_out), _F32),
        compiler_params=_params(2),
        name="ada",
    )(cond, w_ada, b_ada.reshape(depth, 1, n_out))


def _mod_spec(d, chunk, row_fn):
    return pl.BlockSpec((None, 1, d), lambda i, *_: (row_fn(i), 0, chunk))


def _proj_kernel(x_ref, g_ref, sh_ref, sc_ref, w_ref, o_ref):
    h = _modulate(x_ref[...], g_ref[...], sh_ref[...], sc_ref[...])
    o_ref[...] = _bdot(h.astype(_BF16), w_ref[...]).astype(o_ref.dtype)


def _proj(x2d, mods, row_fn, g, w, tm):
    n, d = x2d.shape
    n_out = w.shape[1]
    return pl.pallas_call(
        _proj_kernel,
        grid=(n // tm,),
        in_specs=[
            pl.BlockSpec((tm, d), lambda i: (i, 0)),
            _resident((1, d)),
            _mod_spec(d, 0, row_fn),
            _mod_spec(d, 1, row_fn),
            _resident(w.shape),
        ],
        out_specs=pl.BlockSpec((tm, n_out), lambda i: (i, 0)),
        out_shape=jax.ShapeDtypeStruct((n, n_out), _BF16),
        compiler_params=_params(1),
        name="proj",
    )(x2d, g, mods, mods, w)


def _bias_table_kernel(rpb_ref, o_ref, *, union, patterns, col_starts):
    sub = _F32_SUBLANES
    span = _NA_COLS - 1
    slots = _LANES // _K_COLS
    rp = rpb_ref[...]
    lane = lax.broadcasted_iota(jnp.int32, (sub, _LANES), 1)
    srow = lax.broadcasted_iota(jnp.int32, (sub, _LANES), 0)
    ck_local = jnp.bitwise_and(lane, _K_COLS - 1)
    slot_of_lane = jnp.right_shift(lane, _K_COLS.bit_length() - 1)
    neg = jnp.full((sub, _LANES), _NEG_INF, _F32)

    def row_vec(dr):
        return jnp.broadcast_to(rp[dr + _NA_ROWS - 1:dr + _NA_ROWS, :], (sub, _LANES))

    for g, c0 in enumerate(col_starts):
        for gg in range(_Q_COLS // sub):
            k0 = c0 - _Q_COLS * g - sub * gg + span
            lo, hi = max(-(sub - 1), -k0), min(_K_COLS - 1, 2 * span - k0)
            cq = srow + (_Q_COLS * g + sub * gg)
            cs = jnp.clip(cq - _NA_COLS // 2, 0, _GRID_W - _NA_COLS)
            ck = ck_local + c0
            in_win = jnp.logical_and(ck >= cs, ck < cs + _NA_COLS)
            part = [jnp.bitwise_and(lane - (_K_COLS * jq + lo), _LANES - 1) <= hi - lo
                    for jq in range(slots)]
            tiles = {}

            def make_tile(drs):
                base = jnp.zeros((sub, _LANES), _F32)
                keep = None
                for jq, dr in enumerate(drs):
                    if dr is None:
                        continue
                    shifted = pltpu.roll(row_vec(dr), (_K_COLS * jq - k0) % _LANES, 1)
                    base = jnp.where(part[jq], shifted, base)
                    here = slot_of_lane == jq
                    keep = here if keep is None else jnp.logical_or(keep, here)
                if keep is None:
                    return neg
                tile = pltpu.roll(base, 0, 1, stride=1, stride_axis=0)
                return jnp.where(jnp.logical_and(in_win, keep), tile, neg)

            for v, pat in enumerate(patterns):
                for a in range(_Q_ROWS):
                    rows = slice(a * _Q_COLS + sub * gg, a * _Q_COLS + sub * (gg + 1))
                    for quad in range(-(-union // slots)):
                        js = [quad * slots + jq for jq in range(slots)]
                        drs = tuple(pat[a * union + j] if j < union else None for j in js)
                        width = min(slots, union - quad * slots) * _K_COLS
                        if drs not in tiles:
                            tiles[drs] = make_tile(drs)
                        o_ref[v, g, rows, quad * _LANES:quad * _LANES + width] = (
                            tiles[drs][:, :width])


def _bias_table(rpb_l, union, patterns, col_starts):
    h, n_dr, n_dc = rpb_l.shape
    assert n_dc <= _LANES and _LANES % _K_COLS == 0
    n_dr_pad = _round_up(n_dr, _F32_SUBLANES)
    padded = jnp.pad(rpb_l, ((0, 0), (0, n_dr_pad - n_dr), (0, _LANES - n_dc)))
    n_var, n_grp = len(patterns), len(col_starts)
    qt, kt = _Q_ROWS * _Q_COLS, union * _K_COLS
    return pl.pallas_call(
        functools.partial(_bias_table_kernel, union=union, patterns=patterns,
                          col_starts=col_starts),
        grid=(h,),
        in_specs=[pl.BlockSpec((None, n_dr_pad, _LANES), lambda i: (i, 0, 0))],
        out_specs=pl.BlockSpec((n_var, n_grp, None, qt, kt), lambda i: (0, 0, i, 0, 0)),
        out_shape=jax.ShapeDtypeStruct((n_var, n_grp, h, qt, kt), _F32),
        compiler_params=_params(1),
        name="bias_table",
    )(padded)


def _attn_plan(rows):
    kh = min(_NA_ROWS, rows)
    union = min(_Q_ROWS + kh - 1, rows)
    n_blocks = rows // _Q_ROWS
    starts, variant_of, patterns = [], [], []
    for t in range(n_blocks):
        start = int(np.clip(_Q_ROWS * t - kh // 2, 0, rows - union))
        pat = []
        for a in range(_Q_ROWS):
            r = _Q_ROWS * t + a
            rs = int(np.clip(r - kh // 2, 0, rows - kh))
            for j in range(union):
                kr = start + j
                pat.append(kr - r if rs <= kr < rs + kh else None)
        pat = tuple(pat)
        if pat not in patterns:
            patterns.append(pat)
        starts.append(start)
        variant_of.append(patterns.index(pat))
    col_starts = []
    for g in range(_GRID_W // _Q_COLS):
        c0 = int(np.clip(_Q_COLS * g - _NA_COLS // 2, 0, _GRID_W - _K_COLS))
        for cq in range(_Q_COLS * g, _Q_COLS * (g + 1)):
            cs = int(np.clip(cq - _NA_COLS // 2, 0, _GRID_W - _NA_COLS))
            assert c0 <= cs and cs + _NA_COLS <= c0 + _K_COLS
        col_starts.append(c0)
    return union, starts, variant_of, tuple(patterns), tuple(col_starts)


def _first_head_lanes(shape):
    return lax.broadcasted_iota(jnp.int32, shape, len(shape) - 1) < _HEAD_DIM


def _values_with_ones(v):
    lo = _first_head_lanes(v.shape)
    ones = jnp.ones_like(v)
    return jnp.where(lo, v, ones), jnp.where(lo, ones, v)


def _scores(qb, keys, bias_fns):
    lo = _first_head_lanes(qb.shape)
    out = []
    for hh in range(_HEADS_PER_BLOCK):
        qm = jnp.where(lo if hh == 0 else jnp.logical_not(lo), qb, jnp.zeros_like(qb))
        scores = []
        for kk, bias_fn in zip(keys, bias_fns):
            s = lax.dot_general(qm, kk, _NT, preferred_element_type=_F32)
            scores.append(s if bias_fn is None else s + bias_fn(hh))
        out.append(scores)
    return out


def _probabilities(scores):
    m = functools.reduce(jnp.maximum, [jnp.max(s, axis=-1, keepdims=True) for s in scores])
    return [jnp.exp(s - m).astype(_BF16) for s in scores]


def _weighted_values(probs, values):
    res = []
    for hh in range(_HEADS_PER_BLOCK):
        acc = None
        for e, v1 in zip(probs[hh], values):
            o = _bdot(e, v1[hh])
            acc = o if acc is None else acc + o
        res.append(acc)
    lo = _first_head_lanes(res[0].shape)
    num = jnp.where(lo, res[0], res[1])
    den = pltpu.roll(jnp.where(lo, res[1], res[0]), _HEAD_DIM, 1)
    return num / den


def _attn_kernel(start_ref, var_ref, q_ref, k_ref, v_ref, kc_ref, vc_ref, tab_ref, *rest,
                 n_row_blocks, union, col_starts, with_ctx):
    if with_ctx:
        qc_ref, o_ref, oc_ref, s_loc, s_ctx, e_loc, e_ctx = rest
    else:
        o_ref, s_loc, s_ctx, e_loc, e_ctx = rest
    heads = range(_HEADS_PER_BLOCK)
    n_groups = len(col_starts)
    assert n_groups % 2 == 0 and n_groups >= 2 and n_row_blocks >= 1
    scale = _HEAD_DIM ** -0.5
    kc = kc_ref[...]
    vc1 = _values_with_ones(vc_ref[...])

    def gather(ref, first_row, n_rows, col, width):
        return jnp.concatenate(
            [ref[pl.ds(first_row + r * _GRID_W + col, width), :] for r in range(n_rows)], axis=0)

    def key_rows(t):
        return pl.multiple_of(start_ref[t] * _GRID_W, _GRID_W)

    def query_rows(t):
        return pl.multiple_of(t * (_Q_ROWS * _GRID_W), _Q_ROWS * _GRID_W)

    def scores_stage(t, g):
        slot, var = g % 2, var_ref[t]
        qb = gather(q_ref, query_rows(t), _Q_ROWS, _Q_COLS * g, _Q_COLS) * scale
        kb = gather(k_ref, key_rows(t), union, col_starts[g], _K_COLS)
        scores = _scores(qb, [kb, kc], [lambda hh: tab_ref[var, g, hh], None])
        for hh in heads:
            s_loc[slot, hh], s_ctx[slot, hh] = scores[hh]

    def softmax_stage(g):
        slot = g % 2
        for hh in heads:
            e_loc[slot, hh], e_ctx[slot, hh] = _probabilities([s_loc[slot, hh], s_ctx[slot, hh]])

    def values_stage(t, g):
        slot = g % 2
        vb1 = _values_with_ones(gather(v_ref, key_rows(t), union, col_starts[g], _K_COLS))
        probs = [[e_loc[slot, hh], e_ctx[slot, hh]] for hh in heads]
        o = _weighted_values(probs, [vb1, vc1]).astype(o_ref.dtype)
        for a in range(_Q_ROWS):
            o_ref[pl.ds(query_rows(t) + a * _GRID_W + _Q_COLS * g, _Q_COLS), :] = (
                o[a * _Q_COLS:(a + 1) * _Q_COLS])

    def issue(t, k):
        values_stage(t + k // n_groups, k % n_groups)
        softmax_stage((k + 1) % n_groups)
        scores_stage(t + (k + 2) // n_groups, (k + 2) % n_groups)

    scores_stage(0, 0)
    softmax_stage(0)
    scores_stage(0, 1)

    def body(t, carry):
        for k in range(n_groups):
            issue(t, k)
        return carry

    last = n_row_blocks - 1
    lax.fori_loop(0, last, body, 0)
    for k in range(n_groups - 2):
        issue(last, k)
    values_stage(last, n_groups - 2)
    softmax_stage(n_groups - 1)
    values_stage(last, n_groups - 1)

    if with_ctx:
        scores = _scores(qc_ref[...] * scale, [kc], [None])
        probs = [_probabilities(scores[hh]) for hh in heads]
        oc_ref[...] = _weighted_values(probs, [vc1]).astype(oc_ref.dtype)


def _attention(p, pc, kc_col, table, starts, variant_of, union, col_starts, batch, seq, ctx_len,
               with_ctx):
    attn_w = _N_HEADS * _HEAD_DIM
    n_pairs = attn_w // _LANES
    n_var, n_grp = table.shape[:2]
    qb_tokens = _Q_ROWS * _Q_COLS
    union_tokens = union * _K_COLS
    kern = functools.partial(_attn_kernel, n_row_blocks=len(starts), union=union,
                             col_starts=col_starts, with_ctx=with_ctx)
    in_specs = [
        pl.BlockSpec((seq, _LANES), lambda hp, b, *_: (b, hp)),
        pl.BlockSpec((seq, _LANES), lambda hp, b, *_: (b, n_pairs + hp)),
        pl.BlockSpec((seq, _LANES), lambda hp, b, *_: (b, 2 * n_pairs + hp)),
        pl.BlockSpec((ctx_len, _LANES), lambda hp, b, *_: (b, kc_col + hp)),
        pl.BlockSpec((ctx_len, _LANES), lambda hp, b, *_: (b, kc_col + n_pairs + hp)),
        pl.BlockSpec((n_var, n_grp, _HEADS_PER_BLOCK, qb_tokens, union_tokens),
                     lambda hp, b, *_: (0, 0, hp, 0, 0)),
    ]
    args = [p, p, p, pc, pc, table]
    out_specs = [pl.BlockSpec((seq, _LANES), lambda hp, b, *_: (b, hp))]
    out_shape = [jax.ShapeDtypeStruct((batch * seq, attn_w), _BF16)]
    if with_ctx:
        in_specs.append(pl.BlockSpec((ctx_len, _LANES), lambda hp, b, *_: (b, hp)))
        args.append(pc)
        out_specs.append(pl.BlockSpec((ctx_len, _LANES), lambda hp, b, *_: (b, hp)))
        out_shape.append(jax.ShapeDtypeStruct((batch * ctx_len, attn_w), _BF16))
    outs = pl.pallas_call(
        kern,
        grid_spec=pltpu.PrefetchScalarGridSpec(
            num_scalar_prefetch=2,
            grid=(n_pairs, batch),
            in_specs=in_specs,
            out_specs=out_specs,
            scratch_shapes=[
                pltpu.VMEM((2, _HEADS_PER_BLOCK, qb_tokens, union_tokens), _F32),
                pltpu.VMEM((2, _HEADS_PER_BLOCK, qb_tokens, ctx_len), _F32),
                pltpu.VMEM((2, _HEADS_PER_BLOCK, qb_tokens, union_tokens), _BF16),
                pltpu.VMEM((2, _HEADS_PER_BLOCK, qb_tokens, ctx_len), _BF16),
            ],
        ),
        out_shape=out_shape,
        compiler_params=_params(2),
        name="attn",
    )(jnp.asarray(starts, jnp.int32), jnp.asarray(variant_of, jnp.int32), *args)
    return outs if with_ctx else (outs[0], None)


def _mixer_update(i, x_ref, ya_ref, bcu_ref, prev_ref, next_ref, nm_ref, sh1_ref, sc1_ref, g1_ref,
                  wconv_ref, wao_ref, wco_ref, wgate_ref, bgate_ref, wout_ref, seq_len):
    tm, d = x_ref.shape
    cw = wco_ref.shape[0]

    bcu = bcu_ref[...].astype(_F32)
    bg, z = bcu[:, :cw], bcu[:, cw:2 * cw] * bcu[:, 2 * cw:]
    prev = prev_ref[...].astype(_F32)
    nxt = next_ref[...].astype(_F32)
    has_prev = ((i * tm) % seq_len != 0).astype(_F32)
    has_next = (((i + 1) * tm) % seq_len != 0).astype(_F32)
    last = prev.shape[0] - 1
    z_prev = prev[last:last + 1, cw:2 * cw] * prev[last:last + 1, 2 * cw:] * has_prev
    z_next = nxt[0:1, cw:2 * cw] * nxt[0:1, 2 * cw:] * has_next
    row = lax.broadcasted_iota(jnp.int32, z.shape, 0)
    z_dn = jnp.where(row == 0, z_prev, pltpu.roll(z, 1, 0))
    z_up = jnp.where(row == tm - 1, z_next, pltpu.roll(z, tm - 1, 0))
    wconv = wconv_ref[...]
    y_conv = (bg * (z_dn * wconv[0:1] + z * wconv[1:2] + z_up * wconv[2:3])).astype(_BF16)

    chunks = []
    rc = min(tm, _MIX_CHUNK_ROWS)
    for c in range(tm // rc):
        rows = slice(c * rc, (c + 1) * rc)
        x = x_ref[rows, :]
        hb = _modulate(x, nm_ref[...], sh1_ref[...], sc1_ref[...]).astype(_BF16)
        gates = _sigmoid(_bdot(hb, wgate_ref[...]) + bgate_ref[...])
        a_proj = _bdot(ya_ref[rows, :], wao_ref[...])
        c_proj = _bdot(y_conv[rows, :], wco_ref[...])
        merged = gates[:, :d] * a_proj + gates[:, d:] * c_proj
        chunks.append(x + g1_ref[...] * _bdot(merged.astype(_BF16), wout_ref[...]))
    return jnp.concatenate(chunks, axis=0)


_N_MIX_REFS = 15


def _merge_ffn_kernel(*refs, seq_len):
    mix = refs[:_N_MIX_REFS]
    nf_ref, sh2_ref, sc2_ref, g2_ref, wg_ref, wu_ref, wd_ref, o_ref = refs[_N_MIX_REFS:]
    x1 = _mixer_update(pl.program_id(0), *mix, seq_len=seq_len)
    h2 = _modulate(x1, nf_ref[...], sh2_ref[...], sc2_ref[...]).astype(_BF16)
    gate = _bdot(h2, wg_ref[...])
    act = (gate * _sigmoid(gate)) * _bdot(h2, wu_ref[...])
    o_ref[...] = x1 + g2_ref[...] * _bdot(act.astype(_BF16), wd_ref[...])


def _top2(vals, idx, n_idx, axis):
    v1 = jnp.max(vals, axis=axis, keepdims=True)
    i1 = jnp.min(jnp.where(vals == v1, idx, n_idx), axis=axis, keepdims=True)
    rest = jnp.where(idx == i1, -jnp.inf, vals)
    v2 = jnp.max(rest, axis=axis, keepdims=True)
    i2 = jnp.min(jnp.where(rest == v2, idx, n_idx), axis=axis, keepdims=True)
    return i1, i2, v1, v2


def _pick(idx, sel, vals, axis):
    return jnp.sum(jnp.where(idx == sel, vals, 0.0), axis=axis, keepdims=True)


def _merge_router_kernel(*refs, seq_len, n_experts, route_rows):
    mix = refs[:_N_MIX_REFS]
    (nf_ref, sh2_ref, sc2_ref, wr_ref,
     x1_ref, h2_ref, ms_ref, ml_ref, cnt_ref) = refs[_N_MIX_REFS:]
    x1 = _mixer_update(pl.program_id(0), *mix, seq_len=seq_len)
    x1_ref[...] = x1
    h2 = _modulate(x1, nf_ref[...], sh2_ref[...], sc2_ref[...])
    h_hi = h2.astype(_BF16)
    h2_ref[...] = h_hi
    tm = h2.shape[0]
    h_lo = (h2 - h_hi.astype(_F32)).astype(_BF16)
    wr = wr_ref[...]
    by_hi = lax.dot_general(wr, h_hi, _NT, preferred_element_type=_F32)
    by_lo = lax.dot_general(wr, h_lo, _NT, preferred_element_type=_F32)
    lt_all = by_hi[:n_experts] + by_hi[n_experts:] + by_lo[:n_experts]
    row = lax.broadcasted_iota(jnp.int32, (route_rows, route_rows), 0)
    col = lax.broadcasted_iota(jnp.int32, (route_rows, route_rows), 1)
    earlier_cols = _indicator(row < col)
    sub = lax.broadcasted_iota(jnp.int32, (n_experts, route_rows), 0)

    for u in range(tm // route_rows):
        lt = lt_all[:, u * route_rows:(u + 1) * route_rows]
        j1, j2, v1, v2 = _top2(lt, sub, n_experts, 0)
        e2 = jnp.exp(v2 - v1)
        den = 1.0 + e2
        sel_t = _indicator(jnp.logical_or(sub == j1, sub == j2))
        ranks_t = _bdot(sel_t, earlier_cols)
        fields = (j1.astype(_F32), j2.astype(_F32), _pick(sub, j1, ranks_t, 0),
                  _pick(sub, j2, ranks_t, 0), 1.0 / den, e2 / den)
        assert len(fields) <= n_experts
        meta_t = jnp.zeros(lt.shape, _F32)
        for k, f in enumerate(fields):
            meta_t = jnp.where(sub == k, f, meta_t)
        experts = slice(u * n_experts, (u + 1) * n_experts)
        ml_ref[experts, :] = meta_t
        ms_ref[u * route_rows:(u + 1) * route_rows, :] = jnp.concatenate(
            [meta_t, jnp.zeros((_LANES - n_experts, route_rows), _F32)], axis=0).T
        cnt = jnp.sum(sel_t.astype(_F32), axis=1, keepdims=True)
        cnt_ref[experts, :] = jnp.broadcast_to(cnt, (n_experts, _LANES))


def _mixer_specs(n, d, tm, p_cols, row_fn, weights):
    half = p_cols // 2
    sub = _BF16_SUBLANES
    n_sub = n // sub
    specs = [
        pl.BlockSpec((tm, d), lambda i: (i, 0)),
        pl.BlockSpec((tm, weights["wao"].shape[0]), lambda i: (i, 0)),
        pl.BlockSpec((tm, half), lambda i: (i, 1)),
        pl.BlockSpec((sub, half), lambda i: (jnp.maximum(i * (tm // sub) - 1, 0), 1)),
        pl.BlockSpec((sub, half), lambda i: (jnp.minimum((i + 1) * (tm // sub), n_sub - 1), 1)),
        _resident((1, d)),
        _mod_spec(d, 0, row_fn), _mod_spec(d, 1, row_fn), _mod_spec(d, 2, row_fn),
    ]
    for name in ("wconv", "wao", "wco", "wgate", "bgate", "wout"):
        specs.append(_resident(weights[name].shape))
    return specs


def _merge_ffn(x2d, ya, p, mods, row_fn, seq_len, wts, tm):
    n, d = x2d.shape
    specs = _mixer_specs(n, d, tm, p.shape[1], row_fn, wts)
    specs += [_resident((1, d)), _mod_spec(d, 3, row_fn), _mod_spec(d, 4, row_fn),
              _mod_spec(d, 5, row_fn),
              _resident(wts["ffn_gate"].shape), _resident(wts["ffn_up"].shape),
              _resident(wts["ffn_down"].shape)]
    return pl.pallas_call(
        functools.partial(_merge_ffn_kernel, seq_len=seq_len),
        grid=(n // tm,),
        in_specs=specs,
        out_specs=pl.BlockSpec((tm, d), lambda i: (i, 0)),
        out_shape=jax.ShapeDtypeStruct((n, d), _F32),
        compiler_params=_params(1),
        name="merge_ffn",
    )(x2d, ya, p, p, p, wts["norm_mix"], mods, mods, mods,
      wts["wconv"], wts["wao"], wts["wco"], wts["wgate"], wts["bgate"], wts["wout"],
      wts["norm_ffn"], mods, mods, mods, wts["ffn_gate"], wts["ffn_up"], wts["ffn_down"])


def _merge_router(x2d, ya, p, mods, row_fn, seq_len, wts, n_experts, tm, route_rows):
    n, d = x2d.shape
    assert tm % route_rows == 0
    specs = _mixer_specs(n, d, tm, p.shape[1], row_fn, wts)
    specs += [_resident((1, d)), _mod_spec(d, 3, row_fn), _mod_spec(d, 4, row_fn),
              _resident(wts["router"].shape)]
    tile = lambda w: pl.BlockSpec((tm, w), lambda i: (i, 0))
    meta_rows = n_experts * (tm // route_rows)
    n_route = n // route_rows
    return pl.pallas_call(
        functools.partial(_merge_router_kernel, seq_len=seq_len, n_experts=n_experts,
                          route_rows=route_rows),
        grid=(n // tm,),
        in_specs=specs,
        out_specs=[tile(d), tile(d), tile(_LANES),
                   pl.BlockSpec((meta_rows, route_rows), lambda i: (i, 0)),
                   pl.BlockSpec((meta_rows, _LANES), lambda i: (i, 0))],
        out_shape=[jax.ShapeDtypeStruct((n, d), _F32),
                   jax.ShapeDtypeStruct((n, d), _BF16),
                   jax.ShapeDtypeStruct((n, _LANES), _F32),
                   jax.ShapeDtypeStruct((n_route * n_experts, route_rows), _F32),
                   jax.ShapeDtypeStruct((n_route * n_experts, _LANES), _F32)],
        compiler_params=_params(1),
        name="merge_router",
    )(x2d, ya, p, p, p, wts["norm_mix"], mods, mods, mods,
      wts["wconv"], wts["wao"], wts["wco"], wts["wgate"], wts["bgate"], wts["wout"],
      wts["norm_ffn"], mods, mods, wts["router"])


_SEG_ALIGN = _BF16_SUBLANES
_FFN_ROWS = 256


def _round_up(v, m):
    return (v + m - 1) // m * m


def _local_slots(tm, n_experts):
    return _round_up(_TOP_K * tm + n_experts * (_SEG_ALIGN - 1), _SEG_ALIGN)


def _routing_plan(cnt, n_slots):
    pad = _round_up(cnt, _SEG_ALIGN)
    total = jnp.sum(pad, axis=0)
    region = _round_up(total, _FFN_ROWS)
    ends = jnp.cumsum(region)
    seg_start = (ends - region)[None, :] + jnp.cumsum(pad, axis=0) - pad
    loc_off = jnp.cumsum(pad, axis=1) - pad
    n_used = (ends[-1] // _FFN_ROWS).astype(jnp.int32).reshape(1)
    tails = jnp.concatenate([ends - region + total, region - total, n_used])
    tile_row = jnp.arange(n_slots // _FFN_ROWS, dtype=jnp.int32) * _FFN_ROWS
    tile_expert = jnp.minimum(jnp.sum(tile_row[:, None] >= ends[None, :], axis=1),
                              cnt.shape[1] - 1).astype(jnp.int32)
    ids = jnp.arange(cnt.shape[1], dtype=jnp.int32)
    later = jnp.logical_and(region[None, :] > 0, ids[None, :] > ids[:, None])
    nxt = jnp.min(jnp.where(later, ids[None, :], cnt.shape[1]), axis=1)
    next_expert = jnp.where(nxt == cnt.shape[1], ids, nxt).astype(jnp.int32)
    flat = lambda a: a.reshape(-1).astype(jnp.int32)
    return ((flat(seg_start), flat(pad), flat(loc_off)), flat(tails), tile_expert, n_used,
            next_expert)


def _segment_copies(i, n_experts, seg_ref, pad_ref, loc_ref, hbm_ref, buf_ref, sem, to_hbm):
    copies = []
    for e in range(n_experts):
        n = pl.multiple_of(pad_ref[i * n_experts + e], _SEG_ALIGN)
        loc = pl.multiple_of(loc_ref[i * n_experts + e], _SEG_ALIGN)
        seg = pl.multiple_of(seg_ref[i * n_experts + e], _SEG_ALIGN)
        local, remote = buf_ref.at[pl.ds(loc, n)], hbm_ref.at[pl.ds(seg, n)]
        src, dst = (local, remote) if to_hbm else (remote, local)
        copies.append((n, pltpu.make_async_copy(src, dst, sem)))
    return copies


def _start_copies(copies):
    for n, cp in copies:
        pl.when(n > 0)(cp.start)


def _wait_copies(copies):
    for n, cp in copies:
        pl.when(n > 0)(cp.wait)


def _run_copies(copies):
    _start_copies(copies)
    _wait_copies(copies)


def _expert_offset(expert, loc_ref, i, n_experts):
    off = jnp.zeros(expert.shape, jnp.int32)
    for e in range(n_experts):
        off = jnp.where(expert == e, loc_ref[i * n_experts + e], off)
    return off


def _dispatch_kernel(seg_ref, pad_ref, loc_ref, tail_ref, h_ref, ml_ref, xs_ref, bufs, zeros, sems,
                     *, n_experts):
    i = pl.program_id(0)
    last = pl.num_programs(0) - 1
    cur = lax.rem(i, 2)

    def copies(tile, b):
        return _segment_copies(tile, n_experts, seg_ref, pad_ref, loc_ref, xs_ref, bufs.at[b],
                               sems.at[b], True)

    @pl.when(i >= 2)
    def _():
        _wait_copies(copies(i - 2, cur))

    tm = h_ref.shape[0]
    meta = ml_ref[...]
    slot1 = _expert_offset(meta[0:1].astype(jnp.int32), loc_ref, i, n_experts) + meta[2:3].astype(jnp.int32)
    slot2 = _expert_offset(meta[1:2].astype(jnp.int32), loc_ref, i, n_experts) + meta[3:4].astype(jnp.int32)
    slot = lax.broadcasted_iota(jnp.int32, (bufs.shape[1], tm), 0)
    perm = jnp.where(slot == slot1, 1.0, jnp.where(slot == slot2, 1.0, 0.0)).astype(_BF16)
    bufs[cur] = _bdot(perm, h_ref[...]).astype(bufs.dtype)
    _start_copies(copies(i, cur))

    @pl.when(i == last)
    def _():
        @pl.when(i >= 1)
        def _():
            _wait_copies(copies(i - 1, 1 - cur))

        _wait_copies(copies(i, cur))

        zeros[...] = jnp.zeros_like(zeros)
        fills = []
        for e in range(n_experts):
            start = pl.multiple_of(tail_ref[e], _SEG_ALIGN)
            n = pl.multiple_of(tail_ref[n_experts + e], _SEG_ALIGN)
            fills.append((n, pltpu.make_async_copy(zeros.at[pl.ds(0, n)],
                                                   xs_ref.at[pl.ds(start, n)], sems.at[0])))
        _run_copies(fills)

        def fill_tile(j, carry):
            cp = pltpu.make_async_copy(
                zeros, xs_ref.at[pl.ds(pl.multiple_of(j * _FFN_ROWS, _FFN_ROWS), _FFN_ROWS)],
                sems.at[0])
            cp.start()
            cp.wait()
            return carry

        lax.fori_loop(tail_ref[2 * n_experts], xs_ref.shape[0] // _FFN_ROWS, fill_tile, 0)


def _dispatch(h2, meta_l, plan, tails, n_slots, n_experts, tm):
    n, d = h2.shape
    seg_start, pad, loc_off = plan
    return pl.pallas_call(
        functools.partial(_dispatch_kernel, n_experts=n_experts),
        grid_spec=pltpu.PrefetchScalarGridSpec(
            num_scalar_prefetch=4,
            grid=(n // tm,),
            in_specs=[pl.BlockSpec((tm, d), lambda i, *_: (i, 0)),
                      pl.BlockSpec((n_experts, tm), lambda i, *_: (i, 0))],
            out_specs=pl.BlockSpec(memory_space=pl.ANY),
            scratch_shapes=[pltpu.VMEM((2, _local_slots(tm, n_experts), d), _BF16),
                            pltpu.VMEM((_FFN_ROWS, d), _BF16),
                            pltpu.SemaphoreType.DMA((2,))],
        ),
        out_shape=jax.ShapeDtypeStruct((n_slots, d), _BF16),
        compiler_params=_params(1),
        name="dispatch",
    )(seg_start, pad, loc_off, tails, h2, meta_l)


_W_ROUNDS = 16
_W_ROUNDS_PER_STEP = 2
_ST_CUR, _ST_LOADED, _ST_TARGET, _ST_DONE = range(4)


def _expert_ffn_kernel(te_ref, nu_ref, nxt_ref, x_ref, wg_hbm, wu_hbm, wd_hbm, o_ref,
                       wg_v, wu_v, wd_v, stage_gu, stage_d, sems, state):
    j = pl.program_id(0)
    rg = wg_v.shape[1] // _W_ROUNDS
    rd = wd_v.shape[1] // _W_ROUNDS

    def round_copies(e, r, s):
        g_rows = pl.ds(pl.multiple_of(r * rg, rg), rg)
        d_rows = pl.ds(pl.multiple_of(r * rd, rd), rd)
        return (pltpu.make_async_copy(wg_hbm.at[e, g_rows], stage_gu.at[s, 0], sems.at[s, 0]),
                pltpu.make_async_copy(wu_hbm.at[e, g_rows], stage_gu.at[s, 1], sems.at[s, 1]),
                pltpu.make_async_copy(wd_hbm.at[e, d_rows], stage_d.at[s], sems.at[s, 2]))

    def narrow_round(r, s, copy):
        g_rows = pl.ds(pl.multiple_of(r * rg, rg), rg)
        d_rows = pl.ds(pl.multiple_of(r * rd, rd), rd)
        wg_v[copy, g_rows, :] = stage_gu[s, 0].astype(_BF16)
        wu_v[copy, g_rows, :] = stage_gu[s, 1].astype(_BF16)
        wd_v[copy, d_rows, :] = stage_d[s].astype(_BF16)

    @pl.when(j == 0)
    def _():
        state[_ST_CUR] = 0
        state[_ST_LOADED] = -1
        state[_ST_TARGET] = -1
        state[_ST_DONE] = _W_ROUNDS

    active = j < nu_ref[0]
    e = te_ref[j]

    @pl.when(jnp.logical_and(active, e != state[_ST_LOADED]))
    def _():
        other = 1 - state[_ST_CUR]

        @pl.when(state[_ST_TARGET] != e)
        def _():
            state[_ST_TARGET] = e
            state[_ST_DONE] = 0

        def load_round(r, carry):
            cps = round_copies(e, r, 0)
            for cp in cps:
                cp.start()
            for cp in cps:
                cp.wait()
            narrow_round(r, 0, other)
            return carry

        lax.fori_loop(state[_ST_DONE], _W_ROUNDS, load_round, 0)
        nxt = nxt_ref[e]
        state[_ST_CUR] = other
        state[_ST_LOADED] = e
        state[_ST_TARGET] = nxt
        state[_ST_DONE] = jnp.where(nxt != e, 0, _W_ROUNDS)

    cur = state[_ST_CUR]
    target = state[_ST_TARGET]
    done = state[_ST_DONE]
    prefetch = jnp.logical_and(active, done < _W_ROUNDS)

    @pl.when(prefetch)
    def _():
        for s in range(_W_ROUNDS_PER_STEP):
            for cp in round_copies(target, done + s, s):
                cp.start()

    @pl.when(active)
    def _():
        x = x_ref[...]
        gate = _bdot(x, wg_v[cur])
        act = (gate * _sigmoid(gate)) * _bdot(x, wu_v[cur])
        o_ref[...] = _bdot(act.astype(_BF16), wd_v[cur]).astype(o_ref.dtype)

    @pl.when(jnp.logical_not(active))
    def _():
        o_ref[...] = jnp.zeros_like(o_ref)

    @pl.when(prefetch)
    def _():
        for s in range(_W_ROUNDS_PER_STEP):
            for cp in round_copies(target, done + s, s):
                cp.wait()
            narrow_round(done + s, s, 1 - cur)
        state[_ST_DONE] = done + _W_ROUNDS_PER_STEP


def _expert_ffn(xs, tile_expert, n_used, next_expert, wg, wu, wd):
    n_slots, d = xs.shape
    n_experts, _, f = wg.shape
    assert d % (_W_ROUNDS * _BF16_SUBLANES) == 0 and f % (_W_ROUNDS * _BF16_SUBLANES) == 0
    assert _W_ROUNDS % _W_ROUNDS_PER_STEP == 0
    any_spec = pl.BlockSpec(memory_space=pl.ANY)
    return pl.pallas_call(
        _expert_ffn_kernel,
        grid_spec=pltpu.PrefetchScalarGridSpec(
            num_scalar_prefetch=3,
            grid=(n_slots // _FFN_ROWS,),
            in_specs=[
                pl.BlockSpec((_FFN_ROWS, d),
                             lambda j, te, nu, nx: (jnp.maximum(jnp.minimum(j, nu[0] - 1), 0), 0)),
                any_spec, any_spec, any_spec,
            ],
            out_specs=pl.BlockSpec((_FFN_ROWS, d), lambda j, te, nu, nx: (j, 0)),
            scratch_shapes=[
                pltpu.VMEM((2, d, f), _BF16), pltpu.VMEM((2, d, f), _BF16),
                pltpu.VMEM((2, f, d), _BF16),
                pltpu.VMEM((_W_ROUNDS_PER_STEP, 2, d // _W_ROUNDS, f), _F32),
                pltpu.VMEM((_W_ROUNDS_PER_STEP, f // _W_ROUNDS, d), _F32),
                pltpu.SemaphoreType.DMA((_W_ROUNDS_PER_STEP, 3)),
                pltpu.SMEM((4,), jnp.int32),
            ],
        ),
        out_shape=jax.ShapeDtypeStruct((n_slots, d), _BF16),
        compiler_params=_params(1),
        name="expert_ffn",
    )(tile_expert, n_used, next_expert, xs, wg, wu, wd)


def _combine_kernel(seg_ref, pad_ref, loc_ref, ys_ref, ms_ref, x1_ref, g2_ref, nf_ref, o_ref,
                    bufs, sems, *, n_experts):
    i = pl.program_id(0)
    tm = x1_ref.shape[0]
    cur = lax.rem(i, 2)

    def copies(tile, b):
        return _segment_copies(tile, n_experts, seg_ref, pad_ref, loc_ref, ys_ref, bufs.at[b],
                               sems.at[b], False)

    @pl.when(i == 0)
    def _():
        bufs[...] = jnp.zeros_like(bufs)
        _start_copies(copies(0, 0))

    @pl.when(i + 1 < pl.num_programs(0))
    def _():
        _start_copies(copies(i + 1, 1 - cur))

    _wait_copies(copies(i, cur))
    rc = min(tm, _MIX_CHUNK_ROWS)
    slot = lax.broadcasted_iota(jnp.int32, (rc, bufs.shape[1]), 1)
    for c in range(tm // rc):
        rows = slice(c * rc, (c + 1) * rc)
        meta = ms_ref[rows, :]
        col = lambda k: meta[:, k:k + 1]
        slot1 = _expert_offset(col(0).astype(jnp.int32), loc_ref, i, n_experts) + col(2).astype(jnp.int32)
        slot2 = _expert_offset(col(1).astype(jnp.int32), loc_ref, i, n_experts) + col(3).astype(jnp.int32)
        weights = jnp.where(slot == slot1, col(4), jnp.where(slot == slot2, col(5), 0.0))
        y = _bdot(weights.astype(_BF16), bufs[cur])
        o_ref[rows, :] = _rmsnorm(x1_ref[rows, :] + g2_ref[...] * y, nf_ref[...])


def _combine(ys, meta_s, x1, plan, mods, row_fn, norm_final, n_experts, tm):
    n, d = x1.shape
    seg_start, pad, loc_off = plan
    return pl.pallas_call(
        functools.partial(_combine_kernel, n_experts=n_experts),
        grid_spec=pltpu.PrefetchScalarGridSpec(
            num_scalar_prefetch=3,
            grid=(n // tm,),
            in_specs=[pl.BlockSpec(memory_space=pl.ANY),
                      pl.BlockSpec((tm, _LANES), lambda i, *_: (i, 0)),
                      pl.BlockSpec((tm, d), lambda i, *_: (i, 0)),
                      _mod_spec(d, 5, row_fn),
                      _resident((1, d))],
            out_specs=pl.BlockSpec((tm, d), lambda i, *_: (i, 0)),
            scratch_shapes=[pltpu.VMEM((2, _local_slots(tm, n_experts), d), _BF16),
                            pltpu.SemaphoreType.DMA((2,))],
        ),
        out_shape=jax.ShapeDtypeStruct((n, d), _F32),
        compiler_params=_params(1),
        name="combine",
    )(seg_start, pad, loc_off, ys, meta_s, x1, mods, norm_final)


def kernel(x, c, ctx, c_ctx, w_ada, b_ada, norm_mix, norm_ffn, w_in, w_conv, rpb, w_attn_out,
           w_conv_out, w_gate, b_gate, w_out, w_ffn_gate, w_ffn_up, w_ffn_down, w_router,
           w_exp_gate, w_exp_up, w_exp_down, norm_final):
    batch, seq, d = x.shape
    ctx_len = ctx.shape[1]
    depth = w_ada.shape[0]
    attn_w = _N_HEADS * _HEAD_DIM
    n_experts = w_router.shape[-1]
    rows = seq // _GRID_W
    n_pairs = attn_w // _LANES
    assert depth == 2 and seq % (_Q_ROWS * _GRID_W) == 0 and attn_w % _LANES == 0
    assert n_experts == _F32_SUBLANES

    ctx_row = batch
    cond_rows = -(-(batch + 1) // 8) * 8
    cond = jnp.zeros((cond_rows, d), _F32).at[:batch].set(c).at[ctx_row].set(c_ctx)
    mods_all = _ada(cond, w_ada, b_ada, tn=1536).reshape(depth, cond_rows, 1, 6 * d)

    union, starts, variant_of, patterns, col_starts = _attn_plan(rows)

    x2d = x.reshape(batch * seq, d)
    c2d = ctx.reshape(batch * ctx_len, d)
    tm_lat = 512
    ctx_row_fn = lambda i: ctx_row

    def lat_rows(tm):
        return lambda i: (i * tm) // seq

    out = None
    for l in range(depth):
        last = l == depth - 1
        mods = mods_all[l]
        wts = {
            "norm_mix": norm_mix[l].reshape(1, d),
            "norm_ffn": norm_ffn[l].reshape(1, d),
            "wconv": w_conv[l],
            "wao": w_attn_out[l].astype(_BF16),
            "wco": w_conv_out[l].astype(_BF16),
            "wgate": w_gate[l].astype(_BF16),
            "bgate": b_gate[l].reshape(1, -1),
            "wout": w_out[l].astype(_BF16),
        }
        w_in_l = w_in[l].astype(_BF16)
        p = _proj(x2d, mods, lat_rows(1024), wts["norm_mix"], w_in_l, tm=1024)
        if last:
            pc = _proj(c2d, mods, ctx_row_fn, wts["norm_mix"], w_in_l[:, attn_w:3 * attn_w], tm=512)
            kc_col = 0
        else:
            pc = _proj(c2d, mods, ctx_row_fn, wts["norm_mix"], w_in_l, tm=512)
            kc_col = n_pairs
        table = _bias_table(rpb[l], union, patterns, col_starts)
        ya, yca = _attention(p, pc, kc_col, table, starts, variant_of, union, col_starts, batch,
                             seq, ctx_len, with_ctx=not last)
        if not last:
            wts["ffn_gate"] = w_ffn_gate[l // 2].astype(_BF16)
            wts["ffn_up"] = w_ffn_up[l // 2].astype(_BF16)
            wts["ffn_down"] = w_ffn_down[l // 2].astype(_BF16)
            x2d = _merge_ffn(x2d, ya, p, mods, lat_rows(tm_lat), seq, wts, tm=tm_lat)
            c2d = _merge_ffn(c2d, yca, pc, mods, ctx_row_fn, ctx_len, wts, tm=ctx_len)
        else:
            wr_t = w_router[l // 2].T
            wr_hi = wr_t.astype(_BF16)
            wr_lo = (wr_t - wr_hi.astype(_F32)).astype(_BF16)
            wts["router"] = jnp.concatenate([wr_hi, wr_lo], axis=0)
            tm_moe = 512
            n_tok = batch * seq
            n_tiles = n_tok // tm_moe
            tm_mix = 2 * tm_moe
            x1, h2, meta_s, meta_l, cnt = _merge_router(
                x2d, ya, p, mods, lat_rows(tm_mix), seq, wts, n_experts, tm=tm_mix,
                route_rows=tm_moe)
            n_slots = _round_up(_TOP_K * n_tok + n_tiles * n_experts * (_SEG_ALIGN - 1)
                                + n_experts * (_FFN_ROWS - 1), _FFN_ROWS)
            cnt = cnt[:, 0].reshape(n_tiles, n_experts).astype(jnp.int32)
            plan, tails, tile_expert, n_used, next_expert = _routing_plan(cnt, n_slots)
            xs = _dispatch(h2, meta_l, plan, tails, n_slots, n_experts, tm=tm_moe)
            ys = _expert_ffn(xs, tile_expert, n_used, next_expert, w_exp_gate[l // 2],
                             w_exp_up[l // 2], w_exp_down[l // 2])
            out = _combine(ys, meta_s, x1, plan, mods, lat_rows(tm_moe), norm_final.reshape(1, d),
                           n_experts, tm=tm_moe)
    return out.reshape(batch, seq, d)
```

```python
import functools

import numpy as np
import jax
import jax.numpy as jnp
from jax import lax
from jax.experimental import pallas as pl
from jax.experimental.pallas import tpu as pltpu

_F32 = jnp.float32
_BF16 = jnp.bfloat16

_GRID_W = 64
_N_HEADS = 8
_HEAD_DIM = 64
_NA_ROWS = 8
_NA_COLS = 16
_CONV_K = 3
_TOP_K = 2
_EPS = 1e-6
_NEG_INF = -1e30

_LANES = 128
_F32_SUBLANES = 8
_BF16_SUBLANES = 16
_VMEM_LIMIT_BYTES = 56 * 1024 * 1024

_Q_ROWS = 8
_Q_COLS = 16
_K_COLS = 32
_MIX_CHUNK_ROWS = 256
_HEADS_PER_BLOCK = _LANES // _HEAD_DIM
_NT = (((1,), (1,)), ((), ()))
_TN = (((0,), (0,)), ((), ()))


def _params(n_axes):
    return pltpu.CompilerParams(
        dimension_semantics=("arbitrary",) * n_axes,
        vmem_limit_bytes=_VMEM_LIMIT_BYTES)


def _resident(shape):
    nd = len(shape)
    return pl.BlockSpec(shape, lambda *_: (0,) * nd, pipeline_mode=pl.Buffered(1))


def _sigmoid(x):
    return 1.0 / (1.0 + jnp.exp(-x))


def _rmsnorm(x, g):
    ms = jnp.mean(x * x, axis=-1, keepdims=True)
    return (x * lax.rsqrt(ms + _EPS)) * g


def _modulate(x, g, shift, scale):
    return _rmsnorm(x, g) * (1.0 + scale) + shift


def _bdot(a, b):
    return jnp.dot(a, b, preferred_element_type=_F32)


def _indicator(mask):
    return jnp.where(mask, 1.0, 0.0).astype(_BF16)


def _ada_kernel(cond_ref, w_ref, b_ref, o_ref):
    cond = cond_ref[...]
    act = cond * _sigmoid(cond)
    rows = act.shape[0]
    a_hi = act.astype(_BF16)
    a_lo = (act - a_hi.astype(_F32)).astype(_BF16)
    r = _bdot(jnp.concatenate([a_hi, a_lo], axis=0), w_ref[...].astype(_BF16))
    o_ref[...] = r[:rows] + r[rows:] + b_ref[...]


def _ada(cond, w_ada, b_ada, tn):
    depth, d, n_out = w_ada.shape
    rows = cond.shape[0]
    return pl.pallas_call(
        _ada_kernel,
        grid=(depth, n_out // tn),
        in_specs=[
            pl.BlockSpec((rows, d), lambda l, j: (0, 0)),
            pl.BlockSpec((None, d, tn), lambda l, j: (l, 0, j)),
            pl.BlockSpec((None, 1, tn), lambda l, j: (l, 0, j)),
        ],
        out_specs=pl.BlockSpec((None, rows, tn), lambda l, j: (l, 0, j)),
        out_shape=jax.ShapeDtypeStruct((depth, rows, n_out), _F32),
        compiler_params=_params(2),
        name="ada",
    )(cond, w_ada, b_ada.reshape(depth, 1, n_out))


def _mod_spec(d, chunk, row_fn):
    return pl.BlockSpec((None, 1, d), lambda i, *_: (row_fn(i), 0, chunk))


def _proj_kernel(x_ref, g_ref, sh_ref, sc_ref, w_ref, o_ref):
    h = _modulate(x_ref[...], g_ref[...], sh_ref[...], sc_ref[...])
    o_ref[...] = _bdot(h.astype(_BF16), w_ref[...]).astype(o_ref.dtype)


def _proj(x2d, mods, row_fn, g, w, tm):
    n, d = x2d.shape
    n_out = w.shape[1]
    return pl.pallas_call(
        _proj_kernel,
        grid=(n // tm,),
        in_specs=[
            pl.BlockSpec((tm, d), lambda i: (i, 0)),
            _resident((1, d)),
            _mod_spec(d, 0, row_fn),
            _mod_spec(d, 1, row_fn),
            _resident(w.shape),
        ],
        out_specs=pl.BlockSpec((tm, n_out), lambda i: (i, 0)),
        out_shape=jax.ShapeDtypeStruct((n, n_out), _BF16),
        compiler_params=_params(1),
        name="proj",
    )(x2d, g, mods, mods, w)


def _bias_table_kernel(rpb_ref, o_ref, *, union, patterns, col_starts):
    sub = _F32_SUBLANES
    span = _NA_COLS - 1
    slots = _LANES // _K_COLS
    rp = rpb_ref[...]
    lane = lax.broadcasted_iota(jnp.int32, (sub, _LANES), 1)
    srow = lax.broadcasted_iota(jnp.int32, (sub, _LANES), 0)
    ck_local = jnp.bitwise_and(lane, _K_COLS - 1)
    slot_of_lane = jnp.right_shift(lane, _K_COLS.bit_length() - 1)
    neg = jnp.full((sub, _LANES), _NEG_INF, _F32)
    skewed, pieces = {}, {}

    def piece(dr, shift):
        if dr not in skewed:
            row = jnp.broadcast_to(rp[dr + _NA_ROWS - 1:dr + _NA_ROWS, :], (sub, _LANES))
            skewed[dr] = pltpu.roll(row, 0, 1, stride=1, stride_axis=0)
        if (dr, shift) not in pieces:
            pieces[dr, shift] = pltpu.roll(skewed[dr], shift, 1) if shift else skewed[dr]
        return pieces[dr, shift]

    for g, c0 in enumerate(col_starts):
        for gg in range(_Q_COLS // sub):
            k0 = c0 - _Q_COLS * g - sub * gg + span
            cq = srow + (_Q_COLS * g + sub * gg)
            cs = jnp.clip(cq - _NA_COLS // 2, 0, _GRID_W - _NA_COLS)
            ck = ck_local + c0
            in_win = jnp.logical_and(ck >= cs, ck < cs + _NA_COLS)
            tiles = {}

            def make_tile(drs):
                tile = neg
                for jq, dr in enumerate(drs):
                    if dr is not None:
                        here = jnp.logical_and(in_win, slot_of_lane == jq)
                        tile = jnp.where(here, piece(dr, (_K_COLS * jq - k0) % _LANES), tile)
                return tile

            for v, pat in enumerate(patterns):
                for a in range(_Q_ROWS):
                    rows = slice(a * _Q_COLS + sub * gg, a * _Q_COLS + sub * (gg + 1))
                    for quad in range(-(-union // slots)):
                        js = [quad * slots + jq for jq in range(slots)]
                        drs = tuple(pat[a * union + j] if j < union else None for j in js)
                        width = min(slots, union - quad * slots) * _K_COLS
                        if drs not in tiles:
                            tiles[drs] = make_tile(drs)
                        o_ref[v, g, rows, quad * _LANES:quad * _LANES + width] = (
                            tiles[drs][:, :width])


def _bias_table(rpb_l, union, patterns, col_starts):
    h, n_dr, n_dc = rpb_l.shape
    assert n_dc <= _LANES and _LANES % _K_COLS == 0
    n_dr_pad = _round_up(n_dr, _F32_SUBLANES)
    padded = jnp.pad(rpb_l, ((0, 0), (0, n_dr_pad - n_dr), (0, _LANES - n_dc)))
    n_var, n_grp = len(patterns), len(col_starts)
    qt, kt = _Q_ROWS * _Q_COLS, union * _K_COLS
    return pl.pallas_call(
        functools.partial(_bias_table_kernel, union=union, patterns=patterns,
                          col_starts=col_starts),
        grid=(h,),
        in_specs=[pl.BlockSpec((None, n_dr_pad, _LANES), lambda i: (i, 0, 0))],
        out_specs=pl.BlockSpec((n_var, n_grp, None, qt, kt), lambda i: (0, 0, i, 0, 0)),
        out_shape=jax.ShapeDtypeStruct((n_var, n_grp, h, qt, kt), _F32),
        compiler_params=_params(1),
        name="bias_table",
    )(padded)


def _attn_plan(rows):
    kh = min(_NA_ROWS, rows)
    union = min(_Q_ROWS + kh - 1, rows)
    n_blocks = rows // _Q_ROWS
    starts, variant_of, patterns = [], [], []
    for t in range(n_blocks):
        start = int(np.clip(_Q_ROWS * t - kh // 2, 0, rows - union))
        pat = []
        for a in range(_Q_ROWS):
            r = _Q_ROWS * t + a
            rs = int(np.clip(r - kh // 2, 0, rows - kh))
            for j in range(union):
                kr = start + j
                pat.append(kr - r if rs <= kr < rs + kh else None)
        pat = tuple(pat)
        if pat not in patterns:
            patterns.append(pat)
        starts.append(start)
        variant_of.append(patterns.index(pat))
    col_starts = []
    for g in range(_GRID_W // _Q_COLS):
        c0 = int(np.clip(_Q_COLS * g - _NA_COLS // 2, 0, _GRID_W - _K_COLS))
        for cq in range(_Q_COLS * g, _Q_COLS * (g + 1)):
            cs = int(np.clip(cq - _NA_COLS // 2, 0, _GRID_W - _NA_COLS))
            assert c0 <= cs and cs + _NA_COLS <= c0 + _K_COLS
        col_starts.append(c0)
    return union, starts, variant_of, tuple(patterns), tuple(col_starts)


def _first_head_lanes(shape):
    return lax.broadcasted_iota(jnp.int32, shape, len(shape) - 1) < _HEAD_DIM


def _values_with_ones(v):
    lo = _first_head_lanes(v.shape)
    ones = jnp.ones_like(v)
    return jnp.where(lo, v, ones), jnp.where(lo, ones, v)


def _scores(qb, keys, bias_fns):
    lo = _first_head_lanes(qb.shape)
    out = []
    for hh in range(_HEADS_PER_BLOCK):
        qm = jnp.where(lo if hh == 0 else jnp.logical_not(lo), qb, jnp.zeros_like(qb))
        scores = []
        for kk, bias_fn in zip(keys, bias_fns):
            s = lax.dot_general(qm, kk, _NT, preferred_element_type=_F32)
            scores.append(s if bias_fn is None else s + bias_fn(hh))
        out.append(scores)
    return out


def _probabilities(scores):
    m = functools.reduce(jnp.maximum, [jnp.max(s, axis=-1, keepdims=True) for s in scores])
    return [jnp.exp(s - m).astype(_BF16) for s in scores]


def _weighted_values(probs, values):
    res = []
    for hh in range(_HEADS_PER_BLOCK):
        acc = None
        for e, v1 in zip(probs[hh], values):
            o = _bdot(e, v1[hh])
            acc = o if acc is None else acc + o
        res.append(acc)
    lo = _first_head_lanes(res[0].shape)
    num = jnp.where(lo, res[0], res[1])
    den = pltpu.roll(jnp.where(lo, res[1], res[0]), _HEAD_DIM, 1)
    return num / den


def _attn_kernel(start_ref, var_ref, q_ref, k_ref, v_ref, kc_ref, vc_ref, tab_ref, *rest,
                 n_row_blocks, union, col_starts, with_ctx):
    if with_ctx:
        qc_ref, o_ref, oc_ref, s_loc, s_ctx, e_loc, e_ctx = rest
    else:
        o_ref, s_loc, s_ctx, e_loc, e_ctx = rest
    heads = range(_HEADS_PER_BLOCK)
    n_groups = len(col_starts)
    assert n_groups % 2 == 0 and n_groups >= 2 and n_row_blocks >= 1
    scale = _HEAD_DIM ** -0.5
    kc = kc_ref[...]
    vc1 = _values_with_ones(vc_ref[...])

    def gather(ref, first_row, n_rows, col, width):
        return jnp.concatenate(
            [ref[pl.ds(first_row + r * _GRID_W + col, width), :] for r in range(n_rows)], axis=0)

    def key_rows(t):
        return pl.multiple_of(start_ref[t] * _GRID_W, _GRID_W)

    def query_rows(t):
        return pl.multiple_of(t * (_Q_ROWS * _GRID_W), _Q_ROWS * _GRID_W)

    def scores_stage(t, g):
        slot, var = g % 2, var_ref[t]
        qb = gather(q_ref, query_rows(t), _Q_ROWS, _Q_COLS * g, _Q_COLS) * scale
        kb = gather(k_ref, key_rows(t), union, col_starts[g], _K_COLS)
        scores = _scores(qb, [kb, kc], [lambda hh: tab_ref[var, g, hh], None])
        for hh in heads:
            s_loc[slot, hh], s_ctx[slot, hh] = scores[hh]

    def softmax_stage(g):
        slot = g % 2
        for hh in heads:
            e_loc[slot, hh], e_ctx[slot, hh] = _probabilities([s_loc[slot, hh], s_ctx[slot, hh]])

    def values_stage(t, g):
        slot = g % 2
        vb1 = _values_with_ones(gather(v_ref, key_rows(t), union, col_starts[g], _K_COLS))
        probs = [[e_loc[slot, hh], e_ctx[slot, hh]] for hh in heads]
        o = _weighted_values(probs, [vb1, vc1]).astype(o_ref.dtype)
        for a in range(_Q_ROWS):
            o_ref[pl.ds(query_rows(t) + a * _GRID_W + _Q_COLS * g, _Q_COLS), :] = (
                o[a * _Q_COLS:(a + 1) * _Q_COLS])

    def issue(t, k):
        values_stage(t + k // n_groups, k % n_groups)
        softmax_stage((k + 1) % n_groups)
        scores_stage(t + (k + 2) // n_groups, (k + 2) % n_groups)

    scores_stage(0, 0)
    softmax_stage(0)
    scores_stage(0, 1)

    def body(t, carry):
        for k in range(n_groups):
            issue(t, k)
        return carry

    last = n_row_blocks - 1
    lax.fori_loop(0, last, body, 0)
    for k in range(n_groups - 2):
        issue(last, k)
    values_stage(last, n_groups - 2)
    softmax_stage(n_groups - 1)
    values_stage(last, n_groups - 1)

    if with_ctx:
        scores = _scores(qc_ref[...] * scale, [kc], [None])
        probs = [_probabilities(scores[hh]) for hh in heads]
        oc_ref[...] = _weighted_values(probs, [vc1]).astype(oc_ref.dtype)


def _attention(p, pc, kc_col, table, starts, variant_of, union, col_starts, batch, seq, ctx_len,
               with_ctx):
    attn_w = _N_HEADS * _HEAD_DIM
    n_pairs = attn_w // _LANES
    n_var, n_grp = table.shape[:2]
    qb_tokens = _Q_ROWS * _Q_COLS
    union_tokens = union * _K_COLS
    kern = functools.partial(_attn_kernel, n_row_blocks=len(starts), union=union,
                             col_starts=col_starts, with_ctx=with_ctx)
    in_specs = [
        pl.BlockSpec((seq, _LANES), lambda hp, b, *_: (b, hp)),
        pl.BlockSpec((seq, _LANES), lambda hp, b, *_: (b, n_pairs + hp)),
        pl.BlockSpec((seq, _LANES), lambda hp, b, *_: (b, 2 * n_pairs + hp)),
        pl.BlockSpec((ctx_len, _LANES), lambda hp, b, *_: (b, kc_col + hp)),
        pl.BlockSpec((ctx_len, _LANES), lambda hp, b, *_: (b, kc_col + n_pairs + hp)),
        pl.BlockSpec((n_var, n_grp, _HEADS_PER_BLOCK, qb_tokens, union_tokens),
                     lambda hp, b, *_: (0, 0, hp, 0, 0)),
    ]
    args = [p, p, p, pc, pc, table]
    out_specs = [pl.BlockSpec((seq, _LANES), lambda hp, b, *_: (b, hp))]
    out_shape = [jax.ShapeDtypeStruct((batch * seq, attn_w), _BF16)]
    if with_ctx:
        in_specs.append(pl.BlockSpec((ctx_len, _LANES), lambda hp, b, *_: (b, hp)))
        args.append(pc)
        out_specs.append(pl.BlockSpec((ctx_len, _LANES), lambda hp, b, *_: (b, hp)))
        out_shape.append(jax.ShapeDtypeStruct((batch * ctx_len, attn_w), _BF16))
    outs = pl.pallas_call(
        kern,
        grid_spec=pltpu.PrefetchScalarGridSpec(
            num_scalar_prefetch=2,
            grid=(n_pairs, batch),
            in_specs=in_specs,
            out_specs=out_specs,
            scratch_shapes=[
                pltpu.VMEM((2, _HEADS_PER_BLOCK, qb_tokens, union_tokens), _F32),
                pltpu.VMEM((2, _HEADS_PER_BLOCK, qb_tokens, ctx_len), _F32),
                pltpu.VMEM((2, _HEADS_PER_BLOCK, qb_tokens, union_tokens), _BF16),
                pltpu.VMEM((2, _HEADS_PER_BLOCK, qb_tokens, ctx_len), _BF16),
            ],
        ),
        out_shape=out_shape,
        compiler_params=_params(2),
        name="attn",
    )(jnp.asarray(starts, jnp.int32), jnp.asarray(variant_of, jnp.int32), *args)
    return outs if with_ctx else (outs[0], None)


def _mixer_update(i, x_ref, ya_ref, bcu_ref, prev_ref, next_ref, nm_ref, sh1_ref, sc1_ref, g1_ref,
                  wconv_ref, wao_ref, wco_ref, wgate_ref, bgate_ref, wout_ref, seq_len):
    tm, d = x_ref.shape
    cw = wco_ref.shape[0]

    bcu = bcu_ref[...].astype(_F32)
    bg, z = bcu[:, :cw], bcu[:, cw:2 * cw] * bcu[:, 2 * cw:]
    prev = prev_ref[...].astype(_F32)
    nxt = next_ref[...].astype(_F32)
    has_prev = ((i * tm) % seq_len != 0).astype(_F32)
    has_next = (((i + 1) * tm) % seq_len != 0).astype(_F32)
    last = prev.shape[0] - 1
    z_prev = prev[last:last + 1, cw:2 * cw] * prev[last:last + 1, 2 * cw:] * has_prev
    z_next = nxt[0:1, cw:2 * cw] * nxt[0:1, 2 * cw:] * has_next
    row = lax.broadcasted_iota(jnp.int32, z.shape, 0)
    z_dn = jnp.where(row == 0, z_prev, pltpu.roll(z, 1, 0))
    z_up = jnp.where(row == tm - 1, z_next, pltpu.roll(z, tm - 1, 0))
    wconv = wconv_ref[...]
    y_conv = (bg * (z_dn * wconv[0:1] + z * wconv[1:2] + z_up * wconv[2:3])).astype(_BF16)

    chunks = []
    rc = min(tm, _MIX_CHUNK_ROWS)
    for c in range(tm // rc):
        rows = slice(c * rc, (c + 1) * rc)
        x = x_ref[rows, :]
        hb = _modulate(x, nm_ref[...], sh1_ref[...], sc1_ref[...]).astype(_BF16)
        gates = _sigmoid(_bdot(hb, wgate_ref[...]) + bgate_ref[...])
        a_proj = _bdot(ya_ref[rows, :], wao_ref[...])
        c_proj = _bdot(y_conv[rows, :], wco_ref[...])
        merged = gates[:, :d] * a_proj + gates[:, d:] * c_proj
        chunks.append(x + g1_ref[...] * _bdot(merged.astype(_BF16), wout_ref[...]))
    return jnp.concatenate(chunks, axis=0)


_N_MIX_REFS = 15


def _merge_ffn_kernel(*refs, seq_len):
    mix = refs[:_N_MIX_REFS]
    nf_ref, sh2_ref, sc2_ref, g2_ref, wg_ref, wu_ref, wd_ref, o_ref = refs[_N_MIX_REFS:]
    x1 = _mixer_update(pl.program_id(0), *mix, seq_len=seq_len)
    h2 = _modulate(x1, nf_ref[...], sh2_ref[...], sc2_ref[...]).astype(_BF16)
    gate = _bdot(h2, wg_ref[...])
    act = (gate * _sigmoid(gate)) * _bdot(h2, wu_ref[...])
    o_ref[...] = x1 + g2_ref[...] * _bdot(act.astype(_BF16), wd_ref[...])


def _top2(vals, idx, n_idx, axis):
    v1 = jnp.max(vals, axis=axis, keepdims=True)
    i1 = jnp.min(jnp.where(vals == v1, idx, n_idx), axis=axis, keepdims=True)
    rest = jnp.where(idx == i1, -jnp.inf, vals)
    v2 = jnp.max(rest, axis=axis, keepdims=True)
    i2 = jnp.min(jnp.where(rest == v2, idx, n_idx), axis=axis, keepdims=True)
    return i1, i2, v1, v2


def _pick(idx, sel, vals, axis):
    return jnp.sum(jnp.where(idx == sel, vals, 0.0), axis=axis, keepdims=True)


def _merge_router_kernel(*refs, seq_len, n_experts, route_rows):
    mix = refs[:_N_MIX_REFS]
    (nf_ref, sh2_ref, sc2_ref, wr_ref,
     x1_ref, h2_ref, ms_ref, ml_ref, cnt_ref) = refs[_N_MIX_REFS:]
    x1 = _mixer_update(pl.program_id(0), *mix, seq_len=seq_len)
    x1_ref[...] = x1
    h2 = _modulate(x1, nf_ref[...], sh2_ref[...], sc2_ref[...])
    h_hi = h2.astype(_BF16)
    h2_ref[...] = h_hi
    tm = h2.shape[0]
    h_lo = (h2 - h_hi.astype(_F32)).astype(_BF16)
    wr = wr_ref[...]
    by_hi = lax.dot_general(wr, h_hi, _NT, preferred_element_type=_F32)
    by_lo = lax.dot_general(wr, h_lo, _NT, preferred_element_type=_F32)
    lt_all = by_hi[:n_experts] + by_hi[n_experts:] + by_lo[:n_experts]
    row = lax.broadcasted_iota(jnp.int32, (route_rows, route_rows), 0)
    col = lax.broadcasted_iota(jnp.int32, (route_rows, route_rows), 1)
    earlier_cols = _indicator(row < col)
    sub = lax.broadcasted_iota(jnp.int32, (n_experts, route_rows), 0)

    for u in range(tm // route_rows):
        lt = lt_all[:, u * route_rows:(u + 1) * route_rows]
        j1, j2, v1, v2 = _top2(lt, sub, n_experts, 0)
        e2 = jnp.exp(v2 - v1)
        den = 1.0 + e2
        sel_t = _indicator(jnp.logical_or(sub == j1, sub == j2))
        ranks_t = _bdot(sel_t, earlier_cols)
        fields = (j1.astype(_F32), j2.astype(_F32), _pick(sub, j1, ranks_t, 0),
                  _pick(sub, j2, ranks_t, 0), 1.0 / den, e2 / den)
        assert len(fields) <= n_experts
        meta_t = jnp.zeros(lt.shape, _F32)
        for k, f in enumerate(fields):
            meta_t = jnp.where(sub == k, f, meta_t)
        experts = slice(u * n_experts, (u + 1) * n_experts)
        ml_ref[experts, :] = meta_t
        ms_ref[u * route_rows:(u + 1) * route_rows, :] = jnp.concatenate(
            [meta_t, jnp.zeros((_LANES - n_experts, route_rows), _F32)], axis=0).T
        cnt = jnp.sum(sel_t.astype(_F32), axis=1, keepdims=True)
        cnt_ref[experts, :] = jnp.broadcast_to(cnt, (n_experts, _LANES))


def _mixer_specs(n, d, tm, p_cols, row_fn, weights):
    half = p_cols // 2
    sub = _BF16_SUBLANES
    n_sub = n // sub
    specs = [
        pl.BlockSpec((tm, d), lambda i: (i, 0)),
        pl.BlockSpec((tm, weights["wao"].shape[0]), lambda i: (i, 0)),
        pl.BlockSpec((tm, half), lambda i: (i, 1)),
        pl.BlockSpec((sub, half), lambda i: (jnp.maximum(i * (tm // sub) - 1, 0), 1)),
        pl.BlockSpec((sub, half), lambda i: (jnp.minimum((i + 1) * (tm // sub), n_sub - 1), 1)),
        _resident((1, d)),
        _mod_spec(d, 0, row_fn), _mod_spec(d, 1, row_fn), _mod_spec(d, 2, row_fn),
    ]
    for name in ("wconv", "wao", "wco", "wgate", "bgate", "wout"):
        specs.append(_resident(weights[name].shape))
    return specs


def _merge_ffn(x2d, ya, p, mods, row_fn, seq_len, wts, tm):
    n, d = x2d.shape
    specs = _mixer_specs(n, d, tm, p.shape[1], row_fn, wts)
    specs += [_resident((1, d)), _mod_spec(d, 3, row_fn), _mod_spec(d, 4, row_fn),
              _mod_spec(d, 5, row_fn),
              _resident(wts["ffn_gate"].shape), _resident(wts["ffn_up"].shape),
              _resident(wts["ffn_down"].shape)]
    return pl.pallas_call(
        functools.partial(_merge_ffn_kernel, seq_len=seq_len),
        grid=(n // tm,),
        in_specs=specs,
        out_specs=pl.BlockSpec((tm, d), lambda i: (i, 0)),
        out_shape=jax.ShapeDtypeStruct((n, d), _F32),
        compiler_params=_params(1),
        name="merge_ffn",
    )(x2d, ya, p, p, p, wts["norm_mix"], mods, mods, mods,
      wts["wconv"], wts["wao"], wts["wco"], wts["wgate"], wts["bgate"], wts["wout"],
      wts["norm_ffn"], mods, mods, mods, wts["ffn_gate"], wts["ffn_up"], wts["ffn_down"])


def _merge_router(x2d, ya, p, mods, row_fn, seq_len, wts, n_experts, tm, route_rows):
    n, d = x2d.shape
    assert tm % route_rows == 0
    specs = _mixer_specs(n, d, tm, p.shape[1], row_fn, wts)
    specs += [_resident((1, d)), _mod_spec(d, 3, row_fn), _mod_spec(d, 4, row_fn),
              _resident(wts["router"].shape)]
    tile = lambda w: pl.BlockSpec((tm, w), lambda i: (i, 0))
    meta_rows = n_experts * (tm // route_rows)
    n_route = n // route_rows
    return pl.pallas_call(
        functools.partial(_merge_router_kernel, seq_len=seq_len, n_experts=n_experts,
                          route_rows=route_rows),
        grid=(n // tm,),
        in_specs=specs,
        out_specs=[tile(d), tile(d), tile(_LANES),
                   pl.BlockSpec((meta_rows, route_rows), lambda i: (i, 0)),
                   pl.BlockSpec((meta_rows, _LANES), lambda i: (i, 0))],
        out_shape=[jax.ShapeDtypeStruct((n, d), _F32),
                   jax.ShapeDtypeStruct((n, d), _BF16),
                   jax.ShapeDtypeStruct((n, _LANES), _F32),
                   jax.ShapeDtypeStruct((n_route * n_experts, route_rows), _F32),
                   jax.ShapeDtypeStruct((n_route * n_experts, _LANES), _F32)],
        compiler_params=_params(1),
        name="merge_router",
    )(x2d, ya, p, p, p, wts["norm_mix"], mods, mods, mods,
      wts["wconv"], wts["wao"], wts["wco"], wts["wgate"], wts["bgate"], wts["wout"],
      wts["norm_ffn"], mods, mods, wts["router"])


_SEG_ALIGN = _BF16_SUBLANES
_FFN_ROWS = 256


def _round_up(v, m):
    return (v + m - 1) // m * m


def _local_slots(tm, n_experts):
    return _round_up(_TOP_K * tm + n_experts * (_SEG_ALIGN - 1), _SEG_ALIGN)


def _routing_plan(cnt, n_slots):
    pad = _round_up(cnt, _SEG_ALIGN)
    total = jnp.sum(pad, axis=0)
    region = _round_up(total, _FFN_ROWS)
    ends = jnp.cumsum(region)
    seg_start = (ends - region)[None, :] + jnp.cumsum(pad, axis=0) - pad
    loc_off = jnp.cumsum(pad, axis=1) - pad
    n_used = (ends[-1] // _FFN_ROWS).astype(jnp.int32).reshape(1)
    tails = jnp.concatenate([ends - region + total, region - total, n_used])
    tile_row = jnp.arange(n_slots // _FFN_ROWS, dtype=jnp.int32) * _FFN_ROWS
    tile_expert = jnp.minimum(jnp.sum(tile_row[:, None] >= ends[None, :], axis=1),
                              cnt.shape[1] - 1).astype(jnp.int32)
    ids = jnp.arange(cnt.shape[1], dtype=jnp.int32)
    later = jnp.logical_and(region[None, :] > 0, ids[None, :] > ids[:, None])
    nxt = jnp.min(jnp.where(later, ids[None, :], cnt.shape[1]), axis=1)
    next_expert = jnp.where(nxt == cnt.shape[1], ids, nxt).astype(jnp.int32)
    flat = lambda a: a.reshape(-1).astype(jnp.int32)
    return ((flat(seg_start), flat(pad), flat(loc_off)), flat(tails), tile_expert, n_used,
            next_expert)


def _segment_copies(i, n_experts, seg_ref, pad_ref, loc_ref, hbm_ref, buf_ref, sem, to_hbm):
    copies = []
    for e in range(n_experts):
        n = pl.multiple_of(pad_ref[i * n_experts + e], _SEG_ALIGN)
        loc = pl.multiple_of(loc_ref[i * n_experts + e], _SEG_ALIGN)
        seg = pl.multiple_of(seg_ref[i * n_experts + e], _SEG_ALIGN)
        local, remote = buf_ref.at[pl.ds(loc, n)], hbm_ref.at[pl.ds(seg, n)]
        src, dst = (local, remote) if to_hbm else (remote, local)
        copies.append((n, pltpu.make_async_copy(src, dst, sem)))
    return copies


def _start_copies(copies):
    for n, cp in copies:
        pl.when(n > 0)(cp.start)


def _wait_copies(copies):
    for n, cp in copies:
        pl.when(n > 0)(cp.wait)


def _run_copies(copies):
    _start_copies(copies)
    _wait_copies(copies)


def _expert_offset(expert, loc_ref, i, n_experts):
    off = jnp.zeros(expert.shape, jnp.int32)
    for e in range(n_experts):
        off = jnp.where(expert == e, loc_ref[i * n_experts + e], off)
    return off


def _dispatch_kernel(seg_ref, pad_ref, loc_ref, tail_ref, h_ref, ml_ref, xs_ref, bufs, zeros, sems,
                     *, n_experts):
    i = pl.program_id(0)
    last = pl.num_programs(0) - 1
    cur = lax.rem(i, 2)

    def copies(tile, b):
        return _segment_copies(tile, n_experts, seg_ref, pad_ref, loc_ref, xs_ref, bufs.at[b],
                               sems.at[b], True)

    @pl.when(i >= 2)
    def _():
        _wait_copies(copies(i - 2, cur))

    tm = h_ref.shape[0]
    meta = ml_ref[...]
    slot1 = _expert_offset(meta[0:1].astype(jnp.int32), loc_ref, i, n_experts) + meta[2:3].astype(jnp.int32)
    slot2 = _expert_offset(meta[1:2].astype(jnp.int32), loc_ref, i, n_experts) + meta[3:4].astype(jnp.int32)
    slot = lax.broadcasted_iota(jnp.int32, (bufs.shape[1], tm), 0)
    perm = jnp.where(slot == slot1, 1.0, jnp.where(slot == slot2, 1.0, 0.0)).astype(_BF16)
    bufs[cur] = _bdot(perm, h_ref[...]).astype(bufs.dtype)
    _start_copies(copies(i, cur))

    @pl.when(i == last)
    def _():
        @pl.when(i >= 1)
        def _():
            _wait_copies(copies(i - 1, 1 - cur))

        _wait_copies(copies(i, cur))

        zeros[...] = jnp.zeros_like(zeros)
        fills = []
        for e in range(n_experts):
            start = pl.multiple_of(tail_ref[e], _SEG_ALIGN)
            n = pl.multiple_of(tail_ref[n_experts + e], _SEG_ALIGN)
            fills.append((n, pltpu.make_async_copy(zeros.at[pl.ds(0, n)],
                                                   xs_ref.at[pl.ds(start, n)], sems.at[0])))
        _run_copies(fills)

        def fill_tile(j, carry):
            cp = pltpu.make_async_copy(
                zeros, xs_ref.at[pl.ds(pl.multiple_of(j * _FFN_ROWS, _FFN_ROWS), _FFN_ROWS)],
                sems.at[0])
            cp.start()
            cp.wait()
            return carry

        lax.fori_loop(tail_ref[2 * n_experts], xs_ref.shape[0] // _FFN_ROWS, fill_tile, 0)


def _dispatch(h2, meta_l, plan, tails, n_slots, n_experts, tm):
    n, d = h2.shape
    seg_start, pad, loc_off = plan
    return pl.pallas_call(
        functools.partial(_dispatch_kernel, n_experts=n_experts),
        grid_spec=pltpu.PrefetchScalarGridSpec(
            num_scalar_prefetch=4,
            grid=(n // tm,),
            in_specs=[pl.BlockSpec((tm, d), lambda i, *_: (i, 0)),
                      pl.BlockSpec((n_experts, tm), lambda i, *_: (i, 0))],
            out_specs=pl.BlockSpec(memory_space=pl.ANY),
            scratch_shapes=[pltpu.VMEM((2, _local_slots(tm, n_experts), d), _BF16),
                            pltpu.VMEM((_FFN_ROWS, d), _BF16),
                            pltpu.SemaphoreType.DMA((2,))],
        ),
        out_shape=jax.ShapeDtypeStruct((n_slots, d), _BF16),
        compiler_params=_params(1),
        name="dispatch",
    )(seg_start, pad, loc_off, tails, h2, meta_l)


_W_ROUNDS = 16
_W_ROUNDS_PER_STEP = 2
_ST_CUR, _ST_LOADED, _ST_TARGET, _ST_DONE = range(4)


def _expert_ffn_kernel(te_ref, nu_ref, nxt_ref, x_ref, wg_hbm, wu_hbm, wd_hbm, o_ref,
                       wg_v, wu_v, wd_v, stage_gu, stage_d, sems, state):
    j = pl.program_id(0)
    rg = wg_v.shape[1] // _W_ROUNDS
    rd = wd_v.shape[1] // _W_ROUNDS

    def round_copies(e, r, s):
        g_rows = pl.ds(pl.multiple_of(r * rg, rg), rg)
        d_rows = pl.ds(pl.multiple_of(r * rd, rd), rd)
        return (pltpu.make_async_copy(wg_hbm.at[e, g_rows], stage_gu.at[s, 0], sems.at[s, 0]),
                pltpu.make_async_copy(wu_hbm.at[e, g_rows], stage_gu.at[s, 1], sems.at[s, 1]),
                pltpu.make_async_copy(wd_hbm.at[e, d_rows], stage_d.at[s], sems.at[s, 2]))

    def narrow_round(r, s, copy):
        g_rows = pl.ds(pl.multiple_of(r * rg, rg), rg)
        d_rows = pl.ds(pl.multiple_of(r * rd, rd), rd)
        wg_v[copy, g_rows, :] = stage_gu[s, 0].astype(_BF16)
        wu_v[copy, g_rows, :] = stage_gu[s, 1].astype(_BF16)
        wd_v[copy, d_rows, :] = stage_d[s].astype(_BF16)

    @pl.when(j == 0)
    def _():
        state[_ST_CUR] = 0
        state[_ST_LOADED] = -1
        state[_ST_TARGET] = -1
        state[_ST_DONE] = _W_ROUNDS

    active = j < nu_ref[0]
    e = te_ref[j]

    @pl.when(jnp.logical_and(active, e != state[_ST_LOADED]))
    def _():
        other = 1 - state[_ST_CUR]

        @pl.when(state[_ST_TARGET] != e)
        def _():
            state[_ST_TARGET] = e
            state[_ST_DONE] = 0

        def load_round(r, carry):
            cps = round_copies(e, r, 0)
            for cp in cps:
                cp.start()
            for cp in cps:
                cp.wait()
            narrow_round(r, 0, other)
            return carry

        lax.fori_loop(state[_ST_DONE], _W_ROUNDS, load_round, 0)
        nxt = nxt_ref[e]
        state[_ST_CUR] = other
        state[_ST_LOADED] = e
        state[_ST_TARGET] = nxt
        state[_ST_DONE] = jnp.where(nxt != e, 0, _W_ROUNDS)

    cur = state[_ST_CUR]
    target = state[_ST_TARGET]
    done = state[_ST_DONE]
    prefetch = jnp.logical_and(active, done < _W_ROUNDS)

    @pl.when(prefetch)
    def _():
        for s in range(_W_ROUNDS_PER_STEP):
            for cp in round_copies(target, done + s, s):
                cp.start()

    @pl.when(active)
    def _():
        x = x_ref[...]
        gate = _bdot(x, wg_v[cur])
        act = (gate * _sigmoid(gate)) * _bdot(x, wu_v[cur])
        o_ref[...] = _bdot(act.astype(_BF16), wd_v[cur]).astype(o_ref.dtype)

    @pl.when(jnp.logical_not(active))
    def _():
        o_ref[...] = jnp.zeros_like(o_ref)

    @pl.when(prefetch)
    def _():
        for s in range(_W_ROUNDS_PER_STEP):
            for cp in round_copies(target, done + s, s):
                cp.wait()
            narrow_round(done + s, s, 1 - cur)
        state[_ST_DONE] = done + _W_ROUNDS_PER_STEP


def _expert_ffn(xs, tile_expert, n_used, next_expert, wg, wu, wd):
    n_slots, d = xs.shape
    n_experts, _, f = wg.shape
    assert d % (_W_ROUNDS * _BF16_SUBLANES) == 0 and f % (_W_ROUNDS * _BF16_SUBLANES) == 0
    assert _W_ROUNDS % _W_ROUNDS_PER_STEP == 0
    any_spec = pl.BlockSpec(memory_space=pl.ANY)
    return pl.pallas_call(
        _expert_ffn_kernel,
        grid_spec=pltpu.PrefetchScalarGridSpec(
            num_scalar_prefetch=3,
            grid=(n_slots // _FFN_ROWS,),
            in_specs=[
                pl.BlockSpec((_FFN_ROWS, d),
                             lambda j, te, nu, nx: (jnp.maximum(jnp.minimum(j, nu[0] - 1), 0), 0)),
                any_spec, any_spec, any_spec,
            ],
            out_specs=pl.BlockSpec((_FFN_ROWS, d), lambda j, te, nu, nx: (j, 0)),
            scratch_shapes=[
                pltpu.VMEM((2, d, f), _BF16), pltpu.VMEM((2, d, f), _BF16),
                pltpu.VMEM((2, f, d), _BF16),
                pltpu.VMEM((_W_ROUNDS_PER_STEP, 2, d // _W_ROUNDS, f), _F32),
                pltpu.VMEM((_W_ROUNDS_PER_STEP, f // _W_ROUNDS, d), _F32),
                pltpu.SemaphoreType.DMA((_W_ROUNDS_PER_STEP, 3)),
                pltpu.SMEM((4,), jnp.int32),
            ],
        ),
        out_shape=jax.ShapeDtypeStruct((n_slots, d), _BF16),
        compiler_params=_params(1),
        name="expert_ffn",
    )(tile_expert, n_used, next_expert, xs, wg, wu, wd)


def _combine_kernel(seg_ref, pad_ref, loc_ref, ys_ref, ms_ref, x1_ref, g2_ref, nf_ref, o_ref,
                    bufs, sems, *, n_experts):
    i = pl.program_id(0)
    tm = x1_ref.shape[0]
    cur = lax.rem(i, 2)

    def copies(tile, b):
        return _segment_copies(tile, n_experts, seg_ref, pad_ref, loc_ref, ys_ref, bufs.at[b],
                               sems.at[b], False)

    @pl.when(i == 0)
    def _():
        bufs[...] = jnp.zeros_like(bufs)
        _start_copies(copies(0, 0))

    @pl.when(i + 1 < pl.num_programs(0))
    def _():
        _start_copies(copies(i + 1, 1 - cur))

    _wait_copies(copies(i, cur))
    rc = min(tm, _MIX_CHUNK_ROWS)
    slot = lax.broadcasted_iota(jnp.int32, (rc, bufs.shape[1]), 1)
    for c in range(tm // rc):
        rows = slice(c * rc, (c + 1) * rc)
        meta = ms_ref[rows, :]
        col = lambda k: meta[:, k:k + 1]
        slot1 = _expert_offset(col(0).astype(jnp.int32), loc_ref, i, n_experts) + col(2).astype(jnp.int32)
        slot2 = _expert_offset(col(1).astype(jnp.int32), loc_ref, i, n_experts) + col(3).astype(jnp.int32)
        weights = jnp.where(slot == slot1, col(4), jnp.where(slot == slot2, col(5), 0.0))
        y = _bdot(weights.astype(_BF16), bufs[cur])
        o_ref[rows, :] = _rmsnorm(x1_ref[rows, :] + g2_ref[...] * y, nf_ref[...])


def _combine(ys, meta_s, x1, plan, mods, row_fn, norm_final, n_experts, tm):
    n, d = x1.shape
    seg_start, pad, loc_off = plan
    return pl.pallas_call(
        functools.partial(_combine_kernel, n_experts=n_experts),
        grid_spec=pltpu.PrefetchScalarGridSpec(
            num_scalar_prefetch=3,
            grid=(n // tm,),
            in_specs=[pl.BlockSpec(memory_space=pl.ANY),
                      pl.BlockSpec((tm, _LANES), lambda i, *_: (i, 0)),
                      pl.BlockSpec((tm, d), lambda i, *_: (i, 0)),
                      _mod_spec(d, 5, row_fn),
                      _resident((1, d))],
            out_specs=pl.BlockSpec((tm, d), lambda i, *_: (i, 0)),
            scratch_shapes=[pltpu.VMEM((2, _local_slots(tm, n_experts), d), _BF16),
                            pltpu.SemaphoreType.DMA((2,))],
        ),
        out_shape=jax.ShapeDtypeStruct((n, d), _F32),
        compiler_params=_params(1),
        name="combine",
    )(seg_start, pad, loc_off, ys, meta_s, x1, mods, norm_final)


def kernel(x, c, ctx, c_ctx, w_ada, b_ada, norm_mix, norm_ffn, w_in, w_conv, rpb, w_attn_out,
           w_conv_out, w_gate, b_gate, w_out, w_ffn_gate, w_ffn_up, w_ffn_down, w_router,
           w_exp_gate, w_exp_up, w_exp_down, norm_final):
    batch, seq, d = x.shape
    ctx_len = ctx.shape[1]
    depth = w_ada.shape[0]
    attn_w = _N_HEADS * _HEAD_DIM
    n_experts = w_router.shape[-1]
    rows = seq // _GRID_W
    n_pairs = attn_w // _LANES
    assert depth == 2 and seq % (_Q_ROWS * _GRID_W) == 0 and attn_w % _LANES == 0
    assert n_experts == _F32_SUBLANES

    ctx_row = batch
    cond_rows = -(-(batch + 1) // 8) * 8
    cond = jnp.zeros((cond_rows, d), _F32).at[:batch].set(c).at[ctx_row].set(c_ctx)
    mods_all = _ada(cond, w_ada, b_ada, tn=1536).reshape(depth, cond_rows, 1, 6 * d)

    union, starts, variant_of, patterns, col_starts = _attn_plan(rows)

    x2d = x.reshape(batch * seq, d)
    c2d = ctx.reshape(batch * ctx_len, d)
    tm_lat = 512
    ctx_row_fn = lambda i: ctx_row

    def lat_rows(tm):
        return lambda i: (i * tm) // seq

    out = None
    for l in range(depth):
        last = l == depth - 1
        mods = mods_all[l]
        wts = {
            "norm_mix": norm_mix[l].reshape(1, d),
            "norm_ffn": norm_ffn[l].reshape(1, d),
            "wconv": w_conv[l],
            "wao": w_attn_out[l].astype(_BF16),
            "wco": w_conv_out[l].astype(_BF16),
            "wgate": w_gate[l].astype(_BF16),
            "bgate": b_gate[l].reshape(1, -1),
            "wout": w_out[l].astype(_BF16),
        }
        w_in_l = w_in[l].astype(_BF16)
        p = _proj(x2d, mods, lat_rows(1024), wts["norm_mix"], w_in_l, tm=1024)
        if last:
            pc = _proj(c2d, mods, ctx_row_fn, wts["norm_mix"], w_in_l[:, attn_w:3 * attn_w], tm=512)
            kc_col = 0
        else:
            pc = _proj(c2d, mods, ctx_row_fn, wts["norm_mix"], w_in_l, tm=512)
            kc_col = n_pairs
        table = _bias_table(rpb[l], union, patterns, col_starts)
        ya, yca = _attention(p, pc, kc_col, table, starts, variant_of, union, col_starts, batch,
                             seq, ctx_len, with_ctx=not last)
        if not last:
            wts["ffn_gate"] = w_ffn_gate[l // 2].astype(_BF16)
            wts["ffn_up"] = w_ffn_up[l // 2].astype(_BF16)
            wts["ffn_down"] = w_ffn_down[l // 2].astype(_BF16)
            x2d = _merge_ffn(x2d, ya, p, mods, lat_rows(tm_lat), seq, wts, tm=tm_lat)
            c2d = _merge_ffn(c2d, yca, pc, mods, ctx_row_fn, ctx_len, wts, tm=ctx_len)
        else:
            wr_t = w_router[l // 2].T
            wr_hi = wr_t.astype(_BF16)
            wr_lo = (wr_t - wr_hi.astype(_F32)).astype(_BF16)
            wts["router"] = jnp.concatenate([wr_hi, wr_lo], axis=0)
            tm_moe = 512
            n_tok = batch * seq
            n_tiles = n_tok // tm_moe
            tm_mix = 2 * tm_moe
            x1, h2, meta_s, meta_l, cnt = _merge_router(
                x2d, ya, p, mods, lat_rows(tm_mix), seq, wts, n_experts, tm=tm_mix,
                route_rows=tm_moe)
            n_slots = _round_up(_TOP_K * n_tok + n_tiles * n_experts * (_SEG_ALIGN - 1)
                                + n_experts * (_FFN_ROWS - 1), _FFN_ROWS)
            cnt = cnt[:, 0].reshape(n_tiles, n_experts).astype(jnp.int32)
            plan, tails, tile_expert, n_used, next_expert = _routing_plan(cnt, n_slots)
            xs = _dispatch(h2, meta_l, plan, tails, n_slots, n_experts, tm=tm_moe)
            ys = _expert_ffn(xs, tile_expert, n_used, next_expert, w_exp_gate[l // 2],
                             w_exp_up[l // 2], w_exp_down[l // 2])
            out = _combine(ys, meta_s, x1, plan, mods, lat_rows(tm_moe), norm_final.reshape(1, d),
                           n_experts, tm=tm_moe)
    return out.reshape(batch, seq, d)
```

```python
import functools

import numpy as np
import jax
import jax.numpy as jnp
from jax import lax
from jax.experimental import pallas as pl
from jax.experimental.pallas import tpu as pltpu

_F32 = jnp.float32
_BF16 = jnp.bfloat16

_GRID_W = 64
_N_HEADS = 8
_HEAD_DIM = 64
_NA_ROWS = 8
_NA_COLS = 16
_CONV_K = 3
_TOP_K = 2
_EPS = 1e-6
_NEG_INF = -1e30

_LANES = 128
_F32_SUBLANES = 8
_BF16_SUBLANES = 16
_VMEM_LIMIT_BYTES = 56 * 1024 * 1024

_Q_ROWS = 8
_Q_COLS = 16
_K_COLS = 32
_MIX_CHUNK_ROWS = 256
_HEADS_PER_BLOCK = _LANES // _HEAD_DIM
_NT = (((1,), (1,)), ((), ()))
_TN = (((0,), (0,)), ((), ()))


def _params(n_axes):
    return pltpu.CompilerParams(
        dimension_semantics=("arbitrary",) * n_axes,
        vmem_limit_bytes=_VMEM_LIMIT_BYTES)


def _resident(shape):
    nd = len(shape)
    return pl.BlockSpec(shape, lambda *_: (0,) * nd, pipeline_mode=pl.Buffered(1))


def _sigmoid(x):
    return 1.0 / (1.0 + jnp.exp(-x))


def _rmsnorm(x, g):
    ms = jnp.mean(x * x, axis=-1, keepdims=True)
    return (x * lax.rsqrt(ms + _EPS)) * g


def _modulate(x, g, shift, scale):
    return _rmsnorm(x, g) * (1.0 + scale) + shift


def _bdot(a, b):
    return jnp.dot(a, b, preferred_element_type=_F32)


def _indicator(mask):
    return jnp.where(mask, 1.0, 0.0).astype(_BF16)


def _ada_kernel(cond_ref, w_ref, b_ref, o_ref):
    cond = cond_ref[...]
    act = cond * _sigmoid(cond)
    rows = act.shape[0]
    a_hi = act.astype(_BF16)
    a_lo = (act - a_hi.astype(_F32)).astype(_BF16)
    r = _bdot(jnp.concatenate([a_hi, a_lo], axis=0), w_ref[...].astype(_BF16))
    o_ref[...] = r[:rows] + r[rows:] + b_ref[...]


def _ada(cond, w_ada, b_ada, tn):
    depth, d, n_out = w_ada.shape
    rows = cond.shape[0]
    return pl.pallas_call(
        _ada_kernel,
        grid=(depth, n_out // tn),
        in_specs=[
            pl.BlockSpec((rows, d), lambda l, j: (0, 0)),
            pl.BlockSpec((None, d, tn), lambda l, j: (l, 0, j)),
            pl.BlockSpec((None, 1, tn), lambda l, j: (l, 0, j)),
        ],
        out_specs=pl.BlockSpec((None, rows, tn), lambda l, j: (l, 0, j)),
        out_shape=jax.ShapeDtypeStruct((depth, rows, n_out), _F32),
        compiler_params=_params(2),
        name="ada",
    )(cond, w_ada, b_ada.reshape(depth, 1, n_out))


def _mod_spec(d, chunk, row_fn):
    return pl.BlockSpec((None, 1, d), lambda i, *_: (row_fn(i), 0, chunk))


def _proj_kernel(x_ref, g_ref, sh_ref, sc_ref, w_ref, o_ref):
    h = _modulate(x_ref[...], g_ref[...], sh_ref[...], sc_ref[...])
    o_ref[...] = _bdot(h.astype(_BF16), w_ref[...]).astype(o_ref.dtype)


def _proj(x2d, mods, row_fn, g, w, tm):
    n, d = x2d.shape
    n_out = w.shape[1]
    return pl.pallas_call(
        _proj_kernel,
        grid=(n // tm,),
        in_specs=[
            pl.BlockSpec((tm, d), lambda i: (i, 0)),
            _resident((1, d)),
            _mod_spec(d, 0, row_fn),
            _mod_spec(d, 1, row_fn),
            _resident(w.shape),
        ],
        out_specs=pl.BlockSpec((tm, n_out), lambda i: (i, 0)),
        out_shape=jax.ShapeDtypeStruct((n, n_out), _BF16),
        compiler_params=_params(1),
        name="proj",
    )(x2d, g, mods, mods, w)


def _bias_table_kernel(rpb_ref, o_ref, *, union, patterns, col_starts):
    sub = _F32_SUBLANES
    span = _NA_COLS - 1
    slots = _LANES // _K_COLS
    rp = rpb_ref[...]
    lane = lax.broadcasted_iota(jnp.int32, (sub, _LANES), 1)
    srow = lax.broadcasted_iota(jnp.int32, (sub, _LANES), 0)
    ck_local = jnp.bitwise_and(lane, _K_COLS - 1)
    slot_of_lane = jnp.right_shift(lane, _K_COLS.bit_length() - 1)
    neg = jnp.full((sub, _LANES), _NEG_INF, _F32)
    skewed, pieces = {}, {}

    def piece(dr, shift):
        if dr not in skewed:
            row = jnp.broadcast_to(rp[dr + _NA_ROWS - 1:dr + _NA_ROWS, :], (sub, _LANES))
            skewed[dr] = pltpu.roll(row, 0, 1, stride=1, stride_axis=0)
        if (dr, shift) not in pieces:
            pieces[dr, shift] = pltpu.roll(skewed[dr], shift, 1) if shift else skewed[dr]
        return pieces[dr, shift]

    for g, c0 in enumerate(col_starts):
        for gg in range(_Q_COLS // sub):
            k0 = c0 - _Q_COLS * g - sub * gg + span
            cq = srow + (_Q_COLS * g + sub * gg)
            cs = jnp.clip(cq - _NA_COLS // 2, 0, _GRID_W - _NA_COLS)
            ck = ck_local + c0
            in_win = jnp.logical_and(ck >= cs, ck < cs + _NA_COLS)
            tiles = {}

            def make_tile(drs):
                tile = neg
                for jq, dr in enumerate(drs):
                    if dr is not None:
                        here = jnp.logical_and(in_win, slot_of_lane == jq)
                        tile = jnp.where(here, piece(dr, (_K_COLS * jq - k0) % _LANES), tile)
                return tile

            for v, pat in enumerate(patterns):
                for a in range(_Q_ROWS):
                    rows = slice(a * _Q_COLS + sub * gg, a * _Q_COLS + sub * (gg + 1))
                    for quad in range(-(-union // slots)):
                        js = [quad * slots + jq for jq in range(slots)]
                        drs = tuple(pat[a * union + j] if j < union else None for j in js)
                        width = min(slots, union - quad * slots) * _K_COLS
                        if drs not in tiles:
                            tiles[drs] = make_tile(drs)
                        o_ref[v, g, rows, quad * _LANES:quad * _LANES + width] = (
                            tiles[drs][:, :width])


def _bias_table(rpb_l, union, patterns, col_starts):
    h, n_dr, n_dc = rpb_l.shape
    assert n_dc <= _LANES and _LANES % _K_COLS == 0
    n_dr_pad = _round_up(n_dr, _F32_SUBLANES)
    padded = jnp.pad(rpb_l, ((0, 0), (0, n_dr_pad - n_dr), (0, _LANES - n_dc)))
    n_var, n_grp = len(patterns), len(col_starts)
    qt, kt = _Q_ROWS * _Q_COLS, union * _K_COLS
    return pl.pallas_call(
        functools.partial(_bias_table_kernel, union=union, patterns=patterns,
                          col_starts=col_starts),
        grid=(h,),
        in_specs=[pl.BlockSpec((None, n_dr_pad, _LANES), lambda i: (i, 0, 0))],
        out_specs=pl.BlockSpec((n_var, n_grp, None, qt, kt), lambda i: (0, 0, i, 0, 0)),
        out_shape=jax.ShapeDtypeStruct((n_var, n_grp, h, qt, kt), _F32),
        compiler_params=_params(1),
        name="bias_table",
    )(padded)


def _attn_plan(rows):
    kh = min(_NA_ROWS, rows)
    union = min(_Q_ROWS + kh - 1, rows)
    n_blocks = rows // _Q_ROWS
    starts, variant_of, patterns = [], [], []
    for t in range(n_blocks):
        start = int(np.clip(_Q_ROWS * t - kh // 2, 0, rows - union))
        pat = []
        for a in range(_Q_ROWS):
            r = _Q_ROWS * t + a
            rs = int(np.clip(r - kh // 2, 0, rows - kh))
            for j in range(union):
                kr = start + j
                pat.append(kr - r if rs <= kr < rs + kh else None)
        pat = tuple(pat)
        if pat not in patterns:
            patterns.append(pat)
        starts.append(start)
        variant_of.append(patterns.index(pat))
    col_starts = []
    for g in range(_GRID_W // _Q_COLS):
        c0 = int(np.clip(_Q_COLS * g - _NA_COLS // 2, 0, _GRID_W - _K_COLS))
        for cq in range(_Q_COLS * g, _Q_COLS * (g + 1)):
            cs = int(np.clip(cq - _NA_COLS // 2, 0, _GRID_W - _NA_COLS))
            assert c0 <= cs and cs + _NA_COLS <= c0 + _K_COLS
        col_starts.append(c0)
    return union, starts, variant_of, tuple(patterns), tuple(col_starts)


def _first_head_lanes(shape):
    return lax.broadcasted_iota(jnp.int32, shape, len(shape) - 1) < _HEAD_DIM


def _values_with_ones(v):
    lo = _first_head_lanes(v.shape)
    ones = jnp.ones_like(v)
    return jnp.where(lo, v, ones), jnp.where(lo, ones, v)


def _scores(qb, keys, bias_fns):
    lo = _first_head_lanes(qb.shape)
    out = []
    for hh in range(_HEADS_PER_BLOCK):
        qm = jnp.where(lo if hh == 0 else jnp.logical_not(lo), qb, jnp.zeros_like(qb))
        scores = []
        for kk, bias_fn in zip(keys, bias_fns):
            s = lax.dot_general(qm, kk, _NT, preferred_element_type=_F32)
            scores.append(s if bias_fn is None else s + bias_fn(hh))
        out.append(scores)
    return out


def _probabilities(scores):
    m = functools.reduce(jnp.maximum, [jnp.max(s, axis=-1, keepdims=True) for s in scores])
    return [jnp.exp(s - m).astype(_BF16) for s in scores]


def _weighted_values(probs, values):
    res = []
    for hh in range(_HEADS_PER_BLOCK):
        acc = None
        for e, v1 in zip(probs[hh], values):
            o = _bdot(e, v1[hh])
            acc = o if acc is None else acc + o
        res.append(acc)
    lo = _first_head_lanes(res[0].shape)
    num = jnp.where(lo, res[0], res[1])
    den = pltpu.roll(jnp.where(lo, res[1], res[0]), _HEAD_DIM, 1)
    return num / den


def _attn_kernel(start_ref, var_ref, q_ref, k_ref, v_ref, kc_ref, vc_ref, tab_ref, *rest,
                 n_row_blocks, union, col_starts, with_ctx):
    if with_ctx:
        qc_ref, o_ref, oc_ref, s_loc, s_ctx, e_loc, e_ctx = rest
    else:
        o_ref, s_loc, s_ctx, e_loc, e_ctx = rest
    heads = range(_HEADS_PER_BLOCK)
    n_groups = len(col_starts)
    assert n_groups % 2 == 0 and n_groups >= 2 and n_row_blocks >= 1
    scale = _HEAD_DIM ** -0.5
    kc = kc_ref[...]
    vc1 = _values_with_ones(vc_ref[...])

    def gather(ref, first_row, n_rows, col, width):
        return jnp.concatenate(
            [ref[pl.ds(first_row + r * _GRID_W + col, width), :] for r in range(n_rows)], axis=0)

    def key_rows(t):
        return pl.multiple_of(start_ref[t] * _GRID_W, _GRID_W)

    def query_rows(t):
        return pl.multiple_of(t * (_Q_ROWS * _GRID_W), _Q_ROWS * _GRID_W)

    def scores_stage(t, g):
        slot, var = g % 2, var_ref[t]
        qb = gather(q_ref, query_rows(t), _Q_ROWS, _Q_COLS * g, _Q_COLS) * scale
        kb = gather(k_ref, key_rows(t), union, col_starts[g], _K_COLS)
        scores = _scores(qb, [kb, kc], [lambda hh: tab_ref[var, g, hh], None])
        for hh in heads:
            s_loc[slot, hh], s_ctx[slot, hh] = scores[hh]

    def softmax_stage(g):
        slot = g % 2
        for hh in heads:
            e_loc[slot, hh], e_ctx[slot, hh] = _probabilities([s_loc[slot, hh], s_ctx[slot, hh]])

    def values_stage(t, g):
        slot = g % 2
        vb1 = _values_with_ones(gather(v_ref, key_rows(t), union, col_starts[g], _K_COLS))
        probs = [[e_loc[slot, hh], e_ctx[slot, hh]] for hh in heads]
        o = _weighted_values(probs, [vb1, vc1]).astype(o_ref.dtype)
        for a in range(_Q_ROWS):
            o_ref[pl.ds(query_rows(t) + a * _GRID_W + _Q_COLS * g, _Q_COLS), :] = (
                o[a * _Q_COLS:(a + 1) * _Q_COLS])

    def issue(t, k):
        values_stage(t + k // n_groups, k % n_groups)
        softmax_stage((k + 1) % n_groups)
        scores_stage(t + (k + 2) // n_groups, (k + 2) % n_groups)

    scores_stage(0, 0)
    softmax_stage(0)
    scores_stage(0, 1)

    def body(t, carry):
        for k in range(n_groups):
            issue(t, k)
        return carry

    last = n_row_blocks - 1
    lax.fori_loop(0, last, body, 0)
    for k in range(n_groups - 2):
        issue(last, k)
    values_stage(last, n_groups - 2)
    softmax_stage(n_groups - 1)
    values_stage(last, n_groups - 1)

    if with_ctx:
        scores = _scores(qc_ref[...] * scale, [kc], [None])
        probs = [_probabilities(scores[hh]) for hh in heads]
        oc_ref[...] = _weighted_values(probs, [vc1]).astype(oc_ref.dtype)


def _attention(p, pc, kc_col, table, starts, variant_of, union, col_starts, batch, seq, ctx_len,
               with_ctx):
    attn_w = _N_HEADS * _HEAD_DIM
    n_pairs = attn_w // _LANES
    n_var, n_grp = table.shape[:2]
    qb_tokens = _Q_ROWS * _Q_COLS
    union_tokens = union * _K_COLS
    kern = functools.partial(_attn_kernel, n_row_blocks=len(starts), union=union,
                             col_starts=col_starts, with_ctx=with_ctx)
    in_specs = [
        pl.BlockSpec((seq, _LANES), lambda hp, b, *_: (b, hp)),
        pl.BlockSpec((seq, _LANES), lambda hp, b, *_: (b, n_pairs + hp)),
        pl.BlockSpec((seq, _LANES), lambda hp, b, *_: (b, 2 * n_pairs + hp)),
        pl.BlockSpec((ctx_len, _LANES), lambda hp, b, *_: (b, kc_col + hp)),
        pl.BlockSpec((ctx_len, _LANES), lambda hp, b, *_: (b, kc_col + n_pairs + hp)),
        pl.BlockSpec((n_var, n_grp, _HEADS_PER_BLOCK, qb_tokens, union_tokens),
                     lambda hp, b, *_: (0, 0, hp, 0, 0)),
    ]
    args = [p, p, p, pc, pc, table]
    out_specs = [pl.BlockSpec((seq, _LANES), lambda hp, b, *_: (b, hp))]
    out_shape = [jax.ShapeDtypeStruct((batch * seq, attn_w), _BF16)]
    if with_ctx:
        in_specs.append(pl.BlockSpec((ctx_len, _LANES), lambda hp, b, *_: (b, hp)))
        args.append(pc)
        out_specs.append(pl.BlockSpec((ctx_len, _LANES), lambda hp, b, *_: (b, hp)))
        out_shape.append(jax.ShapeDtypeStruct((batch * ctx_len, attn_w), _BF16))
    outs = pl.pallas_call(
        kern,
        grid_spec=pltpu.PrefetchScalarGridSpec(
            num_scalar_prefetch=2,
            grid=(n_pairs, batch),
            in_specs=in_specs,
            out_specs=out_specs,
            scratch_shapes=[
                pltpu.VMEM((2, _HEADS_PER_BLOCK, qb_tokens, union_tokens), _F32),
                pltpu.VMEM((2, _HEADS_PER_BLOCK, qb_tokens, ctx_len), _F32),
                pltpu.VMEM((2, _HEADS_PER_BLOCK, qb_tokens, union_tokens), _BF16),
                pltpu.VMEM((2, _HEADS_PER_BLOCK, qb_tokens, ctx_len), _BF16),
            ],
        ),
        out_shape=out_shape,
        compiler_params=_params(2),
        name="attn",
    )(jnp.asarray(starts, jnp.int32), jnp.asarray(variant_of, jnp.int32), *args)
    return outs if with_ctx else (outs[0], None)


def _mixer_update(i, x_ref, ya_ref, bcu_ref, prev_ref, next_ref, nm_ref, sh1_ref, sc1_ref, g1_ref,
                  wconv_ref, wao_ref, wco_ref, wgate_ref, bgate_ref, wout_ref, seq_len):
    tm, d = x_ref.shape
    cw = wco_ref.shape[0]

    bcu = bcu_ref[...].astype(_F32)
    bg, z = bcu[:, :cw], bcu[:, cw:2 * cw] * bcu[:, 2 * cw:]
    prev = prev_ref[...].astype(_F32)
    nxt = next_ref[...].astype(_F32)
    has_prev = ((i * tm) % seq_len != 0).astype(_F32)
    has_next = (((i + 1) * tm) % seq_len != 0).astype(_F32)
    last = prev.shape[0] - 1
    z_prev = prev[last:last + 1, cw:2 * cw] * prev[last:last + 1, 2 * cw:] * has_prev
    z_next = nxt[0:1, cw:2 * cw] * nxt[0:1, 2 * cw:] * has_next
    row = lax.broadcasted_iota(jnp.int32, z.shape, 0)
    z_dn = jnp.where(row == 0, z_prev, pltpu.roll(z, 1, 0))
    z_up = jnp.where(row == tm - 1, z_next, pltpu.roll(z, tm - 1, 0))
    wconv = wconv_ref[...]
    y_conv = (bg * (z_dn * wconv[0:1] + z * wconv[1:2] + z_up * wconv[2:3])).astype(_BF16)

    chunks = []
    rc = min(tm, _MIX_CHUNK_ROWS)
    for c in range(tm // rc):
        rows = slice(c * rc, (c + 1) * rc)
        x = x_ref[rows, :]
        hb = _modulate(x, nm_ref[...], sh1_ref[...], sc1_ref[...]).astype(_BF16)
        gates = _sigmoid(_bdot(hb, wgate_ref[...]) + bgate_ref[...])
        a_proj = _bdot(ya_ref[rows, :], wao_ref[...])
        c_proj = _bdot(y_conv[rows, :], wco_ref[...])
        merged = gates[:, :d] * a_proj + gates[:, d:] * c_proj
        chunks.append(x + g1_ref[...] * _bdot(merged.astype(_BF16), wout_ref[...]))
    return jnp.concatenate(chunks, axis=0)


_N_MIX_REFS = 15


def _merge_ffn_kernel(*refs, seq_len):
    mix = refs[:_N_MIX_REFS]
    nf_ref, sh2_ref, sc2_ref, g2_ref, wg_ref, wu_ref, wd_ref, o_ref = refs[_N_MIX_REFS:]
    x1 = _mixer_update(pl.program_id(0), *mix, seq_len=seq_len)
    h2 = _modulate(x1, nf_ref[...], sh2_ref[...], sc2_ref[...]).astype(_BF16)
    gate = _bdot(h2, wg_ref[...])
    act = (gate * _sigmoid(gate)) * _bdot(h2, wu_ref[...])
    o_ref[...] = x1 + g2_ref[...] * _bdot(act.astype(_BF16), wd_ref[...])


def _top2(vals, idx, n_idx, axis):
    v1 = jnp.max(vals, axis=axis, keepdims=True)
    i1 = jnp.min(jnp.where(vals == v1, idx, n_idx), axis=axis, keepdims=True)
    rest = jnp.where(idx == i1, -jnp.inf, vals)
    v2 = jnp.max(rest, axis=axis, keepdims=True)
    i2 = jnp.min(jnp.where(rest == v2, idx, n_idx), axis=axis, keepdims=True)
    return i1, i2, v1, v2


def _pick(idx, sel, vals, axis):
    return jnp.sum(jnp.where(idx == sel, vals, 0.0), axis=axis, keepdims=True)


def _merge_router_kernel(*refs, seq_len, n_experts, route_rows):
    mix = refs[:_N_MIX_REFS]
    (nf_ref, sh2_ref, sc2_ref, wr_ref, x1_ref, h2_ref, ml_ref, cnt_ref) = refs[_N_MIX_REFS:]
    x1 = _mixer_update(pl.program_id(0), *mix, seq_len=seq_len)
    x1_ref[...] = x1
    h2 = _modulate(x1, nf_ref[...], sh2_ref[...], sc2_ref[...])
    h_hi = h2.astype(_BF16)
    h2_ref[...] = h_hi
    tm = h2.shape[0]
    h_lo = (h2 - h_hi.astype(_F32)).astype(_BF16)
    wr = wr_ref[...]
    by_hi = lax.dot_general(wr, h_hi, _NT, preferred_element_type=_F32)
    by_lo = lax.dot_general(wr, h_lo, _NT, preferred_element_type=_F32)
    lt_all = by_hi[:n_experts] + by_hi[n_experts:] + by_lo[:n_experts]
    row = lax.broadcasted_iota(jnp.int32, (route_rows, route_rows), 0)
    col = lax.broadcasted_iota(jnp.int32, (route_rows, route_rows), 1)
    earlier_cols = _indicator(row < col)
    sub = lax.broadcasted_iota(jnp.int32, (n_experts, route_rows), 0)

    for u in range(tm // route_rows):
        lt = lt_all[:, u * route_rows:(u + 1) * route_rows]
        j1, j2, v1, v2 = _top2(lt, sub, n_experts, 0)
        e2 = jnp.exp(v2 - v1)
        den = 1.0 + e2
        sel_t = _indicator(jnp.logical_or(sub == j1, sub == j2))
        ranks_t = _bdot(sel_t, earlier_cols)
        fields = (j1.astype(_F32), j2.astype(_F32), _pick(sub, j1, ranks_t, 0),
                  _pick(sub, j2, ranks_t, 0), 1.0 / den, e2 / den)
        assert len(fields) <= n_experts
        meta_t = jnp.zeros(lt.shape, _F32)
        for k, f in enumerate(fields):
            meta_t = jnp.where(sub == k, f, meta_t)
        experts = slice(u * n_experts, (u + 1) * n_experts)
        ml_ref[experts, :] = meta_t
        cnt =jnp.sum(sel_t.astype(_F32), axis=1, keepdims=True)
        cnt_ref[experts, :] = jnp.broadcast_to(cnt, (n_experts, _LANES))


def _mixer_specs(n, d, tm, p_cols, row_fn, weights):
    half = p_cols // 2
    sub = _BF16_SUBLANES
    n_sub = n // sub
    specs = [
        pl.BlockSpec((tm, d), lambda i: (i, 0)),
        pl.BlockSpec((tm, weights["wao"].shape[0]), lambda i: (i, 0)),
        pl.BlockSpec((tm, half), lambda i: (i, 1)),
        pl.BlockSpec((sub, half), lambda i: (jnp.maximum(i * (tm // sub) - 1, 0), 1)),
        pl.BlockSpec((sub, half), lambda i: (jnp.minimum((i + 1) * (tm // sub), n_sub - 1), 1)),
        _resident((1, d)),
        _mod_spec(d, 0, row_fn), _mod_spec(d, 1, row_fn), _mod_spec(d, 2, row_fn),
    ]
    for name in ("wconv", "wao", "wco", "wgate", "bgate", "wout"):
        specs.append(_resident(weights[name].shape))
    return specs


def _merge_ffn(x2d, ya, p, mods, row_fn, seq_len, wts, tm):
    n, d = x2d.shape
    specs = _mixer_specs(n, d, tm, p.shape[1], row_fn, wts)
    specs += [_resident((1, d)), _mod_spec(d, 3, row_fn), _mod_spec(d, 4, row_fn),
              _mod_spec(d, 5, row_fn),
              _resident(wts["ffn_gate"].shape), _resident(wts["ffn_up"].shape),
              _resident(wts["ffn_down"].shape)]
    return pl.pallas_call(
        functools.partial(_merge_ffn_kernel, seq_len=seq_len),
        grid=(n // tm,),
        in_specs=specs,
        out_specs=pl.BlockSpec((tm, d), lambda i: (i, 0)),
        out_shape=jax.ShapeDtypeStruct((n, d), _F32),
        compiler_params=_params(1),
        name="merge_ffn",
    )(x2d, ya, p, p, p, wts["norm_mix"], mods, mods, mods,
      wts["wconv"], wts["wao"], wts["wco"], wts["wgate"], wts["bgate"], wts["wout"],
      wts["norm_ffn"], mods, mods, mods, wts["ffn_gate"], wts["ffn_up"], wts["ffn_down"])


def _merge_router(x2d, ya, p, mods, row_fn, seq_len, wts, n_experts, tm, route_rows):
    n, d = x2d.shape
    assert tm % route_rows == 0
    specs = _mixer_specs(n, d, tm, p.shape[1], row_fn, wts)
    specs += [_resident((1, d)), _mod_spec(d, 3, row_fn), _mod_spec(d, 4, row_fn),
              _resident(wts["router"].shape)]
    tile = lambda w: pl.BlockSpec((tm, w), lambda i: (i, 0))
    meta_rows = n_experts * (tm // route_rows)
    n_route = n // route_rows
    return pl.pallas_call(
        functools.partial(_merge_router_kernel, seq_len=seq_len, n_experts=n_experts,
                          route_rows=route_rows),
        grid=(n // tm,),
        in_specs=specs,
        out_specs=[tile(d), tile(d),
                   pl.BlockSpec((meta_rows, route_rows), lambda i: (i, 0)),
                   pl.BlockSpec((meta_rows, _LANES), lambda i: (i, 0))],
        out_shape=[jax.ShapeDtypeStruct((n, d), _F32),
                   jax.ShapeDtypeStruct((n, d), _BF16),
                   jax.ShapeDtypeStruct((n_route * n_experts, route_rows), _F32),
                   jax.ShapeDtypeStruct((n_route * n_experts, _LANES), _F32)],
        compiler_params=_params(1),
        name="merge_router",
    )(x2d, ya, p, p, p, wts["norm_mix"], mods, mods, mods,
      wts["wconv"], wts["wao"], wts["wco"], wts["wgate"], wts["bgate"], wts["wout"],
      wts["norm_ffn"], mods, mods, wts["router"])


_SEG_ALIGN = _BF16_SUBLANES
_FFN_ROWS = 256


def _round_up(v, m):
    return (v + m - 1) // m * m


def _local_slots(tm, n_experts):
    return _round_up(_TOP_K * tm + n_experts * (_SEG_ALIGN - 1), _SEG_ALIGN)


def _routing_plan(cnt, n_slots):
    pad = _round_up(cnt, _SEG_ALIGN)
    total = jnp.sum(pad, axis=0)
    region = _round_up(total, _FFN_ROWS)
    ends = jnp.cumsum(region)
    seg_start = (ends - region)[None, :] + jnp.cumsum(pad, axis=0) - pad
    loc_off = jnp.cumsum(pad, axis=1) - pad
    n_used = (ends[-1] // _FFN_ROWS).astype(jnp.int32).reshape(1)
    tails = jnp.concatenate([ends - region + total, region - total, n_used])
    tile_row = jnp.arange(n_slots // _FFN_ROWS, dtype=jnp.int32) * _FFN_ROWS
    tile_expert = jnp.minimum(jnp.sum(tile_row[:, None] >= ends[None, :], axis=1),
                              cnt.shape[1] - 1).astype(jnp.int32)
    ids = jnp.arange(cnt.shape[1], dtype=jnp.int32)
    later = jnp.logical_and(region[None, :] > 0, ids[None, :] > ids[:, None])
    nxt = jnp.min(jnp.where(later, ids[None, :], cnt.shape[1]), axis=1)
    next_expert = jnp.where(nxt == cnt.shape[1], ids, nxt).astype(jnp.int32)
    flat = lambda a: a.reshape(-1).astype(jnp.int32)
    return ((flat(seg_start), flat(pad), flat(loc_off)), flat(tails), tile_expert, n_used,
            next_expert)


def _segment_copies(i, n_experts, seg_ref, pad_ref, loc_ref, hbm_ref, buf_ref, sem, to_hbm):
    copies = []
    for e in range(n_experts):
        n = pl.multiple_of(pad_ref[i * n_experts + e], _SEG_ALIGN)
        loc = pl.multiple_of(loc_ref[i * n_experts + e], _SEG_ALIGN)
        seg = pl.multiple_of(seg_ref[i * n_experts + e], _SEG_ALIGN)
        local, remote = buf_ref.at[pl.ds(loc, n)], hbm_ref.at[pl.ds(seg, n)]
        src, dst = (local, remote) if to_hbm else (remote, local)
        copies.append((n, pltpu.make_async_copy(src, dst, sem)))
    return copies


def _start_copies(copies):
    for n, cp in copies:
        pl.when(n > 0)(cp.start)


def _wait_copies(copies):
    for n, cp in copies:
        pl.when(n > 0)(cp.wait)


def _run_copies(copies):
    _start_copies(copies)
    _wait_copies(copies)


def _expert_offset(expert, loc_ref, i, n_experts):
    off = jnp.zeros(expert.shape, jnp.int32)
    for e in range(n_experts):
        off = jnp.where(expert == e, loc_ref[i * n_experts + e], off)
    return off


def _dispatch_kernel(seg_ref, pad_ref, loc_ref, tail_ref, h_ref, ml_ref, xs_ref, bufs, zeros, sems,
                     *, n_experts):
    i = pl.program_id(0)
    last = pl.num_programs(0) - 1
    cur = lax.rem(i, 2)

    def copies(tile, b):
        return _segment_copies(tile, n_experts, seg_ref, pad_ref, loc_ref, xs_ref, bufs.at[b],
                               sems.at[b], True)

    @pl.when(i >= 2)
    def _():
        _wait_copies(copies(i - 2, cur))

    tm = h_ref.shape[0]
    meta = ml_ref[...]
    slot1 = _expert_offset(meta[0:1].astype(jnp.int32), loc_ref, i, n_experts) + meta[2:3].astype(jnp.int32)
    slot2 = _expert_offset(meta[1:2].astype(jnp.int32), loc_ref, i, n_experts) + meta[3:4].astype(jnp.int32)
    slot = lax.broadcasted_iota(jnp.int32, (bufs.shape[1], tm), 0)
    perm = jnp.where(slot == slot1, 1.0, jnp.where(slot == slot2, 1.0, 0.0)).astype(_BF16)
    bufs[cur] = _bdot(perm, h_ref[...]).astype(bufs.dtype)
    _start_copies(copies(i, cur))

    @pl.when(i == last)
    def _():
        @pl.when(i >= 1)
        def _():
            _wait_copies(copies(i - 1, 1 - cur))

        _wait_copies(copies(i, cur))

        zeros[...] = jnp.zeros_like(zeros)
        fills = []
        for e in range(n_experts):
            start = pl.multiple_of(tail_ref[e], _SEG_ALIGN)
            n = pl.multiple_of(tail_ref[n_experts + e], _SEG_ALIGN)
            fills.append((n, pltpu.make_async_copy(zeros.at[pl.ds(0, n)],
                                                   xs_ref.at[pl.ds(start, n)], sems.at[0])))
        _run_copies(fills)

        def fill_copy(j):
            return pltpu.make_async_copy(
                zeros, xs_ref.at[pl.ds(pl.multiple_of(j * _FFN_ROWS, _FFN_ROWS), _FFN_ROWS)],
                sems.at[0])

        def start_fill(j, carry):
            fill_copy(j).start()
            return carry

        def wait_fill(j, carry):
            fill_copy(j).wait()
            return carry

        unused = (tail_ref[2 * n_experts], xs_ref.shape[0] // _FFN_ROWS)
        lax.fori_loop(*unused, start_fill, 0)
        lax.fori_loop(*unused, wait_fill, 0)


def _dispatch(h2, meta_l, plan, tails, n_slots, n_experts, tm):
    n, d = h2.shape
    seg_start, pad, loc_off = plan
    return pl.pallas_call(
        functools.partial(_dispatch_kernel, n_experts=n_experts),
        grid_spec=pltpu.PrefetchScalarGridSpec(
            num_scalar_prefetch=4,
            grid=(n // tm,),
            in_specs=[pl.BlockSpec((tm, d), lambda i, *_: (i, 0)),
                      pl.BlockSpec((n_experts, tm), lambda i, *_: (i, 0))],
            out_specs=pl.BlockSpec(memory_space=pl.ANY),
            scratch_shapes=[pltpu.VMEM((2, _local_slots(tm, n_experts), d), _BF16),
                            pltpu.VMEM((_FFN_ROWS, d), _BF16),
                            pltpu.SemaphoreType.DMA((2,))],
        ),
        out_shape=jax.ShapeDtypeStruct((n_slots, d), _BF16),
        compiler_params=_params(1),
        name="dispatch",
    )(seg_start, pad, loc_off, tails, h2, meta_l)


_W_ROUNDS = 16
_W_ROUNDS_PER_STEP = 2
_ST_CUR, _ST_LOADED, _ST_TARGET, _ST_DONE = range(4)


def _expert_ffn_kernel(te_ref, nu_ref, nxt_ref, x_ref, wg_hbm, wu_hbm, wd_hbm, o_ref,
                       wg_v, wu_v, wd_v, stage_gu, stage_d, sems, state):
    j = pl.program_id(0)
    rg = wg_v.shape[1] // _W_ROUNDS
    rd = wd_v.shape[1] // _W_ROUNDS

    def round_copies(e, r, s):
        g_rows = pl.ds(pl.multiple_of(r * rg, rg), rg)
        d_rows = pl.ds(pl.multiple_of(r * rd, rd), rd)
        return (pltpu.make_async_copy(wg_hbm.at[e, g_rows], stage_gu.at[s, 0], sems.at[s, 0]),
                pltpu.make_async_copy(wu_hbm.at[e, g_rows], stage_gu.at[s, 1], sems.at[s, 1]),
                pltpu.make_async_copy(wd_hbm.at[e, d_rows], stage_d.at[s], sems.at[s, 2]))

    def narrow_round(r, s, copy):
        g_rows = pl.ds(pl.multiple_of(r * rg, rg), rg)
        d_rows = pl.ds(pl.multiple_of(r * rd, rd), rd)
        wg_v[copy, g_rows, :] = stage_gu[s, 0].astype(_BF16)
        wu_v[copy, g_rows, :] = stage_gu[s, 1].astype(_BF16)
        wd_v[copy, d_rows, :] = stage_d[s].astype(_BF16)

    @pl.when(j == 0)
    def _():
        state[_ST_CUR] = 0
        state[_ST_LOADED] = -1
        state[_ST_TARGET] = -1
        state[_ST_DONE] = _W_ROUNDS

    active = j < nu_ref[0]
    e = te_ref[j]

    @pl.when(jnp.logical_and(active, e != state[_ST_LOADED]))
    def _():
        other = 1 - state[_ST_CUR]

        @pl.when(state[_ST_TARGET] != e)
        def _():
            state[_ST_TARGET] = e
            state[_ST_DONE] = 0

        first = state[_ST_DONE]

        @pl.when(first < _W_ROUNDS)
        def _():
            for cp in round_copies(e, first, 0):
                cp.start()

        def load_round(r, carry):
            s = lax.rem(r - first, 2)

            @pl.when(r + 1 < _W_ROUNDS)
            def _():
                for cp in round_copies(e, r + 1, 1 - s):
                    cp.start()

            for cp in round_copies(e, r, s):
                cp.wait()
            narrow_round(r, s, other)
            return carry

        lax.fori_loop(first, _W_ROUNDS, load_round, 0)
        nxt = nxt_ref[e]
        state[_ST_CUR] = other
        state[_ST_LOADED] = e
        state[_ST_TARGET] = nxt
        state[_ST_DONE] = jnp.where(nxt != e, 0, _W_ROUNDS)

    cur = state[_ST_CUR]
    target = state[_ST_TARGET]
    done = state[_ST_DONE]
    prefetch = jnp.logical_and(active, done < _W_ROUNDS)

    @pl.when(prefetch)
    def _():
        for s in range(_W_ROUNDS_PER_STEP):
            for cp in round_copies(target, done + s, s):
                cp.start()

    @pl.when(active)
    def _():
        x = x_ref[...]
        gate = _bdot(x, wg_v[cur])
        act = (gate * _sigmoid(gate)) * _bdot(x, wu_v[cur])
        o_ref[...] = _bdot(act.astype(_BF16), wd_v[cur]).astype(o_ref.dtype)

    @pl.when(jnp.logical_not(active))
    def _():
        o_ref[...] = jnp.zeros_like(o_ref)

    @pl.when(prefetch)
    def _():
        for s in range(_W_ROUNDS_PER_STEP):
            for cp in round_copies(target, done + s, s):
                cp.wait()
            narrow_round(done + s, s, 1 - cur)
        state[_ST_DONE] = done + _W_ROUNDS_PER_STEP


def _expert_ffn(xs, tile_expert, n_used, next_expert, wg, wu, wd):
    n_slots, d = xs.shape
    n_experts, _, f = wg.shape
    assert d % (_W_ROUNDS * _BF16_SUBLANES) == 0 and f % (_W_ROUNDS * _BF16_SUBLANES) == 0
    assert _W_ROUNDS % _W_ROUNDS_PER_STEP == 0 and _W_ROUNDS_PER_STEP >= 2
    any_spec = pl.BlockSpec(memory_space=pl.ANY)
    return pl.pallas_call(
        _expert_ffn_kernel,
        grid_spec=pltpu.PrefetchScalarGridSpec(
            num_scalar_prefetch=3,
            grid=(n_slots // _FFN_ROWS,),
            in_specs=[
                pl.BlockSpec((_FFN_ROWS, d),
                             lambda j, te, nu, nx: (jnp.maximum(jnp.minimum(j, nu[0] - 1), 0), 0)),
                any_spec, any_spec, any_spec,
            ],
            out_specs=pl.BlockSpec((_FFN_ROWS, d), lambda j, te, nu, nx: (j, 0)),
            scratch_shapes=[
                pltpu.VMEM((2, d, f), _BF16), pltpu.VMEM((2, d, f), _BF16),
                pltpu.VMEM((2, f, d), _BF16),
                pltpu.VMEM((_W_ROUNDS_PER_STEP, 2, d // _W_ROUNDS, f), _F32),
                pltpu.VMEM((_W_ROUNDS_PER_STEP, f // _W_ROUNDS, d), _F32),
                pltpu.SemaphoreType.DMA((_W_ROUNDS_PER_STEP, 3)),
                pltpu.SMEM((4,), jnp.int32),
            ],
        ),
        out_shape=jax.ShapeDtypeStruct((n_slots, d), _BF16),
        compiler_params=_params(1),
        name="expert_ffn",
    )(tile_expert, n_used, next_expert, xs, wg, wu, wd)


def _combine_kernel(seg_ref, pad_ref, loc_ref, ys_ref, ml_ref, x1_ref, g2_ref, nf_ref, o_ref,
                    bufs, sems, *, n_experts):
    i = pl.program_id(0)
    tm = x1_ref.shape[0]
    cur = lax.rem(i, 2)

    def copies(tile, b):
        return _segment_copies(tile, n_experts, seg_ref, pad_ref, loc_ref, ys_ref, bufs.at[b],
                               sems.at[b], False)

    @pl.when(i == 0)
    def _():
        bufs[...] = jnp.zeros_like(bufs)
        _start_copies(copies(0, 0))

    @pl.when(i + 1 < pl.num_programs(0))
    def _():
        _start_copies(copies(i + 1, 1 - cur))

    _wait_copies(copies(i, cur))
    meta = ml_ref[...]
    slot1 = _expert_offset(meta[0:1].astype(jnp.int32), loc_ref, i, n_experts) + meta[2:3].astype(jnp.int32)
    slot2 = _expert_offset(meta[1:2].astype(jnp.int32), loc_ref, i, n_experts) + meta[3:4].astype(jnp.int32)
    fields = [slot1.astype(_F32), slot2.astype(_F32), meta[4:5], meta[5:6]]
    per_token = jnp.concatenate(
        fields + [jnp.zeros((_LANES - len(fields), tm), _F32)], axis=0).T
    rc = min(tm, _MIX_CHUNK_ROWS)
    slot = lax.broadcasted_iota(jnp.int32, (rc, bufs.shape[1]), 1)
    for c in range(tm // rc):
        rows = slice(c * rc, (c + 1) * rc)
        col = lambda k: per_token[rows, k:k + 1]
        weights = jnp.where(slot == col(0).astype(jnp.int32), col(2),
                            jnp.where(slot == col(1).astype(jnp.int32), col(3), 0.0))
        y = _bdot(weights.astype(_BF16), bufs[cur])
        o_ref[rows, :] = _rmsnorm(x1_ref[rows, :] + g2_ref[...] * y, nf_ref[...])


def _combine(ys, meta_l, x1, plan, mods, row_fn, norm_final, n_experts, tm):
    n, d = x1.shape
    seg_start, pad, loc_off = plan
    return pl.pallas_call(
        functools.partial(_combine_kernel, n_experts=n_experts),
        grid_spec=pltpu.PrefetchScalarGridSpec(
            num_scalar_prefetch=3,
            grid=(n // tm,),
            in_specs=[pl.BlockSpec(memory_space=pl.ANY),
                      pl.BlockSpec((n_experts, tm), lambda i, *_: (i, 0)),
                      pl.BlockSpec((tm, d), lambda i, *_: (i, 0)),
                      _mod_spec(d, 5, row_fn),
                      _resident((1, d))],
            out_specs=pl.BlockSpec((tm, d), lambda i, *_: (i, 0)),
            scratch_shapes=[pltpu.VMEM((2, _local_slots(tm, n_experts), d), _BF16),
                            pltpu.SemaphoreType.DMA((2,))],
        ),
        out_shape=jax.ShapeDtypeStruct((n, d), _F32),
        compiler_params=_params(1),
        name="combine",
    )(seg_start, pad, loc_off, ys, meta_l, x1, mods, norm_final)


def kernel(x, c, ctx, c_ctx, w_ada, b_ada, norm_mix, norm_ffn, w_in, w_conv, rpb, w_attn_out,
           w_conv_out, w_gate, b_gate, w_out, w_ffn_gate, w_ffn_up, w_ffn_down, w_router,
           w_exp_gate, w_exp_up, w_exp_down, norm_final):
    batch, seq, d = x.shape
    ctx_len = ctx.shape[1]
    depth = w_ada.shape[0]
    attn_w = _N_HEADS * _HEAD_DIM
    n_experts = w_router.shape[-1]
    rows = seq // _GRID_W
    n_pairs = attn_w // _LANES
    assert depth == 2 and seq % (_Q_ROWS * _GRID_W) == 0 and attn_w % _LANES == 0
    assert n_experts == _F32_SUBLANES

    ctx_row = batch
    cond_rows = -(-(batch + 1) // 8) * 8
    cond = jnp.zeros((cond_rows, d), _F32).at[:batch].set(c).at[ctx_row].set(c_ctx)
    mods_all = _ada(cond, w_ada, b_ada, tn=1536).reshape(depth, cond_rows, 1, 6 * d)

    union, starts, variant_of, patterns, col_starts = _attn_plan(rows)

    x2d = x.reshape(batch * seq, d)
    c2d = ctx.reshape(batch * ctx_len, d)
    tm_lat = 512
    ctx_row_fn = lambda i: ctx_row

    def lat_rows(tm):
        return lambda i: (i * tm) // seq

    out = None
    for l in range(depth):
        last = l == depth - 1
        mods = mods_all[l]
        wts = {
            "norm_mix": norm_mix[l].reshape(1, d),
            "norm_ffn": norm_ffn[l].reshape(1, d),
            "wconv": w_conv[l],
            "wao": w_attn_out[l].astype(_BF16),
            "wco": w_conv_out[l].astype(_BF16),
            "wgate": w_gate[l].astype(_BF16),
            "bgate": b_gate[l].reshape(1, -1),
            "wout": w_out[l].astype(_BF16),
        }
        w_in_l = w_in[l].astype(_BF16)
        p = _proj(x2d, mods, lat_rows(1024), wts["norm_mix"], w_in_l, tm=1024)
        if last:
            pc = _proj(c2d, mods, ctx_row_fn, wts["norm_mix"], w_in_l[:, attn_w:3 * attn_w], tm=512)
            kc_col = 0
        else:
            pc = _proj(c2d, mods, ctx_row_fn, wts["norm_mix"], w_in_l, tm=512)
            kc_col = n_pairs
        table = _bias_table(rpb[l], union, patterns, col_starts)
        ya, yca = _attention(p, pc, kc_col, table, starts, variant_of, union, col_starts, batch,
                             seq, ctx_len, with_ctx=not last)
        if not last:
            wts["ffn_gate"] = w_ffn_gate[l // 2].astype(_BF16)
            wts["ffn_up"] = w_ffn_up[l // 2].astype(_BF16)
            wts["ffn_down"] = w_ffn_down[l // 2].astype(_BF16)
            x2d = _merge_ffn(x2d, ya, p, mods, lat_rows(tm_lat), seq, wts, tm=tm_lat)
            c2d = _merge_ffn(c2d, yca, pc, mods, ctx_row_fn, ctx_len, wts, tm=ctx_len)
        else:
            wr_t = w_router[l // 2].T
            wr_hi = wr_t.astype(_BF16)
            wr_lo = (wr_t - wr_hi.astype(_F32)).astype(_BF16)
            wts["router"] = jnp.concatenate([wr_hi, wr_lo], axis=0)
            tm_moe = 512
            n_tok = batch * seq
            n_tiles = n_tok // tm_moe
            tm_mix = 2 * tm_moe
            x1, h2, meta_l, cnt = _merge_router(
                x2d, ya, p, mods, lat_rows(tm_mix), seq, wts, n_experts, tm=tm_mix,
                route_rows=tm_moe)
            n_slots = _round_up(_TOP_K * n_tok + n_tiles * n_experts * (_SEG_ALIGN - 1)
                                + n_experts * (_FFN_ROWS - 1), _FFN_ROWS)
            cnt = cnt[:, 0].reshape(n_tiles, n_experts).astype(jnp.int32)
            plan, tails, tile_expert, n_used, next_expert = _routing_plan(cnt, n_slots)
            xs = _dispatch(h2, meta_l, plan, tails, n_slots, n_experts, tm=tm_moe)
            ys = _expert_ffn(xs, tile_expert, n_used, next_expert, w_exp_gate[l // 2],
                             w_exp_up[l // 2], w_exp_down[l // 2])
            out = _combine(ys, meta_l, x1, plan, mods, lat_rows(tm_moe), norm_final.reshape(1, d),
                           n_experts, tm=tm_moe)
    return out.reshape(batch, seq, d)
```

```python
import functools

import numpy as np
import jax
import jax.numpy as jnp
from jax import lax
from jax.experimental import pallas as pl
from jax.experimental.pallas import tpu as pltpu

_F32 = jnp.float32
_BF16 = jnp.bfloat16

_GRID_W = 64
_N_HEADS = 8
_HEAD_DIM = 64
_NA_ROWS = 8
_NA_COLS = 16
_TOP_K = 2
_EPS = 1e-6
_NEG_INF = -1e30

_LANES = 128
_F32_SUBLANES = 8
_BF16_SUBLANES = 16
_VMEM_LIMIT_BYTES = 56 * 1024 * 1024

_Q_ROWS = 8
_Q_COLS = 16
_K_COLS = 32
_MIX_CHUNK_ROWS = 256
_HEADS_PER_BLOCK = _LANES // _HEAD_DIM
_NT = (((1,), (1,)), ((), ()))


def _params(n_axes):
    return pltpu.CompilerParams(
        dimension_semantics=("arbitrary",) * n_axes,
        vmem_limit_bytes=_VMEM_LIMIT_BYTES)


def _resident(shape):
    nd = len(shape)
    return pl.BlockSpec(shape, lambda *_: (0,) * nd, pipeline_mode=pl.Buffered(1))


def _sigmoid(x):
    return 1.0 / (1.0 + jnp.exp(-x))


def _rmsnorm(x, g):
    ms = jnp.mean(x * x, axis=-1, keepdims=True)
    return (x * lax.rsqrt(ms + _EPS)) * g


def _modulate(x, g, shift, scale):
    return _rmsnorm(x, g) * (1.0 + scale) + shift


def _bdot(a, b):
    return jnp.dot(a, b, preferred_element_type=_F32)


def _indicator(mask):
    return jnp.where(mask, 1.0, 0.0).astype(_BF16)


def _ada_kernel(cond_ref, w_ref, b_ref, o_ref):
    cond = cond_ref[...]
    act = cond * _sigmoid(cond)
    rows = act.shape[0]
    a_hi = act.astype(_BF16)
    a_lo = (act - a_hi.astype(_F32)).astype(_BF16)
    r = _bdot(jnp.concatenate([a_hi, a_lo], axis=0), w_ref[...].astype(_BF16))
    o_ref[...] = r[:rows] + r[rows:] + b_ref[...]


def _ada(cond, w_ada, b_ada, tn):
    depth, d, n_out = w_ada.shape
    rows = cond.shape[0]
    return pl.pallas_call(
        _ada_kernel,
        grid=(depth, n_out // tn),
        in_specs=[
            pl.BlockSpec((rows, d), lambda l, j: (0, 0)),
            pl.BlockSpec((None, d, tn), lambda l, j: (l, 0, j)),
            pl.BlockSpec((None, 1, tn), lambda l, j: (l, 0, j)),
        ],
        out_specs=pl.BlockSpec((None, rows, tn), lambda l, j: (l, 0, j)),
        out_shape=jax.ShapeDtypeStruct((depth, rows, n_out), _F32),
        compiler_params=_params(2),
        name="ada",
    )(cond, w_ada, b_ada.reshape(depth, 1, n_out))


def _mod_spec(d, chunk, row_fn):
    return pl.BlockSpec((None, 1, d), lambda i, *_: (row_fn(i), 0, chunk))


def _proj_kernel(x_ref, g_ref, sh_ref, sc_ref, w_ref, o_ref):
    h = _modulate(x_ref[...], g_ref[...], sh_ref[...], sc_ref[...])
    o_ref[...] = _bdot(h.astype(_BF16), w_ref[...]).astype(o_ref.dtype)


def _proj(x2d, mods, row_fn, g, w, tm):
    n, d = x2d.shape
    n_out = w.shape[1]
    return pl.pallas_call(
        _proj_kernel,
        grid=(n // tm,),
        in_specs=[
            pl.BlockSpec((tm, d), lambda i: (i, 0)),
            _resident((1, d)),
            _mod_spec(d, 0, row_fn),
            _mod_spec(d, 1, row_fn),
            _resident(w.shape),
        ],
        out_specs=pl.BlockSpec((tm, n_out), lambda i: (i, 0)),
        out_shape=jax.ShapeDtypeStruct((n, n_out), _BF16),
        compiler_params=_params(1),
        name="proj",
    )(x2d, g, mods, mods, w)


def _bias_table_kernel(rpb_ref, o_ref, *, union, patterns, col_starts):
    sub = _F32_SUBLANES
    span = _NA_COLS - 1
    slots = _LANES // _K_COLS
    rp = rpb_ref[...]
    lane = lax.broadcasted_iota(jnp.int32, (sub, _LANES), 1)
    srow = lax.broadcasted_iota(jnp.int32, (sub, _LANES), 0)
    ck_local = jnp.bitwise_and(lane, _K_COLS - 1)
    slot_of_lane = jnp.right_shift(lane, _K_COLS.bit_length() - 1)
    neg = jnp.full((sub, _LANES), _NEG_INF, _F32)
    skewed, pieces = {}, {}

    def piece(dr, shift):
        if dr not in skewed:
            row = jnp.broadcast_to(rp[dr + _NA_ROWS - 1:dr + _NA_ROWS, :], (sub, _LANES))
            skewed[dr] = pltpu.roll(row, 0, 1, stride=1, stride_axis=0)
        if (dr, shift) not in pieces:
            pieces[dr, shift] = pltpu.roll(skewed[dr], shift, 1) if shift else skewed[dr]
        return pieces[dr, shift]

    for g, c0 in enumerate(col_starts):
        for gg in range(_Q_COLS // sub):
            k0 = c0 - _Q_COLS * g - sub * gg + span
            cq = srow + (_Q_COLS * g + sub * gg)
            cs = jnp.clip(cq - _NA_COLS // 2, 0, _GRID_W - _NA_COLS)
            ck = ck_local + c0
            in_win = jnp.logical_and(ck >= cs, ck < cs + _NA_COLS)
            tiles = {}

            def make_tile(drs):
                tile = neg
                for jq, dr in enumerate(drs):
                    if dr is not None:
                        here = jnp.logical_and(in_win, slot_of_lane == jq)
                        tile = jnp.where(here, piece(dr, (_K_COLS * jq - k0) % _LANES), tile)
                return tile

            for v, pat in enumerate(patterns):
                for a in range(_Q_ROWS):
                    rows = slice(a * _Q_COLS + sub * gg, a * _Q_COLS + sub * (gg + 1))
                    for quad in range(-(-union // slots)):
                        js = [quad * slots + jq for jq in range(slots)]
                        drs = tuple(pat[a * union + j] if j < union else None for j in js)
                        width = min(slots, union - quad * slots) * _K_COLS
                        if drs not in tiles:
                            tiles[drs] = make_tile(drs)
                        o_ref[v, g, rows, quad * _LANES:quad * _LANES + width] = (
                            tiles[drs][:, :width])


def _bias_table(rpb_l, union, patterns, col_starts):
    h, n_dr, n_dc = rpb_l.shape
    assert n_dc <= _LANES and _LANES % _K_COLS == 0
    n_dr_pad = _round_up(n_dr, _F32_SUBLANES)
    padded = jnp.pad(rpb_l, ((0, 0), (0, n_dr_pad - n_dr), (0, _LANES - n_dc)))
    n_var, n_grp = len(patterns), len(col_starts)
    qt, kt = _Q_ROWS * _Q_COLS, union * _K_COLS
    return pl.pallas_call(
        functools.partial(_bias_table_kernel, union=union, patterns=patterns,
                          col_starts=col_starts),
        grid=(h,),
        in_specs=[pl.BlockSpec((None, n_dr_pad, _LANES), lambda i: (i, 0, 0))],
        out_specs=pl.BlockSpec((n_var, n_grp, None, qt, kt), lambda i: (0, 0, i, 0, 0)),
        out_shape=jax.ShapeDtypeStruct((n_var, n_grp, h, qt, kt), _F32),
        compiler_params=_params(1),
        name="bias_table",
    )(padded)


def _attn_plan(rows):
    kh = min(_NA_ROWS, rows)
    union = min(_Q_ROWS + kh - 1, rows)
    n_blocks = rows // _Q_ROWS
    starts, variant_of, patterns = [], [], []
    for t in range(n_blocks):
        start = int(np.clip(_Q_ROWS * t - kh // 2, 0, rows - union))
        pat = []
        for a in range(_Q_ROWS):
            r = _Q_ROWS * t + a
            rs = int(np.clip(r - kh // 2, 0, rows - kh))
            for j in range(union):
                kr = start + j
                pat.append(kr - r if rs <= kr < rs + kh else None)
        pat = tuple(pat)
        if pat not in patterns:
            patterns.append(pat)
        starts.append(start)
        variant_of.append(patterns.index(pat))
    col_starts = []
    for g in range(_GRID_W // _Q_COLS):
        c0 = int(np.clip(_Q_COLS * g - _NA_COLS // 2, 0, _GRID_W - _K_COLS))
        for cq in range(_Q_COLS * g, _Q_COLS * (g + 1)):
            cs = int(np.clip(cq - _NA_COLS // 2, 0, _GRID_W - _NA_COLS))
            assert c0 <= cs and cs + _NA_COLS <= c0 + _K_COLS
        col_starts.append(c0)
    return union, starts, variant_of, tuple(patterns), tuple(col_starts)


def _first_head_lanes(shape):
    return lax.broadcasted_iota(jnp.int32, shape, len(shape) - 1) < _HEAD_DIM


def _values_with_ones(v):
    lo = _first_head_lanes(v.shape)
    ones = jnp.ones_like(v)
    return jnp.where(lo, v, ones), jnp.where(lo, ones, v)


def _scores(qb, keys, bias_fns):
    lo = _first_head_lanes(qb.shape)
    out = []
    for hh in range(_HEADS_PER_BLOCK):
        qm = jnp.where(lo if hh == 0 else jnp.logical_not(lo), qb, jnp.zeros_like(qb))
        scores = []
        for kk, bias_fn in zip(keys, bias_fns):
            s = lax.dot_general(qm, kk, _NT, preferred_element_type=_F32)
            scores.append(s if bias_fn is None else s + bias_fn(hh))
        out.append(scores)
    return out


def _probabilities(scores):
    m = functools.reduce(jnp.maximum, [jnp.max(s, axis=-1, keepdims=True) for s in scores])
    return [jnp.exp(s - m).astype(_BF16) for s in scores]


def _weighted_values(probs, values):
    res = []
    for hh in range(_HEADS_PER_BLOCK):
        acc = None
        for e, v1 in zip(probs[hh], values):
            o = _bdot(e, v1[hh])
            acc = o if acc is None else acc + o
        res.append(acc)
    lo = _first_head_lanes(res[0].shape)
    num = jnp.where(lo, res[0], res[1])
    den = pltpu.roll(jnp.where(lo, res[1], res[0]), _HEAD_DIM, 1)
    return num / den


def _attn_kernel(start_ref, var_ref, q_ref, k_ref, v_ref, kc_ref, vc_ref, tab_ref, *rest,
                 n_row_blocks, union, col_starts, with_ctx):
    if with_ctx:
        qc_ref, o_ref, oc_ref, s_loc, s_ctx, e_loc, e_ctx = rest
    else:
        o_ref, s_loc, s_ctx, e_loc, e_ctx = rest
    heads = range(_HEADS_PER_BLOCK)
    n_groups = len(col_starts)
    assert n_groups % 2 == 0 and n_groups >= 2 and n_row_blocks >= 1
    scale = _HEAD_DIM ** -0.5
    kc = kc_ref[...]
    vc1 = _values_with_ones(vc_ref[...])

    def gather(ref, first_row, n_rows, col, width):
        return jnp.concatenate(
            [ref[pl.ds(first_row + r * _GRID_W + col, width), :] for r in range(n_rows)], axis=0)

    def key_rows(t):
        return pl.multiple_of(start_ref[t] * _GRID_W, _GRID_W)

    def query_rows(t):
        return pl.multiple_of(t * (_Q_ROWS * _GRID_W), _Q_ROWS * _GRID_W)

    def scores_stage(t, g):
        slot, var = g % 2, var_ref[t]
        qb = gather(q_ref, query_rows(t), _Q_ROWS, _Q_COLS * g, _Q_COLS) * scale
        kb = gather(k_ref, key_rows(t), union, col_starts[g], _K_COLS)
        scores = _scores(qb, [kb, kc], [lambda hh: tab_ref[var, g, hh], None])
        for hh in heads:
            s_loc[slot, hh], s_ctx[slot, hh] = scores[hh]

    def softmax_stage(g):
        slot = g % 2
        for hh in heads:
            e_loc[slot, hh], e_ctx[slot, hh] = _probabilities([s_loc[slot, hh], s_ctx[slot, hh]])

    def values_stage(t, g):
        slot = g % 2
        vb1 = _values_with_ones(gather(v_ref, key_rows(t), union, col_starts[g], _K_COLS))
        probs = [[e_loc[slot, hh], e_ctx[slot, hh]] for hh in heads]
        o = _weighted_values(probs, [vb1, vc1]).astype(o_ref.dtype)
        for a in range(_Q_ROWS):
            o_ref[pl.ds(query_rows(t) + a * _GRID_W + _Q_COLS * g, _Q_COLS), :] = (
                o[a * _Q_COLS:(a + 1) * _Q_COLS])

    def issue(t, k):
        values_stage(t + k // n_groups, k % n_groups)
        softmax_stage((k + 1) % n_groups)
        scores_stage(t + (k + 2) // n_groups, (k + 2) % n_groups)

    scores_stage(0, 0)
    softmax_stage(0)
    scores_stage(0, 1)

    def body(t, carry):
        for k in range(n_groups):
            issue(t, k)
        return carry

    last = n_row_blocks - 1
    lax.fori_loop(0, last, body, 0)
    for k in range(n_groups - 2):
        issue(last, k)
    values_stage(last, n_groups - 2)
    softmax_stage(n_groups - 1)
    values_stage(last, n_groups - 1)

    if with_ctx:
        scores = _scores(qc_ref[...] * scale, [kc], [None])
        probs = [_probabilities(scores[hh]) for hh in heads]
        oc_ref[...] = _weighted_values(probs, [vc1]).astype(oc_ref.dtype)


def _attention(p, pc, kc_col, table, starts, variant_of, union, col_starts, batch, seq, ctx_len,
               with_ctx):
    attn_w = _N_HEADS * _HEAD_DIM
    n_pairs = attn_w // _LANES
    n_var, n_grp = table.shape[:2]
    qb_tokens = _Q_ROWS * _Q_COLS
    union_tokens = union * _K_COLS
    kern = functools.partial(_attn_kernel, n_row_blocks=len(starts), union=union,
                             col_starts=col_starts, with_ctx=with_ctx)
    in_specs = [
        pl.BlockSpec((seq, _LANES), lambda hp, b, *_: (b, hp)),
        pl.BlockSpec((seq, _LANES), lambda hp, b, *_: (b, n_pairs + hp)),
        pl.BlockSpec((seq, _LANES), lambda hp, b, *_: (b, 2 * n_pairs + hp)),
        pl.BlockSpec((ctx_len, _LANES), lambda hp, b, *_: (b, kc_col + hp)),
        pl.BlockSpec((ctx_len, _LANES), lambda hp, b, *_: (b, kc_col + n_pairs + hp)),
        pl.BlockSpec((n_var, n_grp, _HEADS_PER_BLOCK, qb_tokens, union_tokens),
                     lambda hp, b, *_: (0, 0, hp, 0, 0)),
    ]
    args = [p, p, p, pc, pc, table]
    out_specs = [pl.BlockSpec((seq, _LANES), lambda hp, b, *_: (b, hp))]
    out_shape = [jax.ShapeDtypeStruct((batch * seq, attn_w), _BF16)]
    if with_ctx:
        in_specs.append(pl.BlockSpec((ctx_len, _LANES), lambda hp, b, *_: (b, hp)))
        args.append(pc)
        out_specs.append(pl.BlockSpec((ctx_len, _LANES), lambda hp, b, *_: (b, hp)))
        out_shape.append(jax.ShapeDtypeStruct((batch * ctx_len, attn_w), _BF16))
    outs = pl.pallas_call(
        kern,
        grid_spec=pltpu.PrefetchScalarGridSpec(
            num_scalar_prefetch=2,
            grid=(n_pairs, batch),
            in_specs=in_specs,
            out_specs=out_specs,
            scratch_shapes=[
                pltpu.VMEM((2, _HEADS_PER_BLOCK, qb_tokens, union_tokens), _F32),
                pltpu.VMEM((2, _HEADS_PER_BLOCK, qb_tokens, ctx_len), _F32),
                pltpu.VMEM((2, _HEADS_PER_BLOCK, qb_tokens, union_tokens), _BF16),
                pltpu.VMEM((2, _HEADS_PER_BLOCK, qb_tokens, ctx_len), _BF16),
            ],
        ),
        out_shape=out_shape,
        compiler_params=_params(2),
        name="attn",
    )(jnp.asarray(starts, jnp.int32), jnp.asarray(variant_of, jnp.int32), *args)
    return outs if with_ctx else (outs[0], None)


def _mixer_update(i, x_ref, ya_ref, bcu_ref, prev_ref, next_ref, nm_ref, sh1_ref, sc1_ref, g1_ref,
                  wconv_ref, wao_ref, wco_ref, wgate_ref, bgate_ref, wout_ref, seq_len):
    tm, d = x_ref.shape
    cw = wco_ref.shape[0]

    bcu = bcu_ref[...].astype(_F32)
    bg, z = bcu[:, :cw], bcu[:, cw:2 * cw] * bcu[:, 2 * cw:]
    prev = prev_ref[...].astype(_F32)
    nxt = next_ref[...].astype(_F32)
    has_prev = ((i * tm) % seq_len != 0).astype(_F32)
    has_next = (((i + 1) * tm) % seq_len != 0).astype(_F32)
    last = prev.shape[0] - 1
    z_prev = prev[last:last + 1, cw:2 * cw] * prev[last:last + 1, 2 * cw:] * has_prev
    z_next = nxt[0:1, cw:2 * cw] * nxt[0:1, 2 * cw:] * has_next
    row = lax.broadcasted_iota(jnp.int32, z.shape, 0)
    z_dn = jnp.where(row == 0, z_prev, pltpu.roll(z, 1, 0))
    z_up = jnp.where(row == tm - 1, z_next, pltpu.roll(z, tm - 1, 0))
    wconv = wconv_ref[...]
    y_conv = (bg * (z_dn * wconv[0:1] + z * wconv[1:2] + z_up * wconv[2:3])).astype(_BF16)

    chunks = []
    rc = min(tm, _MIX_CHUNK_ROWS)
    for c in range(tm // rc):
        rows = slice(c * rc, (c + 1) * rc)
        x = x_ref[rows, :]
        hb = _modulate(x, nm_ref[...], sh1_ref[...], sc1_ref[...]).astype(_BF16)
        gates = _sigmoid(_bdot(hb, wgate_ref[...]) + bgate_ref[...])
        a_proj = _bdot(ya_ref[rows, :], wao_ref[...])
        c_proj = _bdot(y_conv[rows, :], wco_ref[...])
        merged = gates[:, :d] * a_proj + gates[:, d:] * c_proj
        chunks.append(x + g1_ref[...] * _bdot(merged.astype(_BF16), wout_ref[...]))
    return jnp.concatenate(chunks, axis=0)


_N_MIX_REFS = 15


def _merge_ffn_kernel(*refs, seq_len):
    mix = refs[:_N_MIX_REFS]
    nf_ref, sh2_ref, sc2_ref, g2_ref, wg_ref, wu_ref, wd_ref, o_ref = refs[_N_MIX_REFS:]
    x1 = _mixer_update(pl.program_id(0), *mix, seq_len=seq_len)
    h2 = _modulate(x1, nf_ref[...], sh2_ref[...], sc2_ref[...]).astype(_BF16)
    gate = _bdot(h2, wg_ref[...])
    act = (gate * _sigmoid(gate)) * _bdot(h2, wu_ref[...])
    o_ref[...] = x1 + g2_ref[...] * _bdot(act.astype(_BF16), wd_ref[...])


def _top2(vals, idx, n_idx, axis):
    v1 = jnp.max(vals, axis=axis, keepdims=True)
    i1 = jnp.min(jnp.where(vals == v1, idx, n_idx), axis=axis, keepdims=True)
    rest = jnp.where(idx == i1, -jnp.inf, vals)
    v2 = jnp.max(rest, axis=axis, keepdims=True)
    i2 = jnp.min(jnp.where(rest == v2, idx, n_idx), axis=axis, keepdims=True)
    return i1, i2, v1, v2


def _pick(idx, sel, vals, axis):
    return jnp.sum(jnp.where(idx == sel, vals, 0.0), axis=axis, keepdims=True)


def _merge_router_kernel(*refs, seq_len, n_experts, route_rows):
    mix = refs[:_N_MIX_REFS]
    (nf_ref, sh2_ref, sc2_ref, wr_ref, x1_ref, h2_ref, ml_ref, cnt_ref) = refs[_N_MIX_REFS:]
    x1 = _mixer_update(pl.program_id(0), *mix, seq_len=seq_len)
    x1_ref[...] = x1
    h2 = _modulate(x1, nf_ref[...], sh2_ref[...], sc2_ref[...])
    h_hi = h2.astype(_BF16)
    h2_ref[...] = h_hi
    tm = h2.shape[0]
    h_lo = (h2 - h_hi.astype(_F32)).astype(_BF16)
    wr = wr_ref[...]
    by_hi = lax.dot_general(wr, h_hi, _NT, preferred_element_type=_F32)
    by_lo = lax.dot_general(wr, h_lo, _NT, preferred_element_type=_F32)
    lt_all = by_hi[:n_experts] + by_hi[n_experts:] + by_lo[:n_experts]
    row = lax.broadcasted_iota(jnp.int32, (route_rows, route_rows), 0)
    col = lax.broadcasted_iota(jnp.int32, (route_rows, route_rows), 1)
    earlier_cols = _indicator(row < col)
    sub = lax.broadcasted_iota(jnp.int32, (n_experts, route_rows), 0)

    for u in range(tm // route_rows):
        lt = lt_all[:, u * route_rows:(u + 1) * route_rows]
        j1, j2, v1, v2 = _top2(lt, sub, n_experts, 0)
        e2 = jnp.exp(v2 - v1)
        den = 1.0 + e2
        sel_t = _indicator(jnp.logical_or(sub == j1, sub == j2))
        ranks_t = _bdot(sel_t, earlier_cols)
        fields = (j1.astype(_F32), j2.astype(_F32), _pick(sub, j1, ranks_t, 0),
                  _pick(sub, j2, ranks_t, 0), 1.0 / den, e2 / den)
        assert len(fields) <= n_experts
        meta_t = jnp.zeros(lt.shape, _F32)
        for k, f in enumerate(fields):
            meta_t = jnp.where(sub == k, f, meta_t)
        experts = slice(u * n_experts, (u + 1) * n_experts)
        ml_ref[experts, :] = meta_t
        cnt =jnp.sum(sel_t.astype(_F32), axis=1, keepdims=True)
        cnt_ref[experts, :] = jnp.broadcast_to(cnt, (n_experts, _LANES))


def _mixer_specs(n, d, tm, p_cols, row_fn, weights):
    half = p_cols // 2
    sub = _BF16_SUBLANES
    n_sub = n // sub
    specs = [
        pl.BlockSpec((tm, d), lambda i: (i, 0)),
        pl.BlockSpec((tm, weights["wao"].shape[0]), lambda i: (i, 0)),
        pl.BlockSpec((tm, half), lambda i: (i, 1)),
        pl.BlockSpec((sub, half), lambda i: (jnp.maximum(i * (tm // sub) - 1, 0), 1)),
        pl.BlockSpec((sub, half), lambda i: (jnp.minimum((i + 1) * (tm // sub), n_sub - 1), 1)),
        _resident((1, d)),
        _mod_spec(d, 0, row_fn), _mod_spec(d, 1, row_fn), _mod_spec(d, 2, row_fn),
    ]
    for name in ("wconv", "wao", "wco", "wgate", "bgate", "wout"):
        specs.append(_resident(weights[name].shape))
    return specs


def _merge_ffn(x2d, ya, p, mods, row_fn, seq_len, wts, tm):
    n, d = x2d.shape
    specs = _mixer_specs(n, d, tm, p.shape[1], row_fn, wts)
    specs += [_resident((1, d)), _mod_spec(d, 3, row_fn), _mod_spec(d, 4, row_fn),
              _mod_spec(d, 5, row_fn),
              _resident(wts["ffn_gate"].shape), _resident(wts["ffn_up"].shape),
              _resident(wts["ffn_down"].shape)]
    return pl.pallas_call(
        functools.partial(_merge_ffn_kernel, seq_len=seq_len),
        grid=(n // tm,),
        in_specs=specs,
        out_specs=pl.BlockSpec((tm, d), lambda i: (i, 0)),
        out_shape=jax.ShapeDtypeStruct((n, d), _F32),
        compiler_params=_params(1),
        name="merge_ffn",
    )(x2d, ya, p, p, p, wts["norm_mix"], mods, mods, mods,
      wts["wconv"], wts["wao"], wts["wco"], wts["wgate"], wts["bgate"], wts["wout"],
      wts["norm_ffn"], mods, mods, mods, wts["ffn_gate"], wts["ffn_up"], wts["ffn_down"])


def _merge_router(x2d, ya, p, mods, row_fn, seq_len, wts, n_experts, tm, route_rows):
    n, d = x2d.shape
    assert tm % route_rows == 0
    specs = _mixer_specs(n, d, tm, p.shape[1], row_fn, wts)
    specs += [_resident((1, d)), _mod_spec(d, 3, row_fn), _mod_spec(d, 4, row_fn),
              _resident(wts["router"].shape)]
    tile = lambda w: pl.BlockSpec((tm, w), lambda i: (i, 0))
    meta_rows = n_experts * (tm // route_rows)
    n_route = n // route_rows
    return pl.pallas_call(
        functools.partial(_merge_router_kernel, seq_len=seq_len, n_experts=n_experts,
                          route_rows=route_rows),
        grid=(n // tm,),
        in_specs=specs,
        out_specs=[tile(d), tile(d),
                   pl.BlockSpec((meta_rows, route_rows), lambda i: (i, 0)),
                   pl.BlockSpec((meta_rows, _LANES), lambda i: (i, 0))],
        out_shape=[jax.ShapeDtypeStruct((n, d), _F32),
                   jax.ShapeDtypeStruct((n, d), _BF16),
                   jax.ShapeDtypeStruct((n_route * n_experts, route_rows), _F32),
                   jax.ShapeDtypeStruct((n_route * n_experts, _LANES), _F32)],
        compiler_params=_params(1),
        name="merge_router",
    )(x2d, ya, p, p, p, wts["norm_mix"], mods, mods, mods,
      wts["wconv"], wts["wao"], wts["wco"], wts["wgate"], wts["bgate"], wts["wout"],
      wts["norm_ffn"], mods, mods, wts["router"])


_SEG_ALIGN = _BF16_SUBLANES
_FFN_ROWS = 256


def _round_up(v, m):
    return (v + m - 1) // m * m


def _local_slots(tm, n_experts):
    return _round_up(_TOP_K * tm + n_experts * (_SEG_ALIGN - 1), _SEG_ALIGN)


def _routing_plan(cnt, n_slots):
    pad = _round_up(cnt, _SEG_ALIGN)
    total = jnp.sum(pad, axis=0)
    region = _round_up(total, _FFN_ROWS)
    ends = jnp.cumsum(region)
    seg_start = (ends - region)[None, :] + jnp.cumsum(pad, axis=0) - pad
    loc_off = jnp.cumsum(pad, axis=1) - pad
    n_used = (ends[-1] // _FFN_ROWS).astype(jnp.int32).reshape(1)
    tails = jnp.concatenate([ends - region + total, region - total, n_used])
    tile_row = jnp.arange(n_slots // _FFN_ROWS, dtype=jnp.int32) * _FFN_ROWS
    tile_expert = jnp.minimum(jnp.sum(tile_row[:, None] >= ends[None, :], axis=1),
                              cnt.shape[1] - 1).astype(jnp.int32)
    ids = jnp.arange(cnt.shape[1], dtype=jnp.int32)
    later = jnp.logical_and(region[None, :] > 0, ids[None, :] > ids[:, None])
    nxt = jnp.min(jnp.where(later, ids[None, :], cnt.shape[1]), axis=1)
    next_expert = jnp.where(nxt == cnt.shape[1], ids, nxt).astype(jnp.int32)
    flat = lambda a: a.reshape(-1).astype(jnp.int32)
    return ((flat(seg_start), flat(pad), flat(loc_off)), flat(tails), tile_expert, n_used,
            next_expert)


def _segment_copies(i, n_experts, seg_ref, pad_ref, loc_ref, hbm_ref, buf_ref, sem, to_hbm):
    copies = []
    for e in range(n_experts):
        n = pl.multiple_of(pad_ref[i * n_experts + e], _SEG_ALIGN)
        loc = pl.multiple_of(loc_ref[i * n_experts + e], _SEG_ALIGN)
        seg = pl.multiple_of(seg_ref[i * n_experts + e], _SEG_ALIGN)
        local, remote = buf_ref.at[pl.ds(loc, n)], hbm_ref.at[pl.ds(seg, n)]
        src, dst = (local, remote) if to_hbm else (remote, local)
        copies.append((n, pltpu.make_async_copy(src, dst, sem)))
    return copies


def _start_copies(copies):
    for n, cp in copies:
        pl.when(n > 0)(cp.start)


def _wait_copies(copies):
    for n, cp in copies:
        pl.when(n > 0)(cp.wait)


def _run_copies(copies):
    _start_copies(copies)
    _wait_copies(copies)


def _expert_offset(expert, loc_ref, i, n_experts):
    off = jnp.zeros(expert.shape, jnp.int32)
    for e in range(n_experts):
        off = jnp.where(expert == e, loc_ref[i * n_experts + e], off)
    return off


def _dispatch_kernel(seg_ref, pad_ref, loc_ref, tail_ref, h_ref, ml_ref, xs_ref, bufs, zeros, sems,
                     *, n_experts):
    i = pl.program_id(0)
    last = pl.num_programs(0) - 1
    cur = lax.rem(i, 2)

    def copies(tile, b):
        return _segment_copies(tile, n_experts, seg_ref, pad_ref, loc_ref, xs_ref, bufs.at[b],
                               sems.at[b], True)

    @pl.when(i >= 2)
    def _():
        _wait_copies(copies(i - 2, cur))

    tm = h_ref.shape[0]
    meta = ml_ref[...]
    slot1 = _expert_offset(meta[0:1].astype(jnp.int32), loc_ref, i, n_experts) + meta[2:3].astype(jnp.int32)
    slot2 = _expert_offset(meta[1:2].astype(jnp.int32), loc_ref, i, n_experts) + meta[3:4].astype(jnp.int32)
    slot = lax.broadcasted_iota(jnp.int32, (bufs.shape[1], tm), 0)
    perm = jnp.where(slot == slot1, 1.0, jnp.where(slot == slot2, 1.0, 0.0)).astype(_BF16)
    bufs[cur] = _bdot(perm, h_ref[...]).astype(bufs.dtype)
    _start_copies(copies(i, cur))

    @pl.when(i == last)
    def _():
        @pl.when(i >= 1)
        def _():
            _wait_copies(copies(i - 1, 1 - cur))

        _wait_copies(copies(i, cur))

        zeros[...] = jnp.zeros_like(zeros)
        fills = []
        for e in range(n_experts):
            start = pl.multiple_of(tail_ref[e], _SEG_ALIGN)
            n = pl.multiple_of(tail_ref[n_experts + e], _SEG_ALIGN)
            fills.append((n, pltpu.make_async_copy(zeros.at[pl.ds(0, n)],
                                                   xs_ref.at[pl.ds(start, n)], sems.at[0])))
        _run_copies(fills)

        def fill_copy(j):
            return pltpu.make_async_copy(
                zeros, xs_ref.at[pl.ds(pl.multiple_of(j * _FFN_ROWS, _FFN_ROWS), _FFN_ROWS)],
                sems.at[0])

        def start_fill(j, carry):
            fill_copy(j).start()
            return carry

        def wait_fill(j, carry):
            fill_copy(j).wait()
            return carry

        unused = (tail_ref[2 * n_experts], xs_ref.shape[0] // _FFN_ROWS)
        lax.fori_loop(*unused, start_fill, 0)
        lax.fori_loop(*unused, wait_fill, 0)


def _dispatch(h2, meta_l, plan, tails, n_slots, n_experts, tm):
    n, d = h2.shape
    seg_start, pad, loc_off = plan
    return pl.pallas_call(
        functools.partial(_dispatch_kernel, n_experts=n_experts),
        grid_spec=pltpu.PrefetchScalarGridSpec(
            num_scalar_prefetch=4,
            grid=(n // tm,),
            in_specs=[pl.BlockSpec((tm, d), lambda i, *_: (i, 0)),
                      pl.BlockSpec((n_experts, tm), lambda i, *_: (i, 0))],
            out_specs=pl.BlockSpec(memory_space=pl.ANY),
            scratch_shapes=[pltpu.VMEM((2, _local_slots(tm, n_experts), d), _BF16),
                            pltpu.VMEM((_FFN_ROWS, d), _BF16),
                            pltpu.SemaphoreType.DMA((2,))],
        ),
        out_shape=jax.ShapeDtypeStruct((n_slots, d), _BF16),
        compiler_params=_params(1),
        name="dispatch",
    )(seg_start, pad, loc_off, tails, h2, meta_l)


_W_ROUNDS = 16
_W_ROUNDS_PER_STEP = 2
_ST_CUR, _ST_LOADED, _ST_TARGET, _ST_DONE = range(4)


def _expert_ffn_kernel(te_ref, nu_ref, nxt_ref, x_ref, wg_hbm, wu_hbm, wd_hbm, o_ref,
                       wg_v, wu_v, wd_v, stage_gu, stage_d, sems, state):
    j = pl.program_id(0)
    rg = wg_v.shape[1] // _W_ROUNDS
    rd = wd_v.shape[1] // _W_ROUNDS

    def round_copies(e, r, s):
        g_rows = pl.ds(pl.multiple_of(r * rg, rg), rg)
        d_rows = pl.ds(pl.multiple_of(r * rd, rd), rd)
        return (pltpu.make_async_copy(wg_hbm.at[e, g_rows], stage_gu.at[s, 0], sems.at[s, 0]),
                pltpu.make_async_copy(wu_hbm.at[e, g_rows], stage_gu.at[s, 1], sems.at[s, 1]),
                pltpu.make_async_copy(wd_hbm.at[e, d_rows], stage_d.at[s], sems.at[s, 2]))

    def narrow_round(r, s, copy):
        g_rows = pl.ds(pl.multiple_of(r * rg, rg), rg)
        d_rows = pl.ds(pl.multiple_of(r * rd, rd), rd)
        wg_v[copy, g_rows, :] = stage_gu[s, 0].astype(_BF16)
        wu_v[copy, g_rows, :] = stage_gu[s, 1].astype(_BF16)
        wd_v[copy, d_rows, :] = stage_d[s].astype(_BF16)

    @pl.when(j == 0)
    def _():
        state[_ST_CUR] = 0
        state[_ST_LOADED] = -1
        state[_ST_TARGET] = -1
        state[_ST_DONE] = _W_ROUNDS

    active = j < nu_ref[0]
    e = te_ref[j]

    @pl.when(jnp.logical_and(active, e != state[_ST_LOADED]))
    def _():
        other = 1 - state[_ST_CUR]

        @pl.when(state[_ST_TARGET] != e)
        def _():
            state[_ST_TARGET] = e
            state[_ST_DONE] = 0

        first = state[_ST_DONE]

        @pl.when(first < _W_ROUNDS)
        def _():
            for cp in round_copies(e, first, 0):
                cp.start()

        def load_round(r, carry):
            s = lax.rem(r - first, 2)

            @pl.when(r + 1 < _W_ROUNDS)
            def _():
                for cp in round_copies(e, r + 1, 1 - s):
                    cp.start()

            for cp in round_copies(e, r, s):
                cp.wait()
            narrow_round(r, s, other)
            return carry

        lax.fori_loop(first, _W_ROUNDS, load_round, 0)
        nxt = nxt_ref[e]
        state[_ST_CUR] = other
        state[_ST_LOADED] = e
        state[_ST_TARGET] = nxt
        state[_ST_DONE] = jnp.where(nxt != e, 0, _W_ROUNDS)

    cur = state[_ST_CUR]
    target = state[_ST_TARGET]
    done = state[_ST_DONE]
    prefetch = jnp.logical_and(active, done < _W_ROUNDS)

    @pl.when(prefetch)
    def _():
        for s in range(_W_ROUNDS_PER_STEP):
            for cp in round_copies(target, done + s, s):
                cp.start()

    @pl.when(active)
    def _():
        x = x_ref[...]
        gate = _bdot(x, wg_v[cur])
        act = (gate * _sigmoid(gate)) * _bdot(x, wu_v[cur])
        o_ref[...] = _bdot(act.astype(_BF16), wd_v[cur]).astype(o_ref.dtype)

    @pl.when(jnp.logical_not(active))
    def _():
        o_ref[...] = jnp.zeros_like(o_ref)

    @pl.when(prefetch)
    def _():
        for s in range(_W_ROUNDS_PER_STEP):
            for cp in round_copies(target, done + s, s):
                cp.wait()
            narrow_round(done + s, s, 1 - cur)
        state[_ST_DONE] = done + _W_ROUNDS_PER_STEP


def _expert_ffn(xs, tile_expert, n_used, next_expert, wg, wu, wd):
    n_slots, d = xs.shape
    n_experts, _, f = wg.shape
    assert d % (_W_ROUNDS * _BF16_SUBLANES) == 0 and f % (_W_ROUNDS * _BF16_SUBLANES) == 0
    assert _W_ROUNDS % _W_ROUNDS_PER_STEP == 0 and _W_ROUNDS_PER_STEP >= 2
    any_spec = pl.BlockSpec(memory_space=pl.ANY)
    return pl.pallas_call(
        _expert_ffn_kernel,
        grid_spec=pltpu.PrefetchScalarGridSpec(
            num_scalar_prefetch=3,
            grid=(n_slots // _FFN_ROWS,),
            in_specs=[
                pl.BlockSpec((_FFN_ROWS, d),
                             lambda j, te, nu, nx: (jnp.maximum(jnp.minimum(j, nu[0] - 1), 0), 0)),
                any_spec, any_spec, any_spec,
            ],
            out_specs=pl.BlockSpec((_FFN_ROWS, d), lambda j, te, nu, nx: (j, 0)),
            scratch_shapes=[
                pltpu.VMEM((2, d, f), _BF16), pltpu.VMEM((2, d, f), _BF16),
                pltpu.VMEM((2, f, d), _BF16),
                pltpu.VMEM((_W_ROUNDS_PER_STEP, 2, d // _W_ROUNDS, f), _F32),
                pltpu.VMEM((_W_ROUNDS_PER_STEP, f // _W_ROUNDS, d), _F32),
                pltpu.SemaphoreType.DMA((_W_ROUNDS_PER_STEP, 3)),
                pltpu.SMEM((4,), jnp.int32),
            ],
        ),
        out_shape=jax.ShapeDtypeStruct((n_slots, d), _BF16),
        compiler_params=_params(1),
        name="expert_ffn",
    )(tile_expert, n_used, next_expert, xs, wg, wu, wd)


def _combine_kernel(seg_ref, pad_ref, loc_ref, ys_ref, ml_ref, x1_ref, g2_ref, nf_ref, o_ref,
                    bufs, sems, *, n_experts):
    i = pl.program_id(0)
    tm = x1_ref.shape[0]
    cur = lax.rem(i, 2)

    def copies(tile, b):
        return _segment_copies(tile, n_experts, seg_ref, pad_ref, loc_ref, ys_ref, bufs.at[b],
                               sems.at[b], False)

    @pl.when(i == 0)
    def _():
        bufs[...] = jnp.zeros_like(bufs)
        _start_copies(copies(0, 0))

    @pl.when(i + 1 < pl.num_programs(0))
    def _():
        _start_copies(copies(i + 1, 1 - cur))

    _wait_copies(copies(i, cur))
    meta = ml_ref[...]
    slot1 = _expert_offset(meta[0:1].astype(jnp.int32), loc_ref, i, n_experts) + meta[2:3].astype(jnp.int32)
    slot2 = _expert_offset(meta[1:2].astype(jnp.int32), loc_ref, i, n_experts) + meta[3:4].astype(jnp.int32)
    fields = [slot1.astype(_F32), slot2.astype(_F32), meta[4:5], meta[5:6]]
    per_token = jnp.concatenate(
        fields + [jnp.zeros((_LANES - len(fields), tm), _F32)], axis=0).T
    rc = min(tm, _MIX_CHUNK_ROWS)
    slot = lax.broadcasted_iota(jnp.int32, (rc, bufs.shape[1]), 1)
    for c in range(tm // rc):
        rows = slice(c * rc, (c + 1) * rc)
        col = lambda k: per_token[rows, k:k + 1]
        weights = jnp.where(slot == col(0).astype(jnp.int32), col(2),
                            jnp.where(slot == col(1).astype(jnp.int32), col(3), 0.0))
        y = _bdot(weights.astype(_BF16), bufs[cur])
        o_ref[rows, :] = _rmsnorm(x1_ref[rows, :] + g2_ref[...] * y, nf_ref[...])


def _combine(ys, meta_l, x1, plan, mods, row_fn, norm_final, n_experts, tm):
    n, d = x1.shape
    seg_start, pad, loc_off = plan
    return pl.pallas_call(
        functools.partial(_combine_kernel, n_experts=n_experts),
        grid_spec=pltpu.PrefetchScalarGridSpec(
            num_scalar_prefetch=3,
            grid=(n // tm,),
            in_specs=[pl.BlockSpec(memory_space=pl.ANY),
                      pl.BlockSpec((n_experts, tm), lambda i, *_: (i, 0)),
                      pl.BlockSpec((tm, d), lambda i, *_: (i, 0)),
                      _mod_spec(d, 5, row_fn),
                      _resident((1, d))],
            out_specs=pl.BlockSpec((tm, d), lambda i, *_: (i, 0)),
            scratch_shapes=[pltpu.VMEM((2, _local_slots(tm, n_experts), d), _BF16),
                            pltpu.SemaphoreType.DMA((2,))],
        ),
        out_shape=jax.ShapeDtypeStruct((n, d), _F32),
        compiler_params=_params(1),
        name="combine",
    )(seg_start, pad, loc_off, ys, meta_l, x1, mods, norm_final)


def kernel(x, c, ctx, c_ctx, w_ada, b_ada, norm_mix, norm_ffn, w_in, w_conv, rpb, w_attn_out,
           w_conv_out, w_gate, b_gate, w_out, w_ffn_gate, w_ffn_up, w_ffn_down, w_router,
           w_exp_gate, w_exp_up, w_exp_down, norm_final):
    batch, seq, d = x.shape
    ctx_len = ctx.shape[1]
    depth = w_ada.shape[0]
    attn_w = _N_HEADS * _HEAD_DIM
    n_experts = w_router.shape[-1]
    rows = seq // _GRID_W
    n_pairs = attn_w // _LANES
    assert depth == 2 and seq % (_Q_ROWS * _GRID_W) == 0 and attn_w % _LANES == 0
    assert n_experts == _F32_SUBLANES
    assert w_conv.shape[1] == 3

    ctx_row = batch
    cond_rows = -(-(batch + 1) // 8) * 8
    cond = jnp.zeros((cond_rows, d), _F32).at[:batch].set(c).at[ctx_row].set(c_ctx)
    mods_all = _ada(cond, w_ada, b_ada, tn=1536).reshape(depth, cond_rows, 1, 6 * d)

    union, starts, variant_of, patterns, col_starts = _attn_plan(rows)

    x2d = x.reshape(batch * seq, d)
    c2d = ctx.reshape(batch * ctx_len, d)
    tm_lat = 512
    ctx_row_fn = lambda i: ctx_row

    def lat_rows(tm):
        return lambda i: (i * tm) // seq

    out = None
    for l in range(depth):
        last = l == depth - 1
        mods = mods_all[l]
        wts = {
            "norm_mix": norm_mix[l].reshape(1, d),
            "norm_ffn": norm_ffn[l].reshape(1, d),
            "wconv": w_conv[l],
            "wao": w_attn_out[l].astype(_BF16),
            "wco": w_conv_out[l].astype(_BF16),
            "wgate": w_gate[l].astype(_BF16),
            "bgate": b_gate[l].reshape(1, -1),
            "wout": w_out[l].astype(_BF16),
        }
        w_in_l = w_in[l].astype(_BF16)
        p = _proj(x2d, mods, lat_rows(1024), wts["norm_mix"], w_in_l, tm=1024)
        if last:
            pc = _proj(c2d, mods, ctx_row_fn, wts["norm_mix"], w_in_l[:, attn_w:3 * attn_w], tm=512)
            kc_col = 0
        else:
            pc = _proj(c2d, mods, ctx_row_fn, wts["norm_mix"], w_in_l, tm=512)
            kc_col = n_pairs
        table = _bias_table(rpb[l], union, patterns, col_starts)
        ya, yca = _attention(p, pc, kc_col, table, starts, variant_of, union, col_starts, batch,
                             seq, ctx_len, with_ctx=not last)
        if not last:
            wts["ffn_gate"] = w_ffn_gate[l // 2].astype(_BF16)
            wts["ffn_up"] = w_ffn_up[l // 2].astype(_BF16)
            wts["ffn_down"] = w_ffn_down[l // 2].astype(_BF16)
            x2d = _merge_ffn(x2d, ya, p, mods, lat_rows(tm_lat), seq, wts, tm=tm_lat)
            c2d = _merge_ffn(c2d, yca, pc, mods, ctx_row_fn, ctx_len, wts, tm=ctx_len)
        else:
            wr_t = w_router[l // 2].T
            wr_hi = wr_t.astype(_BF16)
            wr_lo = (wr_t - wr_hi.astype(_F32)).astype(_BF16)
            wts["router"] = jnp.concatenate([wr_hi, wr_lo], axis=0)
            tm_moe = 512
            n_tok = batch * seq
            n_tiles = n_tok // tm_moe
            tm_mix = 2 * tm_moe
            x1, h2, meta_l, cnt = _merge_router(
                x2d, ya, p, mods, lat_rows(tm_mix), seq, wts, n_experts, tm=tm_mix,
                route_rows=tm_moe)
            n_slots = _round_up(_TOP_K * n_tok + n_tiles * n_experts * (_SEG_ALIGN - 1)
                                + n_experts * (_FFN_ROWS - 1), _FFN_ROWS)
            cnt = cnt[:, 0].reshape(n_tiles, n_experts).astype(jnp.int32)
            plan, tails, tile_expert, n_used, next_expert = _routing_plan(cnt, n_slots)
            xs = _dispatch(h2, meta_l, plan, tails, n_slots, n_experts, tm=tm_moe)
            ys = _expert_ffn(xs, tile_expert, n_used, next_expert, w_exp_gate[l // 2],
                             w_exp_up[l // 2], w_exp_down[l // 2])
            out = _combine(ys, meta_l, x1, plan, mods, lat_rows(tm_moe), norm_final.reshape(1, d),
                           n_experts, tm=tm_moe)
    return out.reshape(batch, seq, d)
```

```python
import functools

import numpy as np
import jax
import jax.numpy as jnp
from jax import lax
from jax.experimental import pallas as pl
from jax.experimental.pallas import tpu as pltpu

_F32 = jnp.float32
_BF16 = jnp.bfloat16

_GRID_W = 64
_N_HEADS = 8
_HEAD_DIM = 64
_NA_ROWS = 8
_NA_COLS = 16
_TOP_K = 2
_EPS = 1e-6
_NEG_INF = -1e30

_LANES = 128
_F32_SUBLANES = 8
_BF16_SUBLANES = 16
_VMEM_LIMIT_BYTES = 56 * 1024 * 1024

_Q_ROWS = 8
_Q_COLS = 16
_K_COLS = 32
_MIX_CHUNK_ROWS = 256
_HEADS_PER_BLOCK = _LANES // _HEAD_DIM
_NT = (((1,), (1,)), ((), ()))


def _params(n_axes):
    return pltpu.CompilerParams(
        dimension_semantics=("arbitrary",) * n_axes,
        vmem_limit_bytes=_VMEM_LIMIT_BYTES)


def _resident(shape):
    nd = len(shape)
    return pl.BlockSpec(shape, lambda *_: (0,) * nd, pipeline_mode=pl.Buffered(1))


def _sigmoid(x):
    return 1.0 / (1.0 + jnp.exp(-x))


def _rmsnorm(x, g):
    ms = jnp.mean(x * x, axis=-1, keepdims=True)
    return (x * lax.rsqrt(ms + _EPS)) * g


def _modulate(x, g, shift, scale):
    return _rmsnorm(x, g) * (1.0 + scale) + shift


def _bdot(a, b):
    return jnp.dot(a, b, preferred_element_type=_F32)


def _indicator(mask):
    return jnp.where(mask, 1.0, 0.0).astype(_BF16)


def _ada_kernel(cond_ref, w_ref, b_ref, o_ref):
    cond = cond_ref[...]
    act = cond * _sigmoid(cond)
    rows = act.shape[0]
    a_hi = act.astype(_BF16)
    a_lo = (act - a_hi.astype(_F32)).astype(_BF16)
    r = _bdot(jnp.concatenate([a_hi, a_lo], axis=0), w_ref[...].astype(_BF16))
    o_ref[...] = r[:rows] + r[rows:] + b_ref[...]


def _ada(cond, w_ada, b_ada, tn):
    depth, d, n_out = w_ada.shape
    rows = cond.shape[0]
    return pl.pallas_call(
        _ada_kernel,
        grid=(depth, n_out // tn),
        in_specs=[
            pl.BlockSpec((rows, d), lambda l, j: (0, 0)),
            pl.BlockSpec((None, d, tn), lambda l, j: (l, 0, j)),
            pl.BlockSpec((None, 1, tn), lambda l, j: (l, 0, j)),
        ],
        out_specs=pl.BlockSpec((None, rows, tn), lambda l, j: (l, 0, j)),
        out_shape=jax.ShapeDtypeStruct((depth, rows, n_out), _F32),
        compiler_params=_params(2),
        name="ada",
    )(cond, w_ada, b_ada.reshape(depth, 1, n_out))


def _mod_spec(d, chunk, row_fn):
    return pl.BlockSpec((None, 1, d), lambda i, *_: (row_fn(i), 0, chunk))


def _proj_kernel(x_ref, g_ref, sh_ref, sc_ref, w_ref, blk_ref, *rest_ref):
    h = _modulate(x_ref[...], g_ref[...], sh_ref[...], sc_ref[...])
    acc = _bdot(h.astype(_BF16), w_ref[...])
    n_blk = blk_ref.shape[0]
    for j in range(n_blk):
        blk_ref[j] = acc[:, j * _LANES:(j + 1) * _LANES].astype(blk_ref.dtype)
    if rest_ref:
        rest_ref[0][...] = acc[:, n_blk * _LANES:].astype(rest_ref[0].dtype)


def _proj(x2d, mods, row_fn, g, w, tm, n_blocked):
    n, d = x2d.shape
    rest = w.shape[1] - n_blocked * _LANES
    out_specs = [pl.BlockSpec((n_blocked, tm, _LANES), lambda i: (0, i, 0))]
    out_shape = [jax.ShapeDtypeStruct((n_blocked, n, _LANES), _BF16)]
    if rest:
        out_specs.append(pl.BlockSpec((tm, rest), lambda i: (i, 0)))
        out_shape.append(jax.ShapeDtypeStruct((n, rest), _BF16))
    outs = pl.pallas_call(
        _proj_kernel,
        grid=(n // tm,),
        in_specs=[
            pl.BlockSpec((tm, d), lambda i: (i, 0)),
            _resident((1, d)),
            _mod_spec(d, 0, row_fn),
            _mod_spec(d, 1, row_fn),
            _resident(w.shape),
        ],
        out_specs=out_specs,
        out_shape=out_shape,
        compiler_params=_params(1),
        name="proj",
    )(x2d, g, mods, mods, w)
    return (outs[0], outs[1]) if rest else (outs[0], None)


def _bias_table_kernel(rpb_ref, o_ref, *, union, patterns, col_starts):
    sub = _F32_SUBLANES
    span = _NA_COLS - 1
    slots = _LANES // _K_COLS
    rp = rpb_ref[...]
    lane = lax.broadcasted_iota(jnp.int32, (sub, _LANES), 1)
    srow = lax.broadcasted_iota(jnp.int32, (sub, _LANES), 0)
    ck_local = jnp.bitwise_and(lane, _K_COLS - 1)
    slot_of_lane = jnp.right_shift(lane, _K_COLS.bit_length() - 1)
    neg = jnp.full((sub, _LANES), _NEG_INF, _F32)
    skewed, pieces = {}, {}

    def piece(dr, shift):
        if dr not in skewed:
            row = jnp.broadcast_to(rp[dr + _NA_ROWS - 1:dr + _NA_ROWS, :], (sub, _LANES))
            skewed[dr] = pltpu.roll(row, 0, 1, stride=1, stride_axis=0)
        if (dr, shift) not in pieces:
            pieces[dr, shift] = pltpu.roll(skewed[dr], shift, 1) if shift else skewed[dr]
        return pieces[dr, shift]

    for g, c0 in enumerate(col_starts):
        for gg in range(_Q_COLS // sub):
            k0 = c0 - _Q_COLS * g - sub * gg + span
            cq = srow + (_Q_COLS * g + sub * gg)
            cs = jnp.clip(cq - _NA_COLS // 2, 0, _GRID_W - _NA_COLS)
            ck = ck_local + c0
            in_win = jnp.logical_and(ck >= cs, ck < cs + _NA_COLS)
            tiles = {}

            def make_tile(drs):
                tile = neg
                for jq, dr in enumerate(drs):
                    if dr is not None:
                        here = jnp.logical_and(in_win, slot_of_lane == jq)
                        tile = jnp.where(here, piece(dr, (_K_COLS * jq - k0) % _LANES), tile)
                return tile

            for v, pat in enumerate(patterns):
                for a in range(_Q_ROWS):
                    rows = slice(a * _Q_COLS + sub * gg, a * _Q_COLS + sub * (gg + 1))
                    for quad in range(-(-union // slots)):
                        js = [quad * slots + jq for jq in range(slots)]
                        drs = tuple(pat[a * union + j] if j < union else None for j in js)
                        width = min(slots, union - quad * slots) * _K_COLS
                        if drs not in tiles:
                            tiles[drs] = make_tile(drs)
                        o_ref[v, g, rows, quad * _LANES:quad * _LANES + width] = (
                            tiles[drs][:, :width])


def _bias_table(rpb_l, union, patterns, col_starts):
    h, n_dr, n_dc = rpb_l.shape
    assert n_dc <= _LANES and _LANES % _K_COLS == 0
    n_dr_pad = _round_up(n_dr, _F32_SUBLANES)
    padded = jnp.pad(rpb_l, ((0, 0), (0, n_dr_pad - n_dr), (0, _LANES - n_dc)))
    n_var, n_grp = len(patterns), len(col_starts)
    qt, kt = _Q_ROWS * _Q_COLS, union * _K_COLS
    return pl.pallas_call(
        functools.partial(_bias_table_kernel, union=union, patterns=patterns,
                          col_starts=col_starts),
        grid=(h,),
        in_specs=[pl.BlockSpec((None, n_dr_pad, _LANES), lambda i: (i, 0, 0))],
        out_specs=pl.BlockSpec((n_var, n_grp, None, qt, kt), lambda i: (0, 0, i, 0, 0)),
        out_shape=jax.ShapeDtypeStruct((n_var, n_grp, h, qt, kt), _F32),
        compiler_params=_params(1),
        name="bias_table",
    )(padded)


def _attn_plan(rows):
    kh = min(_NA_ROWS, rows)
    union = min(_Q_ROWS + kh - 1, rows)
    n_blocks = rows // _Q_ROWS
    starts, variant_of, patterns = [], [], []
    for t in range(n_blocks):
        start = int(np.clip(_Q_ROWS * t - kh // 2, 0, rows - union))
        pat = []
        for a in range(_Q_ROWS):
            r = _Q_ROWS * t + a
            rs = int(np.clip(r - kh // 2, 0, rows - kh))
            for j in range(union):
                kr = start + j
                pat.append(kr - r if rs <= kr < rs + kh else None)
        pat = tuple(pat)
        if pat not in patterns:
            patterns.append(pat)
        starts.append(start)
        variant_of.append(patterns.index(pat))
    col_starts = []
    for g in range(_GRID_W // _Q_COLS):
        c0 = int(np.clip(_Q_COLS * g - _NA_COLS // 2, 0, _GRID_W - _K_COLS))
        for cq in range(_Q_COLS * g, _Q_COLS * (g + 1)):
            cs = int(np.clip(cq - _NA_COLS // 2, 0, _GRID_W - _NA_COLS))
            assert c0 <= cs and cs + _NA_COLS <= c0 + _K_COLS
        col_starts.append(c0)
    return union, starts, variant_of, tuple(patterns), tuple(col_starts)


def _first_head_lanes(shape):
    return lax.broadcasted_iota(jnp.int32, shape, len(shape) - 1) < _HEAD_DIM


def _values_with_ones(v):
    lo = _first_head_lanes(v.shape)
    ones = jnp.ones_like(v)
    return jnp.where(lo, v, ones), jnp.where(lo, ones, v)


def _scores(qb, keys, bias_fns):
    lo = _first_head_lanes(qb.shape)
    out = []
    for hh in range(_HEADS_PER_BLOCK):
        qm = jnp.where(lo if hh == 0 else jnp.logical_not(lo), qb, jnp.zeros_like(qb))
        scores = []
        for kk, bias_fn in zip(keys, bias_fns):
            s = lax.dot_general(qm, kk, _NT, preferred_element_type=_F32)
            scores.append(s if bias_fn is None else s + bias_fn(hh))
        out.append(scores)
    return out


def _probabilities(scores):
    m = functools.reduce(jnp.maximum, [jnp.max(s, axis=-1, keepdims=True) for s in scores])
    return [jnp.exp(s - m).astype(_BF16) for s in scores]


def _weighted_values(probs, values):
    res = []
    for hh in range(_HEADS_PER_BLOCK):
        acc = None
        for e, v1 in zip(probs[hh], values):
            o = _bdot(e, v1[hh])
            acc = o if acc is None else acc + o
        res.append(acc)
    lo = _first_head_lanes(res[0].shape)
    num = jnp.where(lo, res[0], res[1])
    den = pltpu.roll(jnp.where(lo, res[1], res[0]), _HEAD_DIM, 1)
    return num / den


def _attn_kernel(start_ref, var_ref, q_ref, k_ref, v_ref, kc_ref, vc_ref, tab_ref, *rest,
                 n_row_blocks, union, col_starts, with_ctx):
    if with_ctx:
        qc_ref, o_ref, oc_ref, s_loc, s_ctx, e_loc, e_ctx = rest
    else:
        o_ref, s_loc, s_ctx, e_loc, e_ctx = rest
    heads = range(_HEADS_PER_BLOCK)
    n_groups = len(col_starts)
    assert n_groups % 2 == 0 and n_groups >= 2 and n_row_blocks >= 1
    scale = _HEAD_DIM ** -0.5
    kc = kc_ref[...]
    vc1 = _values_with_ones(vc_ref[...])

    def gather(ref, first_row, n_rows, col, width):
        return jnp.concatenate(
            [ref[pl.ds(first_row + r * _GRID_W + col, width), :] for r in range(n_rows)], axis=0)

    def key_rows(t):
        return pl.multiple_of(start_ref[t] * _GRID_W, _GRID_W)

    def query_rows(t):
        return pl.multiple_of(t * (_Q_ROWS * _GRID_W), _Q_ROWS * _GRID_W)

    def scores_stage(t, g):
        slot, var = g % 2, var_ref[t]
        qb = gather(q_ref, query_rows(t), _Q_ROWS, _Q_COLS * g, _Q_COLS) * scale
        kb = gather(k_ref, key_rows(t), union, col_starts[g], _K_COLS)
        scores = _scores(qb, [kb, kc], [lambda hh: tab_ref[var, g, hh], None])
        for hh in heads:
            s_loc[slot, hh], s_ctx[slot, hh] = scores[hh]

    def softmax_stage(g):
        slot = g % 2
        for hh in heads:
            e_loc[slot, hh], e_ctx[slot, hh] = _probabilities([s_loc[slot, hh], s_ctx[slot, hh]])

    def values_stage(t, g):
        slot = g % 2
        vb1 = _values_with_ones(gather(v_ref, key_rows(t), union, col_starts[g], _K_COLS))
        probs = [[e_loc[slot, hh], e_ctx[slot, hh]] for hh in heads]
        o = _weighted_values(probs, [vb1, vc1]).astype(o_ref.dtype)
        for a in range(_Q_ROWS):
            o_ref[pl.ds(query_rows(t) + a * _GRID_W + _Q_COLS * g, _Q_COLS), :] = (
                o[a * _Q_COLS:(a + 1) * _Q_COLS])

    def issue(t, k):
        values_stage(t + k // n_groups, k % n_groups)
        softmax_stage((k + 1) % n_groups)
        scores_stage(t + (k + 2) // n_groups, (k + 2) % n_groups)

    scores_stage(0, 0)
    softmax_stage(0)
    scores_stage(0, 1)

    def body(t, carry):
        for k in range(n_groups):
            issue(t, k)
        return carry

    last = n_row_blocks - 1
    lax.fori_loop(0, last, body, 0)
    for k in range(n_groups - 2):
        issue(last, k)
    values_stage(last, n_groups - 2)
    softmax_stage(n_groups - 1)
    values_stage(last, n_groups - 1)

    if with_ctx:
        scores = _scores(qc_ref[...] * scale, [kc], [None])
        probs = [_probabilities(scores[hh]) for hh in heads]
        oc_ref[...] = _weighted_values(probs, [vc1]).astype(oc_ref.dtype)


def _attention(p, pc, kc_col, table, starts, variant_of, union, col_starts, batch, seq, ctx_len,
               with_ctx):
    attn_w = _N_HEADS * _HEAD_DIM
    n_pairs = attn_w // _LANES
    n_var, n_grp = table.shape[:2]
    qb_tokens = _Q_ROWS * _Q_COLS
    union_tokens = union * _K_COLS
    kern = functools.partial(_attn_kernel, n_row_blocks=len(starts), union=union,
                             col_starts=col_starts, with_ctx=with_ctx)
    lat = lambda blk: pl.BlockSpec((None, seq, _LANES), lambda hp, b, *_: (blk(hp), b, 0))
    cx = lambda blk: pl.BlockSpec((None, ctx_len, _LANES), lambda hp, b, *_: (blk(hp), b, 0))
    in_specs = [
        lat(lambda hp: hp), lat(lambda hp: n_pairs + hp), lat(lambda hp: 2 * n_pairs + hp),
        cx(lambda hp: kc_col + hp), cx(lambda hp: kc_col + n_pairs + hp),
        pl.BlockSpec((n_var, n_grp, _HEADS_PER_BLOCK, qb_tokens, union_tokens),
                     lambda hp, b, *_: (0, 0, hp, 0, 0)),
    ]
    args = [p, p, p, pc, pc, table]
    out_specs = [pl.BlockSpec((seq, _LANES), lambda hp, b, *_: (b, hp))]
    out_shape = [jax.ShapeDtypeStruct((batch * seq, attn_w), _BF16)]
    if with_ctx:
        in_specs.append(cx(lambda hp: hp))
        args.append(pc)
        out_specs.append(pl.BlockSpec((ctx_len, _LANES), lambda hp, b, *_: (b, hp)))
        out_shape.append(jax.ShapeDtypeStruct((batch * ctx_len, attn_w), _BF16))
    outs = pl.pallas_call(
        kern,
        grid_spec=pltpu.PrefetchScalarGridSpec(
            num_scalar_prefetch=2,
            grid=(n_pairs, batch),
            in_specs=in_specs,
            out_specs=out_specs,
            scratch_shapes=[
                pltpu.VMEM((2, _HEADS_PER_BLOCK, qb_tokens, union_tokens), _F32),
                pltpu.VMEM((2, _HEADS_PER_BLOCK, qb_tokens, ctx_len), _F32),
                pltpu.VMEM((2, _HEADS_PER_BLOCK, qb_tokens, union_tokens), _BF16),
                pltpu.VMEM((2, _HEADS_PER_BLOCK, qb_tokens, ctx_len), _BF16),
            ],
        ),
        out_shape=out_shape,
        compiler_params=_params(2),
        name="attn",
    )(jnp.asarray(starts, jnp.int32), jnp.asarray(variant_of, jnp.int32), *args)
    return outs if with_ctx else (outs[0], None)


def _mixer_update(i, x_ref, ya_ref, bcu_ref, prev_ref, next_ref, nm_ref, sh1_ref, sc1_ref, g1_ref,
                  wconv_ref, wao_ref, wco_ref, wgate_ref, bgate_ref, wout_ref, seq_len):
    tm, d = x_ref.shape
    cw = wco_ref.shape[0]

    bcu = bcu_ref[...].astype(_F32)
    bg, z = bcu[:, :cw], bcu[:, cw:2 * cw] * bcu[:, 2 * cw:]
    prev = prev_ref[...].astype(_F32)
    nxt = next_ref[...].astype(_F32)
    has_prev = ((i * tm) % seq_len != 0).astype(_F32)
    has_next = (((i + 1) * tm) % seq_len != 0).astype(_F32)
    last = prev.shape[0] - 1
    z_prev = prev[last:last + 1, cw:2 * cw] * prev[last:last + 1, 2 * cw:] * has_prev
    z_next = nxt[0:1, cw:2 * cw] * nxt[0:1, 2 * cw:] * has_next
    row = lax.broadcasted_iota(jnp.int32, z.shape, 0)
    z_dn = jnp.where(row == 0, z_prev, pltpu.roll(z, 1, 0))
    z_up = jnp.where(row == tm - 1, z_next, pltpu.roll(z, tm - 1, 0))
    wconv = wconv_ref[...]
    y_conv = (bg * (z_dn * wconv[0:1] + z * wconv[1:2] + z_up * wconv[2:3])).astype(_BF16)

    chunks = []
    rc = min(tm, _MIX_CHUNK_ROWS)
    for c in range(tm // rc):
        rows = slice(c * rc, (c + 1) * rc)
        x = x_ref[rows, :]
        hb = _modulate(x, nm_ref[...], sh1_ref[...], sc1_ref[...]).astype(_BF16)
        gates = _sigmoid(_bdot(hb, wgate_ref[...]) + bgate_ref[...])
        a_proj = _bdot(ya_ref[rows, :], wao_ref[...])
        c_proj = _bdot(y_conv[rows, :], wco_ref[...])
        merged = gates[:, :d] * a_proj + gates[:, d:] * c_proj
        chunks.append(x + g1_ref[...] * _bdot(merged.astype(_BF16), wout_ref[...]))
    return jnp.concatenate(chunks, axis=0)


_N_MIX_REFS = 15


def _merge_ffn_kernel(*refs, seq_len):
    mix = refs[:_N_MIX_REFS]
    nf_ref, sh2_ref, sc2_ref, g2_ref, wg_ref, wu_ref, wd_ref, o_ref = refs[_N_MIX_REFS:]
    x1 = _mixer_update(pl.program_id(0), *mix, seq_len=seq_len)
    h2 = _modulate(x1, nf_ref[...], sh2_ref[...], sc2_ref[...]).astype(_BF16)
    gate = _bdot(h2, wg_ref[...])
    act = (gate * _sigmoid(gate)) * _bdot(h2, wu_ref[...])
    o_ref[...] = x1 + g2_ref[...] * _bdot(act.astype(_BF16), wd_ref[...])


def _top2(vals, idx, n_idx, axis):
    v1 = jnp.max(vals, axis=axis, keepdims=True)
    i1 = jnp.min(jnp.where(vals == v1, idx, n_idx), axis=axis, keepdims=True)
    rest = jnp.where(idx == i1, -jnp.inf, vals)
    v2 = jnp.max(rest, axis=axis, keepdims=True)
    i2 = jnp.min(jnp.where(rest == v2, idx, n_idx), axis=axis, keepdims=True)
    return i1, i2, v1, v2


def _pick(idx, sel, vals, axis):
    return jnp.sum(jnp.where(idx == sel, vals, 0.0), axis=axis, keepdims=True)


def _merge_router_kernel(*refs, seq_len, n_experts, route_rows):
    mix = refs[:_N_MIX_REFS]
    (nf_ref, sh2_ref, sc2_ref, wr_ref, x1_ref, h2_ref, ml_ref, cnt_ref) = refs[_N_MIX_REFS:]
    x1 = _mixer_update(pl.program_id(0), *mix, seq_len=seq_len)
    x1_ref[...] = x1
    h2 = _modulate(x1, nf_ref[...], sh2_ref[...], sc2_ref[...])
    h_hi = h2.astype(_BF16)
    h2_ref[...] = h_hi
    tm = h2.shape[0]
    h_lo = (h2 - h_hi.astype(_F32)).astype(_BF16)
    wr = wr_ref[...]
    by_hi = lax.dot_general(wr, h_hi, _NT, preferred_element_type=_F32)
    by_lo = lax.dot_general(wr, h_lo, _NT, preferred_element_type=_F32)
    lt_all = by_hi[:n_experts] + by_hi[n_experts:] + by_lo[:n_experts]
    row = lax.broadcasted_iota(jnp.int32, (route_rows, route_rows), 0)
    col = lax.broadcasted_iota(jnp.int32, (route_rows, route_rows), 1)
    earlier_cols = _indicator(row < col)
    sub = lax.broadcasted_iota(jnp.int32, (n_experts, route_rows), 0)

    for u in range(tm // route_rows):
        lt = lt_all[:, u * route_rows:(u + 1) * route_rows]
        j1, j2, v1, v2 = _top2(lt, sub, n_experts, 0)
        e2 = jnp.exp(v2 - v1)
        den = 1.0 + e2
        sel_t = _indicator(jnp.logical_or(sub == j1, sub == j2))
        ranks_t = _bdot(sel_t, earlier_cols)
        fields = (j1.astype(_F32), j2.astype(_F32), _pick(sub, j1, ranks_t, 0),
                  _pick(sub, j2, ranks_t, 0), 1.0 / den, e2 / den)
        assert len(fields) <= n_experts
        meta_t = jnp.zeros(lt.shape, _F32)
        for k, f in enumerate(fields):
            meta_t = jnp.where(sub == k, f, meta_t)
        experts = slice(u * n_experts, (u + 1) * n_experts)
        ml_ref[experts, :] = meta_t
        cnt =jnp.sum(sel_t.astype(_F32), axis=1, keepdims=True)
        cnt_ref[experts, :] = jnp.broadcast_to(cnt, (n_experts, _LANES))


def _mixer_specs(n, d, tm, bcu_cols, row_fn, weights):
    half = bcu_cols
    sub = _BF16_SUBLANES
    n_sub = n // sub
    specs = [
        pl.BlockSpec((tm, d), lambda i: (i, 0)),
        pl.BlockSpec((tm, weights["wao"].shape[0]), lambda i: (i, 0)),
        pl.BlockSpec((tm, half), lambda i: (i, 0)),
        pl.BlockSpec((sub, half), lambda i: (jnp.maximum(i * (tm // sub) - 1, 0), 0)),
        pl.BlockSpec((sub, half), lambda i: (jnp.minimum((i + 1) * (tm // sub), n_sub - 1), 0)),
        _resident((1, d)),
        _mod_spec(d, 0, row_fn), _mod_spec(d, 1, row_fn), _mod_spec(d, 2, row_fn),
    ]
    for name in ("wconv", "wao", "wco", "wgate", "bgate", "wout"):
        specs.append(_resident(weights[name].shape))
    return specs


def _merge_ffn(x2d, ya, p, mods, row_fn, seq_len, wts, tm):
    n, d = x2d.shape
    specs = _mixer_specs(n, d, tm, p.shape[1], row_fn, wts)
    specs += [_resident((1, d)), _mod_spec(d, 3, row_fn), _mod_spec(d, 4, row_fn),
              _mod_spec(d, 5, row_fn),
              _resident(wts["ffn_gate"].shape), _resident(wts["ffn_up"].shape),
              _resident(wts["ffn_down"].shape)]
    return pl.pallas_call(
        functools.partial(_merge_ffn_kernel, seq_len=seq_len),
        grid=(n // tm,),
        in_specs=specs,
        out_specs=pl.BlockSpec((tm, d), lambda i: (i, 0)),
        out_shape=jax.ShapeDtypeStruct((n, d), _F32),
        compiler_params=_params(1),
        name="merge_ffn",
    )(x2d, ya, p, p, p, wts["norm_mix"], mods, mods, mods,
      wts["wconv"], wts["wao"], wts["wco"], wts["wgate"], wts["bgate"], wts["wout"],
      wts["norm_ffn"], mods, mods, mods, wts["ffn_gate"], wts["ffn_up"], wts["ffn_down"])


def _merge_router(x2d, ya, p, mods, row_fn, seq_len, wts, n_experts, tm, route_rows):
    n, d = x2d.shape
    assert tm % route_rows == 0
    specs = _mixer_specs(n, d, tm, p.shape[1], row_fn, wts)
    specs += [_resident((1, d)), _mod_spec(d, 3, row_fn), _mod_spec(d, 4, row_fn),
              _resident(wts["router"].shape)]
    tile = lambda w: pl.BlockSpec((tm, w), lambda i: (i, 0))
    meta_rows = n_experts * (tm // route_rows)
    n_route = n // route_rows
    return pl.pallas_call(
        functools.partial(_merge_router_kernel, seq_len=seq_len, n_experts=n_experts,
                          route_rows=route_rows),
        grid=(n // tm,),
        in_specs=specs,
        out_specs=[tile(d), tile(d),
                   pl.BlockSpec((meta_rows, route_rows), lambda i: (i, 0)),
                   pl.BlockSpec((meta_rows, _LANES), lambda i: (i, 0))],
        out_shape=[jax.ShapeDtypeStruct((n, d), _F32),
                   jax.ShapeDtypeStruct((n, d), _BF16),
                   jax.ShapeDtypeStruct((n_route * n_experts, route_rows), _F32),
                   jax.ShapeDtypeStruct((n_route * n_experts, _LANES), _F32)],
        compiler_params=_params(1),
        name="merge_router",
    )(x2d, ya, p, p, p, wts["norm_mix"], mods, mods, mods,
      wts["wconv"], wts["wao"], wts["wco"], wts["wgate"], wts["bgate"], wts["wout"],
      wts["norm_ffn"], mods, mods, wts["router"])


_SEG_ALIGN = _BF16_SUBLANES
_FFN_ROWS = 256


def _round_up(v, m):
    return (v + m - 1) // m * m


def _local_slots(tm, n_experts):
    return _round_up(_TOP_K * tm + n_experts * (_SEG_ALIGN - 1), _SEG_ALIGN)


def _routing_plan(cnt, n_slots):
    pad = _round_up(cnt, _SEG_ALIGN)
    total = jnp.sum(pad, axis=0)
    region = _round_up(total, _FFN_ROWS)
    ends = jnp.cumsum(region)
    seg_start = (ends - region)[None, :] + jnp.cumsum(pad, axis=0) - pad
    loc_off = jnp.cumsum(pad, axis=1) - pad
    n_used = (ends[-1] // _FFN_ROWS).astype(jnp.int32).reshape(1)
    tails = jnp.concatenate([ends - region + total, region - total, n_used])
    tile_row = jnp.arange(n_slots // _FFN_ROWS, dtype=jnp.int32) * _FFN_ROWS
    tile_expert = jnp.minimum(jnp.sum(tile_row[:, None] >= ends[None, :], axis=1),
                              cnt.shape[1] - 1).astype(jnp.int32)
    ids = jnp.arange(cnt.shape[1], dtype=jnp.int32)
    later = jnp.logical_and(region[None, :] > 0, ids[None, :] > ids[:, None])
    nxt = jnp.min(jnp.where(later, ids[None, :], cnt.shape[1]), axis=1)
    next_expert = jnp.where(nxt == cnt.shape[1], ids, nxt).astype(jnp.int32)
    flat = lambda a: a.reshape(-1).astype(jnp.int32)
    return ((flat(seg_start), flat(pad), flat(loc_off)), flat(tails), tile_expert, n_used,
            next_expert)


def _segment_copies(i, n_experts, seg_ref, pad_ref, loc_ref, hbm_ref, buf_ref, sem, to_hbm):
    copies = []
    for e in range(n_experts):
        n = pl.multiple_of(pad_ref[i * n_experts + e], _SEG_ALIGN)
        loc = pl.multiple_of(loc_ref[i * n_experts + e], _SEG_ALIGN)
        seg = pl.multiple_of(seg_ref[i * n_experts + e], _SEG_ALIGN)
        local, remote = buf_ref.at[pl.ds(loc, n)], hbm_ref.at[pl.ds(seg, n)]
        src, dst = (local, remote) if to_hbm else (remote, local)
        copies.append((n, pltpu.make_async_copy(src, dst, sem)))
    return copies


def _start_copies(copies):
    for n, cp in copies:
        pl.when(n > 0)(cp.start)


def _wait_copies(copies):
    for n, cp in copies:
        pl.when(n > 0)(cp.wait)


def _run_copies(copies):
    _start_copies(copies)
    _wait_copies(copies)


def _expert_offset(expert, loc_ref, i, n_experts):
    off = jnp.zeros(expert.shape, jnp.int32)
    for e in range(n_experts):
        off = jnp.where(expert == e, loc_ref[i * n_experts + e], off)
    return off


def _dispatch_kernel(seg_ref, pad_ref, loc_ref, tail_ref, h_ref, ml_ref, xs_ref, bufs, zeros, sems,
                     *, n_experts):
    i = pl.program_id(0)
    last = pl.num_programs(0) - 1
    cur = lax.rem(i, 2)

    def copies(tile, b):
        return _segment_copies(tile, n_experts, seg_ref, pad_ref, loc_ref, xs_ref, bufs.at[b],
                               sems.at[b], True)

    @pl.when(i >= 2)
    def _():
        _wait_copies(copies(i - 2, cur))

    tm = h_ref.shape[0]
    meta = ml_ref[...]
    slot1 = _expert_offset(meta[0:1].astype(jnp.int32), loc_ref, i, n_experts) + meta[2:3].astype(jnp.int32)
    slot2 = _expert_offset(meta[1:2].astype(jnp.int32), loc_ref, i, n_experts) + meta[3:4].astype(jnp.int32)
    slot = lax.broadcasted_iota(jnp.int32, (bufs.shape[1], tm), 0)
    perm = jnp.where(slot == slot1, 1.0, jnp.where(slot == slot2, 1.0, 0.0)).astype(_BF16)
    bufs[cur] = _bdot(perm, h_ref[...]).astype(bufs.dtype)
    _start_copies(copies(i, cur))

    @pl.when(i == last)
    def _():
        @pl.when(i >= 1)
        def _():
            _wait_copies(copies(i - 1, 1 - cur))

        _wait_copies(copies(i, cur))

        zeros[...] = jnp.zeros_like(zeros)
        fills = []
        for e in range(n_experts):
            start = pl.multiple_of(tail_ref[e], _SEG_ALIGN)
            n = pl.multiple_of(tail_ref[n_experts + e], _SEG_ALIGN)
            fills.append((n, pltpu.make_async_copy(zeros.at[pl.ds(0, n)],
                                                   xs_ref.at[pl.ds(start, n)], sems.at[0])))
        _run_copies(fills)

        def fill_copy(j):
            return pltpu.make_async_copy(
                zeros, xs_ref.at[pl.ds(pl.multiple_of(j * _FFN_ROWS, _FFN_ROWS), _FFN_ROWS)],
                sems.at[0])

        def start_fill(j, carry):
            fill_copy(j).start()
            return carry

        def wait_fill(j, carry):
            fill_copy(j).wait()
            return carry

        unused = (tail_ref[2 * n_experts], xs_ref.shape[0] // _FFN_ROWS)
        lax.fori_loop(*unused, start_fill, 0)
        lax.fori_loop(*unused, wait_fill, 0)


def _dispatch(h2, meta_l, plan, tails, n_slots, n_experts, tm):
    n, d = h2.shape
    seg_start, pad, loc_off = plan
    return pl.pallas_call(
        functools.partial(_dispatch_kernel, n_experts=n_experts),
        grid_spec=pltpu.PrefetchScalarGridSpec(
            num_scalar_prefetch=4,
            grid=(n // tm,),
            in_specs=[pl.BlockSpec((tm, d), lambda i, *_: (i, 0)),
                      pl.BlockSpec((n_experts, tm), lambda i, *_: (i, 0))],
            out_specs=pl.BlockSpec(memory_space=pl.ANY),
            scratch_shapes=[pltpu.VMEM((2, _local_slots(tm, n_experts), d), _BF16),
                            pltpu.VMEM((_FFN_ROWS, d), _BF16),
                            pltpu.SemaphoreType.DMA((2,))],
        ),
        out_shape=jax.ShapeDtypeStruct((n_slots, d), _BF16),
        compiler_params=_params(1),
        name="dispatch",
    )(seg_start, pad, loc_off, tails, h2, meta_l)


_W_ROUNDS = 16
_W_ROUNDS_PER_STEP = 2
_ST_CUR, _ST_LOADED, _ST_TARGET, _ST_DONE = range(4)


def _expert_ffn_kernel(te_ref, nu_ref, nxt_ref, x_ref, wg_hbm, wu_hbm, wd_hbm, o_ref,
                       wg_v, wu_v, wd_v, stage_gu, stage_d, sems, state):
    j = pl.program_id(0)
    rg = wg_v.shape[1] // _W_ROUNDS
    rd = wd_v.shape[1] // _W_ROUNDS

    def round_copies(e, r, s):
        g_rows = pl.ds(pl.multiple_of(r * rg, rg), rg)
        d_rows = pl.ds(pl.multiple_of(r * rd, rd), rd)
        return (pltpu.make_async_copy(wg_hbm.at[e, g_rows], stage_gu.at[s, 0], sems.at[s, 0]),
                pltpu.make_async_copy(wu_hbm.at[e, g_rows], stage_gu.at[s, 1], sems.at[s, 1]),
                pltpu.make_async_copy(wd_hbm.at[e, d_rows], stage_d.at[s], sems.at[s, 2]))

    def narrow_round(r, s, copy):
        g_rows = pl.ds(pl.multiple_of(r * rg, rg), rg)
        d_rows = pl.ds(pl.multiple_of(r * rd, rd), rd)
        wg_v[copy, g_rows, :] = stage_gu[s, 0].astype(_BF16)
        wu_v[copy, g_rows, :] = stage_gu[s, 1].astype(_BF16)
        wd_v[copy, d_rows, :] = stage_d[s].astype(_BF16)

    @pl.when(j == 0)
    def _():
        state[_ST_CUR] = 0
        state[_ST_LOADED] = -1
        state[_ST_TARGET] = -1
        state[_ST_DONE] = _W_ROUNDS

    active = j < nu_ref[0]
    e = te_ref[j]

    @pl.when(jnp.logical_and(active, e != state[_ST_LOADED]))
    def _():
        other = 1 - state[_ST_CUR]

        @pl.when(state[_ST_TARGET] != e)
        def _():
            state[_ST_TARGET] = e
            state[_ST_DONE] = 0

        first = state[_ST_DONE]

        @pl.when(first < _W_ROUNDS)
        def _():
            for cp in round_copies(e, first, 0):
                cp.start()

        def load_round(r, carry):
            s = lax.rem(r - first, 2)

            @pl.when(r + 1 < _W_ROUNDS)
            def _():
                for cp in round_copies(e, r + 1, 1 - s):
                    cp.start()

            for cp in round_copies(e, r, s):
                cp.wait()
            narrow_round(r, s, other)
            return carry

        lax.fori_loop(first, _W_ROUNDS, load_round, 0)
        nxt = nxt_ref[e]
        state[_ST_CUR] = other
        state[_ST_LOADED] = e
        state[_ST_TARGET] = nxt
        state[_ST_DONE] = jnp.where(nxt != e, 0, _W_ROUNDS)

    cur = state[_ST_CUR]
    target = state[_ST_TARGET]
    done = state[_ST_DONE]
    prefetch = jnp.logical_and(active, done < _W_ROUNDS)

    @pl.when(prefetch)
    def _():
        for s in range(_W_ROUNDS_PER_STEP):
            for cp in round_copies(target, done + s, s):
                cp.start()

    @pl.when(active)
    def _():
        x = x_ref[...]
        gate = _bdot(x, wg_v[cur])
        act = (gate * _sigmoid(gate)) * _bdot(x, wu_v[cur])
        o_ref[...] = _bdot(act.astype(_BF16), wd_v[cur]).astype(o_ref.dtype)

    @pl.when(jnp.logical_not(active))
    def _():
        o_ref[...] = jnp.zeros_like(o_ref)

    @pl.when(prefetch)
    def _():
        for s in range(_W_ROUNDS_PER_STEP):
            for cp in round_copies(target, done + s, s):
                cp.wait()
            narrow_round(done + s, s, 1 - cur)
        state[_ST_DONE] = done + _W_ROUNDS_PER_STEP


def _expert_ffn(xs, tile_expert, n_used, next_expert, wg, wu, wd):
    n_slots, d = xs.shape
    n_experts, _, f = wg.shape
    assert d % (_W_ROUNDS * _BF16_SUBLANES) == 0 and f % (_W_ROUNDS * _BF16_SUBLANES) == 0
    assert _W_ROUNDS % _W_ROUNDS_PER_STEP == 0 and _W_ROUNDS_PER_STEP >= 2
    any_spec = pl.BlockSpec(memory_space=pl.ANY)
    return pl.pallas_call(
        _expert_ffn_kernel,
        grid_spec=pltpu.PrefetchScalarGridSpec(
            num_scalar_prefetch=3,
            grid=(n_slots // _FFN_ROWS,),
            in_specs=[
                pl.BlockSpec((_FFN_ROWS, d),
                             lambda j, te, nu, nx: (jnp.maximum(jnp.minimum(j, nu[0] - 1), 0), 0)),
                any_spec, any_spec, any_spec,
            ],
            out_specs=pl.BlockSpec((_FFN_ROWS, d), lambda j, te, nu, nx: (j, 0)),
            scratch_shapes=[
                pltpu.VMEM((2, d, f), _BF16), pltpu.VMEM((2, d, f), _BF16),
                pltpu.VMEM((2, f, d), _BF16),
                pltpu.VMEM((_W_ROUNDS_PER_STEP, 2, d // _W_ROUNDS, f), _F32),
                pltpu.VMEM((_W_ROUNDS_PER_STEP, f // _W_ROUNDS, d), _F32),
                pltpu.SemaphoreType.DMA((_W_ROUNDS_PER_STEP, 3)),
                pltpu.SMEM((4,), jnp.int32),
            ],
        ),
        out_shape=jax.ShapeDtypeStruct((n_slots, d), _BF16),
        compiler_params=_params(1),
        name="expert_ffn",
    )(tile_expert, n_used, next_expert, xs, wg, wu, wd)


def _combine_kernel(seg_ref, pad_ref, loc_ref, ys_ref, ml_ref, x1_ref, g2_ref, nf_ref, o_ref,
                    bufs, sems, *, n_experts):
    i = pl.program_id(0)
    tm = x1_ref.shape[0]
    cur = lax.rem(i, 2)

    def copies(tile, b):
        return _segment_copies(tile, n_experts, seg_ref, pad_ref, loc_ref, ys_ref, bufs.at[b],
                               sems.at[b], False)

    @pl.when(i == 0)
    def _():
        bufs[...] = jnp.zeros_like(bufs)
        _start_copies(copies(0, 0))

    @pl.when(i + 1 < pl.num_programs(0))
    def _():
        _start_copies(copies(i + 1, 1 - cur))

    _wait_copies(copies(i, cur))
    meta = ml_ref[...]
    slot1 = _expert_offset(meta[0:1].astype(jnp.int32), loc_ref, i, n_experts) + meta[2:3].astype(jnp.int32)
    slot2 = _expert_offset(meta[1:2].astype(jnp.int32), loc_ref, i, n_experts) + meta[3:4].astype(jnp.int32)
    fields = [slot1.astype(_F32), slot2.astype(_F32), meta[4:5], meta[5:6]]
    per_token = jnp.concatenate(
        fields + [jnp.zeros((_LANES - len(fields), tm), _F32)], axis=0).T
    rc = min(tm, _MIX_CHUNK_ROWS)
    slot = lax.broadcasted_iota(jnp.int32, (rc, bufs.shape[1]), 1)
    for c in range(tm // rc):
        rows = slice(c * rc, (c + 1) * rc)
        col = lambda k: per_token[rows, k:k + 1]
        weights = jnp.where(slot == col(0).astype(jnp.int32), col(2),
                            jnp.where(slot == col(1).astype(jnp.int32), col(3), 0.0))
        y = _bdot(weights.astype(_BF16), bufs[cur])
        o_ref[rows, :] = _rmsnorm(x1_ref[rows, :] + g2_ref[...] * y, nf_ref[...])


def _combine(ys, meta_l, x1, plan, mods, row_fn, norm_final, n_experts, tm):
    n, d = x1.shape
    seg_start, pad, loc_off = plan
    return pl.pallas_call(
        functools.partial(_combine_kernel, n_experts=n_experts),
        grid_spec=pltpu.PrefetchScalarGridSpec(
            num_scalar_prefetch=3,
            grid=(n // tm,),
            in_specs=[pl.BlockSpec(memory_space=pl.ANY),
                      pl.BlockSpec((n_experts, tm), lambda i, *_: (i, 0)),
                      pl.BlockSpec((tm, d), lambda i, *_: (i, 0)),
                      _mod_spec(d, 5, row_fn),
                      _resident((1, d))],
            out_specs=pl.BlockSpec((tm, d), lambda i, *_: (i, 0)),
            scratch_shapes=[pltpu.VMEM((2, _local_slots(tm, n_experts), d), _BF16),
                            pltpu.SemaphoreType.DMA((2,))],
        ),
        out_shape=jax.ShapeDtypeStruct((n, d), _F32),
        compiler_params=_params(1),
        name="combine",
    )(seg_start, pad, loc_off, ys, meta_l, x1, mods, norm_final)


def kernel(x, c, ctx, c_ctx, w_ada, b_ada, norm_mix, norm_ffn, w_in, w_conv, rpb, w_attn_out,
           w_conv_out, w_gate, b_gate, w_out, w_ffn_gate, w_ffn_up, w_ffn_down, w_router,
           w_exp_gate, w_exp_up, w_exp_down, norm_final):
    batch, seq, d = x.shape
    ctx_len = ctx.shape[1]
    depth = w_ada.shape[0]
    attn_w = _N_HEADS * _HEAD_DIM
    n_experts = w_router.shape[-1]
    rows = seq // _GRID_W
    n_pairs = attn_w // _LANES
    assert depth == 2 and seq % (_Q_ROWS * _GRID_W) == 0 and attn_w % _LANES == 0
    assert n_experts == _F32_SUBLANES
    assert w_conv.shape[1] == 3

    ctx_row = batch
    cond_rows = -(-(batch + 1) // 8) * 8
    cond = jnp.zeros((cond_rows, d), _F32).at[:batch].set(c).at[ctx_row].set(c_ctx)
    mods_all = _ada(cond, w_ada, b_ada, tn=1536).reshape(depth, cond_rows, 1, 6 * d)

    union, starts, variant_of, patterns, col_starts = _attn_plan(rows)

    x2d = x.reshape(batch * seq, d)
    c2d = ctx.reshape(batch * ctx_len, d)
    tm_lat = 512
    ctx_row_fn = lambda i: ctx_row

    def lat_rows(tm):
        return lambda i: (i * tm) // seq

    out = None
    for l in range(depth):
        last = l == depth - 1
        mods = mods_all[l]
        wts = {
            "norm_mix": norm_mix[l].reshape(1, d),
            "norm_ffn": norm_ffn[l].reshape(1, d),
            "wconv": w_conv[l],
            "wao": w_attn_out[l].astype(_BF16),
            "wco": w_conv_out[l].astype(_BF16),
            "wgate": w_gate[l].astype(_BF16),
            "bgate": b_gate[l].reshape(1, -1),
            "wout": w_out[l].astype(_BF16),
        }
        w_in_l = w_in[l].astype(_BF16)
        qkv, bcu = _proj(x2d, mods, lat_rows(1024), wts["norm_mix"], w_in_l, tm=1024,
                         n_blocked=3 * n_pairs)
        if last:
            qkv_c, bcu_c = _proj(c2d, mods, ctx_row_fn, wts["norm_mix"],
                                 w_in_l[:, attn_w:3 * attn_w], tm=512, n_blocked=2 * n_pairs)
            kc_col = 0
        else:
            qkv_c, bcu_c = _proj(c2d, mods, ctx_row_fn, wts["norm_mix"], w_in_l, tm=512,
                                 n_blocked=3 * n_pairs)
            kc_col = n_pairs
        table = _bias_table(rpb[l], union, patterns, col_starts)
        ya, yca = _attention(qkv, qkv_c, kc_col, table, starts, variant_of, union, col_starts, batch,
                             seq, ctx_len, with_ctx=not last)
        if not last:
            wts["ffn_gate"] = w_ffn_gate[l // 2].astype(_BF16)
            wts["ffn_up"] = w_ffn_up[l // 2].astype(_BF16)
            wts["ffn_down"] = w_ffn_down[l // 2].astype(_BF16)
            x2d = _merge_ffn(x2d, ya, bcu, mods, lat_rows(tm_lat), seq, wts, tm=tm_lat)
            c2d = _merge_ffn(c2d, yca, bcu_c, mods, ctx_row_fn, ctx_len, wts, tm=ctx_len)
        else:
            wr_t = w_router[l // 2].T
            wr_hi = wr_t.astype(_BF16)
            wr_lo = (wr_t - wr_hi.astype(_F32)).astype(_BF16)
            wts["router"] = jnp.concatenate([wr_hi, wr_lo], axis=0)
            tm_moe = 512
            n_tok = batch * seq
            n_tiles = n_tok // tm_moe
            tm_mix = 2 * tm_moe
            x1, h2, meta_l, cnt = _merge_router(
                x2d, ya, bcu, mods, lat_rows(tm_mix), seq, wts, n_experts, tm=tm_mix,
                route_rows=tm_moe)
            n_slots = _round_up(_TOP_K * n_tok + n_tiles * n_experts * (_SEG_ALIGN - 1)
                                + n_experts * (_FFN_ROWS - 1), _FFN_ROWS)
            cnt = cnt[:, 0].reshape(n_tiles, n_experts).astype(jnp.int32)
            plan, tails, tile_expert, n_used, next_expert = _routing_plan(cnt, n_slots)
            xs = _dispatch(h2, meta_l, plan, tails, n_slots, n_experts, tm=tm_moe)
            ys = _expert_ffn(xs, tile_expert, n_used, next_expert, w_exp_gate[l // 2],
                             w_exp_up[l // 2], w_exp_down[l // 2])
            out = _combine(ys, meta_l, x1, plan, mods, lat_rows(tm_moe), norm_final.reshape(1, d),
                           n_experts, tm=tm_moe)
    return out.reshape(batch, seq, d)
```
